```python
import math
import jax, jax.numpy as jnp
from jax import lax
import numpy as np

D_MODEL = 2048
BATCH = 4
SEQ = 2048
DEPTH = 1
DEC_BATCH = 8
DEC_SEQ = 1
PAST_LEN = 16384
PAGE_SIZE = 128

HEAD_DIM = 128
MIX_WIDTH = D_MODEL
NSA_WIDTH = MIX_WIDTH // 2
GDN_WIDTH = MIX_WIDTH - NSA_WIDTH
NSA_HEADS = NSA_WIDTH // HEAD_DIM
NSA_KV_HEADS = max(1, NSA_HEADS // 4)
NSA_GROUP = NSA_HEADS // NSA_KV_HEADS
KV_WIDTH = NSA_KV_HEADS * HEAD_DIM
CMP_BLOCK = 32
SEL_BLOCK = 64
CMP_PER_SEL = SEL_BLOCK // CMP_BLOCK
SEL_TOPN = 16
WINDOW = 512
NSA_Q_BLOCK = 64
CMP_HIDDEN = HEAD_DIM
GDN_HEADS = GDN_WIDTH // HEAD_DIM
CONV_WIDTH = 4
GDN_CHUNK = 64
PEER_HEADS = 8
PEER_NKEYS = 128
PEER_N_EXPERTS = PEER_NKEYS * PEER_NKEYS
PEER_QDIM = 256
PEER_TOPK = 16
PEER_BLOCK = 128
NORM_EPS = 1e-6
NEG_INF = -1e30
FORCE_SCORE = 1e4
IN_SPLITS = (NSA_WIDTH, 2 * KV_WIDTH, 2 * KV_WIDTH, 2 * KV_WIDTH, 3 * NSA_HEADS,
             3 * GDN_WIDTH, GDN_WIDTH, GDN_HEADS, GDN_HEADS)
IN_COLS = sum(IN_SPLITS)

kernel_name = 'hybrid_nsa_gdn_peer_adaln_step'


def rmsnorm(x, w):
    xf = x.astype(jnp.float32)
    y = xf * lax.rsqrt(jnp.mean(xf * xf, axis=-1, keepdims=True) + NORM_EPS)
    return (y * w.astype(jnp.float32)).astype(x.dtype)


def l2norm(x):
    return x * lax.rsqrt(jnp.sum(x * x, axis=-1, keepdims=True) + NORM_EPS)


def masked_softmax(s, mask, axis=-1):
    p = jax.nn.softmax(jnp.where(mask, s, NEG_INF), axis=axis)
    return jnp.where(mask, p, 0.0)


def alibi_slopes(n):
    return 2.0 ** (-8.0 * jnp.arange(1, n + 1, dtype=jnp.float32) / n)


def adaln(c, w, b):
    mod = jax.nn.silu(c) @ w + b
    return jnp.split(mod, 6, axis=-1)


def split_projection(p):
    B, T = p.shape[:2]
    offs = [int(o) for o in np.cumsum(IN_SPLITS)[:-1]]
    q, kvc, kvs, kvw, gts, qkv, z, bl, al = jnp.split(p, offs, axis=-1)
    kv_shape = (B, T, 2, NSA_KV_HEADS, HEAD_DIM)
    return (q.reshape(B, T, NSA_KV_HEADS, NSA_GROUP, HEAD_DIM), kvc.reshape(kv_shape),
            kvs.reshape(kv_shape), kvw.reshape(kv_shape),
            gts.reshape(B, T, NSA_KV_HEADS, NSA_GROUP, 3), qkv, z, bl, al)


def compress_blocks(kv, cmp_pe, cmp_w1, cmp_b1, cmp_w2):
    B, L = kv.shape[:2]
    nc = L // CMP_BLOCK
    blocks = kv[:, :nc * CMP_BLOCK].reshape(B, nc, CMP_BLOCK, 2, NSA_KV_HEADS, HEAD_DIM)
    blocks = blocks + jnp.transpose(cmp_pe, (1, 0, 2))[:, :, None, :]
    hid = jax.nn.gelu(jnp.einsum('bnlshd,slde->bnshe', blocks, cmp_w1) + cmp_b1[:, None, :])
    ckv = jnp.einsum('bnshe,sed->bnshd', hid, cmp_w2)
    cmp_end = jnp.arange(nc, dtype=jnp.int32) * CMP_BLOCK + (CMP_BLOCK - 1)
    return ckv, cmp_end


def selection_blocks(kv):
    B, L = kv.shape[:2]
    nsb = -(-L // SEL_BLOCK)
    kv = jnp.pad(kv, ((0, 0), (0, nsb * SEL_BLOCK - L), (0, 0), (0, 0), (0, 0)))
    kv = kv.reshape(B, nsb, SEL_BLOCK, 2, NSA_KV_HEADS, HEAD_DIM).transpose(3, 0, 4, 1, 2, 5)
    return kv[0], kv[1]


def nsa_attend(q, t_pos, gts, cmp_k, cmp_v, cmp_end, sel_k, sel_v, win_k, win_v, win_pos, slopes):
    f32 = jnp.float32
    B = q.shape[0]
    qf = q.astype(f32) * HEAD_DIM ** -0.5
    m = slopes[None, :, :, None, None]
    s = jnp.einsum('bqhgd,bnhd->bhgqn', qf, cmp_k.astype(f32))
    dist = (t_pos[:, None] - cmp_end[None, :]).astype(f32)
    p_cmp = masked_softmax(s - m * dist, cmp_end[None, :] <= t_pos[:, None])
    o_cmp = jnp.einsum('bhgqn,bnhd->bqhgd', p_cmp, cmp_v.astype(f32))
    nc = cmp_k.shape[1]
    nsb = sel_k.shape[2]
    imp = jnp.sum(p_cmp, axis=2)
    imp = jnp.pad(imp, ((0, 0), (0, 0), (0, 0), (0, nsb * CMP_PER_SEL - nc)))
    imp = imp.reshape(*imp.shape[:3], nsb, CMP_PER_SEL).sum(-1)
    blk = jnp.arange(nsb, dtype=jnp.int32)[None, :]
    cur = (t_pos // SEL_BLOCK)[:, None]
    forced = (blk == 0) | (blk == cur) | (blk == cur - 1)
    score = jnp.where(blk <= cur, jnp.where(forced, FORCE_SCORE, imp), NEG_INF)
    top_s, top_i = lax.top_k(score, min(SEL_TOPN, nsb))
    bi = jnp.arange(B)[:, None, None, None]
    hi = jnp.arange(NSA_KV_HEADS)[None, :, None, None]
    k_sel = sel_k[bi, hi, top_i].astype(f32)
    v_sel = sel_v[bi, hi, top_i].astype(f32)
    pos = top_i[..., None] * SEL_BLOCK + jnp.arange(SEL_BLOCK, dtype=jnp.int32)
    tq = t_pos[None, None, :, None, None]
    mask = ((top_s > 0.5 * NEG_INF)[..., None] & (pos <= tq))[:, :, None]
    dist = (tq - pos).astype(f32)[:, :, None]
    s = jnp.einsum('bqhgd,bhqnld->bhgqnl', qf, k_sel)
    p_sel = masked_softmax(s - m[..., None] * dist, mask, axis=(-2, -1))
    o_sel = jnp.einsum('bhgqnl,bhqnld->bqhgd', p_sel, v_sel)
    s = jnp.einsum('bqhgd,bkhd->bhgqk', qf, win_k.astype(f32))
    delta = t_pos[:, None] - win_pos[None, :]
    wmask = (delta >= 0) & (delta < WINDOW) & (win_pos[None, :] >= 0)
    p_win = masked_softmax(s - m * delta.astype(f32), wmask)
    o_win = jnp.einsum('bhgqk,bkhd->bqhgd', p_win, win_v.astype(f32))
    gate = jax.nn.sigmoid(gts.astype(f32))
    o = gate[..., 0:1] * o_cmp + gate[..., 1:2] * o_sel + gate[..., 2:3] * o_win
    return o.reshape(*q.shape[:2], NSA_WIDTH).astype(q.dtype)


def nsa_prompt(q, gts, kvc, kvs, kvw, cw, slopes):
    B, T = q.shape[:2]
    ckv, cmp_end = compress_blocks(kvc, *cw)
    sel_k, sel_v = selection_blocks(kvs)
    win = jnp.pad(kvw, ((0, 0), (WINDOW, 0), (0, 0), (0, 0), (0, 0)))
    nqb = T // NSA_Q_BLOCK
    qb = q.reshape(B, nqb, NSA_Q_BLOCK, *q.shape[2:]).swapaxes(0, 1)
    gb = gts.reshape(B, nqb, NSA_Q_BLOCK, *gts.shape[2:]).swapaxes(0, 1)

    def one_block(args):
        i, q_i, g_i = args
        start = i * NSA_Q_BLOCK
        t_pos = start + jnp.arange(NSA_Q_BLOCK, dtype=jnp.int32)
        w_i = lax.dynamic_slice_in_dim(win, start, WINDOW + NSA_Q_BLOCK, axis=1)
        w_pos = start - WINDOW + jnp.arange(WINDOW + NSA_Q_BLOCK, dtype=jnp.int32)
        return nsa_attend(q_i, t_pos, g_i, ckv[:, :, 0], ckv[:, :, 1], cmp_end, sel_k, sel_v,
                          w_i[:, :, 0], w_i[:, :, 1], w_pos, slopes)

    out = lax.map(one_block, (jnp.arange(nqb, dtype=jnp.int32), qb, gb))
    return out.swapaxes(0, 1).reshape(B, T, NSA_WIDTH)


def nsa_sample(q, gts, kvc, kvs, kvw, cache_cmp, cache_sel, page_table, win_buf, cw, slopes):
    Bd, Tn = q.shape[:2]
    past = page_table.shape[1] * PAGE_SIZE

    def with_past(cache, new):
        rows = cache[page_table].reshape(Bd, past, *cache.shape[2:])
        return jnp.concatenate([rows, new.astype(cache.dtype)], axis=1)

    ckv, cmp_end = compress_blocks(with_past(cache_cmp, kvc), *cw)
    sel_k, sel_v = selection_blocks(with_past(cache_sel, kvs))
    wkv = jnp.concatenate([win_buf, kvw.astype(win_buf.dtype)], axis=1)
    n_buf = win_buf.shape[1]
    t_pos = past + jnp.arange(Tn, dtype=jnp.int32)
    w_pos = past - n_buf + jnp.arange(n_buf + Tn, dtype=jnp.int32)
    o = nsa_attend(q, t_pos, gts, ckv[:, :, 0], ckv[:, :, 1], cmp_end, sel_k, sel_v,
                   wkv[:, :, 0], wkv[:, :, 1], w_pos, slopes)
    return o, wkv[:, Tn:]


def chunk_gated_delta(q, k, v, g, beta, s0):
    B, T, H, dk = q.shape
    dv = v.shape[-1]
    C = GDN_CHUNK
    n = -(-T // C)
    pad = n * C - T

    def chunks(a):
        a = jnp.pad(a, ((0, 0), (0, pad)) + ((0, 0),) * (a.ndim - 2))
        a = a.reshape(B, n, C, *a.shape[2:])
        return jnp.moveaxis(a, 3, 1)

    q, k, v, g, beta = (chunks(a) for a in (q, k, v, g, beta))
    G = jnp.cumsum(g, axis=-1)
    tri = jnp.tril(jnp.ones((C, C), bool))
    strict = jnp.tril(jnp.ones((C, C), bool), -1)
    diff = G[..., :, None] - G[..., None, :]
    decay = jnp.where(tri, jnp.exp(jnp.where(tri, diff, 0.0)), 0.0)
    kb = k * beta[..., None]
    A = jnp.einsum('bhncd,bhnmd->bhncm', kb, k) * jnp.where(strict, decay, 0.0)
    M = A + jnp.eye(C, dtype=A.dtype)
    rhs = jnp.concatenate([v * beta[..., None], kb * jnp.exp(G)[..., None]], axis=-1)
    sol = lax.linalg.triangular_solve(M, rhs, left_side=True, lower=True, unit_diagonal=True)
    U, W = sol[..., :dv], sol[..., dv:]
    qk = jnp.einsum('bhncd,bhnmd->bhncm', q, k) * decay
    qg = q * jnp.exp(G)[..., None]
    kd = k * jnp.exp(G[..., -1:] - G)[..., None]
    g_last = jnp.exp(G[..., -1])

    def step(S, xs):
        U_c, W_c, qk_c, qg_c, kd_c, gl_c = xs
        v_new = U_c - jnp.einsum('bhcd,bhde->bhce', W_c, S)
        o_c = jnp.einsum('bhcd,bhde->bhce', qg_c, S) + jnp.einsum('bhcm,bhme->bhce', qk_c, v_new)
        S = S * gl_c[..., None, None] + jnp.einsum('bhcd,bhce->bhde', kd_c, v_new)
        return S, o_c

    xs = tuple(jnp.moveaxis(a, 2, 0) for a in (U, W, qk, qg, kd, g_last))
    S, o = lax.scan(step, s0, xs)
    o = jnp.moveaxis(o, 0, 2).transpose(0, 2, 3, 1, 4).reshape(B, n * C, H, dv)[:, :T]
    return o, S


def gdn_mix(qkv, z, b_logit, a_logit, conv_buf, s0, conv_w, a_log, dt_bias, gn_w):
    f32 = jnp.float32
    B, T = qkv.shape[:2]
    xpad = jnp.concatenate([conv_buf.astype(qkv.dtype), qkv], axis=1)
    conv = sum(xpad[:, i:i + T] * conv_w[i] for i in range(CONV_WIDTH))
    conv = jax.nn.silu(conv.astype(f32))

    def heads(a):
        return a.reshape(B, T, GDN_HEADS, HEAD_DIM)

    q = l2norm(heads(conv[..., :GDN_WIDTH])) * HEAD_DIM ** -0.5
    k = l2norm(heads(conv[..., GDN_WIDTH:2 * GDN_WIDTH]))
    v = heads(conv[..., 2 * GDN_WIDTH:])
    beta = jax.nn.sigmoid(b_logit.astype(f32))
    g = -jnp.exp(a_log.astype(f32)) * jax.nn.softplus(a_logit.astype(f32) + dt_bias.astype(f32))
    o, s_new = chunk_gated_delta(q, k, v, g, beta, s0.astype(f32))
    o = rmsnorm(o, gn_w) * jax.nn.silu(heads(z).astype(f32))
    return o.reshape(B, T, GDN_WIDTH).astype(qkv.dtype), xpad[:, T:], s_new.astype(s0.dtype)


def peer_ffn(h, w_q, sub_keys, u_tab, v_tab):
    B, T, D = h.shape
    n = B * T
    nb = -(-n // PEER_BLOCK)
    flat = jnp.pad(h.reshape(n, D), ((0, nb * PEER_BLOCK - n), (0, 0))).reshape(nb, PEER_BLOCK, D)

    def one_block(hb):
        f32 = jnp.float32
        q = (hb @ w_q).astype(f32).reshape(PEER_BLOCK, PEER_HEADS, 2, PEER_QDIM // 2)
        s = jnp.einsum('nhpd,pkd->nhpk', q, sub_keys.astype(f32))
        s_half, i_half = lax.top_k(s, PEER_TOPK)
        cand = (s_half[:, :, 0, :, None] + s_half[:, :, 1, None, :]).reshape(PEER_BLOCK, PEER_HEADS, PEER_TOPK * PEER_TOPK)
        cid = (i_half[:, :, 0, :, None] * PEER_NKEYS + i_half[:, :, 1, None, :]).reshape(PEER_BLOCK, PEER_HEADS, PEER_TOPK * PEER_TOPK)
        best_s, best_j = lax.top_k(cand, PEER_TOPK)
        eid = jnp.take_along_axis(cid, best_j, axis=-1)
        gate = jax.nn.softmax(best_s, axis=-1)
        act = jax.nn.gelu(jnp.einsum('nd,nhkd->nhk', hb.astype(f32), u_tab[eid].astype(f32)))
        out = jnp.einsum('nhk,nhkd->nd', gate * act, v_tab[eid].astype(f32))
        return out.astype(hb.dtype)

    out = lax.map(one_block, flat)
    return out.reshape(nb * PEER_BLOCK, D)[:n].reshape(B, T, D)


def conditioned_layer(x, c, mixer, w_ada, b_ada, norm_mix, norm_ffn, w_in, w_out,
                      peer_wq, peer_keys, peer_u, peer_v):
    sh1, sc1, g1, sh2, sc2, g2 = adaln(c, w_ada, b_ada)
    h = rmsnorm(x, norm_mix) * (1.0 + sc1[:, None]) + sh1[:, None]
    mixed, states = mixer(split_projection(h @ w_in))
    x = x + g1[:, None] * (mixed @ w_out)
    h = rmsnorm(x, norm_ffn) * (1.0 + sc2[:, None]) + sh2[:, None]
    x = x + g2[:, None] * peer_ffn(h, peer_wq, peer_keys, peer_u, peer_v)
    return x, states


def stack_layers(per_layer, i):
    return jnp.stack([st[i] for st in per_layer])


def setup_inputs(seed: int = 0) -> dict:
    key = jax.random.key(seed)
    ks = iter(jax.random.split(key, 40))

    def nrm(shape, scale):
        return jax.random.normal(next(ks), shape, jnp.float32) * scale

    n_pages = PAST_LEN // PAGE_SIZE
    n_phys = (DEC_BATCH * n_pages * 5) // 4
    w_buf = min(WINDOW, PAST_LEN)
    row = (2, NSA_KV_HEADS, HEAD_DIM)
    page_table = jax.random.permutation(next(ks), n_phys)[:DEC_BATCH * n_pages].reshape(DEC_BATCH, n_pages).astype(jnp.int32)
    dt = jnp.exp(jax.random.uniform(next(ks), (DEPTH, GDN_HEADS), jnp.float32, minval=math.log(1e-3), maxval=math.log(1e-1)))
    a_init = jax.random.uniform(next(ks), (DEPTH, GDN_HEADS), jnp.float32, minval=1.0, maxval=16.0)
    return {
        'x_prompt': nrm((BATCH, SEQ, D_MODEL), 1.0),
        'x_sample': nrm((DEC_BATCH, DEC_SEQ, D_MODEL), 1.0),
        'cache_cmp_kv': nrm((DEPTH, n_phys, PAGE_SIZE) + row, 1.0),
        'cache_sel_kv': nrm((DEPTH, n_phys, PAGE_SIZE) + row, 1.0),
        'state_win_kv': nrm((DEPTH, DEC_BATCH, w_buf) + row, 1.0),
        'state_conv': nrm((DEPTH, DEC_BATCH, CONV_WIDTH - 1, 3 * GDN_WIDTH), 1.0),
        'state_gdn': nrm((DEPTH, DEC_BATCH, GDN_HEADS, HEAD_DIM, HEAD_DIM), 0.1),
        'page_table': page_table,
        'c_prompt': nrm((BATCH, D_MODEL), 1.0),
        'c_sample': nrm((DEC_BATCH, D_MODEL), 1.0),
        'w_ada': nrm((DEPTH, D_MODEL, 6 * D_MODEL), 0.5 * D_MODEL ** -0.5),
        'b_ada': nrm((DEPTH, 6 * D_MODEL), 0.01),
        'norm_mix': 1.0 + nrm((DEPTH, D_MODEL), 0.02),
        'norm_ffn': 1.0 + nrm((DEPTH, D_MODEL), 0.02),
        'w_in': nrm((DEPTH, D_MODEL, IN_COLS), D_MODEL ** -0.5),
        'cmp_pe': nrm((DEPTH, 2, CMP_BLOCK, HEAD_DIM), 0.02),
        'cmp_w1': nrm((DEPTH, 2, CMP_BLOCK, HEAD_DIM, CMP_HIDDEN), (CMP_BLOCK * HEAD_DIM) ** -0.5),
        'cmp_b1': nrm((DEPTH, 2, CMP_HIDDEN), 0.01),
        'cmp_w2': nrm((DEPTH, 2, CMP_HIDDEN, HEAD_DIM), CMP_HIDDEN ** -0.5),
        'conv_w': nrm((DEPTH, CONV_WIDTH, 3 * GDN_WIDTH), CONV_WIDTH ** -0.5),
        'gdn_a_log': jnp.log(a_init),
        'gdn_dt_bias': jnp.log(jnp.expm1(dt)),
        'gdn_norm': 1.0 + nrm((DEPTH, HEAD_DIM), 0.02),
        'w_out': nrm((DEPTH, D_MODEL, D_MODEL), D_MODEL ** -0.5),
        'peer_wq': nrm((DEPTH, D_MODEL, PEER_HEADS * PEER_QDIM), D_MODEL ** -0.5),
        'peer_keys': nrm((DEPTH, 2, PEER_NKEYS, PEER_QDIM // 2), (PEER_QDIM // 2) ** -0.5),
        'peer_u': nrm((DEPTH, PEER_N_EXPERTS, D_MODEL), D_MODEL ** -0.5),
        'peer_v': nrm((DEPTH, PEER_N_EXPERTS, D_MODEL), PEER_HEADS ** -0.5),
        'final_norm': 1.0 + nrm((D_MODEL,), 0.02),
    }


def reference(x_prompt, x_sample, cache_cmp_kv, cache_sel_kv, state_win_kv, state_conv, state_gdn,
              page_table, c_prompt, c_sample, w_ada, b_ada, norm_mix, norm_ffn, w_in, cmp_pe, cmp_w1,
              cmp_b1, cmp_w2, conv_w, gdn_a_log, gdn_dt_bias, gdn_norm, w_out, peer_wq, peer_keys,
              peer_u, peer_v, final_norm):
    slopes = alibi_slopes(NSA_HEADS).reshape(NSA_KV_HEADS, NSA_GROUP)
    xp, xs = x_prompt, x_sample
    prompt_states, sample_states = [], []
    for l in range(DEPTH):
        cw = (cmp_pe[l], cmp_w1[l], cmp_b1[l], cmp_w2[l])
        gw = (conv_w[l], gdn_a_log[l], gdn_dt_bias[l], gdn_norm[l])
        shared = (w_ada[l], b_ada[l], norm_mix[l], norm_ffn[l], w_in[l], w_out[l],
                  peer_wq[l], peer_keys[l], peer_u[l], peer_v[l])

        def prompt_mixer(parts):
            q, kvc, kvs, kvw, gts, qkv, z, bl, al = parts
            o_nsa = nsa_prompt(q, gts, kvc, kvs, kvw, cw, slopes)
            Bp, T = q.shape[:2]
            conv0 = jnp.zeros((Bp, CONV_WIDTH - 1, qkv.shape[-1]), qkv.dtype)
            s0 = jnp.zeros((Bp, GDN_HEADS, HEAD_DIM, HEAD_DIM), jnp.float32)
            o_gdn, conv_new, s_new = gdn_mix(qkv, z, bl, al, conv0, s0, *gw)
            keep = min(WINDOW, T)
            return jnp.concatenate([o_nsa, o_gdn], axis=-1), (kvc, kvs, kvw[:, T - keep:], conv_new, s_new)

        def sample_mixer(parts):
            q, kvc, kvs, kvw, gts, qkv, z, bl, al = parts
            o_nsa, win_new = nsa_sample(q, gts, kvc, kvs, kvw, cache_cmp_kv[l], cache_sel_kv[l],
                                        page_table, state_win_kv[l], cw, slopes)
            o_gdn, conv_new, s_new = gdn_mix(qkv, z, bl, al, state_conv[l], state_gdn[l], *gw)
            return jnp.concatenate([o_nsa, o_gdn], axis=-1), (kvc, kvs, win_new, conv_new, s_new)

        xp, st_p = conditioned_layer(xp, c_prompt, prompt_mixer, *shared)
        xs, st_s = conditioned_layer(xs, c_sample, sample_mixer, *shared)
        prompt_states.append(st_p)
        sample_states.append(st_s)
    y_prompt = rmsnorm(xp, final_norm)
    y_sample = rmsnorm(xs, final_norm)
    return (y_prompt, y_sample,
            stack_layers(prompt_states, 0), stack_layers(sample_states, 0),
            stack_layers(prompt_states, 1), stack_layers(sample_states, 1),
            stack_layers(prompt_states, 2), stack_layers(sample_states, 2),
            stack_layers(prompt_states, 3), stack_layers(sample_states, 3),
            stack_layers(prompt_states, 4), stack_layers(sample_states, 4))
```

```python
import functools
import math

import jax
import jax.numpy as jnp
from jax import lax
from jax.experimental import pallas as pl
from jax.experimental.pallas import tpu as pltpu

F32 = jnp.float32
BF16 = jnp.bfloat16
I32 = jnp.int32

D_MODEL = 2048
HEAD_DIM = 128
NSA_WIDTH = 1024
GDN_WIDTH = 1024
NSA_HEADS = 8
NSA_KV_HEADS = 2
NSA_GROUP = 4
KV_WIDTH = 256
CMP_BLOCK = 32
SEL_BLOCK = 64
SEL_TOPN = 16
WINDOW = 512
GDN_HEADS = 8
CONV_WIDTH = 4
GDN_CHUNK = 64
PEER_HEADS = 8
PEER_NKEYS = 128
PEER_QDIM = 256
PEER_TOPK = 16
PAGE_SIZE = 128
NORM_EPS = 1e-6
NEG_INF = -1e30
FORCE_SCORE = 1e4
NEG_BIG = -3.0e38

LANE = 128
SUBLANE = 8
VMEM_LIMIT = 56 * 1024 * 1024

COL_Q = 0
COL_KVC = 1024
COL_KVS = 1536
COL_KVW = 2048
COL_QKV = 2560
COL_Z = 5632
COL_MISC = 6656
N_PROJ = 6912
MISC_BL = 24
MISC_AL = 32

NN = (((1,), (0,)), ((), ()))
NT = (((1,), (1,)), ((), ()))
BNN = (((2,), (1,)), ((0,), (0,)))
BNT = (((2,), (2,)), ((0,), (0,)))


def _dot(a, b, dims=NN):
    return lax.dot_general(a, b, dims, preferred_element_type=F32)


def _split(x):
    hi = x.astype(BF16)
    lo = (x - hi.astype(F32)).astype(BF16)
    return hi, lo


def _mm1(a, b, dims=NN):
    return _dot(a.astype(BF16), b.astype(BF16), dims)


def _mm3(a, b, dims=NN):
    ah, al = _split(a)
    bh, bl = _split(b)
    return _dot(ah, bh, dims) + (_dot(al, bh, dims) + _dot(ah, bl, dims))


def _cparams(sem, vmem=VMEM_LIMIT):
    return pltpu.CompilerParams(dimension_semantics=sem, vmem_limit_bytes=vmem)


def _adaln_body(c_ref, w_ref, b_ref, o_ref):
    a = jax.nn.silu(c_ref[...])
    o_ref[...] = _mm3(a, w_ref[...]) + b_ref[...]


def _adaln(c_all, w, b):
    rows, d = c_all.shape
    n = w.shape[1]
    tn = 1024
    return pl.pallas_call(
        _adaln_body,
        out_shape=jax.ShapeDtypeStruct((rows, n), F32),
        grid=(n // tn,),
        in_specs=[pl.BlockSpec((rows, d), lambda j: (0, 0)),
                  pl.BlockSpec((d, tn), lambda j: (0, j)),
                  pl.BlockSpec((1, tn), lambda j: (0, j))],
        out_specs=pl.BlockSpec((rows, tn), lambda j: (0, j)),
        compiler_params=_cparams(("arbitrary",)),
        name="adaln",
    )(c_all, w, b.reshape(1, n))


def _proj_in_body(x_ref, nw_ref, sc_ref, sh_ref, w_ref, o_ref, h_ref):
    @pl.when(pl.program_id(1) == 0)
    def _():
        x = x_ref[...]
        y = x * lax.rsqrt(jnp.mean(x * x, axis=-1, keepdims=True) + NORM_EPS)
        h = (y * nw_ref[...]) * (1.0 + sc_ref[0]) + sh_ref[0]
        h_ref[...] = h.astype(BF16)

    o_ref[...] = _dot(h_ref[...], w_ref[...])


def _proj_in(x, norm_w, sc, sh, w2, seq_len, tm):
    n, d = x.shape
    ncol = w2.shape[1]
    tn = 768
    tpb = seq_len // tm
    return pl.pallas_call(
        _proj_in_body,
        out_shape=jax.ShapeDtypeStruct((n, ncol), F32),
        grid=(n // tm, ncol // tn),
        in_specs=[pl.BlockSpec((tm, d), lambda i, j: (i, 0)),
                  pl.BlockSpec((1, d), lambda i, j: (0, 0)),
                  pl.BlockSpec((1, 1, d), lambda i, j: (i // tpb, 0, 0)),
                  pl.BlockSpec((1, 1, d), lambda i, j: (i // tpb, 0, 0)),
                  pl.BlockSpec((d, tn), lambda i, j: (0, j))],
        out_specs=pl.BlockSpec((tm, tn), lambda i, j: (i, j)),
        scratch_shapes=[pltpu.VMEM((tm, d), BF16)],
        compiler_params=_cparams(("arbitrary", "arbitrary")),
        name="proj_in",
    )(x, norm_w.reshape(1, d), sc, sh, w2)


def _compress_prompt_body(x_ref, pe_ref, w1_ref, b1_ref, w2_ref, o_ref, *, nblk):
    half = nblk // 2
    acc = jnp.zeros((nblk, HEAD_DIM), F32)
    for l in range(CMP_BLOCK):
        xe = x_ref[pl.ds(l, half, stride=2 * CMP_BLOCK), :]
        xo = x_ref[pl.ds(CMP_BLOCK + l, half, stride=2 * CMP_BLOCK), :]
        xl = jnp.concatenate([xe, xo], axis=0) + pe_ref[0, l:l + 1, :]
        acc = acc + _mm3(xl, w1_ref[0, l * HEAD_DIM:(l + 1) * HEAD_DIM, :])
    hid = jax.nn.gelu(acc + b1_ref[0])
    o_ref[0, 0] = _mm3(hid, w2_ref[0])


def _compress_prompt(p2, pe, w1, b1, w2, nbatch, seq_len):
    nblk = seq_len // CMP_BLOCK
    kv0 = COL_KVC // LANE
    return pl.pallas_call(
        functools.partial(_compress_prompt_body, nblk=nblk),
        out_shape=jax.ShapeDtypeStruct((nbatch, 4, nblk, HEAD_DIM), F32),
        grid=(nbatch, 4),
        in_specs=[pl.BlockSpec((seq_len, LANE), lambda b, sh: (b, kv0 + sh)),
                  pl.BlockSpec((1, CMP_BLOCK, HEAD_DIM), lambda b, sh: (sh // 2, 0, 0)),
                  pl.BlockSpec((1, CMP_BLOCK * HEAD_DIM, HEAD_DIM), lambda b, sh: (sh // 2, 0, 0)),
                  pl.BlockSpec((1, 1, HEAD_DIM), lambda b, sh: (sh // 2, 0, 0)),
                  pl.BlockSpec((1, HEAD_DIM, HEAD_DIM), lambda b, sh: (sh // 2, 0, 0))],
        out_specs=pl.BlockSpec((1, 1, nblk, HEAD_DIM), lambda b, sh: (b, sh, 0, 0)),
        compiler_params=_cparams(("arbitrary", "arbitrary")),
        name="nsa_compress_prompt",
    )(p2, pe, w1.reshape(2, CMP_BLOCK * HEAD_DIM, HEAD_DIM), b1.reshape(2, 1, HEAD_DIM), w2)


TQ = 128


def _nsa_prompt_body(slopes_ref, q_ref, gt_ref, ck_ref, cv_ref, sk_ref, sv_ref, wk_ref, wv_ref,
                     o_ref, selm_ref, *, seq_len):
    h = pl.program_id(1)
    qi = pl.program_id(2)
    t0 = qi * TQ
    nkt = seq_len // TQ
    nsb = seq_len // SEL_BLOCK
    ncb = seq_len // CMP_BLOCK // 2
    rows = NSA_GROUP * TQ

    qb = q_ref[...] * (HEAD_DIM ** -0.5)
    q4 = jnp.concatenate([qb[:, g * HEAD_DIM:(g + 1) * HEAD_DIM] for g in range(NSA_GROUP)], axis=0)
    q4b = q4.astype(BF16)
    row = lax.broadcasted_iota(I32, (rows, 1), 0)
    tq = t0 + (row & (TQ - 1))
    grow = row >> 7
    slope = jnp.zeros((rows, 1), F32)
    for g in range(NSA_GROUP):
        slope = jnp.where(grow == g, slopes_ref[h * NSA_GROUP + g], slope)

    ck = ck_ref[0, 0]
    cv = cv_ref[0, 0]
    midx = lax.broadcasted_iota(I32, (1, ncb), 1)
    logits = []
    valids = []
    for par in range(2):
        s_c = _mm3(q4, ck[par * ncb:(par + 1) * ncb], NT)
        cend = midx * (2 * CMP_BLOCK) + (CMP_BLOCK - 1) + par * CMP_BLOCK
        valid = cend <= tq
        dist = (tq - cend).astype(F32)
        logits.append(jnp.where(valid, s_c - slope * dist, NEG_INF))
        valids.append(valid)
    mx = jnp.maximum(jnp.max(logits[0], axis=1, keepdims=True), jnp.max(logits[1], axis=1, keepdims=True))
    e0 = jnp.exp(logits[0] - mx)
    e1 = jnp.exp(logits[1] - mx)
    den = jnp.sum(e0, axis=1, keepdims=True) + jnp.sum(e1, axis=1, keepdims=True)
    p0 = jnp.where(valids[0], e0 / den, 0.0)
    p1 = jnp.where(valids[1], e1 / den, 0.0)
    o_cmp = _mm1(p0, cv[0:ncb]) + _mm1(p1, cv[ncb:2 * ncb])
    pb = p0 + p1
    imp = pb[0:TQ]
    for g in range(1, NSA_GROUP):
        imp = imp + pb[g * TQ:(g + 1) * TQ]

    tq1 = t0 + lax.broadcasted_iota(I32, (TQ, 1), 0)
    cur = tq1 >> 6
    jb = lax.broadcasted_iota(I32, (1, nsb), 1)
    forced = (jb == 0) | (jb == cur) | (jb == cur - 1)
    score = jnp.where(jb <= cur, jnp.where(forced, FORCE_SCORE, imp), NEG_INF)
    cnt = jnp.zeros((TQ, nsb), I32)
    for i in range(nsb):
        si = score[:, i:i + 1]
        beats = (si > score) | ((si == score) & (jb > i))
        cnt = cnt + beats.astype(I32)
    sel = ((cnt < min(SEL_TOPN, nsb)) & (score > 0.5 * NEG_INF)).astype(BF16)
    kpos = lax.broadcasted_iota(I32, (nsb, seq_len), 1)
    kblk = lax.broadcasted_iota(I32, (nsb, seq_len), 0)
    expand = ((kpos >> 6) == kblk).astype(BF16)
    selk = _dot(sel, expand)
    for kt in range(nkt):
        blk = selk[:, kt * TQ:(kt + 1) * TQ]
        selm_ref[kt] = jnp.concatenate([blk] * NSA_GROUP, axis=0)

    lane = lax.broadcasted_iota(I32, (1, TQ), 1)

    def attend(k_ref, v_ref, lo, hi, mask_fn):
        def step(kt, carry):
            m, l, acc = carry
            start = pl.multiple_of(kt * TQ, TQ)
            k = k_ref[pl.ds(start, TQ), :].astype(BF16)
            v = v_ref[pl.ds(start, TQ), :].astype(BF16)
            s = _dot(q4b, k, NT)
            pos = kt * TQ + lane
            delta = tq - pos
            mask = mask_fn(kt, delta)
            lg = jnp.where(mask, s - slope * delta.astype(F32), NEG_INF)
            m_new = jnp.maximum(m, jnp.max(lg, axis=1, keepdims=True))
            alpha = jnp.exp(m - m_new)
            p = jnp.where(mask, jnp.exp(lg - m_new), 0.0)
            l = alpha * l + jnp.sum(p, axis=1, keepdims=True)
            acc = alpha * acc + _dot(p.astype(BF16), v)
            return m_new, l, acc

        init = (jnp.full((rows, 1), NEG_INF, F32), jnp.zeros((rows, 1), F32), jnp.zeros((rows, HEAD_DIM), F32))
        _, l, acc = lax.fori_loop(lo, hi, step, init)
        return jnp.where(l > 0.0, acc / jnp.where(l > 0.0, l, 1.0), 0.0)

    o_sel = attend(sk_ref, sv_ref, 0, qi + 1, lambda kt, delta: (selm_ref[kt] > 0.5) & (delta >= 0))
    o_win = attend(wk_ref, wv_ref, jnp.maximum(qi - WINDOW // TQ, 0), qi + 1,
                   lambda kt, delta: (delta >= 0) & (delta < WINDOW))

    gs = jax.nn.sigmoid(gt_ref[...])
    for g in range(NSA_GROUP):
        def gate(c):
            a = gs[:, g * 3 + c:g * 3 + c + 1]
            b = gs[:, NSA_GROUP * 3 + g * 3 + c:NSA_GROUP * 3 + g * 3 + c + 1]
            return jnp.where(h == 0, a, b)

        sl = slice(g * TQ, (g + 1) * TQ)
        o_ref[:, g * HEAD_DIM:(g + 1) * HEAD_DIM] = (gate(0) * o_cmp[sl] + gate(1) * o_sel[sl]) + gate(2) * o_win[sl]


def _nsa_prompt(p2, ckv, slopes, nbatch, seq_len):
    nq = seq_len // TQ
    kvs0 = COL_KVS // LANE
    kvw0 = COL_KVW // LANE
    misc = COL_MISC // LANE
    nblk = seq_len // CMP_BLOCK
    full = lambda c0: pl.BlockSpec((seq_len, LANE), lambda b, h, q, c0=c0: (b, c0 + h))
    return pl.pallas_call(
        functools.partial(_nsa_prompt_body, seq_len=seq_len),
        out_shape=jax.ShapeDtypeStruct((nbatch * seq_len, NSA_WIDTH), F32),
        grid=(nbatch, NSA_KV_HEADS, nq),
        in_specs=[pl.BlockSpec(memory_space=pltpu.SMEM),
                  pl.BlockSpec((TQ, NSA_GROUP * HEAD_DIM), lambda b, h, q: (b * nq + q, h)),
                  pl.BlockSpec((TQ, LANE), lambda b, h, q: (b * nq + q, misc)),
                  pl.BlockSpec((1, 1, nblk, HEAD_DIM), lambda b, h, q: (b, h, 0, 0)),
                  pl.BlockSpec((1, 1, nblk, HEAD_DIM), lambda b, h, q: (b, 2 + h, 0, 0)),
                  full(kvs0), full(kvs0 + 2), full(kvw0), full(kvw0 + 2)],
        out_specs=pl.BlockSpec((TQ, NSA_GROUP * HEAD_DIM), lambda b, h, q: (b * nq + q, h)),
        scratch_shapes=[pltpu.VMEM((nq, NSA_GROUP * TQ, TQ), F32)],
        compiler_params=_cparams(("arbitrary", "arbitrary", "arbitrary")),
        name="nsa_prompt",
    )(slopes, p2, p2, ckv, ckv, p2, p2, p2, p2)


GDN_CB = 8


def _bmm3(a, b, dims):
    ah, al = _split(a)
    bh, bl = _split(b)
    return _dot(ah, bh, dims) + (_dot(al, bh, dims) + _dot(ah, bl, dims))


def _gdn_body(alog_ref, dtb_ref, xq_ref, xk_ref, xv_ref, z_ref, gt_ref, cwq_ref, cwk_ref, cwv_ref,
              cbq_ref, cbk_ref, cbv_ref, s0_ref, gn_ref, o_ref, sout_ref,
              q_s, k_s, v_s, b_s, g_s, u_s, w_s, qg_s, kdt_s, qk_s, gl_s, oc_s, *, seq_len, valid):
    hd = pl.program_id(1)
    C = GDN_CHUNK
    nc = seq_len // C
    cb = min(GDN_CB, nc)
    row = lax.broadcasted_iota(I32, (seq_len, 1), 0)

    def conv(x_ref, cw_ref, cb_ref):
        x = x_ref[...]
        cat = jnp.concatenate([cb_ref[0], x], axis=0)
        w = cw_ref[...]
        acc = pltpu.roll(cat, 3, 0)[SUBLANE:] * w[0:1]
        acc = acc + pltpu.roll(cat, 2, 0)[SUBLANE:] * w[1:2]
        acc = acc + pltpu.roll(cat, 1, 0)[SUBLANE:] * w[2:3]
        acc = acc + x * w[3:4]
        return jax.nn.silu(acc)

    def l2n(x):
        return x * lax.rsqrt(jnp.sum(x * x, axis=-1, keepdims=True) + NORM_EPS)

    q = l2n(conv(xq_ref, cwq_ref, cbq_ref)) * (HEAD_DIM ** -0.5)
    k = l2n(conv(xk_ref, cwk_ref, cbk_ref))
    v = conv(xv_ref, cwv_ref, cbv_ref)
    gt = gt_ref[...]
    lane = lax.broadcasted_iota(I32, (1, LANE), 1)
    bl = jnp.sum(jnp.where(lane == MISC_BL + hd, gt, 0.0), axis=1, keepdims=True)
    al = jnp.sum(jnp.where(lane == MISC_AL + hd, gt, 0.0), axis=1, keepdims=True)
    beta = jax.nn.sigmoid(bl)
    a_pos = jnp.exp(jnp.full((1, 1), alog_ref[hd], F32))
    g = -a_pos * jax.nn.softplus(al + dtb_ref[hd])
    if valid < seq_len:
        vm = row < valid
        q = jnp.where(vm, q, 0.0)
        k = jnp.where(vm, k, 0.0)
        v = jnp.where(vm, v, 0.0)
        beta = jnp.where(vm, beta, 0.0)
        g = jnp.where(vm, g, 0.0)
    pos = row & (C - 1)
    gc = g
    sft = 1
    while sft < C:
        gc = gc + jnp.where(pos >= sft, pltpu.roll(gc, sft, 0), 0.0)
        sft *= 2
    q_s[...] = q
    k_s[...] = k
    v_s[...] = v
    b_s[...] = jnp.broadcast_to(beta, (seq_len, LANE))
    g_s[...] = jnp.broadcast_to(gc, (seq_len, LANE))

    ii = lax.broadcasted_iota(I32, (C, C), 0)
    jj = lax.broadcasted_iota(I32, (C, C), 1)
    tri = (ii >= jj)[None]
    strict = (ii > jj)[None]
    eye = (ii == jj).astype(F32)[None]
    ones_b = jnp.ones((cb, C, C), BF16)
    ident_b = jnp.broadcast_to((lax.broadcasted_iota(I32, (HEAD_DIM, HEAD_DIM), 0)
                                == lax.broadcasted_iota(I32, (HEAD_DIM, HEAD_DIM), 1)).astype(BF16)[None],
                               (cb, HEAD_DIM, HEAD_DIM))

    def local(gi, carry):
        r0 = pl.multiple_of(gi * (cb * C), cb * C)
        sl = pl.ds(r0, cb * C)
        qc = q_s[sl, :].reshape(cb, C, HEAD_DIM)
        kc = k_s[sl, :].reshape(cb, C, HEAD_DIM)
        vc = v_s[sl, :].reshape(cb, C, HEAD_DIM)
        bc = b_s[sl, :].reshape(cb, C, LANE)
        gcc = g_s[sl, :].reshape(cb, C, LANE)
        gcol = gcc[:, :, 0:C]
        dg = gcol * eye
        d1 = dg.astype(BF16)
        r1 = dg - d1.astype(F32)
        d2 = r1.astype(BF16)
        d3 = (r1 - d2.astype(F32)).astype(BF16)
        grow = _dot(ones_b, d1, BNN) + (_dot(ones_b, d2, BNN) + _dot(ones_b, d3, BNN))
        diff = gcol - grow
        decay = jnp.where(tri, jnp.exp(jnp.where(tri, diff, 0.0)), 0.0)
        kb = kc * bc
        a = _dot(kb.astype(BF16), kc.astype(BF16), BNT) * jnp.where(strict, decay, 0.0)
        tinv = eye - a
        pw = a
        n = 2
        while n < C:
            pw = _bmm3(pw, pw, BNN)
            tinv = tinv + _bmm3(tinv, pw, BNN)
            n *= 2
        eg = jnp.exp(gcc)
        rhs = jnp.concatenate([vc * bc, kb * eg], axis=2)
        sol = _bmm3(tinv, rhs, BNN)
        u_s[sl, :] = sol[:, :, 0:HEAD_DIM].reshape(cb * C, HEAD_DIM)
        w_s[sl, :] = sol[:, :, HEAD_DIM:2 * HEAD_DIM].reshape(cb * C, HEAD_DIM)
        qk = _dot(qc.astype(BF16), kc.astype(BF16), BNT) * decay
        qk_s[sl, :] = qk.reshape(cb * C, C)
        qg_s[sl, :] = (qc * eg).reshape(cb * C, HEAD_DIM)
        glast = gcc[:, C - 1:C, :]
        kd = kc * jnp.exp(glast - gcc)
        kdt = _dot(ident_b, kd.astype(BF16), BNT)
        kdt_s[pl.ds(pl.multiple_of(gi * (cb * HEAD_DIM), cb * HEAD_DIM), cb * HEAD_DIM), :] = (
            kdt.reshape(cb * HEAD_DIM, C))
        gl_s[pl.ds(pl.multiple_of(gi * (cb * SUBLANE), cb * SUBLANE), cb * SUBLANE), :] = (
            jnp.broadcast_to(jnp.exp(glast), (cb, SUBLANE, LANE)).reshape(cb * SUBLANE, LANE))
        return carry

    lax.fori_loop(0, nc // cb, local, 0)

    def scan(c, s):
        sl = pl.ds(pl.multiple_of(c * C, C), C)
        sb = s.astype(BF16)
        v_new = u_s[sl, :] - _dot(w_s[sl, :].astype(BF16), sb)
        o_c = _dot(qg_s[sl, :].astype(BF16), sb) + _dot(qk_s[sl, :].astype(BF16), v_new.astype(BF16))
        oc_s[sl, :] = o_c
        kdt = kdt_s[pl.ds(pl.multiple_of(c * HEAD_DIM, HEAD_DIM), HEAD_DIM), :]
        gl = gl_s[pl.ds(pl.multiple_of(c * SUBLANE, SUBLANE), 1), :]
        return s * gl + _dot(kdt.astype(BF16), v_new.astype(BF16))

    s_fin = lax.fori_loop(0, nc, scan, s0_ref[0, 0])
    sout_ref[0, 0] = s_fin
    o = oc_s[...]
    y = o * lax.rsqrt(jnp.mean(o * o, axis=-1, keepdims=True) + NORM_EPS) * gn_ref[...]
    o_ref[...] = y * jax.nn.silu(z_ref[...])


def _gdn(p2, conv_w, conv_buf8, s0, a_log, dt_bias, gn_w, nbatch, seq_len, valid):
    q0 = COL_QKV // LANE
    z0 = COL_Z // LANE
    misc = COL_MISC // LANE
    C = GDN_CHUNK
    col = lambda c0: pl.BlockSpec((seq_len, LANE), lambda b, h, c0=c0: (b, c0 + h))
    cw = lambda c0: pl.BlockSpec((CONV_WIDTH, LANE), lambda b, h, c0=c0: (0, c0 + h))
    cbs = lambda c0: pl.BlockSpec((1, SUBLANE, LANE), lambda b, h, c0=c0: (b, 0, c0 + h))
    smem = pl.BlockSpec(memory_space=pltpu.SMEM)
    nc = seq_len // C
    return pl.pallas_call(
        functools.partial(_gdn_body, seq_len=seq_len, valid=valid),
        out_shape=(jax.ShapeDtypeStruct((nbatch * seq_len, GDN_WIDTH), F32),
                   jax.ShapeDtypeStruct((nbatch, GDN_HEADS, HEAD_DIM, HEAD_DIM), F32)),
        grid=(nbatch, GDN_HEADS),
        in_specs=[smem, smem, col(q0), col(q0 + 8), col(q0 + 16), col(z0),
                  pl.BlockSpec((seq_len, LANE), lambda b, h: (b, misc)),
                  cw(0), cw(8), cw(16), cbs(0), cbs(8), cbs(16),
                  pl.BlockSpec((1, 1, HEAD_DIM, HEAD_DIM), lambda b, h: (b, h, 0, 0)),
                  pl.BlockSpec((1, HEAD_DIM), lambda b, h: (0, 0))],
        out_specs=(pl.BlockSpec((seq_len, LANE), lambda b, h: (b, h)),
                   pl.BlockSpec((1, 1, HEAD_DIM, HEAD_DIM), lambda b, h: (b, h, 0, 0))),
        scratch_shapes=[pltpu.VMEM((seq_len, HEAD_DIM), F32)] * 3
        + [pltpu.VMEM((seq_len, LANE), F32)] * 2
        + [pltpu.VMEM((seq_len, HEAD_DIM), F32)] * 3
        + [pltpu.VMEM((nc * HEAD_DIM, C), F32),
           pltpu.VMEM((seq_len, C), F32),
           pltpu.VMEM((nc * SUBLANE, LANE), F32),
           pltpu.VMEM((seq_len, HEAD_DIM), F32)],
        compiler_params=_cparams(("arbitrary", "arbitrary")),
        name="gdn",
    )(a_log, dt_bias, p2, p2, p2, p2, p2, conv_w, conv_w, conv_w, conv_buf8, conv_buf8, conv_buf8,
      s0, gn_w.reshape(1, HEAD_DIM))


def _out_proj_body(on_ref, og_ref, x_ref, w_ref, g1_ref, nw_ref, sc_ref, sh_ref, x1_ref, h2_ref):
    mix = _dot(on_ref[...].astype(BF16), w_ref[0:NSA_WIDTH, :]) + _dot(og_ref[...].astype(BF16), w_ref[NSA_WIDTH:, :])
    x1 = x_ref[...] + g1_ref[0] * mix
    x1_ref[...] = x1
    y = x1 * lax.rsqrt(jnp.mean(x1 * x1, axis=-1, keepdims=True) + NORM_EPS)
    h2_ref[...] = (y * nw_ref[...]) * (1.0 + sc_ref[0]) + sh_ref[0]


def _out_proj(o_nsa, o_gdn, x, w_out_bf, g1, norm_w, sc, sh, seq_len, tm):
    n, d = x.shape
    tpb = seq_len // tm
    mod = pl.BlockSpec((1, 1, d), lambda i: (i // tpb, 0, 0))
    return pl.pallas_call(
        _out_proj_body,
        out_shape=(jax.ShapeDtypeStruct((n, d), F32), jax.ShapeDtypeStruct((n, d), F32)),
        grid=(n // tm,),
        in_specs=[pl.BlockSpec((tm, NSA_WIDTH), lambda i: (i, 0)),
                  pl.BlockSpec((tm, GDN_WIDTH), lambda i: (i, 0)),
                  pl.BlockSpec((tm, d), lambda i: (i, 0)),
                  pl.BlockSpec((d, d), lambda i: (0, 0)),
                  mod,
                  pl.BlockSpec((1, d), lambda i: (0, 0)),
                  mod, mod],
        out_specs=(pl.BlockSpec((tm, d), lambda i: (i, 0)), pl.BlockSpec((tm, d), lambda i: (i, 0))),
        compiler_params=_cparams(("arbitrary",)),
        name="out_proj",
    )(o_nsa, o_gdn, x, w_out_bf, g1, norm_w.reshape(1, d), sc, sh)


def _take_top(src_ref, dst_ref, nrows):
    s = src_ref[...]
    ridx = lax.broadcasted_iota(I32, s.shape, 0)
    for k in range(PEER_TOPK):
        m = jnp.max(s, axis=0, keepdims=True)
        first = jnp.min(jnp.where(s == m, ridx, nrows), axis=0, keepdims=True)
        s = jnp.where(ridx == first, NEG_BIG, s)
        dst_ref[k:k + 1, :] = m


def _peer_sel_body(h_ref, wqh_ref, wql_ref, keys_ref, s0_ref, al_ref, s1_ref, be_ref, tau_ref,
                   sc_s, ta_s, tb_s, cand_s, tc_s):
    hh, hl = _split(h_ref[...])
    half = PEER_QDIM // 2
    tn = h_ref.shape[0]
    r8 = lax.broadcasted_iota(I32, (SUBLANE, tn), 0)
    for hd in range(PEER_HEADS):
        wh = wqh_ref[hd * PEER_QDIM:(hd + 1) * PEER_QDIM, :]
        wl = wql_ref[hd * PEER_QDIM:(hd + 1) * PEER_QDIM, :]
        qt = _dot(wh, hh, NT) + (_dot(wl, hh, NT) + _dot(wh, hl, NT))
        s0 = _mm3(keys_ref[0], qt[0:half])
        s1 = _mm3(keys_ref[1], qt[half:PEER_QDIM])
        sc_s[...] = s0
        _take_top(sc_s, ta_s, PEER_NKEYS)
        sc_s[...] = s1
        _take_top(sc_s, tb_s, PEER_NKEYS)
        a = ta_s[...]
        b = tb_s[...]
        cand_s[0:16, :] = a[0:1] + b
        cand_s[16:24, :] = a[1:2] + b[0:8]
        off = 24
        for r, n in ((2, 5), (3, 4), (4, 3), (5, 2), (6, 2), (7, 2)):
            cand_s[off:off + 8, :] = jnp.where(r8 < n, a[r:r + 1] + b[0:8], NEG_BIG)
            off += 8
        cand_s[off:off + 8, :] = a[8:16] + b[0:1]
        _take_top(cand_s, tc_s, off + 8)
        tc = tc_s[...]
        zsum = jnp.sum(jnp.exp(tc - tc[0:1]), axis=0, keepdims=True)
        s0_ref[hd] = s0
        al_ref[hd] = jnp.exp(s0 - a[0:1]) / zsum
        s1_ref[hd] = s1
        be_ref[hd] = jnp.exp(s1 - b[0:1])
        tau_ref[hd:hd + 1, :] = tc[PEER_TOPK - 1:PEER_TOPK]


def _peer_select(h2, wq_hi, wq_lo, keys, tn):
    n, d = h2.shape
    hk = jax.ShapeDtypeStruct((PEER_HEADS, PEER_NKEYS, n), F32)
    blk = pl.BlockSpec((PEER_HEADS, PEER_NKEYS, tn), lambda i: (0, 0, i))
    return pl.pallas_call(
        _peer_sel_body,
        out_shape=(hk, hk, hk, hk, jax.ShapeDtypeStruct((PEER_HEADS, n), F32)),
        grid=(n // tn,),
        in_specs=[pl.BlockSpec((tn, d), lambda i: (i, 0)),
                  pl.BlockSpec((PEER_HEADS * PEER_QDIM, d), lambda i: (0, 0)),
                  pl.BlockSpec((PEER_HEADS * PEER_QDIM, d), lambda i: (0, 0)),
                  pl.BlockSpec((2, PEER_NKEYS, PEER_QDIM // 2), lambda i: (0, 0, 0))],
        out_specs=(blk, blk, blk, blk, pl.BlockSpec((PEER_HEADS, tn), lambda i: (0, i))),
        scratch_shapes=[pltpu.VMEM((PEER_NKEYS, tn), F32), pltpu.VMEM((PEER_TOPK, tn), F32),
                        pltpu.VMEM((PEER_TOPK, tn), F32), pltpu.VMEM((80, tn), F32),
                        pltpu.VMEM((PEER_TOPK, tn), F32)],
        compiler_params=_cparams(("arbitrary",)),
        name="peer_select",
    )(h2, wq_hi, wq_lo, keys)


PEER_TE = 1024


def _peer_mm_body(h_ref, u_ref, vt_ref, s0_ref, al_ref, s1_ref, be_ref, tau_ref, o_ref, p_s):
    e = pl.program_id(1)

    @pl.when(e == 0)
    def _():
        o_ref[...] = jnp.zeros_like(o_ref)

    hb = h_ref[...]
    for ii in range(PEER_TE // PEER_NKEYS):
        rs = slice(ii * PEER_NKEYS, (ii + 1) * PEER_NKEYS)
        act = jax.nn.gelu(_dot(u_ref[rs, :], hb, NT))
        w = jnp.zeros_like(act)
        for hd in range(PEER_HEADS):
            t = s0_ref[hd, ii:ii + 1, :] + s1_ref[hd]
            sel = t >= tau_ref[hd:hd + 1, :]
            w = w + jnp.where(sel, al_ref[hd, ii:ii + 1, :] * be_ref[hd], 0.0)
        p_s[rs, :] = (w * act).astype(BF16)
    o_ref[...] += _dot(vt_ref[...], p_s[...])


def _peer_mm(h2b, u_bf, vt_bf, s0, al, s1, be, tau, tn):
    n, d = h2b.shape
    ne = u_bf.shape[0]
    rows_i = PEER_TE // PEER_NKEYS
    sel_i = pl.BlockSpec((PEER_HEADS, rows_i, tn), lambda i, e: (0, e, i))
    sel_j = pl.BlockSpec((PEER_HEADS, PEER_NKEYS, tn), lambda i, e: (0, 0, i))
    return pl.pallas_call(
        _peer_mm_body,
        out_shape=jax.ShapeDtypeStruct((d, n), F32),
        grid=(n // tn, ne // PEER_TE),
        in_specs=[pl.BlockSpec((tn, d), lambda i, e: (i, 0)),
                  pl.BlockSpec((PEER_TE, d), lambda i, e: (e, 0)),
                  pl.BlockSpec((d, PEER_TE), lambda i, e: (0, e)),
                  sel_i, sel_i, sel_j, sel_j,
                  pl.BlockSpec((PEER_HEADS, tn), lambda i, e: (0, i))],
        out_specs=pl.BlockSpec((d, tn), lambda i, e: (0, i)),
        scratch_shapes=[pltpu.VMEM((PEER_TE, tn), BF16)],
        compiler_params=_cparams(("arbitrary", "arbitrary")),
        name="peer_experts",
    )(h2b, u_bf, vt_bf, s0, al, s1, be, tau)


def _final_body(x1_ref, pt_ref, g2_ref, fw_ref, y_ref, *, seq_len):
    nseq = g2_ref.shape[0]
    if nseq == 1:
        g2 = g2_ref[0]
    else:
        g2 = jnp.concatenate([jnp.broadcast_to(g2_ref[r], (seq_len, g2_ref.shape[2])) for r in range(nseq)], axis=0)
    x2 = x1_ref[...] + g2 * pt_ref[...].T
    y_ref[...] = x2 * lax.rsqrt(jnp.mean(x2 * x2, axis=-1, keepdims=True) + NORM_EPS) * fw_ref[...]


def _final(x1, peer_t, g2, fw, seq_len, tm):
    n, d = x1.shape
    if tm <= seq_len:
        tpb = seq_len // tm
        g2_spec = pl.BlockSpec((1, 1, d), lambda i: (i // tpb, 0, 0))
    else:
        g2_spec = pl.BlockSpec((tm // seq_len, 1, d), lambda i: (i, 0, 0))
    return pl.pallas_call(
        functools.partial(_final_body, seq_len=seq_len),
        out_shape=jax.ShapeDtypeStruct((n, d), F32),
        grid=(n // tm,),
        in_specs=[pl.BlockSpec((tm, d), lambda i: (i, 0)),
                  pl.BlockSpec((d, tm), lambda i: (0, i)),
                  g2_spec,
                  pl.BlockSpec((1, d), lambda i: (0, 0))],
        out_specs=pl.BlockSpec((tm, d), lambda i: (i, 0)),
        compiler_params=_cparams(("arbitrary",)),
        name="final_norm",
    )(x1, peer_t, g2, fw.reshape(1, d))


SROWS = 64
CMP_PAGES = 16
PAGE_ROWS = 8
PAGE_COLS = PAGE_SIZE * 2 * KV_WIDTH // PAGE_ROWS


def _compress_sample_body(pt_ref, *refs):
    pages = refs[:CMP_PAGES]
    pe_ref, w1_ref, b1_ref, w2_ref, o_ref, a_s, b_s = refs[CMP_PAGES:]
    x = jnp.concatenate([r[0] for r in pages], axis=0)
    x = x + jnp.concatenate([pe_ref[...]] * CMP_PAGES, axis=0)
    nl = CMP_BLOCK // 2
    row_w = 2 * KV_WIDTH
    nh = CMP_PAGES * PAGE_ROWS // 4
    for sh in range(4):
        s = sh // 2
        xc = jnp.concatenate([x[:, l * row_w + sh * HEAD_DIM:l * row_w + (sh + 1) * HEAD_DIM] for l in range(nl)],
                             axis=1)
        a_s[...] = _mm3(xc, w1_ref[s, 0:nl * HEAD_DIM, :])
        b_s[...] = _mm3(xc, w1_ref[s, nl * HEAD_DIM:2 * nl * HEAD_DIM, :])
        hid_e = a_s[pl.ds(0, nh, stride=4), :] + b_s[pl.ds(1, nh, stride=4), :]
        hid_o = a_s[pl.ds(2, nh, stride=4), :] + b_s[pl.ds(3, nh, stride=4), :]
        hid = jax.nn.gelu(jnp.concatenate([hid_e, hid_o], axis=0) + b1_ref[s])
        out = _mm3(hid, w2_ref[s])
        o_ref[0, sh, 0] = out[0:nh]
        o_ref[0, sh, 1] = out[nh:2 * nh]


def _compress_sample(cache, page_table, pe, w1, b1, w2):
    nbs, n_pages = page_table.shape
    n_phys = cache.shape[0]
    cache_v = cache.reshape(n_phys, PAGE_ROWS, PAGE_COLS)
    ngrp = n_pages // CMP_PAGES
    nblk_half = n_pages * PAGE_SIZE // CMP_BLOCK // 2
    nh = CMP_PAGES * PAGE_ROWS // 4
    pe_lshd = jnp.broadcast_to(jnp.transpose(pe, (1, 0, 2))[:, :, None, :], (CMP_BLOCK, 2, NSA_KV_HEADS, HEAD_DIM))
    pe2 = pe_lshd.reshape(2, PAGE_COLS)
    pe8 = jnp.concatenate([pe2] * (PAGE_ROWS // 2), axis=0)
    page_spec = lambda k: pl.BlockSpec((1, PAGE_ROWS, PAGE_COLS),
                                       lambda b, g, pt, k=k: (pt[b * n_pages + g * CMP_PAGES + k], 0, 0))
    const = lambda shape: pl.BlockSpec(shape, lambda b, g, pt: (0,) * len(shape))
    grid_spec = pltpu.PrefetchScalarGridSpec(
        num_scalar_prefetch=1,
        grid=(nbs, ngrp),
        in_specs=[page_spec(k) for k in range(CMP_PAGES)]
        + [const((PAGE_ROWS, PAGE_COLS)), const((2, CMP_BLOCK * HEAD_DIM, HEAD_DIM)), const((2, 1, HEAD_DIM)),
           const((2, HEAD_DIM, HEAD_DIM))],
        out_specs=pl.BlockSpec((1, 4, 2, nh, HEAD_DIM), lambda b, g, pt: (b, 0, 0, g, 0)),
        scratch_shapes=[pltpu.VMEM((CMP_PAGES * PAGE_ROWS, HEAD_DIM), F32)] * 2)
    return pl.pallas_call(
        _compress_sample_body,
        out_shape=jax.ShapeDtypeStruct((nbs, 4, 2, nblk_half, HEAD_DIM), F32),
        grid_spec=grid_spec,
        compiler_params=_cparams(("arbitrary", "arbitrary")),
        name="nsa_compress_sample",
    )(page_table.reshape(-1), *([cache_v] * CMP_PAGES), pe8, w1.reshape(2, CMP_BLOCK * HEAD_DIM, HEAD_DIM),
      b1.reshape(2, 1, HEAD_DIM), w2)


def _rows_from_lanes(row, ngrp):
    ridx = lax.broadcasted_iota(I32, (SUBLANE, HEAD_DIM), 0)
    out = jnp.zeros((SUBLANE, HEAD_DIM), F32)
    for g in range(ngrp):
        out = jnp.where(ridx == g, jnp.broadcast_to(row[:, g * HEAD_DIM:(g + 1) * HEAD_DIM], (SUBLANE, HEAD_DIM)), out)
    return out


def _slope_rows(slopes_ref, h):
    ridx = lax.broadcasted_iota(I32, (SUBLANE, 1), 0)
    slope = jnp.zeros((SUBLANE, 1), F32)
    for g in range(NSA_GROUP):
        slope = jnp.where(ridx == g, slopes_ref[h * NSA_GROUP + g], slope)
    return slope


def _nsa_sample_select_body(slopes_ref, q_ref, ckv_ref, ocmp_ref, idx_ref, *, past):
    ncb = past // CMP_BLOCK // 2
    nblk = past // SEL_BLOCK
    ridx = lax.broadcasted_iota(I32, (SUBLANE, 1), 0)
    midx = lax.broadcasted_iota(I32, (1, ncb), 1)
    jb = lax.broadcasted_iota(I32, (1, nblk), 1)
    slot = lax.broadcasted_iota(I32, (1, SEL_TOPN), 1)
    for h in range(NSA_KV_HEADS):
        q8 = _rows_from_lanes(q_ref[0:1, h * NSA_GROUP * HEAD_DIM:(h + 1) * NSA_GROUP * HEAD_DIM], NSA_GROUP)
        q8 = q8 * (HEAD_DIM ** -0.5)
        slope = _slope_rows(slopes_ref, h)
        lg = []
        for par in range(2):
            s_c = _mm3(q8, ckv_ref[0, h, par], NT)
            cend = midx * (2 * CMP_BLOCK) + (CMP_BLOCK - 1) + par * CMP_BLOCK
            lg.append(s_c - slope * (past - cend).astype(F32))
        mx = jnp.maximum(jnp.max(lg[0], axis=1, keepdims=True), jnp.max(lg[1], axis=1, keepdims=True))
        e0 = jnp.exp(lg[0] - mx)
        e1 = jnp.exp(lg[1] - mx)
        den = jnp.sum(e0, axis=1, keepdims=True) + jnp.sum(e1, axis=1, keepdims=True)
        p0 = e0 / den
        p1 = e1 / den
        o_cmp = _mm1(p0, ckv_ref[0, 2 + h, 0]) + _mm1(p1, ckv_ref[0, 2 + h, 1])
        ocmp_ref[0, h * NSA_GROUP:(h + 1) * NSA_GROUP, :] = o_cmp[0:NSA_GROUP]
        imp = jnp.sum(jnp.where(ridx < NSA_GROUP, p0 + p1, 0.0), axis=0, keepdims=True)
        forced = (jb == 0) | (jb == nblk - 1)
        score = jnp.where(forced, FORCE_SCORE, imp)
        picks = jnp.full((1, SEL_TOPN), nblk, I32)
        for k in range(SEL_TOPN - 1):
            m = jnp.max(score, axis=1, keepdims=True)
            first = jnp.min(jnp.where(score == m, jb, nblk), axis=1, keepdims=True)
            score = jnp.where(jb == first, NEG_BIG, score)
            picks = jnp.where(slot == k, first, picks)
        idx_ref[0, h:h + 1, :] = picks


def _nsa_sample_select(p2s, ckv_s, slopes, nbs, past):
    nblk_half = past // CMP_BLOCK // 2
    return pl.pallas_call(
        functools.partial(_nsa_sample_select_body, past=past),
        out_shape=(jax.ShapeDtypeStruct((nbs, NSA_HEADS, HEAD_DIM), F32),
                   jax.ShapeDtypeStruct((nbs, NSA_KV_HEADS, SEL_TOPN), I32)),
        grid=(nbs,),
        in_specs=[pl.BlockSpec(memory_space=pltpu.SMEM),
                  pl.BlockSpec((SUBLANE, NSA_WIDTH), lambda b: (b * (SROWS // SUBLANE), 0)),
                  pl.BlockSpec((1, 4, 2, nblk_half, HEAD_DIM), lambda b: (b, 0, 0, 0, 0))],
        out_specs=(pl.BlockSpec((1, NSA_HEADS, HEAD_DIM), lambda b: (b, 0, 0)),
                   pl.BlockSpec((1, NSA_KV_HEADS, SEL_TOPN), lambda b: (b, 0, 0))),
        compiler_params=_cparams(("arbitrary",)),
        name="nsa_sample_select",
    )(slopes, p2s, ckv_s)


def _nsa_sample_attend_body(idx_ref, pt_ref, slopes_ref, q_ref, gt_ref, ocmp_ref, skn_ref, svn_ref, wkn_ref, wvn_ref,
                            sk_ref, sv_ref, wk_ref, wv_ref, o_ref, m_s, l_s, acc_s, *, past, n_pages):
    b = pl.program_id(0)
    h = pl.program_id(1)
    j = pl.program_id(2)
    nsel = SEL_TOPN - 1
    q8 = _rows_from_lanes(q_ref[0:1, :], NSA_GROUP) * (HEAD_DIM ** -0.5)
    q8b = q8.astype(BF16)
    slope = _slope_rows(slopes_ref, h)

    @pl.when(j == 0)
    def _():
        s_new = jnp.sum(q8 * skn_ref[0:1, :], axis=1, keepdims=True)
        m_s[...] = s_new
        l_s[...] = jnp.ones_like(s_new)
        acc_s[...] = jnp.broadcast_to(svn_ref[0:1, :], (SUBLANE, HEAD_DIM))

    blk = idx_ref[(b * NSA_KV_HEADS + h) * SEL_TOPN + j]
    lane = lax.broadcasted_iota(I32, (1, SEL_BLOCK), 1)
    s = _dot(q8b, sk_ref[0].astype(BF16), NT)
    dist = (past - (blk * SEL_BLOCK + lane)).astype(F32)
    lg = s - slope * dist
    m_new = jnp.maximum(m_s[...], jnp.max(lg, axis=1, keepdims=True))
    alpha = jnp.exp(m_s[...] - m_new)
    p = jnp.exp(lg - m_new)
    l_s[...] = alpha * l_s[...] + jnp.sum(p, axis=1, keepdims=True)
    acc_s[...] = alpha * acc_s[...] + _dot(p.astype(BF16), sv_ref[0].astype(BF16))
    m_s[...] = m_new

    @pl.when(j == nsel - 1)
    def _():
        o_sel = acc_s[...] / l_s[...]
        nw = wk_ref.shape[1]
        r = lax.broadcasted_iota(I32, (1, nw), 1)
        delta = nw - r
        wmask = (delta < WINDOW) & (past - delta >= 0)
        sw = _dot(q8b, wk_ref[0].astype(BF16), NT)
        lgw = jnp.where(wmask, sw - slope * delta.astype(F32), NEG_INF)
        s_nw = jnp.sum(q8 * wkn_ref[0:1, :], axis=1, keepdims=True)
        mw = jnp.maximum(jnp.max(lgw, axis=1, keepdims=True), s_nw)
        pw = jnp.where(wmask, jnp.exp(lgw - mw), 0.0)
        pn = jnp.exp(s_nw - mw)
        den = jnp.sum(pw, axis=1, keepdims=True) + pn
        o_win = (_dot(pw.astype(BF16), wv_ref[0].astype(BF16))
                 + pn * wvn_ref[0:1, :]) / den
        gs = jax.nn.sigmoid(gt_ref[0:1, :])
        o_ref[...] = jnp.zeros_like(o_ref)
        for g in range(NSA_GROUP):
            def gate(c):
                a = gs[:, g * 3 + c:g * 3 + c + 1]
                bb = gs[:, NSA_GROUP * 3 + g * 3 + c:NSA_GROUP * 3 + g * 3 + c + 1]
                return jnp.where(h == 0, a, bb)

            oc = jnp.where(h == 0, ocmp_ref[0, g:g + 1, :], ocmp_ref[0, NSA_GROUP + g:NSA_GROUP + g + 1, :])
            og = (gate(0) * oc + gate(1) * o_sel[g:g + 1]) + gate(2) * o_win[g:g + 1]
            o_ref[0:1, g * HEAD_DIM:(g + 1) * HEAD_DIM] = og


def _nsa_sample_attend(p2s, ocmp, idx, page_table, cache_sel, win_state, slopes, nbs, past):
    n_pages = page_table.shape[1]
    n_phys = cache_sel.shape[0]
    nw = win_state.shape[1]
    halves = PAGE_SIZE // SEL_BLOCK
    sel_v = cache_sel.reshape(n_phys * halves, SEL_BLOCK, 2 * KV_WIDTH)
    win_v = win_state.reshape(nbs, nw, 2 * KV_WIDTH)
    rb = SROWS // SUBLANE
    kvs0 = COL_KVS // LANE
    kvw0 = COL_KVW // LANE
    misc = COL_MISC // LANE

    def sel_map(c0):
        def index(b, h, j, idx, pt):
            blk = idx[(b * NSA_KV_HEADS + h) * SEL_TOPN + j]
            page = pt[b * n_pages + blk // halves]
            return (page * halves + blk % halves, 0, c0 + h)
        return pl.BlockSpec((1, SEL_BLOCK, LANE), index)

    new = lambda c0: pl.BlockSpec((SUBLANE, LANE), lambda b, h, j, idx, pt, c0=c0: (b * rb, c0 + h))
    winb = lambda c0: pl.BlockSpec((1, nw, LANE), lambda b, h, j, idx, pt, c0=c0: (b, 0, c0 + h))
    grid_spec = pltpu.PrefetchScalarGridSpec(
        num_scalar_prefetch=2,
        grid=(nbs, NSA_KV_HEADS, SEL_TOPN - 1),
        in_specs=[pl.BlockSpec(memory_space=pltpu.SMEM),
                  pl.BlockSpec((SUBLANE, NSA_GROUP * HEAD_DIM), lambda b, h, j, idx, pt: (b * rb, h)),
                  pl.BlockSpec((SUBLANE, LANE), lambda b, h, j, idx, pt: (b * rb, misc)),
                  pl.BlockSpec((1, NSA_HEADS, HEAD_DIM), lambda b, h, j, idx, pt: (b, 0, 0)),
                  new(kvs0), new(kvs0 + 2), new(kvw0), new(kvw0 + 2),
                  sel_map(0), sel_map(2), winb(0), winb(2)],
        out_specs=pl.BlockSpec((SROWS, NSA_GROUP * HEAD_DIM), lambda b, h, j, idx, pt: (b, h)),
        scratch_shapes=[pltpu.VMEM((SUBLANE, 1), F32), pltpu.VMEM((SUBLANE, 1), F32),
                        pltpu.VMEM((SUBLANE, HEAD_DIM), F32)])
    return pl.pallas_call(
        functools.partial(_nsa_sample_attend_body, past=past, n_pages=n_pages),
        out_shape=jax.ShapeDtypeStruct((nbs * SROWS, NSA_WIDTH), F32),
        grid_spec=grid_spec,
        compiler_params=_cparams(("arbitrary", "arbitrary", "arbitrary")),
        name="nsa_sample_attend",
    )(idx.reshape(-1), page_table.reshape(-1), slopes, p2s, p2s, ocmp, p2s, p2s, p2s, p2s,
      sel_v, sel_v, win_v, win_v)


def _perm_w_in(w_in):
    d = w_in.shape[0]
    g0 = NSA_WIDTH + 3 * 2 * KV_WIDTH
    q0 = g0 + 3 * NSA_HEADS
    b0 = q0 + 3 * GDN_WIDTH + GDN_WIDTH
    pad = jnp.zeros((d, N_PROJ - (b0 + 2 * GDN_HEADS)), w_in.dtype)
    w2 = jnp.concatenate([w_in[:, :g0], w_in[:, q0:b0], w_in[:, g0:q0], w_in[:, b0:], pad], axis=1)
    return w2.astype(BF16)


def _split_bf16(w):
    hi = w.astype(BF16)
    lo = (w - hi.astype(F32)).astype(BF16)
    return hi, lo


def _mixer_tail(x, o_nsa, o_gdn, w_out_bf, g1, norm_ffn, sc2, sh2, g2, wq_hi, wq_lo, keys, u_bf, vt_bf,
                final_norm, seq_len, tm, tn_sel, tn_mm, tm_final):
    x1, h2 = _out_proj(o_nsa, o_gdn, x, w_out_bf, g1, norm_ffn, sc2, sh2, seq_len, tm)
    s0, al, s1, be, tau = _peer_select(h2, wq_hi, wq_lo, keys, tn_sel)
    peer_t = _peer_mm(h2.astype(BF16), u_bf, vt_bf, s0, al, s1, be, tau, tn_mm)
    return _final(x1, peer_t, g2, final_norm, seq_len, tm_final)


def kernel(x_prompt, x_sample, cache_cmp_kv, cache_sel_kv, state_win_kv, state_conv, state_gdn, page_table,
           c_prompt, c_sample, w_ada, b_ada, norm_mix, norm_ffn, w_in, cmp_pe, cmp_w1, cmp_b1, cmp_w2, conv_w,
           gdn_a_log, gdn_dt_bias, gdn_norm, w_out, peer_wq, peer_keys, peer_u, peer_v, final_norm):
    nb, seq, d = x_prompt.shape
    nbs, dec_seq, _ = x_sample.shape
    assert w_in.shape[0] == 1 and dec_seq == 1, "single layer, single decode token"
    past = page_table.shape[1] * PAGE_SIZE
    slopes = 2.0 ** (-8.0 * jnp.arange(1, NSA_HEADS + 1, dtype=F32) / NSA_HEADS)
    rows_c = 16
    c_all = jnp.concatenate([c_prompt, c_sample, jnp.zeros((rows_c - nb - nbs, d), F32)], axis=0)
    mod = _adaln(c_all, w_ada[0], b_ada[0]).reshape(rows_c, 6, d)
    pm = lambda k: mod[0:nb, k][:, None, :]
    sm = lambda k: mod[nb:nb + nbs, k][:, None, :]
    w2 = _perm_w_in(w_in[0])
    w_out_bf = w_out[0].astype(BF16)
    wq_hi, wq_lo = _split_bf16(peer_wq[0].T)
    u_bf = peer_u[0].astype(BF16)
    vt_bf = peer_v[0].T.astype(BF16)
    cw = (cmp_pe[0], cmp_w1[0], cmp_b1[0], cmp_w2[0])
    kvw = 2 * KV_WIDTH

    xp = x_prompt.reshape(nb * seq, d)
    p2 = _proj_in(xp, norm_mix[0], pm(1), pm(0), w2, seq, 1024)
    ckv = _compress_prompt(p2, *cw, nb, seq)
    o_nsa = _nsa_prompt(p2, ckv, slopes, nb, seq)
    conv0 = jnp.zeros((nb, SUBLANE, 3 * GDN_WIDTH), F32)
    s0 = jnp.zeros((nb, GDN_HEADS, HEAD_DIM, HEAD_DIM), F32)
    o_gdn, gdn_p = _gdn(p2, conv_w[0], conv0, s0, gdn_a_log[0], gdn_dt_bias[0], gdn_norm[0], nb, seq, seq)
    y_prompt = _mixer_tail(xp, o_nsa, o_gdn, w_out_bf, pm(2), norm_ffn[0], pm(4), pm(3), pm(5), wq_hi, wq_lo,
                           peer_keys[0], u_bf, vt_bf, final_norm, seq, 512, 256, 512, 512).reshape(nb, seq, d)
    p3 = p2.reshape(nb, seq, N_PROJ)
    keep = min(WINDOW, seq)
    kv5 = lambda a: a.reshape(a.shape[0], a.shape[1], 2, NSA_KV_HEADS, HEAD_DIM)
    cmp_p = kv5(p3[:, :, COL_KVC:COL_KVC + kvw])
    sel_p = kv5(p3[:, :, COL_KVS:COL_KVS + kvw])
    win_p = kv5(p3[:, seq - keep:, COL_KVW:COL_KVW + kvw])
    conv_p = p3[:, seq - (CONV_WIDTH - 1):, COL_QKV:COL_QKV + 3 * GDN_WIDTH]

    xs = jnp.pad(x_sample, ((0, 0), (0, SROWS - dec_seq), (0, 0))).reshape(nbs * SROWS, d)
    p2s = _proj_in(xs, norm_mix[0], sm(1), sm(0), w2, SROWS, SROWS)
    ckv_s = _compress_sample(cache_cmp_kv[0], page_table, *cw)
    ocmp, idx = _nsa_sample_select(p2s, ckv_s, slopes, nbs, past)
    o_nsa_s = _nsa_sample_attend(p2s, ocmp, idx, page_table, cache_sel_kv[0], state_win_kv[0], slopes, nbs, past)
    conv_buf = jnp.pad(state_conv[0], ((0, 0), (SUBLANE - (CONV_WIDTH - 1), 0), (0, 0)))
    o_gdn_s, gdn_s = _gdn(p2s, conv_w[0], conv_buf, state_gdn[0], gdn_a_log[0], gdn_dt_bias[0], gdn_norm[0],
                          nbs, SROWS, dec_seq)
    ys = _mixer_tail(xs, o_nsa_s, o_gdn_s, w_out_bf, sm(2), norm_ffn[0], sm(4), sm(3), sm(5), wq_hi, wq_lo,
                     peer_keys[0], u_bf, vt_bf, final_norm, SROWS, SROWS, 256, 512, 2 * SROWS)
    y_sample = ys.reshape(nbs, SROWS, d)[:, 0:dec_seq]
    p3s = p2s.reshape(nbs, SROWS, N_PROJ)[:, 0:dec_seq]
    cmp_s = kv5(p3s[:, :, COL_KVC:COL_KVC + kvw])
    sel_s = kv5(p3s[:, :, COL_KVS:COL_KVS + kvw])
    win_new = kv5(p3s[:, :, COL_KVW:COL_KVW + kvw])
    win_s = jnp.concatenate([state_win_kv[0], win_new], axis=1)[:, dec_seq:]
    conv_s = jnp.concatenate([state_conv[0], p3s[:, :, COL_QKV:COL_QKV + 3 * GDN_WIDTH]], axis=1)[:, dec_seq:]

    return (y_prompt, y_sample, cmp_p[None], cmp_s[None], sel_p[None], sel_s[None], win_p[None], win_s[None],
            conv_p[None], conv_s[None], gdn_p[None], gdn_s[None])
```

```python
import functools
import math

import jax
import jax.numpy as jnp
from jax import lax
from jax.experimental import pallas as pl
from jax.experimental.pallas import tpu as pltpu

F32 = jnp.float32
BF16 = jnp.bfloat16
I32 = jnp.int32

D_MODEL = 2048
HEAD_DIM = 128
NSA_WIDTH = 1024
GDN_WIDTH = 1024
NSA_HEADS = 8
NSA_KV_HEADS = 2
NSA_GROUP = 4
KV_WIDTH = 256
CMP_BLOCK = 32
SEL_BLOCK = 64
SEL_TOPN = 16
WINDOW = 512
GDN_HEADS = 8
CONV_WIDTH = 4
GDN_CHUNK = 64
PEER_HEADS = 8
PEER_NKEYS = 128
PEER_QDIM = 256
PEER_TOPK = 16
PAGE_SIZE = 128
NORM_EPS = 1e-6
NEG_INF = -1e30
FORCE_SCORE = 1e4
NEG_BIG = -3.0e38

LANE = 128
SUBLANE = 8
VMEM_LIMIT = 56 * 1024 * 1024

COL_Q = 0
COL_KVC = 1024
COL_KVS = 1536
COL_KVW = 2048
COL_QKV = 2560
COL_Z = 5632
COL_MISC = 6656
N_PROJ = 6912
MISC_BL = 24
MISC_AL = 32

NN = (((1,), (0,)), ((), ()))
NT = (((1,), (1,)), ((), ()))
BNN = (((2,), (1,)), ((0,), (0,)))
BNT = (((2,), (2,)), ((0,), (0,)))


def _dot(a, b, dims=NN):
    return lax.dot_general(a, b, dims, preferred_element_type=F32)


def _split(x):
    hi = x.astype(BF16)
    lo = (x - hi.astype(F32)).astype(BF16)
    return hi, lo


def _mm1(a, b, dims=NN):
    return _dot(a.astype(BF16), b.astype(BF16), dims)


def _mm3(a, b, dims=NN):
    ah, al = _split(a)
    bh, bl = _split(b)
    return _dot(ah, bh, dims) + (_dot(al, bh, dims) + _dot(ah, bl, dims))


def _cparams(sem, vmem=VMEM_LIMIT):
    return pltpu.CompilerParams(dimension_semantics=sem, vmem_limit_bytes=vmem)


def _adaln_body(c_ref, w_ref, b_ref, o_ref):
    a = jax.nn.silu(c_ref[...])
    o_ref[...] = _mm3(a, w_ref[...]) + b_ref[...]


def _adaln(c_all, w, b):
    rows, d = c_all.shape
    n = w.shape[1]
    tn = 1024
    return pl.pallas_call(
        _adaln_body,
        out_shape=jax.ShapeDtypeStruct((rows, n), F32),
        grid=(n // tn,),
        in_specs=[pl.BlockSpec((rows, d), lambda j: (0, 0)),
                  pl.BlockSpec((d, tn), lambda j: (0, j)),
                  pl.BlockSpec((1, tn), lambda j: (0, j))],
        out_specs=pl.BlockSpec((rows, tn), lambda j: (0, j)),
        compiler_params=_cparams(("arbitrary",)),
        name="adaln",
    )(c_all, w, b.reshape(1, n))


def _seq_rows(m_ref, seq_len):
    nseq = m_ref.shape[0]
    if nseq == 1:
        return m_ref[0]
    return jnp.concatenate([jnp.broadcast_to(m_ref[r], (seq_len, m_ref.shape[2])) for r in range(nseq)], axis=0)


def _seq_spec(seq_len, tm, d):
    if tm <= seq_len:
        tpb = seq_len // tm
        return pl.BlockSpec((1, 1, d), lambda i, *_: (i // tpb, 0, 0))
    return pl.BlockSpec((tm // seq_len, 1, d), lambda i, *_: (i, 0, 0))


def _proj_in_body(x_ref, nw_ref, sc_ref, sh_ref, w_ref, o_ref, h_ref, *, seq_len):
    @pl.when(pl.program_id(1) == 0)
    def _():
        x = x_ref[...]
        y = x * lax.rsqrt(jnp.mean(x * x, axis=-1, keepdims=True) + NORM_EPS)
        h = (y * nw_ref[...]) * (1.0 + _seq_rows(sc_ref, seq_len)) + _seq_rows(sh_ref, seq_len)
        h_ref[...] = h.astype(BF16)

    o_ref[...] = _dot(h_ref[...], w_ref[...])


def _proj_in(x, norm_w, sc, sh, w2, seq_len, tm):
    n, d = x.shape
    ncol = w2.shape[1]
    tn = 768
    mod = _seq_spec(seq_len, tm, d)
    return pl.pallas_call(
        functools.partial(_proj_in_body, seq_len=seq_len),
        out_shape=jax.ShapeDtypeStruct((n, ncol), F32),
        grid=(n // tm, ncol // tn),
        in_specs=[pl.BlockSpec((tm, d), lambda i, j: (i, 0)),
                  pl.BlockSpec((1, d), lambda i, j: (0, 0)),
                  mod, mod,
                  pl.BlockSpec((d, tn), lambda i, j: (0, j))],
        out_specs=pl.BlockSpec((tm, tn), lambda i, j: (i, j)),
        scratch_shapes=[pltpu.VMEM((tm, d), BF16)],
        compiler_params=_cparams(("arbitrary", "arbitrary")),
        name="proj_in",
    )(x, norm_w.reshape(1, d), sc, sh, w2)


def _compress_prompt_body(x_ref, pe_ref, w1_ref, b1_ref, w2_ref, o_ref, *, nblk):
    half = nblk // 2
    acc = jnp.zeros((nblk, HEAD_DIM), F32)
    for l in range(CMP_BLOCK):
        xe = x_ref[pl.ds(l, half, stride=2 * CMP_BLOCK), :]
        xo = x_ref[pl.ds(CMP_BLOCK + l, half, stride=2 * CMP_BLOCK), :]
        xl = jnp.concatenate([xe, xo], axis=0) + pe_ref[0, l:l + 1, :]
        acc = acc + _mm1(xl, w1_ref[0, l * HEAD_DIM:(l + 1) * HEAD_DIM, :])
    hid = jax.nn.gelu(acc + b1_ref[0])
    o_ref[0, 0] = _mm1(hid, w2_ref[0])


def _compress_prompt(p2, pe, w1, b1, w2, nbatch, seq_len):
    nblk = seq_len // CMP_BLOCK
    kv0 = COL_KVC // LANE
    return pl.pallas_call(
        functools.partial(_compress_prompt_body, nblk=nblk),
        out_shape=jax.ShapeDtypeStruct((nbatch, 4, nblk, HEAD_DIM), F32),
        grid=(nbatch, 4),
        in_specs=[pl.BlockSpec((seq_len, LANE), lambda b, sh: (b, kv0 + sh)),
                  pl.BlockSpec((1, CMP_BLOCK, HEAD_DIM), lambda b, sh: (sh // 2, 0, 0)),
                  pl.BlockSpec((1, CMP_BLOCK * HEAD_DIM, HEAD_DIM), lambda b, sh: (sh // 2, 0, 0)),
                  pl.BlockSpec((1, 1, HEAD_DIM), lambda b, sh: (sh // 2, 0, 0)),
                  pl.BlockSpec((1, HEAD_DIM, HEAD_DIM), lambda b, sh: (sh // 2, 0, 0))],
        out_specs=pl.BlockSpec((1, 1, nblk, HEAD_DIM), lambda b, sh: (b, sh, 0, 0)),
        compiler_params=_cparams(("arbitrary", "arbitrary")),
        name="nsa_compress_prompt",
    )(p2, pe, w1.reshape(2, CMP_BLOCK * HEAD_DIM, HEAD_DIM), b1.reshape(2, 1, HEAD_DIM), w2)


TQ = 128


def _nsa_prompt_body(slopes_ref, q_ref, gt_ref, ck_ref, cv_ref, sk_ref, sv_ref, wk_ref, wv_ref,
                     o_ref, selm_ref, *, seq_len):
    h = pl.program_id(1)
    qi = pl.program_id(2)
    t0 = qi * TQ
    nkt = seq_len // TQ
    nsb = seq_len // SEL_BLOCK
    ncb = seq_len // CMP_BLOCK // 2
    rows = NSA_GROUP * TQ

    qb = q_ref[...] * (HEAD_DIM ** -0.5)
    q4 = jnp.concatenate([qb[:, g * HEAD_DIM:(g + 1) * HEAD_DIM] for g in range(NSA_GROUP)], axis=0)
    q4b = q4.astype(BF16)
    row = lax.broadcasted_iota(I32, (rows, 1), 0)
    tq = t0 + (row & (TQ - 1))
    grow = row >> 7
    slope = jnp.zeros((rows, 1), F32)
    for g in range(NSA_GROUP):
        slope = jnp.where(grow == g, slopes_ref[h * NSA_GROUP + g], slope)

    ck = ck_ref[0, 0]
    cv = cv_ref[0, 0]
    midx = lax.broadcasted_iota(I32, (1, ncb), 1)
    logits = []
    valids = []
    for par in range(2):
        s_c = _mm3(q4, ck[par * ncb:(par + 1) * ncb], NT)
        cend = midx * (2 * CMP_BLOCK) + (CMP_BLOCK - 1) + par * CMP_BLOCK
        valid = cend <= tq
        dist = (tq - cend).astype(F32)
        logits.append(jnp.where(valid, s_c - slope * dist, NEG_INF))
        valids.append(valid)
    mx = jnp.maximum(jnp.max(logits[0], axis=1, keepdims=True), jnp.max(logits[1], axis=1, keepdims=True))
    e0 = jnp.exp(logits[0] - mx)
    e1 = jnp.exp(logits[1] - mx)
    den = jnp.sum(e0, axis=1, keepdims=True) + jnp.sum(e1, axis=1, keepdims=True)
    p0 = jnp.where(valids[0], e0 / den, 0.0)
    p1 = jnp.where(valids[1], e1 / den, 0.0)
    o_cmp = _mm1(p0, cv[0:ncb]) + _mm1(p1, cv[ncb:2 * ncb])
    pb = p0 + p1
    imp = pb[0:TQ]
    for g in range(1, NSA_GROUP):
        imp = imp + pb[g * TQ:(g + 1) * TQ]

    tq1 = t0 + lax.broadcasted_iota(I32, (TQ, 1), 0)
    cur = tq1 >> 6
    jb = lax.broadcasted_iota(I32, (1, nsb), 1)
    forced = (jb == 0) | (jb == cur) | (jb == cur - 1)
    score = jnp.where(jb <= cur, jnp.where(forced, FORCE_SCORE, imp), NEG_INF)
    cnt = jnp.zeros((TQ, nsb), I32)
    for i in range(nsb):
        si = score[:, i:i + 1]
        beats = (si > score) | ((si == score) & (jb > i))
        cnt = cnt + beats.astype(I32)
    sel = ((cnt < min(SEL_TOPN, nsb)) & (score > 0.5 * NEG_INF)).astype(BF16)
    kpos = lax.broadcasted_iota(I32, (nsb, seq_len), 1)
    kblk = lax.broadcasted_iota(I32, (nsb, seq_len), 0)
    expand = ((kpos >> 6) == kblk).astype(BF16)
    selk = _dot(sel, expand)
    for kt in range(nkt):
        blk = selk[:, kt * TQ:(kt + 1) * TQ]
        selm_ref[kt] = jnp.concatenate([blk] * NSA_GROUP, axis=0)

    lane = lax.broadcasted_iota(I32, (1, TQ), 1)

    def attend(k_ref, v_ref, lo, hi, mask_fn):
        def step(kt, carry):
            m, l, acc = carry
            start = pl.multiple_of(kt * TQ, TQ)
            k = k_ref[pl.ds(start, TQ), :].astype(BF16)
            v = v_ref[pl.ds(start, TQ), :].astype(BF16)
            s = _dot(q4b, k, NT)
            pos = kt * TQ + lane
            delta = tq - pos
            mask = mask_fn(kt, delta)
            lg = jnp.where(mask, s - slope * delta.astype(F32), NEG_INF)
            m_new = jnp.maximum(m, jnp.max(lg, axis=1, keepdims=True))
            alpha = jnp.exp(m - m_new)
            p = jnp.where(mask, jnp.exp(lg - m_new), 0.0)
            l = alpha * l + jnp.sum(p, axis=1, keepdims=True)
            acc = alpha * acc + _dot(p.astype(BF16), v)
            return m_new, l, acc

        init = (jnp.full((rows, 1), NEG_INF, F32), jnp.zeros((rows, 1), F32), jnp.zeros((rows, HEAD_DIM), F32))
        _, l, acc = lax.fori_loop(lo, hi, step, init)
        return jnp.where(l > 0.0, acc / jnp.where(l > 0.0, l, 1.0), 0.0)

    o_sel = attend(sk_ref, sv_ref, 0, qi + 1, lambda kt, delta: (selm_ref[kt] > 0.5) & (delta >= 0))
    o_win = attend(wk_ref, wv_ref, jnp.maximum(qi - WINDOW // TQ, 0), qi + 1,
                   lambda kt, delta: (delta >= 0) & (delta < WINDOW))

    gs = jax.nn.sigmoid(gt_ref[...])
    for g in range(NSA_GROUP):
        def gate(c):
            a = gs[:, g * 3 + c:g * 3 + c + 1]
            b = gs[:, NSA_GROUP * 3 + g * 3 + c:NSA_GROUP * 3 + g * 3 + c + 1]
            return jnp.where(h == 0, a, b)

        sl = slice(g * TQ, (g + 1) * TQ)
        o_ref[:, g * HEAD_DIM:(g + 1) * HEAD_DIM] = (gate(0) * o_cmp[sl] + gate(1) * o_sel[sl]) + gate(2) * o_win[sl]


def _nsa_prompt(p2, ckv, slopes, nbatch, seq_len):
    nq = seq_len // TQ
    kvs0 = COL_KVS // LANE
    kvw0 = COL_KVW // LANE
    misc = COL_MISC // LANE
    nblk = seq_len // CMP_BLOCK
    full = lambda c0: pl.BlockSpec((seq_len, LANE), lambda b, h, q, c0=c0: (b, c0 + h))
    return pl.pallas_call(
        functools.partial(_nsa_prompt_body, seq_len=seq_len),
        out_shape=jax.ShapeDtypeStruct((nbatch * seq_len, NSA_WIDTH), F32),
        grid=(nbatch, NSA_KV_HEADS, nq),
        in_specs=[pl.BlockSpec(memory_space=pltpu.SMEM),
                  pl.BlockSpec((TQ, NSA_GROUP * HEAD_DIM), lambda b, h, q: (b * nq + q, h)),
                  pl.BlockSpec((TQ, LANE), lambda b, h, q: (b * nq + q, misc)),
                  pl.BlockSpec((1, 1, nblk, HEAD_DIM), lambda b, h, q: (b, h, 0, 0)),
                  pl.BlockSpec((1, 1, nblk, HEAD_DIM), lambda b, h, q: (b, 2 + h, 0, 0)),
                  full(kvs0), full(kvs0 + 2), full(kvw0), full(kvw0 + 2)],
        out_specs=pl.BlockSpec((TQ, NSA_GROUP * HEAD_DIM), lambda b, h, q: (b * nq + q, h)),
        scratch_shapes=[pltpu.VMEM((nq, NSA_GROUP * TQ, TQ), F32)],
        compiler_params=_cparams(("arbitrary", "arbitrary", "arbitrary")),
        name="nsa_prompt",
    )(slopes, p2, p2, ckv, ckv, p2, p2, p2, p2)


GDN_CB = 8


def _bmm3(a, b, dims):
    ah, al = _split(a)
    bh, bl = _split(b)
    return _dot(ah, bh, dims) + (_dot(al, bh, dims) + _dot(ah, bl, dims))


def _gdn_body(alog_ref, dtb_ref, xq_ref, xk_ref, xv_ref, z_ref, gt_ref, cwq_ref, cwk_ref, cwv_ref,
              cbq_ref, cbk_ref, cbv_ref, s0_ref, gn_ref, o_ref, sout_ref,
              q_s, k_s, v_s, b_s, g_s, u_s, w_s, qg_s, kdt_s, qk_s, gl_s, oc_s, *, seq_len, valid):
    hd = pl.program_id(1)
    C = GDN_CHUNK
    nc = seq_len // C
    cb = min(GDN_CB, nc)
    row = lax.broadcasted_iota(I32, (seq_len, 1), 0)

    def conv(x_ref, cw_ref, cb_ref):
        x = x_ref[...]
        cat = jnp.concatenate([cb_ref[0], x], axis=0)
        w = cw_ref[...]
        acc = pltpu.roll(cat, 3, 0)[SUBLANE:] * w[0:1]
        acc = acc + pltpu.roll(cat, 2, 0)[SUBLANE:] * w[1:2]
        acc = acc + pltpu.roll(cat, 1, 0)[SUBLANE:] * w[2:3]
        acc = acc + x * w[3:4]
        return jax.nn.silu(acc)

    def l2n(x):
        return x * lax.rsqrt(jnp.sum(x * x, axis=-1, keepdims=True) + NORM_EPS)

    q = l2n(conv(xq_ref, cwq_ref, cbq_ref)) * (HEAD_DIM ** -0.5)
    k = l2n(conv(xk_ref, cwk_ref, cbk_ref))
    v = conv(xv_ref, cwv_ref, cbv_ref)
    gt = gt_ref[...]
    lane = lax.broadcasted_iota(I32, (1, LANE), 1)
    bl = jnp.sum(jnp.where(lane == MISC_BL + hd, gt, 0.0), axis=1, keepdims=True)
    al = jnp.sum(jnp.where(lane == MISC_AL + hd, gt, 0.0), axis=1, keepdims=True)
    beta = jax.nn.sigmoid(bl)
    a_pos = jnp.exp(jnp.full((1, 1), alog_ref[hd], F32))
    g = -a_pos * jax.nn.softplus(al + dtb_ref[hd])
    if valid < seq_len:
        vm = row < valid
        q = jnp.where(vm, q, 0.0)
        k = jnp.where(vm, k, 0.0)
        v = jnp.where(vm, v, 0.0)
        beta = jnp.where(vm, beta, 0.0)
        g = jnp.where(vm, g, 0.0)
    pos = row & (C - 1)
    gc = g
    sft = 1
    while sft < C:
        gc = gc + jnp.where(pos >= sft, pltpu.roll(gc, sft, 0), 0.0)
        sft *= 2
    q_s[...] = q
    k_s[...] = k
    v_s[...] = v
    b_s[...] = jnp.broadcast_to(beta, (seq_len, LANE))
    g_s[...] = jnp.broadcast_to(gc, (seq_len, LANE))

    ii = lax.broadcasted_iota(I32, (C, C), 0)
    jj = lax.broadcasted_iota(I32, (C, C), 1)
    tri = (ii >= jj)[None]
    strict = (ii > jj)[None]
    eye = (ii == jj).astype(F32)[None]
    ones_b = jnp.ones((cb, C, C), BF16)
    ident_b = jnp.broadcast_to((lax.broadcasted_iota(I32, (HEAD_DIM, HEAD_DIM), 0)
                                == lax.broadcasted_iota(I32, (HEAD_DIM, HEAD_DIM), 1)).astype(BF16)[None],
                               (cb, HEAD_DIM, HEAD_DIM))

    def local(gi, carry):
        r0 = pl.multiple_of(gi * (cb * C), cb * C)
        sl = pl.ds(r0, cb * C)
        qc = q_s[sl, :].reshape(cb, C, HEAD_DIM)
        kc = k_s[sl, :].reshape(cb, C, HEAD_DIM)
        vc = v_s[sl, :].reshape(cb, C, HEAD_DIM)
        bc = b_s[sl, :].reshape(cb, C, LANE)
        gcc = g_s[sl, :].reshape(cb, C, LANE)
        gcol = gcc[:, :, 0:C]
        dg = gcol * eye
        d1 = dg.astype(BF16)
        r1 = dg - d1.astype(F32)
        d2 = r1.astype(BF16)
        d3 = (r1 - d2.astype(F32)).astype(BF16)
        grow = _dot(ones_b, d1, BNN) + (_dot(ones_b, d2, BNN) + _dot(ones_b, d3, BNN))
        diff = gcol - grow
        decay = jnp.where(tri, jnp.exp(jnp.where(tri, diff, 0.0)), 0.0)
        kb = kc * bc
        a = _dot(kb.astype(BF16), kc.astype(BF16), BNT) * jnp.where(strict, decay, 0.0)
        tinv = eye - a
        pw = a
        n = 2
        while n < C:
            pw = _bmm3(pw, pw, BNN)
            tinv = tinv + _bmm3(tinv, pw, BNN)
            n *= 2
        eg = jnp.exp(gcc)
        rhs = jnp.concatenate([vc * bc, kb * eg], axis=2)
        sol = _bmm3(tinv, rhs, BNN)
        u_s[sl, :] = sol[:, :, 0:HEAD_DIM].reshape(cb * C, HEAD_DIM)
        w_s[sl, :] = sol[:, :, HEAD_DIM:2 * HEAD_DIM].reshape(cb * C, HEAD_DIM)
        qk = _dot(qc.astype(BF16), kc.astype(BF16), BNT) * decay
        qk_s[sl, :] = qk.reshape(cb * C, C)
        qg_s[sl, :] = (qc * eg).reshape(cb * C, HEAD_DIM)
        glast = gcc[:, C - 1:C, :]
        kd = kc * jnp.exp(glast - gcc)
        kdt = _dot(ident_b, kd.astype(BF16), BNT)
        kdt_s[pl.ds(pl.multiple_of(gi * (cb * HEAD_DIM), cb * HEAD_DIM), cb * HEAD_DIM), :] = (
            kdt.reshape(cb * HEAD_DIM, C))
        gl_s[pl.ds(pl.multiple_of(gi * (cb * SUBLANE), cb * SUBLANE), cb * SUBLANE), :] = (
            jnp.broadcast_to(jnp.exp(glast), (cb, SUBLANE, LANE)).reshape(cb * SUBLANE, LANE))
        return carry

    lax.fori_loop(0, nc // cb, local, 0)

    def scan(c, s):
        sl = pl.ds(pl.multiple_of(c * C, C), C)
        sb = s.astype(BF16)
        v_new = u_s[sl, :] - _dot(w_s[sl, :].astype(BF16), sb)
        o_c = _dot(qg_s[sl, :].astype(BF16), sb) + _dot(qk_s[sl, :].astype(BF16), v_new.astype(BF16))
        oc_s[sl, :] = o_c
        kdt = kdt_s[pl.ds(pl.multiple_of(c * HEAD_DIM, HEAD_DIM), HEAD_DIM), :]
        gl = gl_s[pl.ds(pl.multiple_of(c * SUBLANE, SUBLANE), 1), :]
        return s * gl + _dot(kdt.astype(BF16), v_new.astype(BF16))

    s_fin = lax.fori_loop(0, nc, scan, s0_ref[0, 0])
    sout_ref[0, 0] = s_fin
    o = oc_s[...]
    y = o * lax.rsqrt(jnp.mean(o * o, axis=-1, keepdims=True) + NORM_EPS) * gn_ref[...]
    o_ref[...] = y * jax.nn.silu(z_ref[...])


def _gdn(p2, conv_w, conv_buf8, s0, a_log, dt_bias, gn_w, nbatch, seq_len, valid):
    q0 = COL_QKV // LANE
    z0 = COL_Z // LANE
    misc = COL_MISC // LANE
    C = GDN_CHUNK
    col = lambda c0: pl.BlockSpec((seq_len, LANE), lambda b, h, c0=c0: (b, c0 + h))
    cw = lambda c0: pl.BlockSpec((CONV_WIDTH, LANE), lambda b, h, c0=c0: (0, c0 + h))
    cbs = lambda c0: pl.BlockSpec((1, SUBLANE, LANE), lambda b, h, c0=c0: (b, 0, c0 + h))
    smem = pl.BlockSpec(memory_space=pltpu.SMEM)
    nc = seq_len // C
    return pl.pallas_call(
        functools.partial(_gdn_body, seq_len=seq_len, valid=valid),
        out_shape=(jax.ShapeDtypeStruct((nbatch * seq_len, GDN_WIDTH), F32),
                   jax.ShapeDtypeStruct((nbatch, GDN_HEADS, HEAD_DIM, HEAD_DIM), F32)),
        grid=(nbatch, GDN_HEADS),
        in_specs=[smem, smem, col(q0), col(q0 + 8), col(q0 + 16), col(z0),
                  pl.BlockSpec((seq_len, LANE), lambda b, h: (b, misc)),
                  cw(0), cw(8), cw(16), cbs(0), cbs(8), cbs(16),
                  pl.BlockSpec((1, 1, HEAD_DIM, HEAD_DIM), lambda b, h: (b, h, 0, 0)),
                  pl.BlockSpec((1, HEAD_DIM), lambda b, h: (0, 0))],
        out_specs=(pl.BlockSpec((seq_len, LANE), lambda b, h: (b, h)),
                   pl.BlockSpec((1, 1, HEAD_DIM, HEAD_DIM), lambda b, h: (b, h, 0, 0))),
        scratch_shapes=[pltpu.VMEM((seq_len, HEAD_DIM), F32)] * 3
        + [pltpu.VMEM((seq_len, LANE), F32)] * 2
        + [pltpu.VMEM((seq_len, HEAD_DIM), F32)] * 3
        + [pltpu.VMEM((nc * HEAD_DIM, C), F32),
           pltpu.VMEM((seq_len, C), F32),
           pltpu.VMEM((nc * SUBLANE, LANE), F32),
           pltpu.VMEM((seq_len, HEAD_DIM), F32)],
        compiler_params=_cparams(("arbitrary", "arbitrary")),
        name="gdn",
    )(a_log, dt_bias, p2, p2, p2, p2, p2, conv_w, conv_w, conv_w, conv_buf8, conv_buf8, conv_buf8,
      s0, gn_w.reshape(1, HEAD_DIM))


def _out_proj_body(on_ref, og_ref, x_ref, w_ref, g1_ref, nw_ref, sc_ref, sh_ref, x1_ref, h2_ref):
    mix = _dot(on_ref[...].astype(BF16), w_ref[0:NSA_WIDTH, :]) + _dot(og_ref[...].astype(BF16), w_ref[NSA_WIDTH:, :])
    x1 = x_ref[...] + g1_ref[0] * mix
    x1_ref[...] = x1
    y = x1 * lax.rsqrt(jnp.mean(x1 * x1, axis=-1, keepdims=True) + NORM_EPS)
    h2_ref[...] = (y * nw_ref[...]) * (1.0 + sc_ref[0]) + sh_ref[0]


def _out_proj(o_nsa, o_gdn, x, w_out_bf, g1, norm_w, sc, sh, seq_len, tm):
    n, d = x.shape
    tpb = seq_len // tm
    mod = pl.BlockSpec((1, 1, d), lambda i: (i // tpb, 0, 0))
    return pl.pallas_call(
        _out_proj_body,
        out_shape=(jax.ShapeDtypeStruct((n, d), F32), jax.ShapeDtypeStruct((n, d), F32)),
        grid=(n // tm,),
        in_specs=[pl.BlockSpec((tm, NSA_WIDTH), lambda i: (i, 0)),
                  pl.BlockSpec((tm, GDN_WIDTH), lambda i: (i, 0)),
                  pl.BlockSpec((tm, d), lambda i: (i, 0)),
                  pl.BlockSpec((d, d), lambda i: (0, 0)),
                  mod,
                  pl.BlockSpec((1, d), lambda i: (0, 0)),
                  mod, mod],
        out_specs=(pl.BlockSpec((tm, d), lambda i: (i, 0)), pl.BlockSpec((tm, d), lambda i: (i, 0))),
        compiler_params=_cparams(("arbitrary",)),
        name="out_proj",
    )(o_nsa, o_gdn, x, w_out_bf, g1, norm_w.reshape(1, d), sc, sh)


def _take_top(src_ref, dst_ref, nrows):
    s = src_ref[...]
    ridx = lax.broadcasted_iota(I32, s.shape, 0)
    for k in range(PEER_TOPK):
        m = jnp.max(s, axis=0, keepdims=True)
        first = jnp.min(jnp.where(s == m, ridx, nrows), axis=0, keepdims=True)
        s = jnp.where(ridx == first, NEG_BIG, s)
        dst_ref[k:k + 1, :] = m


def _peer_sel_body(h_ref, wqh_ref, wql_ref, keys_ref, s0_ref, al_ref, s1_ref, be_ref, tau_ref,
                   sc_s, ta_s, tb_s, cand_s, tc_s):
    hh, hl = _split(h_ref[...])
    half = PEER_QDIM // 2
    tn = h_ref.shape[0]
    r8 = lax.broadcasted_iota(I32, (SUBLANE, tn), 0)
    for hd in range(PEER_HEADS):
        wh = wqh_ref[hd * PEER_QDIM:(hd + 1) * PEER_QDIM, :]
        wl = wql_ref[hd * PEER_QDIM:(hd + 1) * PEER_QDIM, :]
        qt = _dot(wh, hh, NT) + (_dot(wl, hh, NT) + _dot(wh, hl, NT))
        s0 = _mm3(keys_ref[0], qt[0:half])
        s1 = _mm3(keys_ref[1], qt[half:PEER_QDIM])
        sc_s[...] = s0
        _take_top(sc_s, ta_s, PEER_NKEYS)
        sc_s[...] = s1
        _take_top(sc_s, tb_s, PEER_NKEYS)
        a = ta_s[...]
        b = tb_s[...]
        cand_s[0:16, :] = a[0:1] + b
        cand_s[16:24, :] = a[1:2] + b[0:8]
        off = 24
        for r, n in ((2, 5), (3, 4), (4, 3), (5, 2), (6, 2), (7, 2)):
            cand_s[off:off + 8, :] = jnp.where(r8 < n, a[r:r + 1] + b[0:8], NEG_BIG)
            off += 8
        cand_s[off:off + 8, :] = a[8:16] + b[0:1]
        _take_top(cand_s, tc_s, off + 8)
        tc = tc_s[...]
        zsum = jnp.sum(jnp.exp(tc - tc[0:1]), axis=0, keepdims=True)
        s0_ref[hd] = s0
        al_ref[hd] = jnp.exp(s0 - a[0:1]) / zsum
        s1_ref[hd] = s1
        be_ref[hd] = jnp.exp(s1 - b[0:1])
        tau_ref[hd:hd + 1, :] = tc[PEER_TOPK - 1:PEER_TOPK]


def _peer_select(h2, wq_hi, wq_lo, keys, tn):
    n, d = h2.shape
    hk = jax.ShapeDtypeStruct((PEER_HEADS, PEER_NKEYS, n), F32)
    blk = pl.BlockSpec((PEER_HEADS, PEER_NKEYS, tn), lambda i: (0, 0, i))
    return pl.pallas_call(
        _peer_sel_body,
        out_shape=(hk, hk, hk, hk, jax.ShapeDtypeStruct((PEER_HEADS, n), F32)),
        grid=(n // tn,),
        in_specs=[pl.BlockSpec((tn, d), lambda i: (i, 0)),
                  pl.BlockSpec((PEER_HEADS * PEER_QDIM, d), lambda i: (0, 0)),
                  pl.BlockSpec((PEER_HEADS * PEER_QDIM, d), lambda i: (0, 0)),
                  pl.BlockSpec((2, PEER_NKEYS, PEER_QDIM // 2), lambda i: (0, 0, 0))],
        out_specs=(blk, blk, blk, blk, pl.BlockSpec((PEER_HEADS, tn), lambda i: (0, i))),
        scratch_shapes=[pltpu.VMEM((PEER_NKEYS, tn), F32), pltpu.VMEM((PEER_TOPK, tn), F32),
                        pltpu.VMEM((PEER_TOPK, tn), F32), pltpu.VMEM((80, tn), F32),
                        pltpu.VMEM((PEER_TOPK, tn), F32)],
        compiler_params=_cparams(("arbitrary",)),
        name="peer_select",
    )(h2, wq_hi, wq_lo, keys)


PEER_TE = 1024


def _peer_mm_body(h_ref, u_ref, vt_ref, s0_ref, al_ref, s1_ref, be_ref, tau_ref, o_ref, p_s):
    e = pl.program_id(1)

    @pl.when(e == 0)
    def _():
        o_ref[...] = jnp.zeros_like(o_ref)

    at = _dot(u_ref[...], h_ref[...], NT)
    for ii in range(PEER_TE // PEER_NKEYS):
        rs = slice(ii * PEER_NKEYS, (ii + 1) * PEER_NKEYS)
        act = jax.nn.gelu(at[rs])
        w = jnp.zeros_like(act)
        for hd in range(PEER_HEADS):
            t = s0_ref[hd, ii:ii + 1, :] + s1_ref[hd]
            sel = t >= tau_ref[hd:hd + 1, :]
            w = w + jnp.where(sel, al_ref[hd, ii:ii + 1, :] * be_ref[hd], 0.0)
        p_s[rs, :] = (w * act).astype(BF16)
    o_ref[...] += _dot(vt_ref[...], p_s[...])


def _peer_mm(h2b, u_bf, vt_bf, s0, al, s1, be, tau, tn):
    n, d = h2b.shape
    ne = u_bf.shape[0]
    rows_i = PEER_TE // PEER_NKEYS
    sel_i = pl.BlockSpec((PEER_HEADS, rows_i, tn), lambda i, e: (0, e, i))
    sel_j = pl.BlockSpec((PEER_HEADS, PEER_NKEYS, tn), lambda i, e: (0, 0, i))
    return pl.pallas_call(
        _peer_mm_body,
        out_shape=jax.ShapeDtypeStruct((d, n), F32),
        grid=(n // tn, ne // PEER_TE),
        in_specs=[pl.BlockSpec((tn, d), lambda i, e: (i, 0)),
                  pl.BlockSpec((PEER_TE, d), lambda i, e: (e, 0)),
                  pl.BlockSpec((d, PEER_TE), lambda i, e: (0, e)),
                  sel_i, sel_i, sel_j, sel_j,
                  pl.BlockSpec((PEER_HEADS, tn), lambda i, e: (0, i))],
        out_specs=pl.BlockSpec((d, tn), lambda i, e: (0, i)),
        scratch_shapes=[pltpu.VMEM((PEER_TE, tn), BF16)],
        compiler_params=_cparams(("arbitrary", "arbitrary")),
        name="peer_experts",
    )(h2b, u_bf, vt_bf, s0, al, s1, be, tau)


def _final_body(x1_ref, pt_ref, g2_ref, fw_ref, y_ref, *, seq_len):
    x2 = x1_ref[...] + _seq_rows(g2_ref, seq_len) * pt_ref[...].T
    y_ref[...] = x2 * lax.rsqrt(jnp.mean(x2 * x2, axis=-1, keepdims=True) + NORM_EPS) * fw_ref[...]


def _final(x1, peer_t, g2, fw, seq_len, tm):
    n, d = x1.shape
    g2_spec = _seq_spec(seq_len, tm, d)
    return pl.pallas_call(
        functools.partial(_final_body, seq_len=seq_len),
        out_shape=jax.ShapeDtypeStruct((n, d), F32),
        grid=(n // tm,),
        in_specs=[pl.BlockSpec((tm, d), lambda i: (i, 0)),
                  pl.BlockSpec((d, tm), lambda i: (0, i)),
                  g2_spec,
                  pl.BlockSpec((1, d), lambda i: (0, 0))],
        out_specs=pl.BlockSpec((tm, d), lambda i: (i, 0)),
        compiler_params=_cparams(("arbitrary",)),
        name="final_norm",
    )(x1, peer_t, g2, fw.reshape(1, d))


SROWS = 64
CMP_PAGES = 16
KV_COMP = 2 * NSA_KV_HEADS
TOKEN_ROWS = KV_COMP
PAGE_VROWS = PAGE_SIZE * TOKEN_ROWS
CMP_PAIRS = CMP_BLOCK * TOKEN_ROWS // SUBLANE
BLOCK_VROWS = CMP_BLOCK * TOKEN_ROWS


def _compress_sample_body(pt_ref, *refs):
    pages = refs[:CMP_PAGES]
    pe_ref, w1_ref, b1_ref, w2_ref, o_ref, acc_s = refs[CMP_PAGES:]
    bpp = PAGE_SIZE // CMP_BLOCK
    nrow = CMP_PAGES * bpp * SUBLANE
    nh = CMP_PAGES * bpp // 2
    acc = jnp.zeros((nrow, KV_COMP * HEAD_DIM), F32)
    for q in range(CMP_PAIRS):
        parts = [r[0, n * BLOCK_VROWS + q * SUBLANE:n * BLOCK_VROWS + (q + 1) * SUBLANE, :]
                 for r in pages for n in range(bpp)]
        lhs = jnp.concatenate(parts, axis=0) + jnp.concatenate([pe_ref[q]] * (nrow // SUBLANE), axis=0)
        acc = acc + _dot(lhs.astype(BF16), w1_ref[q])
    rid = lax.broadcasted_iota(I32, (nrow, 1), 0) & (SUBLANE - 1)
    want = ((rid & (KV_COMP - 1)) >> 1) * 2 + (rid >> 2)
    picked = acc[:, 0:HEAD_DIM]
    for blk in range(1, KV_COMP):
        picked = jnp.where(want == blk, acc[:, blk * HEAD_DIM:(blk + 1) * HEAD_DIM], picked)
    acc_s[...] = picked + pltpu.roll(picked, nrow - KV_COMP, 0)
    for c in range(KV_COMP):
        s = c // NSA_KV_HEADS
        hid_e = acc_s[pl.ds(c, nh, stride=2 * SUBLANE), :]
        hid_o = acc_s[pl.ds(SUBLANE + c, nh, stride=2 * SUBLANE), :]
        hid = jax.nn.gelu(jnp.concatenate([hid_e, hid_o], axis=0) + b1_ref[s])
        out = _mm1(hid, w2_ref[s])
        o_ref[0, c, 0] = out[0:nh]
        o_ref[0, c, 1] = out[nh:2 * nh]


def _compress_sample(cache, page_table, pe, w1, b1, w2):
    nbs, n_pages = page_table.shape
    n_phys = cache.shape[0]
    cache_v = cache.reshape(n_phys, PAGE_VROWS, HEAD_DIM)
    ngrp = n_pages // CMP_PAGES
    nblk_half = n_pages * PAGE_SIZE // CMP_BLOCK // 2
    nh = CMP_PAGES * (PAGE_SIZE // CMP_BLOCK) // 2
    pe_q = jnp.transpose(pe.reshape(2, CMP_PAIRS, 2, HEAD_DIM), (1, 2, 0, 3))
    pe_q = jnp.broadcast_to(pe_q[:, :, :, None, :], (CMP_PAIRS, 2, 2, NSA_KV_HEADS, HEAD_DIM))
    pe_q = pe_q.reshape(CMP_PAIRS, SUBLANE, HEAD_DIM)
    w1_q = jnp.transpose(w1.reshape(2, CMP_PAIRS, 2, HEAD_DIM, HEAD_DIM), (1, 3, 0, 2, 4))
    w1_q = w1_q.reshape(CMP_PAIRS, HEAD_DIM, KV_COMP * HEAD_DIM).astype(BF16)
    page_spec = lambda k: pl.BlockSpec((1, PAGE_VROWS, HEAD_DIM),
                                       lambda b, g, pt, k=k: (pt[b * n_pages + g * CMP_PAGES + k], 0, 0))
    const = lambda shape: pl.BlockSpec(shape, lambda b, g, pt: (0,) * len(shape))
    grid_spec = pltpu.PrefetchScalarGridSpec(
        num_scalar_prefetch=1,
        grid=(nbs, ngrp),
        in_specs=[page_spec(k) for k in range(CMP_PAGES)]
        + [const((CMP_PAIRS, SUBLANE, HEAD_DIM)), const((CMP_PAIRS, HEAD_DIM, KV_COMP * HEAD_DIM)),
           const((2, 1, HEAD_DIM)), const((2, HEAD_DIM, HEAD_DIM))],
        out_specs=pl.BlockSpec((1, KV_COMP, 2, nh, HEAD_DIM), lambda b, g, pt: (b, 0, 0, g, 0)),
        scratch_shapes=[pltpu.VMEM((CMP_PAGES * (PAGE_SIZE // CMP_BLOCK) * SUBLANE, HEAD_DIM), F32)])
    return pl.pallas_call(
        _compress_sample_body,
        out_shape=jax.ShapeDtypeStruct((nbs, KV_COMP, 2, nblk_half, HEAD_DIM), F32),
        grid_spec=grid_spec,
        compiler_params=_cparams(("arbitrary", "arbitrary")),
        name="nsa_compress_sample",
    )(page_table.reshape(-1), *([cache_v] * CMP_PAGES), pe_q, w1_q, b1.reshape(2, 1, HEAD_DIM), w2)


def _rows_from_lanes(row, ngrp):
    ridx = lax.broadcasted_iota(I32, (SUBLANE, HEAD_DIM), 0)
    out = jnp.zeros((SUBLANE, HEAD_DIM), F32)
    for g in range(ngrp):
        out = jnp.where(ridx == g, jnp.broadcast_to(row[:, g * HEAD_DIM:(g + 1) * HEAD_DIM], (SUBLANE, HEAD_DIM)), out)
    return out


def _slope_rows(slopes_ref, h):
    ridx = lax.broadcasted_iota(I32, (SUBLANE, 1), 0)
    slope = jnp.zeros((SUBLANE, 1), F32)
    for g in range(NSA_GROUP):
        slope = jnp.where(ridx == g, slopes_ref[h * NSA_GROUP + g], slope)
    return slope


def _nsa_sample_select_body(slopes_ref, q_ref, ckv_ref, ocmp_ref, idx_ref, *, past):
    ncb = past // CMP_BLOCK // 2
    nblk = past // SEL_BLOCK
    ridx = lax.broadcasted_iota(I32, (SUBLANE, 1), 0)
    midx = lax.broadcasted_iota(I32, (1, ncb), 1)
    jb = lax.broadcasted_iota(I32, (1, nblk), 1)
    slot = lax.broadcasted_iota(I32, (1, SEL_TOPN), 1)
    for h in range(NSA_KV_HEADS):
        q8 = _rows_from_lanes(q_ref[0:1, h * NSA_GROUP * HEAD_DIM:(h + 1) * NSA_GROUP * HEAD_DIM], NSA_GROUP)
        q8 = q8 * (HEAD_DIM ** -0.5)
        slope = _slope_rows(slopes_ref, h)
        lg = []
        for par in range(2):
            s_c = _mm3(q8, ckv_ref[0, h, par], NT)
            cend = midx * (2 * CMP_BLOCK) + (CMP_BLOCK - 1) + par * CMP_BLOCK
            lg.append(s_c - slope * (past - cend).astype(F32))
        mx = jnp.maximum(jnp.max(lg[0], axis=1, keepdims=True), jnp.max(lg[1], axis=1, keepdims=True))
        e0 = jnp.exp(lg[0] - mx)
        e1 = jnp.exp(lg[1] - mx)
        den = jnp.sum(e0, axis=1, keepdims=True) + jnp.sum(e1, axis=1, keepdims=True)
        p0 = e0 / den
        p1 = e1 / den
        o_cmp = _mm1(p0, ckv_ref[0, 2 + h, 0]) + _mm1(p1, ckv_ref[0, 2 + h, 1])
        ocmp_ref[0, h * NSA_GROUP:(h + 1) * NSA_GROUP, :] = o_cmp[0:NSA_GROUP]
        imp = jnp.sum(jnp.where(ridx < NSA_GROUP, p0 + p1, 0.0), axis=0, keepdims=True)
        forced = (jb == 0) | (jb == nblk - 1)
        score = jnp.where(forced, FORCE_SCORE, imp)
        picks = jnp.full((1, SEL_TOPN), nblk, I32)
        for k in range(SEL_TOPN - 1):
            m = jnp.max(score, axis=1, keepdims=True)
            first = jnp.min(jnp.where(score == m, jb, nblk), axis=1, keepdims=True)
            score = jnp.where(jb == first, NEG_BIG, score)
            picks = jnp.where(slot == k, first, picks)
        idx_ref[0, h:h + 1, :] = picks


def _nsa_sample_select(p2s, ckv_s, slopes, nbs, past):
    nblk_half = past // CMP_BLOCK // 2
    return pl.pallas_call(
        functools.partial(_nsa_sample_select_body, past=past),
        out_shape=(jax.ShapeDtypeStruct((nbs, NSA_HEADS, HEAD_DIM), F32),
                   jax.ShapeDtypeStruct((nbs, NSA_KV_HEADS, SEL_TOPN), I32)),
        grid=(nbs,),
        in_specs=[pl.BlockSpec(memory_space=pltpu.SMEM),
                  pl.BlockSpec((SUBLANE, NSA_WIDTH), lambda b: (b * (SROWS // SUBLANE), 0)),
                  pl.BlockSpec((1, 4, 2, nblk_half, HEAD_DIM), lambda b: (b, 0, 0, 0, 0))],
        out_specs=(pl.BlockSpec((1, NSA_HEADS, HEAD_DIM), lambda b: (b, 0, 0)),
                   pl.BlockSpec((1, NSA_KV_HEADS, SEL_TOPN), lambda b: (b, 0, 0))),
        compiler_params=_cparams(("arbitrary",)),
        name="nsa_sample_select",
    )(slopes, p2s, ckv_s)


def _nsa_sample_attend_body(idx_ref, pt_ref, slopes_ref, q_ref, gt_ref, ocmp_ref, skn_ref, svn_ref, wkn_ref, wvn_ref,
                            sc_ref, wc_ref, o_ref, m_s, l_s, acc_s, *, past, n_pages):
    b = pl.program_id(0)
    h = pl.program_id(1)
    j = pl.program_id(2)
    nsel = SEL_TOPN - 1
    q8 = _rows_from_lanes(q_ref[0:1, :], NSA_GROUP) * (HEAD_DIM ** -0.5)
    q8b = q8.astype(BF16)
    slope = _slope_rows(slopes_ref, h)

    @pl.when(j == 0)
    def _():
        s_new = jnp.sum(q8 * skn_ref[0:1, :], axis=1, keepdims=True)
        m_s[...] = s_new
        l_s[...] = jnp.ones_like(s_new)
        acc_s[...] = jnp.broadcast_to(svn_ref[0:1, :], (SUBLANE, HEAD_DIM))

    blk = idx_ref[(b * NSA_KV_HEADS + h) * SEL_TOPN + j]
    lane = lax.broadcasted_iota(I32, (1, SEL_BLOCK), 1)
    k_sel = sc_ref[0, pl.ds(h, SEL_BLOCK, stride=TOKEN_ROWS), :]
    v_sel = sc_ref[0, pl.ds(NSA_KV_HEADS + h, SEL_BLOCK, stride=TOKEN_ROWS), :]
    s = _dot(q8b, k_sel.astype(BF16), NT)
    dist = (past - (blk * SEL_BLOCK + lane)).astype(F32)
    lg = s - slope * dist
    m_new = jnp.maximum(m_s[...], jnp.max(lg, axis=1, keepdims=True))
    alpha = jnp.exp(m_s[...] - m_new)
    p = jnp.exp(lg - m_new)
    l_s[...] = alpha * l_s[...] + jnp.sum(p, axis=1, keepdims=True)
    acc_s[...] = alpha * acc_s[...] + _dot(p.astype(BF16), v_sel.astype(BF16))
    m_s[...] = m_new

    @pl.when(j == nsel - 1)
    def _():
        o_sel = acc_s[...] / l_s[...]
        nw = wc_ref.shape[1] // TOKEN_ROWS
        r = lax.broadcasted_iota(I32, (1, nw), 1)
        delta = nw - r
        wmask = (delta < WINDOW) & (past - delta >= 0)
        k_win = wc_ref[0, pl.ds(h, nw, stride=TOKEN_ROWS), :]
        v_win = wc_ref[0, pl.ds(NSA_KV_HEADS + h, nw, stride=TOKEN_ROWS), :]
        sw = _dot(q8b, k_win.astype(BF16), NT)
        lgw = jnp.where(wmask, sw - slope * delta.astype(F32), NEG_INF)
        s_nw = jnp.sum(q8 * wkn_ref[0:1, :], axis=1, keepdims=True)
        mw = jnp.maximum(jnp.max(lgw, axis=1, keepdims=True), s_nw)
        pw = jnp.where(wmask, jnp.exp(lgw - mw), 0.0)
        pn = jnp.exp(s_nw - mw)
        den = jnp.sum(pw, axis=1, keepdims=True) + pn
        o_win = (_dot(pw.astype(BF16), v_win.astype(BF16))
                 + pn * wvn_ref[0:1, :]) / den
        gs = jax.nn.sigmoid(gt_ref[0:1, :])
        o_ref[...] = jnp.zeros_like(o_ref)
        for g in range(NSA_GROUP):
            def gate(c):
                a = gs[:, g * 3 + c:g * 3 + c + 1]
                bb = gs[:, NSA_GROUP * 3 + g * 3 + c:NSA_GROUP * 3 + g * 3 + c + 1]
                return jnp.where(h == 0, a, bb)

            oc = jnp.where(h == 0, ocmp_ref[0, g:g + 1, :], ocmp_ref[0, NSA_GROUP + g:NSA_GROUP + g + 1, :])
            og = (gate(0) * oc + gate(1) * o_sel[g:g + 1]) + gate(2) * o_win[g:g + 1]
            o_ref[0:1, g * HEAD_DIM:(g + 1) * HEAD_DIM] = og


def _nsa_sample_attend(p2s, ocmp, idx, page_table, cache_sel, win_state, slopes, nbs, past):
    n_pages = page_table.shape[1]
    n_phys = cache_sel.shape[0]
    nw = win_state.shape[1]
    halves = PAGE_SIZE // SEL_BLOCK
    sel_v = cache_sel.reshape(n_phys * halves, SEL_BLOCK * TOKEN_ROWS, HEAD_DIM)
    win_v = win_state.reshape(nbs, nw * TOKEN_ROWS, HEAD_DIM)
    rb = SROWS // SUBLANE
    kvs0 = COL_KVS // LANE
    kvw0 = COL_KVW // LANE
    misc = COL_MISC // LANE

    def sel_index(b, h, j, idx, pt):
        blk = idx[(b * NSA_KV_HEADS + h) * SEL_TOPN + j]
        page = pt[b * n_pages + blk // halves]
        return (page * halves + blk % halves, 0, 0)

    new = lambda c0: pl.BlockSpec((SUBLANE, LANE), lambda b, h, j, idx, pt, c0=c0: (b * rb, c0 + h))
    grid_spec = pltpu.PrefetchScalarGridSpec(
        num_scalar_prefetch=2,
        grid=(nbs, NSA_KV_HEADS, SEL_TOPN - 1),
        in_specs=[pl.BlockSpec(memory_space=pltpu.SMEM),
                  pl.BlockSpec((SUBLANE, NSA_GROUP * HEAD_DIM), lambda b, h, j, idx, pt: (b * rb, h)),
                  pl.BlockSpec((SUBLANE, LANE), lambda b, h, j, idx, pt: (b * rb, misc)),
                  pl.BlockSpec((1, NSA_HEADS, HEAD_DIM), lambda b, h, j, idx, pt: (b, 0, 0)),
                  new(kvs0), new(kvs0 + 2), new(kvw0), new(kvw0 + 2),
                  pl.BlockSpec((1, SEL_BLOCK * TOKEN_ROWS, HEAD_DIM), sel_index),
                  pl.BlockSpec((1, nw * TOKEN_ROWS, HEAD_DIM), lambda b, h, j, idx, pt: (b, 0, 0))],
        out_specs=pl.BlockSpec((SROWS, NSA_GROUP * HEAD_DIM), lambda b, h, j, idx, pt: (b, h)),
        scratch_shapes=[pltpu.VMEM((SUBLANE, 1), F32), pltpu.VMEM((SUBLANE, 1), F32),
                        pltpu.VMEM((SUBLANE, HEAD_DIM), F32)])
    return pl.pallas_call(
        functools.partial(_nsa_sample_attend_body, past=past, n_pages=n_pages),
        out_shape=jax.ShapeDtypeStruct((nbs * SROWS, NSA_WIDTH), F32),
        grid_spec=grid_spec,
        compiler_params=_cparams(("arbitrary", "arbitrary", "arbitrary")),
        name="nsa_sample_attend",
    )(idx.reshape(-1), page_table.reshape(-1), slopes, p2s, p2s, ocmp, p2s, p2s, p2s, p2s,
      sel_v, win_v)


def _perm_w_in(w_in):
    d = w_in.shape[0]
    g0 = NSA_WIDTH + 3 * 2 * KV_WIDTH
    q0 = g0 + 3 * NSA_HEADS
    b0 = q0 + 3 * GDN_WIDTH + GDN_WIDTH
    pad = jnp.zeros((d, N_PROJ - (b0 + 2 * GDN_HEADS)), w_in.dtype)
    w2 = jnp.concatenate([w_in[:, :g0], w_in[:, q0:b0], w_in[:, g0:q0], w_in[:, b0:], pad], axis=1)
    return w2.astype(BF16)


def _split_bf16(w):
    hi = w.astype(BF16)
    lo = (w - hi.astype(F32)).astype(BF16)
    return hi, lo


def _mixer_tail(x, o_nsa, o_gdn, w_out_bf, g1, norm_ffn, sc2, sh2, g2, wq_hi, wq_lo, keys, u_bf, vt_bf,
                final_norm, seq_len, tm, tn_sel, tn_mm, tm_final):
    x1, h2 = _out_proj(o_nsa, o_gdn, x, w_out_bf, g1, norm_ffn, sc2, sh2, seq_len, tm)
    s0, al, s1, be, tau = _peer_select(h2, wq_hi, wq_lo, keys, tn_sel)
    peer_t = _peer_mm(h2.astype(BF16), u_bf, vt_bf, s0, al, s1, be, tau, tn_mm)
    return _final(x1, peer_t, g2, final_norm, seq_len, tm_final)


def kernel(x_prompt, x_sample, cache_cmp_kv, cache_sel_kv, state_win_kv, state_conv, state_gdn, page_table,
           c_prompt, c_sample, w_ada, b_ada, norm_mix, norm_ffn, w_in, cmp_pe, cmp_w1, cmp_b1, cmp_w2, conv_w,
           gdn_a_log, gdn_dt_bias, gdn_norm, w_out, peer_wq, peer_keys, peer_u, peer_v, final_norm):
    nb, seq, d = x_prompt.shape
    nbs, dec_seq, _ = x_sample.shape
    assert w_in.shape[0] == 1 and dec_seq == 1, "single layer, single decode token"
    past = page_table.shape[1] * PAGE_SIZE
    slopes = 2.0 ** (-8.0 * jnp.arange(1, NSA_HEADS + 1, dtype=F32) / NSA_HEADS)
    rows_c = 16
    c_all = jnp.concatenate([c_prompt, c_sample, jnp.zeros((rows_c - nb - nbs, d), F32)], axis=0)
    mod = _adaln(c_all, w_ada[0], b_ada[0]).reshape(rows_c, 6, d)
    pm = lambda k: mod[0:nb, k][:, None, :]
    sm = lambda k: mod[nb:nb + nbs, k][:, None, :]
    w2 = _perm_w_in(w_in[0])
    w_out_bf = w_out[0].astype(BF16)
    wq_hi, wq_lo = _split_bf16(peer_wq[0].T)
    u_bf = peer_u[0].astype(BF16)
    vt_bf = peer_v[0].T.astype(BF16)
    cw = (cmp_pe[0], cmp_w1[0], cmp_b1[0], cmp_w2[0])
    kvw = 2 * KV_WIDTH

    xp = x_prompt.reshape(nb * seq, d)
    p2 = _proj_in(xp, norm_mix[0], pm(1), pm(0), w2, seq, 1024)
    ckv = _compress_prompt(p2, *cw, nb, seq)
    o_nsa = _nsa_prompt(p2, ckv, slopes, nb, seq)
    conv0 = jnp.zeros((nb, SUBLANE, 3 * GDN_WIDTH), F32)
    s0 = jnp.zeros((nb, GDN_HEADS, HEAD_DIM, HEAD_DIM), F32)
    o_gdn, gdn_p = _gdn(p2, conv_w[0], conv0, s0, gdn_a_log[0], gdn_dt_bias[0], gdn_norm[0], nb, seq, seq)
    y_prompt = _mixer_tail(xp, o_nsa, o_gdn, w_out_bf, pm(2), norm_ffn[0], pm(4), pm(3), pm(5), wq_hi, wq_lo,
                           peer_keys[0], u_bf, vt_bf, final_norm, seq, 512, 256, 512, 512).reshape(nb, seq, d)
    p3 = p2.reshape(nb, seq, N_PROJ)
    keep = min(WINDOW, seq)
    kv5 = lambda a: a.reshape(a.shape[0], a.shape[1], 2, NSA_KV_HEADS, HEAD_DIM)
    cmp_p = kv5(p3[:, :, COL_KVC:COL_KVC + kvw])
    sel_p = kv5(p3[:, :, COL_KVS:COL_KVS + kvw])
    win_p = kv5(p3[:, seq - keep:, COL_KVW:COL_KVW + kvw])
    conv_p = p3[:, seq - (CONV_WIDTH - 1):, COL_QKV:COL_QKV + 3 * GDN_WIDTH]

    xs = jnp.pad(x_sample, ((0, 0), (0, SROWS - dec_seq), (0, 0))).reshape(nbs * SROWS, d)
    p2s = _proj_in(xs, norm_mix[0], sm(1), sm(0), w2, SROWS, nbs * SROWS)
    ckv_s = _compress_sample(cache_cmp_kv[0], page_table, *cw)
    ocmp, idx = _nsa_sample_select(p2s, ckv_s, slopes, nbs, past)
    o_nsa_s = _nsa_sample_attend(p2s, ocmp, idx, page_table, cache_sel_kv[0], state_win_kv[0], slopes, nbs, past)
    conv_buf = jnp.pad(state_conv[0], ((0, 0), (SUBLANE - (CONV_WIDTH - 1), 0), (0, 0)))
    o_gdn_s, gdn_s = _gdn(p2s, conv_w[0], conv_buf, state_gdn[0], gdn_a_log[0], gdn_dt_bias[0], gdn_norm[0],
                          nbs, SROWS, dec_seq)
    ys = _mixer_tail(xs, o_nsa_s, o_gdn_s, w_out_bf, sm(2), norm_ffn[0], sm(4), sm(3), sm(5), wq_hi, wq_lo,
                     peer_keys[0], u_bf, vt_bf, final_norm, SROWS, SROWS, 256, 512, 2 * SROWS)
    y_sample = ys.reshape(nbs, SROWS, d)[:, 0:dec_seq]
    p3s = p2s.reshape(nbs, SROWS, N_PROJ)[:, 0:dec_seq]
    cmp_s = kv5(p3s[:, :, COL_KVC:COL_KVC + kvw])
    sel_s = kv5(p3s[:, :, COL_KVS:COL_KVS + kvw])
    win_new = kv5(p3s[:, :, COL_KVW:COL_KVW + kvw])
    win_s = jnp.concatenate([state_win_kv[0], win_new], axis=1)[:, dec_seq:]
    conv_s = jnp.concatenate([state_conv[0], p3s[:, :, COL_QKV:COL_QKV + 3 * GDN_WIDTH]], axis=1)[:, dec_seq:]

    return (y_prompt, y_sample, cmp_p[None], cmp_s[None], sel_p[None], sel_s[None], win_p[None], win_s[None],
            conv_p[None], conv_s[None], gdn_p[None], gdn_s[None])
```

```python
import functools
import math

import jax
import jax.numpy as jnp
from jax import lax
from jax.experimental import pallas as pl
from jax.experimental.pallas import tpu as pltpu

F32 = jnp.float32
BF16 = jnp.bfloat16
I32 = jnp.int32

D_MODEL = 2048
HEAD_DIM = 128
NSA_WIDTH = 1024
GDN_WIDTH = 1024
NSA_HEADS = 8
NSA_KV_HEADS = 2
NSA_GROUP = 4
KV_WIDTH = 256
CMP_BLOCK = 32
SEL_BLOCK = 64
SEL_TOPN = 16
WINDOW = 512
GDN_HEADS = 8
CONV_WIDTH = 4
GDN_CHUNK = 64
PEER_HEADS = 8
PEER_NKEYS = 128
PEER_QDIM = 256
PEER_TOPK = 16
PAGE_SIZE = 128
NORM_EPS = 1e-6
NEG_INF = -1e30
FORCE_SCORE = 1e4
NEG_BIG = -3.0e38

LANE = 128
SUBLANE = 8
VMEM_LIMIT = 56 * 1024 * 1024

COL_Q = 0
COL_KVC = 1024
COL_KVS = 1536
COL_KVW = 2048
COL_QKV = 2560
COL_Z = 5632
COL_MISC = 6656
N_PROJ = 6912
MISC_BL = 24
MISC_AL = 32

NN = (((1,), (0,)), ((), ()))
NT = (((1,), (1,)), ((), ()))
BNN = (((2,), (1,)), ((0,), (0,)))
BNT = (((2,), (2,)), ((0,), (0,)))


def _dot(a, b, dims=NN):
    return lax.dot_general(a, b, dims, preferred_element_type=F32)


def _split(x):
    hi = x.astype(BF16)
    lo = (x - hi.astype(F32)).astype(BF16)
    return hi, lo


def _mm1(a, b, dims=NN):
    return _dot(a.astype(BF16), b.astype(BF16), dims)


def _mm3(a, b, dims=NN):
    ah, al = _split(a)
    bh, bl = _split(b)
    return _dot(ah, bh, dims) + (_dot(al, bh, dims) + _dot(ah, bl, dims))


def _cparams(sem, vmem=VMEM_LIMIT):
    return pltpu.CompilerParams(dimension_semantics=sem, vmem_limit_bytes=vmem)


def _adaln_body(c_ref, w_ref, b_ref, o_ref):
    a = jax.nn.silu(c_ref[...])
    o_ref[...] = _mm3(a, w_ref[...]) + b_ref[...]


def _adaln(c_all, w, b):
    rows, d = c_all.shape
    n = w.shape[1]
    tn = 1024
    return pl.pallas_call(
        _adaln_body,
        out_shape=jax.ShapeDtypeStruct((rows, n), F32),
        grid=(n // tn,),
        in_specs=[pl.BlockSpec((rows, d), lambda j: (0, 0)),
                  pl.BlockSpec((d, tn), lambda j: (0, j)),
                  pl.BlockSpec((1, tn), lambda j: (0, j))],
        out_specs=pl.BlockSpec((rows, tn), lambda j: (0, j)),
        compiler_params=_cparams(("arbitrary",)),
        name="adaln",
    )(c_all, w, b.reshape(1, n))


def _seq_rows(m_ref, seq_len):
    nseq = m_ref.shape[0]
    if nseq == 1:
        return m_ref[0]
    return jnp.concatenate([jnp.broadcast_to(m_ref[r], (seq_len, m_ref.shape[2])) for r in range(nseq)], axis=0)


def _seq_spec(seq_len, tm, d):
    if tm <= seq_len:
        tpb = seq_len // tm
        return pl.BlockSpec((1, 1, d), lambda i, *_: (i // tpb, 0, 0))
    return pl.BlockSpec((tm // seq_len, 1, d), lambda i, *_: (i, 0, 0))


def _proj_in_body(x_ref, nw_ref, sc_ref, sh_ref, w_ref, o_ref, h_ref, *, seq_len):
    @pl.when(pl.program_id(1) == 0)
    def _():
        x = x_ref[...]
        y = x * lax.rsqrt(jnp.mean(x * x, axis=-1, keepdims=True) + NORM_EPS)
        h = (y * nw_ref[...]) * (1.0 + _seq_rows(sc_ref, seq_len)) + _seq_rows(sh_ref, seq_len)
        h_ref[...] = h.astype(BF16)

    o_ref[...] = _dot(h_ref[...], w_ref[...])


def _proj_in(x, norm_w, sc, sh, w2, seq_len, tm):
    n, d = x.shape
    ncol = w2.shape[1]
    tn = 768
    mod = _seq_spec(seq_len, tm, d)
    return pl.pallas_call(
        functools.partial(_proj_in_body, seq_len=seq_len),
        out_shape=jax.ShapeDtypeStruct((n, ncol), F32),
        grid=(n // tm, ncol // tn),
        in_specs=[pl.BlockSpec((tm, d), lambda i, j: (i, 0)),
                  pl.BlockSpec((1, d), lambda i, j: (0, 0)),
                  mod, mod,
                  pl.BlockSpec((d, tn), lambda i, j: (0, j))],
        out_specs=pl.BlockSpec((tm, tn), lambda i, j: (i, j)),
        scratch_shapes=[pltpu.VMEM((tm, d), BF16)],
        compiler_params=_cparams(("arbitrary", "arbitrary")),
        name="proj_in",
    )(x, norm_w.reshape(1, d), sc, sh, w2)


def _compress_prompt_body(x_ref, pe_ref, w1_ref, b1_ref, w2_ref, o_ref, *, nblk):
    half = nblk // 2
    acc = jnp.zeros((nblk, HEAD_DIM), F32)
    for l in range(CMP_BLOCK):
        xe = x_ref[pl.ds(l, half, stride=2 * CMP_BLOCK), :]
        xo = x_ref[pl.ds(CMP_BLOCK + l, half, stride=2 * CMP_BLOCK), :]
        xl = jnp.concatenate([xe, xo], axis=0) + pe_ref[0, l:l + 1, :]
        acc = acc + _mm1(xl, w1_ref[0, l * HEAD_DIM:(l + 1) * HEAD_DIM, :])
    hid = jax.nn.gelu(acc + b1_ref[0])
    o_ref[0, 0] = _mm1(hid, w2_ref[0])


def _compress_prompt(p2, pe, w1, b1, w2, nbatch, seq_len):
    nblk = seq_len // CMP_BLOCK
    kv0 = COL_KVC // LANE
    return pl.pallas_call(
        functools.partial(_compress_prompt_body, nblk=nblk),
        out_shape=jax.ShapeDtypeStruct((nbatch, 4, nblk, HEAD_DIM), F32),
        grid=(nbatch, 4),
        in_specs=[pl.BlockSpec((seq_len, LANE), lambda b, sh: (b, kv0 + sh)),
                  pl.BlockSpec((1, CMP_BLOCK, HEAD_DIM), lambda b, sh: (sh // 2, 0, 0)),
                  pl.BlockSpec((1, CMP_BLOCK * HEAD_DIM, HEAD_DIM), lambda b, sh: (sh // 2, 0, 0)),
                  pl.BlockSpec((1, 1, HEAD_DIM), lambda b, sh: (sh // 2, 0, 0)),
                  pl.BlockSpec((1, HEAD_DIM, HEAD_DIM), lambda b, sh: (sh // 2, 0, 0))],
        out_specs=pl.BlockSpec((1, 1, nblk, HEAD_DIM), lambda b, sh: (b, sh, 0, 0)),
        compiler_params=_cparams(("arbitrary", "arbitrary")),
        name="nsa_compress_prompt",
    )(p2, pe, w1.reshape(2, CMP_BLOCK * HEAD_DIM, HEAD_DIM), b1.reshape(2, 1, HEAD_DIM), w2)


TQ = 128
NSA_KSPAN = 512


def _nsa_prompt_body(slopes_ref, q_ref, gt_ref, ck_ref, cv_ref, sk_ref, sv_ref, wk_ref, wv_ref,
                     o_ref, selm_ref, *, seq_len):
    h = pl.program_id(1)
    qi = pl.program_id(2)
    t0 = qi * TQ
    nsb = seq_len // SEL_BLOCK
    ncb = seq_len // CMP_BLOCK // 2
    rows = NSA_GROUP * TQ

    qb = q_ref[...] * (HEAD_DIM ** -0.5)
    q4 = jnp.concatenate([qb[:, g * HEAD_DIM:(g + 1) * HEAD_DIM] for g in range(NSA_GROUP)], axis=0)
    q4b = q4.astype(BF16)
    row = lax.broadcasted_iota(I32, (rows, 1), 0)
    tq = t0 + (row & (TQ - 1))
    grow = row >> 7
    slope = jnp.zeros((rows, 1), F32)
    for g in range(NSA_GROUP):
        slope = jnp.where(grow == g, slopes_ref[h * NSA_GROUP + g], slope)

    ck = ck_ref[0, 0]
    cv = cv_ref[0, 0]
    midx = lax.broadcasted_iota(I32, (1, ncb), 1)
    logits = []
    valids = []
    for par in range(2):
        s_c = _mm3(q4, ck[par * ncb:(par + 1) * ncb], NT)
        cend = midx * (2 * CMP_BLOCK) + (CMP_BLOCK - 1) + par * CMP_BLOCK
        valid = cend <= tq
        dist = (tq - cend).astype(F32)
        logits.append(jnp.where(valid, s_c - slope * dist, NEG_INF))
        valids.append(valid)
    mx = jnp.maximum(jnp.max(logits[0], axis=1, keepdims=True), jnp.max(logits[1], axis=1, keepdims=True))
    e0 = jnp.exp(logits[0] - mx)
    e1 = jnp.exp(logits[1] - mx)
    den = jnp.sum(e0, axis=1, keepdims=True) + jnp.sum(e1, axis=1, keepdims=True)
    p0 = jnp.where(valids[0], e0 / den, 0.0)
    p1 = jnp.where(valids[1], e1 / den, 0.0)
    o_cmp = _mm1(p0, cv[0:ncb]) + _mm1(p1, cv[ncb:2 * ncb])
    pb = p0 + p1
    imp = pb[0:TQ]
    for g in range(1, NSA_GROUP):
        imp = imp + pb[g * TQ:(g + 1) * TQ]

    tq1 = t0 + lax.broadcasted_iota(I32, (TQ, 1), 0)
    cur = tq1 >> 6
    jb = lax.broadcasted_iota(I32, (1, nsb), 1)
    forced = (jb == 0) | (jb == cur) | (jb == cur - 1)
    score = jnp.where(jb <= cur, jnp.where(forced, FORCE_SCORE, imp), NEG_INF)
    cnt = jnp.zeros((TQ, nsb), I32)
    for i in range(nsb):
        si = score[:, i:i + 1]
        beats = (si > score) | ((si == score) & (jb > i))
        cnt = cnt + beats.astype(I32)
    sel = ((cnt < min(SEL_TOPN, nsb)) & (score > 0.5 * NEG_INF)).astype(BF16)
    kpos = lax.broadcasted_iota(I32, (nsb, seq_len), 1)
    kblk = lax.broadcasted_iota(I32, (nsb, seq_len), 0)
    expand = ((kpos >> 6) == kblk).astype(BF16)
    selk = _dot(sel, expand)

    kw = NSA_KSPAN
    t_row = t0 + lax.broadcasted_iota(I32, (TQ, kw), 0)
    key = lax.broadcasted_iota(I32, (TQ, kw), 1)
    for st in range(seq_len // kw):
        blk = (selk[:, st * kw:(st + 1) * kw] - 1.0) * (-NEG_INF)
        selm_ref[st] = blk + jnp.where(st * kw + key <= t_row, 0.0, NEG_INF)
    m_floor = 0.1 * NEG_INF

    def span_update(carry, k, v, bias, key0):
        m, l, acc = carry
        width = k.shape[0]
        alibi = slope * lax.broadcasted_iota(I32, (1, width), 1).astype(F32)
        x = _dot(q4b, k, NT) + alibi
        x = (x.reshape(NSA_GROUP, TQ, width) + bias[None]).reshape(rows, width)
        shift = slope * (key0 - tq).astype(F32)
        m_new = jnp.maximum(m, jnp.max(x, axis=1, keepdims=True) + shift)
        alpha = jnp.exp(m - m_new)
        p = jnp.exp(x - (m_new - shift))
        l = alpha * l + jnp.sum(p, axis=1, keepdims=True)
        acc = alpha * acc + _dot(p.astype(BF16), v)
        return m_new, l, acc

    def finish(carry):
        _, l, acc = carry
        return jnp.where(l > 0.0, acc / jnp.where(l > 0.0, l, 1.0), 0.0)

    init = (jnp.full((rows, 1), m_floor, F32), jnp.zeros((rows, 1), F32), jnp.zeros((rows, HEAD_DIM), F32))

    def sel_step(st, carry):
        start = pl.multiple_of(st * kw, kw)
        return span_update(carry, sk_ref[pl.ds(start, kw), :].astype(BF16), sv_ref[pl.ds(start, kw), :].astype(BF16),
                           selm_ref[st], start)

    o_sel = finish(lax.fori_loop(0, (t0 + TQ - 1) // kw + 1, sel_step, init))

    ww = min(WINDOW + TQ, seq_len)
    wstart = pl.multiple_of(jnp.maximum(t0 + TQ - ww, 0), TQ)
    wdelta = (t0 + lax.broadcasted_iota(I32, (TQ, ww), 0)) - (wstart + lax.broadcasted_iota(I32, (TQ, ww), 1))
    wbias = jnp.where((wdelta >= 0) & (wdelta < WINDOW), 0.0, NEG_INF)
    o_win = finish(span_update(init, wk_ref[pl.ds(wstart, ww), :].astype(BF16), wv_ref[pl.ds(wstart, ww), :].astype(BF16),
                               wbias, wstart))

    gs = jax.nn.sigmoid(gt_ref[...])
    for g in range(NSA_GROUP):
        def gate(c):
            a = gs[:, g * 3 + c:g * 3 + c + 1]
            b = gs[:, NSA_GROUP * 3 + g * 3 + c:NSA_GROUP * 3 + g * 3 + c + 1]
            return jnp.where(h == 0, a, b)

        sl = slice(g * TQ, (g + 1) * TQ)
        o_ref[:, g * HEAD_DIM:(g + 1) * HEAD_DIM] = (gate(0) * o_cmp[sl] + gate(1) * o_sel[sl]) + gate(2) * o_win[sl]


def _nsa_prompt(p2, ckv, slopes, nbatch, seq_len):
    nq = seq_len // TQ
    kvs0 = COL_KVS // LANE
    kvw0 = COL_KVW // LANE
    misc = COL_MISC // LANE
    nblk = seq_len // CMP_BLOCK
    full = lambda c0: pl.BlockSpec((seq_len, LANE), lambda b, h, q, c0=c0: (b, c0 + h))
    return pl.pallas_call(
        functools.partial(_nsa_prompt_body, seq_len=seq_len),
        out_shape=jax.ShapeDtypeStruct((nbatch * seq_len, NSA_WIDTH), F32),
        grid=(nbatch, NSA_KV_HEADS, nq),
        in_specs=[pl.BlockSpec(memory_space=pltpu.SMEM),
                  pl.BlockSpec((TQ, NSA_GROUP * HEAD_DIM), lambda b, h, q: (b * nq + q, h)),
                  pl.BlockSpec((TQ, LANE), lambda b, h, q: (b * nq + q, misc)),
                  pl.BlockSpec((1, 1, nblk, HEAD_DIM), lambda b, h, q: (b, h, 0, 0)),
                  pl.BlockSpec((1, 1, nblk, HEAD_DIM), lambda b, h, q: (b, 2 + h, 0, 0)),
                  full(kvs0), full(kvs0 + 2), full(kvw0), full(kvw0 + 2)],
        out_specs=pl.BlockSpec((TQ, NSA_GROUP * HEAD_DIM), lambda b, h, q: (b * nq + q, h)),
        scratch_shapes=[pltpu.VMEM((seq_len // NSA_KSPAN, TQ, NSA_KSPAN), F32)],
        compiler_params=_cparams(("arbitrary", "arbitrary", "arbitrary")),
        name="nsa_prompt",
    )(slopes, p2, p2, ckv, ckv, p2, p2, p2, p2)


GDN_CB = 8


def _gdn_chunk(seq_len):
    return min(GDN_CHUNK, seq_len)


def _bmm3(a, b, dims):
    ah, al = _split(a)
    bh, bl = _split(b)
    return _dot(ah, bh, dims) + (_dot(al, bh, dims) + _dot(ah, bl, dims))


def _gdn_body(alog_ref, dtb_ref, xq_ref, xk_ref, xv_ref, z_ref, gt_ref, cwq_ref, cwk_ref, cwv_ref,
              cbq_ref, cbk_ref, cbv_ref, s0_ref, gn_ref, o_ref, sout_ref,
              q_s, k_s, v_s, b_s, g_s, u_s, w_s, qg_s, kdt_s, qk_s, gl_s, oc_s, *, seq_len, valid):
    hd = pl.program_id(1)
    C = _gdn_chunk(seq_len)
    nc = seq_len // C
    cb = min(GDN_CB, nc)
    row = lax.broadcasted_iota(I32, (seq_len, 1), 0)

    def conv(x_ref, cw_ref, cb_ref):
        x = x_ref[...]
        cat = jnp.concatenate([cb_ref[0], x], axis=0)
        w = cw_ref[...]
        acc = pltpu.roll(cat, 3, 0)[SUBLANE:] * w[0:1]
        acc = acc + pltpu.roll(cat, 2, 0)[SUBLANE:] * w[1:2]
        acc = acc + pltpu.roll(cat, 1, 0)[SUBLANE:] * w[2:3]
        acc = acc + x * w[3:4]
        return jax.nn.silu(acc)

    def l2n(x):
        return x * lax.rsqrt(jnp.sum(x * x, axis=-1, keepdims=True) + NORM_EPS)

    q = l2n(conv(xq_ref, cwq_ref, cbq_ref)) * (HEAD_DIM ** -0.5)
    k = l2n(conv(xk_ref, cwk_ref, cbk_ref))
    v = conv(xv_ref, cwv_ref, cbv_ref)
    gt = gt_ref[...]
    lane = lax.broadcasted_iota(I32, (1, LANE), 1)
    bl = jnp.sum(jnp.where(lane == MISC_BL + hd, gt, 0.0), axis=1, keepdims=True)
    al = jnp.sum(jnp.where(lane == MISC_AL + hd, gt, 0.0), axis=1, keepdims=True)
    beta = jax.nn.sigmoid(bl)
    a_pos = jnp.exp(jnp.full((1, 1), alog_ref[hd], F32))
    g = -a_pos * jax.nn.softplus(al + dtb_ref[hd])
    if valid < seq_len:
        vm = row < valid
        q = jnp.where(vm, q, 0.0)
        k = jnp.where(vm, k, 0.0)
        v = jnp.where(vm, v, 0.0)
        beta = jnp.where(vm, beta, 0.0)
        g = jnp.where(vm, g, 0.0)
    pos = row & (C - 1)
    gc = g
    sft = 1
    while sft < C:
        gc = gc + jnp.where(pos >= sft, pltpu.roll(gc, sft, 0), 0.0)
        sft *= 2
    q_s[...] = q
    k_s[...] = k
    v_s[...] = v
    b_s[...] = jnp.broadcast_to(beta, (seq_len, LANE))
    g_s[...] = jnp.broadcast_to(gc, (seq_len, LANE))

    ii = lax.broadcasted_iota(I32, (C, C), 0)
    jj = lax.broadcasted_iota(I32, (C, C), 1)
    tri = (ii >= jj)[None]
    strict = (ii > jj)[None]
    eye = (ii == jj).astype(F32)[None]
    ones_b = jnp.ones((cb, C, C), BF16)
    ident_b = jnp.broadcast_to((lax.broadcasted_iota(I32, (HEAD_DIM, HEAD_DIM), 0)
                                == lax.broadcasted_iota(I32, (HEAD_DIM, HEAD_DIM), 1)).astype(BF16)[None],
                               (cb, HEAD_DIM, HEAD_DIM))

    def local(gi, carry):
        r0 = pl.multiple_of(gi * (cb * C), cb * C)
        sl = pl.ds(r0, cb * C)
        qc = q_s[sl, :].reshape(cb, C, HEAD_DIM)
        kc = k_s[sl, :].reshape(cb, C, HEAD_DIM)
        vc = v_s[sl, :].reshape(cb, C, HEAD_DIM)
        bc = b_s[sl, :].reshape(cb, C, LANE)
        gcc = g_s[sl, :].reshape(cb, C, LANE)
        gcol = gcc[:, :, 0:C]
        dg = gcol * eye
        d1 = dg.astype(BF16)
        r1 = dg - d1.astype(F32)
        d2 = r1.astype(BF16)
        d3 = (r1 - d2.astype(F32)).astype(BF16)
        grow = _dot(ones_b, d1, BNN) + (_dot(ones_b, d2, BNN) + _dot(ones_b, d3, BNN))
        diff = gcol - grow
        decay = jnp.where(tri, jnp.exp(jnp.where(tri, diff, 0.0)), 0.0)
        kb = kc * bc
        a = _dot(kb.astype(BF16), kc.astype(BF16), BNT) * jnp.where(strict, decay, 0.0)
        tinv = eye - a
        pw = a
        n = 2
        while n < C:
            pw = _bmm3(pw, pw, BNN)
            tinv = tinv + _bmm3(tinv, pw, BNN)
            n *= 2
        eg = jnp.exp(gcc)
        rhs = jnp.concatenate([vc * bc, kb * eg], axis=2)
        sol = _bmm3(tinv, rhs, BNN)
        u_s[sl, :] = sol[:, :, 0:HEAD_DIM].reshape(cb * C, HEAD_DIM)
        w_s[sl, :] = sol[:, :, HEAD_DIM:2 * HEAD_DIM].reshape(cb * C, HEAD_DIM)
        qk = _dot(qc.astype(BF16), kc.astype(BF16), BNT) * decay
        qk_s[sl, :] = qk.reshape(cb * C, C)
        qg_s[sl, :] = (qc * eg).reshape(cb * C, HEAD_DIM)
        glast = gcc[:, C - 1:C, :]
        kd = kc * jnp.exp(glast - gcc)
        kdt = _dot(ident_b, kd.astype(BF16), BNT)
        kdt_s[pl.ds(pl.multiple_of(gi * (cb * HEAD_DIM), cb * HEAD_DIM), cb * HEAD_DIM), :] = (
            kdt.reshape(cb * HEAD_DIM, C))
        gl_s[pl.ds(pl.multiple_of(gi * (cb * SUBLANE), cb * SUBLANE), cb * SUBLANE), :] = (
            jnp.broadcast_to(jnp.exp(glast), (cb, SUBLANE, LANE)).reshape(cb * SUBLANE, LANE))
        return carry

    lax.fori_loop(0, nc // cb, local, 0)

    def scan(c, s):
        sl = pl.ds(pl.multiple_of(c * C, C), C)
        sb = s.astype(BF16)
        v_new = u_s[sl, :] - _dot(w_s[sl, :].astype(BF16), sb)
        o_c = _dot(qg_s[sl, :].astype(BF16), sb) + _dot(qk_s[sl, :].astype(BF16), v_new.astype(BF16))
        oc_s[sl, :] = o_c
        kdt = kdt_s[pl.ds(pl.multiple_of(c * HEAD_DIM, HEAD_DIM), HEAD_DIM), :]
        gl = gl_s[pl.ds(pl.multiple_of(c * SUBLANE, SUBLANE), 1), :]
        return s * gl + _dot(kdt.astype(BF16), v_new.astype(BF16))

    s_fin = lax.fori_loop(0, nc, scan, s0_ref[0, 0])
    sout_ref[0, 0] = s_fin
    o = oc_s[...]
    y = o * lax.rsqrt(jnp.mean(o * o, axis=-1, keepdims=True) + NORM_EPS) * gn_ref[...]
    o_ref[...] = y * jax.nn.silu(z_ref[...])


def _gdn(p2, conv_w, conv_buf8, s0, a_log, dt_bias, gn_w, nbatch, seq_len, valid):
    q0 = COL_QKV // LANE
    z0 = COL_Z // LANE
    misc = COL_MISC // LANE
    C = _gdn_chunk(seq_len)
    col = lambda c0: pl.BlockSpec((seq_len, LANE), lambda b, h, c0=c0: (b, c0 + h))
    cw = lambda c0: pl.BlockSpec((CONV_WIDTH, LANE), lambda b, h, c0=c0: (0, c0 + h))
    cbs = lambda c0: pl.BlockSpec((1, SUBLANE, LANE), lambda b, h, c0=c0: (b, 0, c0 + h))
    smem = pl.BlockSpec(memory_space=pltpu.SMEM)
    nc = seq_len // C
    return pl.pallas_call(
        functools.partial(_gdn_body, seq_len=seq_len, valid=valid),
        out_shape=(jax.ShapeDtypeStruct((nbatch * seq_len, GDN_WIDTH), F32),
                   jax.ShapeDtypeStruct((nbatch, GDN_HEADS, HEAD_DIM, HEAD_DIM), F32)),
        grid=(nbatch, GDN_HEADS),
        in_specs=[smem, smem, col(q0), col(q0 + 8), col(q0 + 16), col(z0),
                  pl.BlockSpec((seq_len, LANE), lambda b, h: (b, misc)),
                  cw(0), cw(8), cw(16), cbs(0), cbs(8), cbs(16),
                  pl.BlockSpec((1, 1, HEAD_DIM, HEAD_DIM), lambda b, h: (b, h, 0, 0)),
                  pl.BlockSpec((1, HEAD_DIM), lambda b, h: (0, 0))],
        out_specs=(pl.BlockSpec((seq_len, LANE), lambda b, h: (b, h)),
                   pl.BlockSpec((1, 1, HEAD_DIM, HEAD_DIM), lambda b, h: (b, h, 0, 0))),
        scratch_shapes=[pltpu.VMEM((seq_len, HEAD_DIM), F32)] * 3
        + [pltpu.VMEM((seq_len, LANE), F32)] * 2
        + [pltpu.VMEM((seq_len, HEAD_DIM), F32)] * 3
        + [pltpu.VMEM((nc * HEAD_DIM, C), F32),
           pltpu.VMEM((seq_len, C), F32),
           pltpu.VMEM((nc * SUBLANE, LANE), F32),
           pltpu.VMEM((seq_len, HEAD_DIM), F32)],
        compiler_params=_cparams(("arbitrary", "arbitrary")),
        name="gdn",
    )(a_log, dt_bias, p2, p2, p2, p2, p2, conv_w, conv_w, conv_w, conv_buf8, conv_buf8, conv_buf8,
      s0, gn_w.reshape(1, HEAD_DIM))


def _out_proj_body(on_ref, og_ref, x_ref, w_ref, g1_ref, nw_ref, sc_ref, sh_ref, x1_ref, h2_ref):
    mix = _dot(on_ref[...].astype(BF16), w_ref[0:NSA_WIDTH, :]) + _dot(og_ref[...].astype(BF16), w_ref[NSA_WIDTH:, :])
    x1 = x_ref[...] + g1_ref[0] * mix
    x1_ref[...] = x1
    y = x1 * lax.rsqrt(jnp.mean(x1 * x1, axis=-1, keepdims=True) + NORM_EPS)
    h2_ref[...] = (y * nw_ref[...]) * (1.0 + sc_ref[0]) + sh_ref[0]


def _out_proj(o_nsa, o_gdn, x, w_out_bf, g1, norm_w, sc, sh, seq_len, tm):
    n, d = x.shape
    tpb = seq_len // tm
    mod = pl.BlockSpec((1, 1, d), lambda i: (i // tpb, 0, 0))
    return pl.pallas_call(
        _out_proj_body,
        out_shape=(jax.ShapeDtypeStruct((n, d), F32), jax.ShapeDtypeStruct((n, d), F32)),
        grid=(n // tm,),
        in_specs=[pl.BlockSpec((tm, NSA_WIDTH), lambda i: (i, 0)),
                  pl.BlockSpec((tm, GDN_WIDTH), lambda i: (i, 0)),
                  pl.BlockSpec((tm, d), lambda i: (i, 0)),
                  pl.BlockSpec((d, d), lambda i: (0, 0)),
                  mod,
                  pl.BlockSpec((1, d), lambda i: (0, 0)),
                  mod, mod],
        out_specs=(pl.BlockSpec((tm, d), lambda i: (i, 0)), pl.BlockSpec((tm, d), lambda i: (i, 0))),
        compiler_params=_cparams(("arbitrary",)),
        name="out_proj",
    )(o_nsa, o_gdn, x, w_out_bf, g1, norm_w.reshape(1, d), sc, sh)


def _take_top(src_ref, dst_ref):
    s = src_ref[...]
    for k in range(PEER_TOPK):
        m = jnp.max(s, axis=0, keepdims=True)
        s = jnp.where(s == m, NEG_BIG, s)
        dst_ref[k:k + 1, :] = m


PAIR_ROWS = ((0, 16, 16), (16, 8, 8), (24, 8, 5), (32, 8, 4), (40, 8, 3), (48, 8, 2), (56, 8, 2), (64, 8, 2))
PAIR_TAIL = 72
PAIR_TOTAL = 80


def _peer_sel_body(h_ref, wqh_ref, wql_ref, keys_ref, cnt_ref, al_ref, code_ref, be_ref,
                   sc_s, ta_s, tb_s, cand_s, tc_s):
    hh, hl = _split(h_ref[...])
    half = PEER_QDIM // 2
    tn = h_ref.shape[0]
    r8 = lax.broadcasted_iota(I32, (SUBLANE, tn), 0)
    for hd in range(PEER_HEADS):
        wh = wqh_ref[hd * PEER_QDIM:(hd + 1) * PEER_QDIM, :]
        wl = wql_ref[hd * PEER_QDIM:(hd + 1) * PEER_QDIM, :]
        qt = _dot(wh, hh, NT) + (_dot(wl, hh, NT) + _dot(wh, hl, NT))
        s0 = _mm3(keys_ref[0], qt[0:half])
        s1 = _mm3(keys_ref[1], qt[half:PEER_QDIM])
        sc_s[...] = s0
        _take_top(sc_s, ta_s)
        sc_s[...] = s1
        _take_top(sc_s, tb_s)
        a = ta_s[...]
        b = tb_s[...]
        for r, (row0, nrow, nval) in enumerate(PAIR_ROWS):
            pair = a[r:r + 1] + b[0:nrow]
            cand_s[row0:row0 + nrow, :] = pair if nval == nrow else jnp.where(r8 < nval, pair, NEG_BIG)
        cand_s[PAIR_TAIL:PAIR_TOTAL, :] = a[SUBLANE:2 * SUBLANE] + b[0:1]
        _take_top(cand_s, tc_s)
        tau = tc_s[PEER_TOPK - 1:PEER_TOPK, :]
        cand = cand_s[...]
        keep = cand >= tau
        zsum = jnp.sum(jnp.where(keep, jnp.exp(cand - cand[0:1]), 0.0), axis=0, keepdims=True)
        keepf = keep.astype(F32)
        cnt = jnp.zeros((PEER_NKEYS, tn), F32)
        for r, (row0, nrow, nval) in enumerate(PAIR_ROWS):
            cnt_r = jnp.sum(keepf[row0:row0 + nrow], axis=0, keepdims=True)
            cnt = jnp.where(s0 == a[r:r + 1], cnt_r, cnt)
        for r in range(SUBLANE, 2 * SUBLANE):
            cnt = jnp.where(s0 == a[r:r + 1], keepf[PAIR_TAIL + r - SUBLANE:PAIR_TAIL + r - SUBLANE + 1], cnt)
        code = jnp.zeros((PEER_NKEYS, tn), F32)
        for r in range(PEER_TOPK):
            code = code + (b[r:r + 1] > s1).astype(F32)
        cnt_ref[hd] = cnt
        al_ref[hd] = jnp.exp(s0 - a[0:1]) / zsum
        code_ref[hd] = code.astype(BF16)
        be_ref[hd] = jnp.exp(s1 - b[0:1]).astype(BF16)


def _peer_select(h2, wq_hi, wq_lo, keys, tn):
    n, d = h2.shape
    hk = jax.ShapeDtypeStruct((PEER_HEADS, PEER_NKEYS, n), F32)
    hk16 = jax.ShapeDtypeStruct((PEER_HEADS, PEER_NKEYS, n), BF16)
    blk = pl.BlockSpec((PEER_HEADS, PEER_NKEYS, tn), lambda i: (0, 0, i))
    return pl.pallas_call(
        _peer_sel_body,
        out_shape=(hk, hk, hk16, hk16),
        grid=(n // tn,),
        in_specs=[pl.BlockSpec((tn, d), lambda i: (i, 0)),
                  pl.BlockSpec((PEER_HEADS * PEER_QDIM, d), lambda i: (0, 0)),
                  pl.BlockSpec((PEER_HEADS * PEER_QDIM, d), lambda i: (0, 0)),
                  pl.BlockSpec((2, PEER_NKEYS, PEER_QDIM // 2), lambda i: (0, 0, 0))],
        out_specs=(blk, blk, blk, blk),
        scratch_shapes=[pltpu.VMEM((PEER_NKEYS, tn), F32), pltpu.VMEM((PEER_TOPK, tn), F32),
                        pltpu.VMEM((PEER_TOPK, tn), F32), pltpu.VMEM((PAIR_TOTAL, tn), F32),
                        pltpu.VMEM((PEER_TOPK, tn), F32)],
        compiler_params=_cparams(("arbitrary",)),
        name="peer_select",
    )(h2, wq_hi, wq_lo, keys)


PEER_TE = 1024
GELU_C0 = math.sqrt(2.0 / math.pi)
GELU_C1 = 0.044715 * math.sqrt(2.0 / math.pi)


def _gelu_tanh(x):
    hx = 0.5 * x
    return hx + hx * jnp.tanh(x * (GELU_C0 + GELU_C1 * (x * x)))


def _peer_mm_body(h_ref, u_ref, vt_ref, cnt_ref, al_ref, code_ref, be_ref, o_ref, at_s, p_s):
    e = pl.program_id(1)

    @pl.when(e == 0)
    def _():
        o_ref[...] = jnp.zeros_like(o_ref)

    at_s[...] = _dot(u_ref[...], h_ref[...], NT)
    tn = h_ref.shape[0]
    for ii in range(PEER_TE // PEER_NKEYS):
        rs = slice(ii * PEER_NKEYS, (ii + 1) * PEER_NKEYS)
        for ck in range(tn // LANE):
            cs = slice(ck * LANE, (ck + 1) * LANE)
            w = jnp.zeros((PEER_NKEYS, LANE), BF16)
            for hd in range(PEER_HEADS):
                cnt = cnt_ref[hd, ii:ii + 1, cs].astype(BF16)
                kept = jnp.where(code_ref[hd, :, cs] < cnt, be_ref[hd, :, cs], jnp.zeros((), BF16))
                w = w + al_ref[hd, ii:ii + 1, cs].astype(BF16) * kept
            p_s[rs, cs] = (w.astype(F32) * _gelu_tanh(at_s[rs, cs])).astype(BF16)
    o_ref[...] += _dot(vt_ref[...], p_s[...])


def _peer_mm(h2b, u_bf, vt_bf, cnt, al, code, be, tn):
    n, d = h2b.shape
    ne = u_bf.shape[0]
    rows_i = PEER_TE // PEER_NKEYS
    sel_i = pl.BlockSpec((PEER_HEADS, rows_i, tn), lambda i, e: (0, e, i))
    sel_j = pl.BlockSpec((PEER_HEADS, PEER_NKEYS, tn), lambda i, e: (0, 0, i))
    return pl.pallas_call(
        _peer_mm_body,
        out_shape=jax.ShapeDtypeStruct((d, n), F32),
        grid=(n // tn, ne // PEER_TE),
        in_specs=[pl.BlockSpec((tn, d), lambda i, e: (i, 0)),
                  pl.BlockSpec((PEER_TE, d), lambda i, e: (e, 0)),
                  pl.BlockSpec((d, PEER_TE), lambda i, e: (0, e)),
                  sel_i, sel_i, sel_j, sel_j],
        out_specs=pl.BlockSpec((d, tn), lambda i, e: (0, i)),
        scratch_shapes=[pltpu.VMEM((PEER_TE, tn), F32), pltpu.VMEM((PEER_TE, tn), BF16)],
        compiler_params=_cparams(("arbitrary", "arbitrary")),
        name="peer_experts",
    )(h2b, u_bf, vt_bf, cnt, al, code, be)


def _final_body(x1_ref, pt_ref, g2_ref, fw_ref, y_ref, *, seq_len):
    x2 = x1_ref[...] + _seq_rows(g2_ref, seq_len) * pt_ref[...].T
    y_ref[...] = x2 * lax.rsqrt(jnp.mean(x2 * x2, axis=-1, keepdims=True) + NORM_EPS) * fw_ref[...]


def _final(x1, peer_t, g2, fw, seq_len, tm):
    n, d = x1.shape
    g2_spec = _seq_spec(seq_len, tm, d)
    return pl.pallas_call(
        functools.partial(_final_body, seq_len=seq_len),
        out_shape=jax.ShapeDtypeStruct((n, d), F32),
        grid=(n // tm,),
        in_specs=[pl.BlockSpec((tm, d), lambda i: (i, 0)),
                  pl.BlockSpec((d, tm), lambda i: (0, i)),
                  g2_spec,
                  pl.BlockSpec((1, d), lambda i: (0, 0))],
        out_specs=pl.BlockSpec((tm, d), lambda i: (i, 0)),
        compiler_params=_cparams(("arbitrary",)),
        name="final_norm",
    )(x1, peer_t, g2, fw.reshape(1, d))


SROWS = 64
CMP_PAGES = 16
KV_COMP = 2 * NSA_KV_HEADS
TOKEN_ROWS = KV_COMP
PAGE_VROWS = PAGE_SIZE * TOKEN_ROWS
CMP_PAIRS = CMP_BLOCK * TOKEN_ROWS // SUBLANE
BLOCK_VROWS = CMP_BLOCK * TOKEN_ROWS


def _compress_sample_body(pt_ref, *refs):
    pages = refs[:CMP_PAGES]
    pe_ref, w1_ref, b1_ref, w2_ref, o_ref, acc_s = refs[CMP_PAGES:]
    bpp = PAGE_SIZE // CMP_BLOCK
    nrow = CMP_PAGES * bpp * SUBLANE
    nh = CMP_PAGES * bpp // 2
    acc = jnp.zeros((nrow, KV_COMP * HEAD_DIM), F32)
    for q in range(CMP_PAIRS):
        parts = [r[0, n * BLOCK_VROWS + q * SUBLANE:n * BLOCK_VROWS + (q + 1) * SUBLANE, :]
                 for r in pages for n in range(bpp)]
        lhs = jnp.concatenate(parts, axis=0) + jnp.concatenate([pe_ref[q]] * (nrow // SUBLANE), axis=0)
        acc = acc + _dot(lhs.astype(BF16), w1_ref[q])
    rid = lax.broadcasted_iota(I32, (nrow, 1), 0) & (SUBLANE - 1)
    want = ((rid & (KV_COMP - 1)) >> 1) * 2 + (rid >> 2)
    picked = acc[:, 0:HEAD_DIM]
    for blk in range(1, KV_COMP):
        picked = jnp.where(want == blk, acc[:, blk * HEAD_DIM:(blk + 1) * HEAD_DIM], picked)
    acc_s[...] = picked + pltpu.roll(picked, nrow - KV_COMP, 0)
    for c in range(KV_COMP):
        s = c // NSA_KV_HEADS
        hid_e = acc_s[pl.ds(c, nh, stride=2 * SUBLANE), :]
        hid_o = acc_s[pl.ds(SUBLANE + c, nh, stride=2 * SUBLANE), :]
        hid = jax.nn.gelu(jnp.concatenate([hid_e, hid_o], axis=0) + b1_ref[s])
        out = _mm1(hid, w2_ref[s])
        o_ref[0, c, 0] = out[0:nh]
        o_ref[0, c, 1] = out[nh:2 * nh]


def _compress_sample(cache, page_table, pe, w1, b1, w2):
    nbs, n_pages = page_table.shape
    n_phys = cache.shape[0]
    cache_v = cache.reshape(n_phys, PAGE_VROWS, HEAD_DIM)
    ngrp = n_pages // CMP_PAGES
    nblk_half = n_pages * PAGE_SIZE // CMP_BLOCK // 2
    nh = CMP_PAGES * (PAGE_SIZE // CMP_BLOCK) // 2
    pe_q = jnp.transpose(pe.reshape(2, CMP_PAIRS, 2, HEAD_DIM), (1, 2, 0, 3))
    pe_q = jnp.broadcast_to(pe_q[:, :, :, None, :], (CMP_PAIRS, 2, 2, NSA_KV_HEADS, HEAD_DIM))
    pe_q = pe_q.reshape(CMP_PAIRS, SUBLANE, HEAD_DIM)
    w1_q = jnp.transpose(w1.reshape(2, CMP_PAIRS, 2, HEAD_DIM, HEAD_DIM), (1, 3, 0, 2, 4))
    w1_q = w1_q.reshape(CMP_PAIRS, HEAD_DIM, KV_COMP * HEAD_DIM).astype(BF16)
    page_spec = lambda k: pl.BlockSpec((1, PAGE_VROWS, HEAD_DIM),
                                       lambda b, g, pt, k=k: (pt[b * n_pages + g * CMP_PAGES + k], 0, 0))
    const = lambda shape: pl.BlockSpec(shape, lambda b, g, pt: (0,) * len(shape))
    grid_spec = pltpu.PrefetchScalarGridSpec(
        num_scalar_prefetch=1,
        grid=(nbs, ngrp),
        in_specs=[page_spec(k) for k in range(CMP_PAGES)]
        + [const((CMP_PAIRS, SUBLANE, HEAD_DIM)), const((CMP_PAIRS, HEAD_DIM, KV_COMP * HEAD_DIM)),
           const((2, 1, HEAD_DIM)), const((2, HEAD_DIM, HEAD_DIM))],
        out_specs=pl.BlockSpec((1, KV_COMP, 2, nh, HEAD_DIM), lambda b, g, pt: (b, 0, 0, g, 0)),
        scratch_shapes=[pltpu.VMEM((CMP_PAGES * (PAGE_SIZE // CMP_BLOCK) * SUBLANE, HEAD_DIM), F32)])
    return pl.pallas_call(
        _compress_sample_body,
        out_shape=jax.ShapeDtypeStruct((nbs, KV_COMP, 2, nblk_half, HEAD_DIM), F32),
        grid_spec=grid_spec,
        compiler_params=_cparams(("arbitrary", "arbitrary")),
        name="nsa_compress_sample",
    )(page_table.reshape(-1), *([cache_v] * CMP_PAGES), pe_q, w1_q, b1.reshape(2, 1, HEAD_DIM), w2)


def _rows_from_lanes(row, ngrp):
    ridx = lax.broadcasted_iota(I32, (SUBLANE, HEAD_DIM), 0)
    out = jnp.zeros((SUBLANE, HEAD_DIM), F32)
    for g in range(ngrp):
        out = jnp.where(ridx == g, jnp.broadcast_to(row[:, g * HEAD_DIM:(g + 1) * HEAD_DIM], (SUBLANE, HEAD_DIM)), out)
    return out


def _slope_rows(slopes_ref, h):
    ridx = lax.broadcasted_iota(I32, (SUBLANE, 1), 0)
    slope = jnp.zeros((SUBLANE, 1), F32)
    for g in range(NSA_GROUP):
        slope = jnp.where(ridx == g, slopes_ref[h * NSA_GROUP + g], slope)
    return slope


def _nsa_sample_select_body(slopes_ref, q_ref, ckv_ref, ocmp_ref, idx_ref, *, past):
    ncb = past // CMP_BLOCK // 2
    nblk = past // SEL_BLOCK
    ridx = lax.broadcasted_iota(I32, (SUBLANE, 1), 0)
    midx = lax.broadcasted_iota(I32, (1, ncb), 1)
    jb = lax.broadcasted_iota(I32, (1, nblk), 1)
    slot = lax.broadcasted_iota(I32, (1, SEL_TOPN), 1)
    for h in range(NSA_KV_HEADS):
        q8 = _rows_from_lanes(q_ref[0:1, h * NSA_GROUP * HEAD_DIM:(h + 1) * NSA_GROUP * HEAD_DIM], NSA_GROUP)
        q8 = q8 * (HEAD_DIM ** -0.5)
        slope = _slope_rows(slopes_ref, h)
        lg = []
        for par in range(2):
            s_c = _mm3(q8, ckv_ref[0, h, par], NT)
            cend = midx * (2 * CMP_BLOCK) + (CMP_BLOCK - 1) + par * CMP_BLOCK
            lg.append(s_c - slope * (past - cend).astype(F32))
        mx = jnp.maximum(jnp.max(lg[0], axis=1, keepdims=True), jnp.max(lg[1], axis=1, keepdims=True))
        e0 = jnp.exp(lg[0] - mx)
        e1 = jnp.exp(lg[1] - mx)
        den = jnp.sum(e0, axis=1, keepdims=True) + jnp.sum(e1, axis=1, keepdims=True)
        p0 = e0 / den
        p1 = e1 / den
        o_cmp = _mm1(p0, ckv_ref[0, 2 + h, 0]) + _mm1(p1, ckv_ref[0, 2 + h, 1])
        ocmp_ref[0, h * NSA_GROUP:(h + 1) * NSA_GROUP, :] = o_cmp[0:NSA_GROUP]
        imp = jnp.sum(jnp.where(ridx < NSA_GROUP, p0 + p1, 0.0), axis=0, keepdims=True)
        forced = (jb == 0) | (jb == nblk - 1)
        score = jnp.where(forced, FORCE_SCORE, imp)
        picks = jnp.full((1, SEL_TOPN), nblk, I32)
        for k in range(SEL_TOPN - 1):
            m = jnp.max(score, axis=1, keepdims=True)
            first = jnp.min(jnp.where(score == m, jb, nblk), axis=1, keepdims=True)
            score = jnp.where(jb == first, NEG_BIG, score)
            picks = jnp.where(slot == k, first, picks)
        idx_ref[0, h:h + 1, :] = picks


def _nsa_sample_select(p2s, ckv_s, slopes, nbs, past):
    nblk_half = past // CMP_BLOCK // 2
    return pl.pallas_call(
        functools.partial(_nsa_sample_select_body, past=past),
        out_shape=(jax.ShapeDtypeStruct((nbs, NSA_HEADS, HEAD_DIM), F32),
                   jax.ShapeDtypeStruct((nbs, NSA_KV_HEADS, SEL_TOPN), I32)),
        grid=(nbs,),
        in_specs=[pl.BlockSpec(memory_space=pltpu.SMEM),
                  pl.BlockSpec((SUBLANE, NSA_WIDTH), lambda b: (b * (SROWS // SUBLANE), 0)),
                  pl.BlockSpec((1, 4, 2, nblk_half, HEAD_DIM), lambda b: (b, 0, 0, 0, 0))],
        out_specs=(pl.BlockSpec((1, NSA_HEADS, HEAD_DIM), lambda b: (b, 0, 0)),
                   pl.BlockSpec((1, NSA_KV_HEADS, SEL_TOPN), lambda b: (b, 0, 0))),
        compiler_params=_cparams(("arbitrary",)),
        name="nsa_sample_select",
    )(slopes, p2s, ckv_s)


def _nsa_sample_attend_body(idx_ref, pt_ref, slopes_ref, q_ref, gt_ref, ocmp_ref, skn_ref, svn_ref, wkn_ref, wvn_ref,
                            sc_ref, wc_ref, o_ref, m_s, l_s, acc_s, *, past, n_pages):
    b = pl.program_id(0)
    h = pl.program_id(1)
    j = pl.program_id(2)
    nsel = SEL_TOPN - 1
    q8 = _rows_from_lanes(q_ref[0:1, :], NSA_GROUP) * (HEAD_DIM ** -0.5)
    q8b = q8.astype(BF16)
    slope = _slope_rows(slopes_ref, h)

    @pl.when(j == 0)
    def _():
        s_new = jnp.sum(q8 * skn_ref[0:1, :], axis=1, keepdims=True)
        m_s[...] = s_new
        l_s[...] = jnp.ones_like(s_new)
        acc_s[...] = jnp.broadcast_to(svn_ref[0:1, :], (SUBLANE, HEAD_DIM))

    blk = idx_ref[(b * NSA_KV_HEADS + h) * SEL_TOPN + j]
    lane = lax.broadcasted_iota(I32, (1, SEL_BLOCK), 1)
    k_sel = sc_ref[0, pl.ds(h, SEL_BLOCK, stride=TOKEN_ROWS), :]
    v_sel = sc_ref[0, pl.ds(NSA_KV_HEADS + h, SEL_BLOCK, stride=TOKEN_ROWS), :]
    s = _dot(q8b, k_sel.astype(BF16), NT)
    dist = (past - (blk * SEL_BLOCK + lane)).astype(F32)
    lg = s - slope * dist
    m_new = jnp.maximum(m_s[...], jnp.max(lg, axis=1, keepdims=True))
    alpha = jnp.exp(m_s[...] - m_new)
    p = jnp.exp(lg - m_new)
    l_s[...] = alpha * l_s[...] + jnp.sum(p, axis=1, keepdims=True)
    acc_s[...] = alpha * acc_s[...] + _dot(p.astype(BF16), v_sel.astype(BF16))
    m_s[...] = m_new

    @pl.when(j == nsel - 1)
    def _():
        o_sel = acc_s[...] / l_s[...]
        nw = wc_ref.shape[1] // TOKEN_ROWS
        r = lax.broadcasted_iota(I32, (1, nw), 1)
        delta = nw - r
        wmask = (delta < WINDOW) & (past - delta >= 0)
        k_win = wc_ref[0, pl.ds(h, nw, stride=TOKEN_ROWS), :]
        v_win = wc_ref[0, pl.ds(NSA_KV_HEADS + h, nw, stride=TOKEN_ROWS), :]
        sw = _dot(q8b, k_win.astype(BF16), NT)
        lgw = jnp.where(wmask, sw - slope * delta.astype(F32), NEG_INF)
        s_nw = jnp.sum(q8 * wkn_ref[0:1, :], axis=1, keepdims=True)
        mw = jnp.maximum(jnp.max(lgw, axis=1, keepdims=True), s_nw)
        pw = jnp.where(wmask, jnp.exp(lgw - mw), 0.0)
        pn = jnp.exp(s_nw - mw)
        den = jnp.sum(pw, axis=1, keepdims=True) + pn
        o_win = (_dot(pw.astype(BF16), v_win.astype(BF16))
                 + pn * wvn_ref[0:1, :]) / den
        gs = jax.nn.sigmoid(gt_ref[0:1, :])
        o_ref[...] = jnp.zeros_like(o_ref)
        for g in range(NSA_GROUP):
            def gate(c):
                a = gs[:, g * 3 + c:g * 3 + c + 1]
                bb = gs[:, NSA_GROUP * 3 + g * 3 + c:NSA_GROUP * 3 + g * 3 + c + 1]
                return jnp.where(h == 0, a, bb)

            oc = jnp.where(h == 0, ocmp_ref[0, g:g + 1, :], ocmp_ref[0, NSA_GROUP + g:NSA_GROUP + g + 1, :])
            og = (gate(0) * oc + gate(1) * o_sel[g:g + 1]) + gate(2) * o_win[g:g + 1]
            o_ref[0:1, g * HEAD_DIM:(g + 1) * HEAD_DIM] = og


def _nsa_sample_attend(p2s, ocmp, idx, page_table, cache_sel, win_state, slopes, nbs, past):
    n_pages = page_table.shape[1]
    n_phys = cache_sel.shape[0]
    nw = win_state.shape[1]
    halves = PAGE_SIZE // SEL_BLOCK
    sel_v = cache_sel.reshape(n_phys * halves, SEL_BLOCK * TOKEN_ROWS, HEAD_DIM)
    win_v = win_state.reshape(nbs, nw * TOKEN_ROWS, HEAD_DIM)
    rb = SROWS // SUBLANE
    kvs0 = COL_KVS // LANE
    kvw0 = COL_KVW // LANE
    misc = COL_MISC // LANE

    def sel_index(b, h, j, idx, pt):
        blk = idx[(b * NSA_KV_HEADS + h) * SEL_TOPN + j]
        page = pt[b * n_pages + blk // halves]
        return (page * halves + blk % halves, 0, 0)

    new = lambda c0: pl.BlockSpec((SUBLANE, LANE), lambda b, h, j, idx, pt, c0=c0: (b * rb, c0 + h))
    grid_spec = pltpu.PrefetchScalarGridSpec(
        num_scalar_prefetch=2,
        grid=(nbs, NSA_KV_HEADS, SEL_TOPN - 1),
        in_specs=[pl.BlockSpec(memory_space=pltpu.SMEM),
                  pl.BlockSpec((SUBLANE, NSA_GROUP * HEAD_DIM), lambda b, h, j, idx, pt: (b * rb, h)),
                  pl.BlockSpec((SUBLANE, LANE), lambda b, h, j, idx, pt: (b * rb, misc)),
                  pl.BlockSpec((1, NSA_HEADS, HEAD_DIM), lambda b, h, j, idx, pt: (b, 0, 0)),
                  new(kvs0), new(kvs0 + 2), new(kvw0), new(kvw0 + 2),
                  pl.BlockSpec((1, SEL_BLOCK * TOKEN_ROWS, HEAD_DIM), sel_index),
                  pl.BlockSpec((1, nw * TOKEN_ROWS, HEAD_DIM), lambda b, h, j, idx, pt: (b, 0, 0))],
        out_specs=pl.BlockSpec((SROWS, NSA_GROUP * HEAD_DIM), lambda b, h, j, idx, pt: (b, h)),
        scratch_shapes=[pltpu.VMEM((SUBLANE, 1), F32), pltpu.VMEM((SUBLANE, 1), F32),
                        pltpu.VMEM((SUBLANE, HEAD_DIM), F32)])
    return pl.pallas_call(
        functools.partial(_nsa_sample_attend_body, past=past, n_pages=n_pages),
        out_shape=jax.ShapeDtypeStruct((nbs * SROWS, NSA_WIDTH), F32),
        grid_spec=grid_spec,
        compiler_params=_cparams(("arbitrary", "arbitrary", "arbitrary")),
        name="nsa_sample_attend",
    )(idx.reshape(-1), page_table.reshape(-1), slopes, p2s, p2s, ocmp, p2s, p2s, p2s, p2s,
      sel_v, win_v)


def _perm_w_in(w_in):
    d = w_in.shape[0]
    g0 = NSA_WIDTH + 3 * 2 * KV_WIDTH
    q0 = g0 + 3 * NSA_HEADS
    b0 = q0 + 3 * GDN_WIDTH + GDN_WIDTH
    pad = jnp.zeros((d, N_PROJ - (b0 + 2 * GDN_HEADS)), w_in.dtype)
    w2 = jnp.concatenate([w_in[:, :g0], w_in[:, q0:b0], w_in[:, g0:q0], w_in[:, b0:], pad], axis=1)
    return w2.astype(BF16)


def _split_bf16(w):
    hi = w.astype(BF16)
    lo = (w - hi.astype(F32)).astype(BF16)
    return hi, lo


def _mixer_tail(x, o_nsa, o_gdn, w_out_bf, g1, norm_ffn, sc2, sh2, g2, wq_hi, wq_lo, keys, u_bf, vt_bf,
                final_norm, seq_len, tm, tn_sel, tn_mm, tm_final):
    x1, h2 = _out_proj(o_nsa, o_gdn, x, w_out_bf, g1, norm_ffn, sc2, sh2, seq_len, tm)
    cnt, al, code, be = _peer_select(h2, wq_hi, wq_lo, keys, tn_sel)
    peer_t = _peer_mm(h2.astype(BF16), u_bf, vt_bf, cnt, al, code, be, tn_mm)
    return _final(x1, peer_t, g2, final_norm, seq_len, tm_final)


def kernel(x_prompt, x_sample, cache_cmp_kv, cache_sel_kv, state_win_kv, state_conv, state_gdn, page_table,
           c_prompt, c_sample, w_ada, b_ada, norm_mix, norm_ffn, w_in, cmp_pe, cmp_w1, cmp_b1, cmp_w2, conv_w,
           gdn_a_log, gdn_dt_bias, gdn_norm, w_out, peer_wq, peer_keys, peer_u, peer_v, final_norm):
    nb, seq, d = x_prompt.shape
    nbs, dec_seq, _ = x_sample.shape
    assert w_in.shape[0] == 1 and dec_seq == 1, "single layer, single decode token"
    past = page_table.shape[1] * PAGE_SIZE
    slopes = 2.0 ** (-8.0 * jnp.arange(1, NSA_HEADS + 1, dtype=F32) / NSA_HEADS)
    rows_c = 16
    c_all = jnp.concatenate([c_prompt, c_sample, jnp.zeros((rows_c - nb - nbs, d), F32)], axis=0)
    mod = _adaln(c_all, w_ada[0], b_ada[0]).reshape(rows_c, 6, d)
    pm = lambda k: mod[0:nb, k][:, None, :]
    sm = lambda k: mod[nb:nb + nbs, k][:, None, :]
    w2 = _perm_w_in(w_in[0])
    w_out_bf = w_out[0].astype(BF16)
    wq_hi, wq_lo = _split_bf16(peer_wq[0].T)
    u_bf = peer_u[0].astype(BF16)
    vt_bf = peer_v[0].T.astype(BF16)
    cw = (cmp_pe[0], cmp_w1[0], cmp_b1[0], cmp_w2[0])
    kvw = 2 * KV_WIDTH

    xp = x_prompt.reshape(nb * seq, d)
    p2 = _proj_in(xp, norm_mix[0], pm(1), pm(0), w2, seq, 1024)
    ckv = _compress_prompt(p2, *cw, nb, seq)
    o_nsa = _nsa_prompt(p2, ckv, slopes, nb, seq)
    conv0 = jnp.zeros((nb, SUBLANE, 3 * GDN_WIDTH), F32)
    s0 = jnp.zeros((nb, GDN_HEADS, HEAD_DIM, HEAD_DIM), F32)
    o_gdn, gdn_p = _gdn(p2, conv_w[0], conv0, s0, gdn_a_log[0], gdn_dt_bias[0], gdn_norm[0], nb, seq, seq)
    y_prompt = _mixer_tail(xp, o_nsa, o_gdn, w_out_bf, pm(2), norm_ffn[0], pm(4), pm(3), pm(5), wq_hi, wq_lo,
                           peer_keys[0], u_bf, vt_bf, final_norm, seq, 512, 256, 512, 512).reshape(nb, seq, d)
    p3 = p2.reshape(nb, seq, N_PROJ)
    keep = min(WINDOW, seq)
    kv5 = lambda a: a.reshape(a.shape[0], a.shape[1], 2, NSA_KV_HEADS, HEAD_DIM)
    cmp_p = kv5(p3[:, :, COL_KVC:COL_KVC + kvw])
    sel_p = kv5(p3[:, :, COL_KVS:COL_KVS + kvw])
    win_p = kv5(p3[:, seq - keep:, COL_KVW:COL_KVW + kvw])
    conv_p = p3[:, seq - (CONV_WIDTH - 1):, COL_QKV:COL_QKV + 3 * GDN_WIDTH]

    xs = jnp.pad(x_sample, ((0, 0), (0, SROWS - dec_seq), (0, 0))).reshape(nbs * SROWS, d)
    p2s = _proj_in(xs, norm_mix[0], sm(1), sm(0), w2, SROWS, nbs * SROWS)
    ckv_s = _compress_sample(cache_cmp_kv[0], page_table, *cw)
    ocmp, idx = _nsa_sample_select(p2s, ckv_s, slopes, nbs, past)
    o_nsa_s = _nsa_sample_attend(p2s, ocmp, idx, page_table, cache_sel_kv[0], state_win_kv[0], slopes, nbs, past)
    conv_buf = jnp.pad(state_conv[0], ((0, 0), (SUBLANE - (CONV_WIDTH - 1), 0), (0, 0)))
    o_gdn_s, gdn_s = _gdn(p2s, conv_w[0], conv_buf, state_gdn[0], gdn_a_log[0], gdn_dt_bias[0], gdn_norm[0],
                          nbs, SROWS, dec_seq)
    ys = _mixer_tail(xs, o_nsa_s, o_gdn_s, w_out_bf, sm(2), norm_ffn[0], sm(4), sm(3), sm(5), wq_hi, wq_lo,
                     peer_keys[0], u_bf, vt_bf, final_norm, SROWS, SROWS, 256, 512, 2 * SROWS)
    y_sample = ys.reshape(nbs, SROWS, d)[:, 0:dec_seq]
    p3s = p2s.reshape(nbs, SROWS, N_PROJ)[:, 0:dec_seq]
    cmp_s = kv5(p3s[:, :, COL_KVC:COL_KVC + kvw])
    sel_s = kv5(p3s[:, :, COL_KVS:COL_KVS + kvw])
    win_new = kv5(p3s[:, :, COL_KVW:COL_KVW + kvw])
    win_s = jnp.concatenate([state_win_kv[0], win_new], axis=1)[:, dec_seq:]
    conv_s = jnp.concatenate([state_conv[0], p3s[:, :, COL_QKV:COL_QKV + 3 * GDN_WIDTH]], axis=1)[:, dec_seq:]

    return (y_prompt, y_sample, cmp_p[None], cmp_s[None], sel_p[None], sel_s[None], win_p[None], win_s[None],
            conv_p[None], conv_s[None], gdn_p[None], gdn_s[None])
```

```python
import functools
import math

import jax
import jax.numpy as jnp
from jax import lax
from jax.experimental import pallas as pl
from jax.experimental.pallas import tpu as pltpu

F32 = jnp.float32
BF16 = jnp.bfloat16
I32 = jnp.int32

D_MODEL = 2048
HEAD_DIM = 128
NSA_WIDTH = 1024
GDN_WIDTH = 1024
NSA_HEADS = 8
NSA_KV_HEADS = 2
NSA_GROUP = 4
KV_WIDTH = 256
CMP_BLOCK = 32
SEL_BLOCK = 64
SEL_TOPN = 16
WINDOW = 512
GDN_HEADS = 8
CONV_WIDTH = 4
GDN_CHUNK = 64
PEER_HEADS = 8
PEER_NKEYS = 128
PEER_QDIM = 256
PEER_TOPK = 16
PAGE_SIZE = 128
NORM_EPS = 1e-6
NEG_INF = -1e30
FORCE_SCORE = 1e4
NEG_BIG = -3.0e38

LANE = 128
SUBLANE = 8
VMEM_LIMIT = 56 * 1024 * 1024

COL_Q = 0
COL_KVC = 1024
COL_KVS = 1536
COL_KVW = 2048
COL_QKV = 2560
COL_Z = 5632
COL_MISC = 6656
N_PROJ = 6912
MISC_BL = 24
MISC_AL = 32

NN = (((1,), (0,)), ((), ()))
NT = (((1,), (1,)), ((), ()))
BNN = (((2,), (1,)), ((0,), (0,)))
BNT = (((2,), (2,)), ((0,), (0,)))


def _dot(a, b, dims=NN):
    return lax.dot_general(a, b, dims, preferred_element_type=F32)


def _split(x):
    hi = x.astype(BF16)
    lo = (x - hi.astype(F32)).astype(BF16)
    return hi, lo


def _mm1(a, b, dims=NN):
    return _dot(a.astype(BF16), b.astype(BF16), dims)


def _mm3(a, b, dims=NN):
    ah, al = _split(a)
    bh, bl = _split(b)
    return _dot(ah, bh, dims) + (_dot(al, bh, dims) + _dot(ah, bl, dims))


def _cparams(sem, vmem=VMEM_LIMIT):
    return pltpu.CompilerParams(dimension_semantics=sem, vmem_limit_bytes=vmem)


def _adaln_body(c_ref, w_ref, b_ref, o_ref):
    a = jax.nn.silu(c_ref[...])
    o_ref[...] = _mm3(a, w_ref[...]) + b_ref[...]


def _adaln(c_all, w, b):
    rows, d = c_all.shape
    n = w.shape[1]
    tn = 1024
    return pl.pallas_call(
        _adaln_body,
        out_shape=jax.ShapeDtypeStruct((rows, n), F32),
        grid=(n // tn,),
        in_specs=[pl.BlockSpec((rows, d), lambda j: (0, 0)),
                  pl.BlockSpec((d, tn), lambda j: (0, j)),
                  pl.BlockSpec((1, tn), lambda j: (0, j))],
        out_specs=pl.BlockSpec((rows, tn), lambda j: (0, j)),
        compiler_params=_cparams(("arbitrary",)),
        name="adaln",
    )(c_all, w, b.reshape(1, n))


def _seq_rows(m_ref, seq_len):
    nseq = m_ref.shape[0]
    if nseq == 1:
        return m_ref[0]
    return jnp.concatenate([jnp.broadcast_to(m_ref[r], (seq_len, m_ref.shape[2])) for r in range(nseq)], axis=0)


def _seq_spec(seq_len, tm, d):
    if tm <= seq_len:
        tpb = seq_len // tm
        return pl.BlockSpec((1, 1, d), lambda i, *_: (i // tpb, 0, 0))
    return pl.BlockSpec((tm // seq_len, 1, d), lambda i, *_: (i, 0, 0))


def _proj_in_body(x_ref, nw_ref, sc_ref, sh_ref, w_ref, o_ref, h_ref, *, seq_len):
    @pl.when(pl.program_id(1) == 0)
    def _():
        x = x_ref[...]
        y = x * lax.rsqrt(jnp.mean(x * x, axis=-1, keepdims=True) + NORM_EPS)
        h = (y * nw_ref[...]) * (1.0 + _seq_rows(sc_ref, seq_len)) + _seq_rows(sh_ref, seq_len)
        h_ref[...] = h.astype(BF16)

    o_ref[...] = _dot(h_ref[...], w_ref[...])


def _proj_in(x, norm_w, sc, sh, w2, seq_len, tm):
    n, d = x.shape
    ncol = w2.shape[1]
    tn = 768
    mod = _seq_spec(seq_len, tm, d)
    return pl.pallas_call(
        functools.partial(_proj_in_body, seq_len=seq_len),
        out_shape=jax.ShapeDtypeStruct((n, ncol), F32),
        grid=(n // tm, ncol // tn),
        in_specs=[pl.BlockSpec((tm, d), lambda i, j: (i, 0)),
                  pl.BlockSpec((1, d), lambda i, j: (0, 0)),
                  mod, mod,
                  pl.BlockSpec((d, tn), lambda i, j: (0, j))],
        out_specs=pl.BlockSpec((tm, tn), lambda i, j: (i, j)),
        scratch_shapes=[pltpu.VMEM((tm, d), BF16)],
        compiler_params=_cparams(("arbitrary", "arbitrary")),
        name="proj_in",
    )(x, norm_w.reshape(1, d), sc, sh, w2)


def _compress_prompt_body(x_ref, pe_ref, w1_ref, b1_ref, w2_ref, o_ref, *, nblk):
    half = nblk // 2
    acc = jnp.zeros((nblk, HEAD_DIM), F32)
    for l in range(CMP_BLOCK):
        xe = x_ref[pl.ds(l, half, stride=2 * CMP_BLOCK), :]
        xo = x_ref[pl.ds(CMP_BLOCK + l, half, stride=2 * CMP_BLOCK), :]
        xl = jnp.concatenate([xe, xo], axis=0) + pe_ref[0, l:l + 1, :]
        acc = acc + _mm1(xl, w1_ref[0, l * HEAD_DIM:(l + 1) * HEAD_DIM, :])
    hid = jax.nn.gelu(acc + b1_ref[0])
    o_ref[0, 0] = _mm1(hid, w2_ref[0])


def _compress_prompt(p2, pe, w1, b1, w2, nbatch, seq_len):
    nblk = seq_len // CMP_BLOCK
    kv0 = COL_KVC // LANE
    return pl.pallas_call(
        functools.partial(_compress_prompt_body, nblk=nblk),
        out_shape=jax.ShapeDtypeStruct((nbatch, 4, nblk, HEAD_DIM), F32),
        grid=(nbatch, 4),
        in_specs=[pl.BlockSpec((seq_len, LANE), lambda b, sh: (b, kv0 + sh)),
                  pl.BlockSpec((1, CMP_BLOCK, HEAD_DIM), lambda b, sh: (sh // 2, 0, 0)),
                  pl.BlockSpec((1, CMP_BLOCK * HEAD_DIM, HEAD_DIM), lambda b, sh: (sh // 2, 0, 0)),
                  pl.BlockSpec((1, 1, HEAD_DIM), lambda b, sh: (sh // 2, 0, 0)),
                  pl.BlockSpec((1, HEAD_DIM, HEAD_DIM), lambda b, sh: (sh // 2, 0, 0))],
        out_specs=pl.BlockSpec((1, 1, nblk, HEAD_DIM), lambda b, sh: (b, sh, 0, 0)),
        compiler_params=_cparams(("arbitrary", "arbitrary")),
        name="nsa_compress_prompt",
    )(p2, pe, w1.reshape(2, CMP_BLOCK * HEAD_DIM, HEAD_DIM), b1.reshape(2, 1, HEAD_DIM), w2)


TQ = 128
NSA_KSPAN = 512


def _nsa_prompt_body(slopes_ref, q_ref, gt_ref, ck_ref, cv_ref, sk_ref, sv_ref, wk_ref, wv_ref,
                     o_ref, selm_ref, *, seq_len):
    h = pl.program_id(1)
    qi = pl.program_id(2)
    t0 = qi * TQ
    nsb = seq_len // SEL_BLOCK
    ncb = seq_len // CMP_BLOCK // 2
    rows = NSA_GROUP * TQ

    qb = q_ref[...] * (HEAD_DIM ** -0.5)
    q4 = jnp.concatenate([qb[:, g * HEAD_DIM:(g + 1) * HEAD_DIM] for g in range(NSA_GROUP)], axis=0)
    q4b = q4.astype(BF16)
    row = lax.broadcasted_iota(I32, (rows, 1), 0)
    tq = t0 + (row & (TQ - 1))
    grow = row >> 7
    slope = jnp.zeros((rows, 1), F32)
    for g in range(NSA_GROUP):
        slope = jnp.where(grow == g, slopes_ref[h * NSA_GROUP + g], slope)

    ck = ck_ref[0, 0]
    cv = cv_ref[0, 0]
    midx = lax.broadcasted_iota(I32, (1, ncb), 1)
    logits = []
    valids = []
    for par in range(2):
        s_c = _mm3(q4, ck[par * ncb:(par + 1) * ncb], NT)
        cend = midx * (2 * CMP_BLOCK) + (CMP_BLOCK - 1) + par * CMP_BLOCK
        valid = cend <= tq
        dist = (tq - cend).astype(F32)
        logits.append(jnp.where(valid, s_c - slope * dist, NEG_INF))
        valids.append(valid)
    mx = jnp.maximum(jnp.max(logits[0], axis=1, keepdims=True), jnp.max(logits[1], axis=1, keepdims=True))
    e0 = jnp.exp(logits[0] - mx)
    e1 = jnp.exp(logits[1] - mx)
    den = jnp.sum(e0, axis=1, keepdims=True) + jnp.sum(e1, axis=1, keepdims=True)
    p0 = jnp.where(valids[0], e0 / den, 0.0)
    p1 = jnp.where(valids[1], e1 / den, 0.0)
    o_cmp = _mm1(p0, cv[0:ncb]) + _mm1(p1, cv[ncb:2 * ncb])
    pb = p0 + p1
    imp = pb[0:TQ]
    for g in range(1, NSA_GROUP):
        imp = imp + pb[g * TQ:(g + 1) * TQ]

    tq1 = t0 + lax.broadcasted_iota(I32, (TQ, 1), 0)
    cur = tq1 >> 6
    jb = lax.broadcasted_iota(I32, (1, nsb), 1)
    forced = (jb == 0) | (jb == cur) | (jb == cur - 1)
    score = jnp.where(jb <= cur, jnp.where(forced, FORCE_SCORE, imp), NEG_INF)
    cnt = jnp.zeros((TQ, nsb), I32)
    for i in range(nsb):
        si = score[:, i:i + 1]
        beats = (si > score) | ((si == score) & (jb > i))
        cnt = cnt + beats.astype(I32)
    sel = ((cnt < min(SEL_TOPN, nsb)) & (score > 0.5 * NEG_INF)).astype(BF16)
    kpos = lax.broadcasted_iota(I32, (nsb, seq_len), 1)
    kblk = lax.broadcasted_iota(I32, (nsb, seq_len), 0)
    expand = ((kpos >> 6) == kblk).astype(BF16)
    selk = _dot(sel, expand)

    kw = NSA_KSPAN
    t_row = t0 + lax.broadcasted_iota(I32, (TQ, kw), 0)
    key = lax.broadcasted_iota(I32, (TQ, kw), 1)
    for st in range(seq_len // kw):
        blk = (selk[:, st * kw:(st + 1) * kw] - 1.0) * (-NEG_INF)
        selm_ref[st] = blk + jnp.where(st * kw + key <= t_row, 0.0, NEG_INF)
    m_floor = 0.1 * NEG_INF

    def span_update(carry, k, v, bias, key0):
        m, l, acc = carry
        width = k.shape[0]
        alibi = slope * lax.broadcasted_iota(I32, (1, width), 1).astype(F32)
        x = _dot(q4b, k, NT) + alibi
        x = (x.reshape(NSA_GROUP, TQ, width) + bias[None]).reshape(rows, width)
        shift = slope * (key0 - tq).astype(F32)
        m_new = jnp.maximum(m, jnp.max(x, axis=1, keepdims=True) + shift)
        alpha = jnp.exp(m - m_new)
        p = jnp.exp(x - (m_new - shift))
        l = alpha * l + jnp.sum(p, axis=1, keepdims=True)
        acc = alpha * acc + _dot(p.astype(BF16), v)
        return m_new, l, acc

    def finish(carry):
        _, l, acc = carry
        return jnp.where(l > 0.0, acc / jnp.where(l > 0.0, l, 1.0), 0.0)

    init = (jnp.full((rows, 1), m_floor, F32), jnp.zeros((rows, 1), F32), jnp.zeros((rows, HEAD_DIM), F32))

    def sel_step(st, carry):
        start = pl.multiple_of(st * kw, kw)
        return span_update(carry, sk_ref[pl.ds(start, kw), :].astype(BF16), sv_ref[pl.ds(start, kw), :].astype(BF16),
                           selm_ref[st], start)

    o_sel = finish(lax.fori_loop(0, (t0 + TQ - 1) // kw + 1, sel_step, init))

    ww = min(WINDOW + TQ, seq_len)
    wstart = pl.multiple_of(jnp.maximum(t0 + TQ - ww, 0), TQ)
    wdelta = (t0 + lax.broadcasted_iota(I32, (TQ, ww), 0)) - (wstart + lax.broadcasted_iota(I32, (TQ, ww), 1))
    wbias = jnp.where((wdelta >= 0) & (wdelta < WINDOW), 0.0, NEG_INF)
    o_win = finish(span_update(init, wk_ref[pl.ds(wstart, ww), :].astype(BF16), wv_ref[pl.ds(wstart, ww), :].astype(BF16),
                               wbias, wstart))

    gs = jax.nn.sigmoid(gt_ref[...])
    for g in range(NSA_GROUP):
        def gate(c):
            a = gs[:, g * 3 + c:g * 3 + c + 1]
            b = gs[:, NSA_GROUP * 3 + g * 3 + c:NSA_GROUP * 3 + g * 3 + c + 1]
            return jnp.where(h == 0, a, b)

        sl = slice(g * TQ, (g + 1) * TQ)
        o_ref[:, g * HEAD_DIM:(g + 1) * HEAD_DIM] = (gate(0) * o_cmp[sl] + gate(1) * o_sel[sl]) + gate(2) * o_win[sl]


def _nsa_prompt(p2, ckv, slopes, nbatch, seq_len):
    nq = seq_len // TQ
    kvs0 = COL_KVS // LANE
    kvw0 = COL_KVW // LANE
    misc = COL_MISC // LANE
    nblk = seq_len // CMP_BLOCK
    full = lambda c0: pl.BlockSpec((seq_len, LANE), lambda b, h, q, c0=c0: (b, c0 + h))
    return pl.pallas_call(
        functools.partial(_nsa_prompt_body, seq_len=seq_len),
        out_shape=jax.ShapeDtypeStruct((nbatch * seq_len, NSA_WIDTH), F32),
        grid=(nbatch, NSA_KV_HEADS, nq),
        in_specs=[pl.BlockSpec(memory_space=pltpu.SMEM),
                  pl.BlockSpec((TQ, NSA_GROUP * HEAD_DIM), lambda b, h, q: (b * nq + q, h)),
                  pl.BlockSpec((TQ, LANE), lambda b, h, q: (b * nq + q, misc)),
                  pl.BlockSpec((1, 1, nblk, HEAD_DIM), lambda b, h, q: (b, h, 0, 0)),
                  pl.BlockSpec((1, 1, nblk, HEAD_DIM), lambda b, h, q: (b, 2 + h, 0, 0)),
                  full(kvs0), full(kvs0 + 2), full(kvw0), full(kvw0 + 2)],
        out_specs=pl.BlockSpec((TQ, NSA_GROUP * HEAD_DIM), lambda b, h, q: (b * nq + q, h)),
        scratch_shapes=[pltpu.VMEM((seq_len // NSA_KSPAN, TQ, NSA_KSPAN), F32)],
        compiler_params=_cparams(("arbitrary", "arbitrary", "arbitrary")),
        name="nsa_prompt",
    )(slopes, p2, p2, ckv, ckv, p2, p2, p2, p2)


GDN_CB = 8


def _gdn_chunk(seq_len):
    return min(GDN_CHUNK, seq_len)


def _bmm3(a, b, dims):
    ah, al = _split(a)
    bh, bl = _split(b)
    return _dot(ah, bh, dims) + (_dot(al, bh, dims) + _dot(ah, bl, dims))


GDN_HG = 2


def _gdn_body(alog_ref, dtb_ref, xq_ref, xk_ref, xv_ref, z_ref, gt_ref, cwq_ref, cwk_ref, cwv_ref,
              cbq_ref, cbk_ref, cbv_ref, s0_ref, gn_ref, o_ref, sout_ref,
              q_s, k_s, v_s, b_s, g_s, u_s, w_s, qg_s, kdt_s, qk_s, gl_s, oc_s, *, seq_len, valid):
    hg = pl.program_id(1)
    C = _gdn_chunk(seq_len)
    nc = seq_len // C
    cb = min(GDN_CB, nc)
    row = lax.broadcasted_iota(I32, (seq_len, 1), 0)
    lane = lax.broadcasted_iota(I32, (1, LANE), 1)
    pos = row & (C - 1)
    gt = gt_ref[...]

    def conv(x_ref, cw_ref, cb_ref, ls):
        x = x_ref[:, ls]
        cat = jnp.concatenate([cb_ref[0, :, ls], x], axis=0)
        w = cw_ref[:, ls]
        acc = pltpu.roll(cat, 3, 0)[SUBLANE:] * w[0:1]
        acc = acc + pltpu.roll(cat, 2, 0)[SUBLANE:] * w[1:2]
        acc = acc + pltpu.roll(cat, 1, 0)[SUBLANE:] * w[2:3]
        acc = acc + x * w[3:4]
        return jax.nn.silu(acc)

    def l2n(x):
        return x * lax.rsqrt(jnp.sum(x * x, axis=-1, keepdims=True) + NORM_EPS)

    ii = lax.broadcasted_iota(I32, (C, C), 0)
    jj = lax.broadcasted_iota(I32, (C, C), 1)
    tri = (ii >= jj)[None]
    strict = (ii > jj)[None]
    eye = (ii == jj).astype(F32)[None]
    ones_b = jnp.ones((cb, C, C), BF16)
    ident_b = jnp.broadcast_to((lax.broadcasted_iota(I32, (HEAD_DIM, HEAD_DIM), 0)
                                == lax.broadcasted_iota(I32, (HEAD_DIM, HEAD_DIM), 1)).astype(BF16)[None],
                               (cb, HEAD_DIM, HEAD_DIM))

    for hl in range(GDN_HG):
        hd = hg * GDN_HG + hl
        ls = slice(hl * HEAD_DIM, (hl + 1) * HEAD_DIM)
        q = l2n(conv(xq_ref, cwq_ref, cbq_ref, ls)) * (HEAD_DIM ** -0.5)
        k = l2n(conv(xk_ref, cwk_ref, cbk_ref, ls))
        v = conv(xv_ref, cwv_ref, cbv_ref, ls)
        bl = jnp.sum(jnp.where(lane == MISC_BL + hd, gt, 0.0), axis=1, keepdims=True)
        al = jnp.sum(jnp.where(lane == MISC_AL + hd, gt, 0.0), axis=1, keepdims=True)
        beta = jax.nn.sigmoid(bl)
        a_pos = jnp.exp(jnp.full((1, 1), alog_ref[hd], F32))
        g = -a_pos * jax.nn.softplus(al + dtb_ref[hd])
        if valid < seq_len:
            vm = row < valid
            q = jnp.where(vm, q, 0.0)
            k = jnp.where(vm, k, 0.0)
            v = jnp.where(vm, v, 0.0)
            beta = jnp.where(vm, beta, 0.0)
            g = jnp.where(vm, g, 0.0)
        gc = g
        sft = 1
        while sft < C:
            gc = gc + jnp.where(pos >= sft, pltpu.roll(gc, sft, 0), 0.0)
            sft *= 2
        q_s[...] = q
        k_s[...] = k
        v_s[...] = v
        b_s[...] = jnp.broadcast_to(beta, (seq_len, LANE))
        g_s[...] = jnp.broadcast_to(gc, (seq_len, LANE))

        def local(gi, carry, hl=hl):
            r0 = pl.multiple_of(gi * (cb * C), cb * C)
            sl = pl.ds(r0, cb * C)
            qc = q_s[sl, :].reshape(cb, C, HEAD_DIM)
            kc = k_s[sl, :].reshape(cb, C, HEAD_DIM)
            vc = v_s[sl, :].reshape(cb, C, HEAD_DIM)
            bc = b_s[sl, :].reshape(cb, C, LANE)
            gcc = g_s[sl, :].reshape(cb, C, LANE)
            gcol = gcc[:, :, 0:C]
            dg = gcol * eye
            d1 = dg.astype(BF16)
            r1 = dg - d1.astype(F32)
            d2 = r1.astype(BF16)
            d3 = (r1 - d2.astype(F32)).astype(BF16)
            grow = _dot(ones_b, d1, BNN) + (_dot(ones_b, d2, BNN) + _dot(ones_b, d3, BNN))
            diff = gcol - grow
            decay = jnp.where(tri, jnp.exp(jnp.where(tri, diff, 0.0)), 0.0)
            kb = kc * bc
            a = _dot(kb.astype(BF16), kc.astype(BF16), BNT) * jnp.where(strict, decay, 0.0)
            tinv = eye - a
            pw = a
            n = 2
            while n < C:
                pw = _bmm3(pw, pw, BNN)
                tinv = tinv + _bmm3(tinv, pw, BNN)
                n *= 2
            eg = jnp.exp(gcc)
            rhs = jnp.concatenate([vc * bc, kb * eg], axis=2)
            sol = _bmm3(tinv, rhs, BNN)
            u_s[hl, sl, :] = sol[:, :, 0:HEAD_DIM].reshape(cb * C, HEAD_DIM)
            w_s[hl, sl, :] = sol[:, :, HEAD_DIM:2 * HEAD_DIM].reshape(cb * C, HEAD_DIM)
            qk = _dot(qc.astype(BF16), kc.astype(BF16), BNT) * decay
            qk_s[hl, sl, :] = qk.reshape(cb * C, C)
            qg_s[hl, sl, :] = (qc * eg).reshape(cb * C, HEAD_DIM)
            glast = gcc[:, C - 1:C, :]
            kd = kc * jnp.exp(glast - gcc)
            kdt = _dot(ident_b, kd.astype(BF16), BNT)
            kdt_s[hl, pl.ds(pl.multiple_of(gi * (cb * HEAD_DIM), cb * HEAD_DIM), cb * HEAD_DIM), :] = (
                kdt.reshape(cb * HEAD_DIM, C))
            gl_s[hl, pl.ds(pl.multiple_of(gi * (cb * SUBLANE), cb * SUBLANE), cb * SUBLANE), :] = (
                jnp.broadcast_to(jnp.exp(glast), (cb, SUBLANE, LANE)).reshape(cb * SUBLANE, LANE))
            return carry

        lax.fori_loop(0, nc // cb, local, 0)

    def scan(c, states):
        sl = pl.ds(pl.multiple_of(c * C, C), C)
        new = []
        for hl in range(GDN_HG):
            s = states[hl]
            sb = s.astype(BF16)
            v_new = u_s[hl, sl, :] - _dot(w_s[hl, sl, :].astype(BF16), sb)
            o_c = (_dot(qg_s[hl, sl, :].astype(BF16), sb)
                   + _dot(qk_s[hl, sl, :].astype(BF16), v_new.astype(BF16)))
            oc_s[hl, sl, :] = o_c
            kdt = kdt_s[hl, pl.ds(pl.multiple_of(c * HEAD_DIM, HEAD_DIM), HEAD_DIM), :]
            gl = gl_s[hl, pl.ds(pl.multiple_of(c * SUBLANE, SUBLANE), 1), :]
            new.append(s * gl + _dot(kdt.astype(BF16), v_new.astype(BF16)))
        return tuple(new)

    s_fin = lax.fori_loop(0, nc, scan, tuple(s0_ref[0, hl] for hl in range(GDN_HG)))
    for hl in range(GDN_HG):
        ls = slice(hl * HEAD_DIM, (hl + 1) * HEAD_DIM)
        sout_ref[0, hl] = s_fin[hl]
        o = oc_s[hl]
        y = o * lax.rsqrt(jnp.mean(o * o, axis=-1, keepdims=True) + NORM_EPS) * gn_ref[...]
        o_ref[:, ls] = y * jax.nn.silu(z_ref[:, ls])


def _gdn(p2, conv_w, conv_buf8, s0, a_log, dt_bias, gn_w, nbatch, seq_len, valid):
    gw = GDN_HG * LANE
    q0 = COL_QKV // gw
    z0 = COL_Z // gw
    hgs = GDN_HEADS // GDN_HG
    misc = COL_MISC // LANE
    C = _gdn_chunk(seq_len)
    col = lambda c0: pl.BlockSpec((seq_len, gw), lambda b, h, c0=c0: (b, c0 + h))
    cw = lambda c0: pl.BlockSpec((CONV_WIDTH, gw), lambda b, h, c0=c0: (0, c0 + h))
    cbs = lambda c0: pl.BlockSpec((1, SUBLANE, gw), lambda b, h, c0=c0: (b, 0, c0 + h))
    smem = pl.BlockSpec(memory_space=pltpu.SMEM)
    nc = seq_len // C
    per_head = lambda rows, cols: pltpu.VMEM((GDN_HG, rows, cols), F32)
    return pl.pallas_call(
        functools.partial(_gdn_body, seq_len=seq_len, valid=valid),
        out_shape=(jax.ShapeDtypeStruct((nbatch * seq_len, GDN_WIDTH), F32),
                   jax.ShapeDtypeStruct((nbatch, GDN_HEADS, HEAD_DIM, HEAD_DIM), F32)),
        grid=(nbatch, hgs),
        in_specs=[smem, smem, col(q0), col(q0 + hgs), col(q0 + 2 * hgs), col(z0),
                  pl.BlockSpec((seq_len, LANE), lambda b, h: (b, misc)),
                  cw(0), cw(hgs), cw(2 * hgs), cbs(0), cbs(hgs), cbs(2 * hgs),
                  pl.BlockSpec((1, GDN_HG, HEAD_DIM, HEAD_DIM), lambda b, h: (b, h, 0, 0)),
                  pl.BlockSpec((1, HEAD_DIM), lambda b, h: (0, 0))],
        out_specs=(pl.BlockSpec((seq_len, gw), lambda b, h: (b, h)),
                   pl.BlockSpec((1, GDN_HG, HEAD_DIM, HEAD_DIM), lambda b, h: (b, h, 0, 0))),
        scratch_shapes=[pltpu.VMEM((seq_len, HEAD_DIM), F32)] * 3
        + [pltpu.VMEM((seq_len, LANE), F32)] * 2
        + [per_head(seq_len, HEAD_DIM)] * 3
        + [per_head(nc * HEAD_DIM, C),
           per_head(seq_len, C),
           per_head(nc * SUBLANE, LANE),
           per_head(seq_len, HEAD_DIM)],
        compiler_params=_cparams(("arbitrary", "arbitrary")),
        name="gdn",
    )(a_log, dt_bias, p2, p2, p2, p2, p2, conv_w, conv_w, conv_w, conv_buf8, conv_buf8, conv_buf8,
      s0, gn_w.reshape(1, HEAD_DIM))


def _out_proj_body(on_ref, og_ref, x_ref, w_ref, g1_ref, nw_ref, sc_ref, sh_ref, x1_ref, h2_ref, *, seq_len):
    mix = _dot(on_ref[...].astype(BF16), w_ref[0:NSA_WIDTH, :]) + _dot(og_ref[...].astype(BF16), w_ref[NSA_WIDTH:, :])
    x1 = x_ref[...] + _seq_rows(g1_ref, seq_len) * mix
    x1_ref[...] = x1
    y = x1 * lax.rsqrt(jnp.mean(x1 * x1, axis=-1, keepdims=True) + NORM_EPS)
    h2_ref[...] = (y * nw_ref[...]) * (1.0 + _seq_rows(sc_ref, seq_len)) + _seq_rows(sh_ref, seq_len)


def _out_proj(o_nsa, o_gdn, x, w_out_bf, g1, norm_w, sc, sh, seq_len, tm):
    n, d = x.shape
    mod = _seq_spec(seq_len, tm, d)
    return pl.pallas_call(
        functools.partial(_out_proj_body, seq_len=seq_len),
        out_shape=(jax.ShapeDtypeStruct((n, d), F32), jax.ShapeDtypeStruct((n, d), F32)),
        grid=(n // tm,),
        in_specs=[pl.BlockSpec((tm, NSA_WIDTH), lambda i: (i, 0)),
                  pl.BlockSpec((tm, GDN_WIDTH), lambda i: (i, 0)),
                  pl.BlockSpec((tm, d), lambda i: (i, 0)),
                  pl.BlockSpec((d, d), lambda i: (0, 0)),
                  mod,
                  pl.BlockSpec((1, d), lambda i: (0, 0)),
                  mod, mod],
        out_specs=(pl.BlockSpec((tm, d), lambda i: (i, 0)), pl.BlockSpec((tm, d), lambda i: (i, 0))),
        compiler_params=_cparams(("arbitrary",)),
        name="out_proj",
    )(o_nsa, o_gdn, x, w_out_bf, g1, norm_w.reshape(1, d), sc, sh)


def _take_top(src_ref, dst_ref):
    s = src_ref[...]
    for k in range(PEER_TOPK):
        m = jnp.max(s, axis=0, keepdims=True)
        s = jnp.where(s == m, NEG_BIG, s)
        dst_ref[k:k + 1, :] = m


PAIR_ROWS = ((0, 16, 16), (16, 8, 8), (24, 8, 5), (32, 8, 4), (40, 8, 3), (48, 8, 2), (56, 8, 2), (64, 8, 2))
PAIR_TAIL = 72
PAIR_TOTAL = 80


def _peer_sel_body(h_ref, wqh_ref, wql_ref, keys_ref, cnt_ref, al_ref, code_ref, be_ref,
                   sc_s, ta_s, tb_s, cand_s, tc_s, *, split_query):
    hh, hl = _split(h_ref[...])
    half = PEER_QDIM // 2
    tn = h_ref.shape[0]
    r8 = lax.broadcasted_iota(I32, (SUBLANE, tn), 0)
    for hd in range(PEER_HEADS):
        wh = wqh_ref[hd * PEER_QDIM:(hd + 1) * PEER_QDIM, :]
        wl = wql_ref[hd * PEER_QDIM:(hd + 1) * PEER_QDIM, :]
        qt = _dot(wh, hh, NT)
        if split_query:
            qt = qt + (_dot(wl, hh, NT) + _dot(wh, hl, NT))
        s0 = _mm3(keys_ref[0], qt[0:half])
        s1 = _mm3(keys_ref[1], qt[half:PEER_QDIM])
        sc_s[...] = s0
        _take_top(sc_s, ta_s)
        sc_s[...] = s1
        _take_top(sc_s, tb_s)
        a = ta_s[...]
        b = tb_s[...]
        for r, (row0, nrow, nval) in enumerate(PAIR_ROWS):
            pair = a[r:r + 1] + b[0:nrow]
            cand_s[row0:row0 + nrow, :] = pair if nval == nrow else jnp.where(r8 < nval, pair, NEG_BIG)
        cand_s[PAIR_TAIL:PAIR_TOTAL, :] = a[SUBLANE:2 * SUBLANE] + b[0:1]
        _take_top(cand_s, tc_s)
        tau = tc_s[PEER_TOPK - 1:PEER_TOPK, :]
        cand = cand_s[...]
        keep = cand >= tau
        zsum = jnp.sum(jnp.where(keep, jnp.exp(cand - cand[0:1]), 0.0), axis=0, keepdims=True)
        keepf = keep.astype(F32)
        cnt = jnp.zeros((PEER_NKEYS, tn), F32)
        for r, (row0, nrow, nval) in enumerate(PAIR_ROWS):
            cnt_r = jnp.sum(keepf[row0:row0 + nrow], axis=0, keepdims=True)
            cnt = jnp.where(s0 == a[r:r + 1], cnt_r, cnt)
        for r in range(SUBLANE, 2 * SUBLANE):
            cnt = jnp.where(s0 == a[r:r + 1], keepf[PAIR_TAIL + r - SUBLANE:PAIR_TAIL + r - SUBLANE + 1], cnt)
        code = jnp.zeros((PEER_NKEYS, tn), F32)
        for r in range(PEER_TOPK):
            code = code + (b[r:r + 1] > s1).astype(F32)
        cnt_ref[hd] = cnt
        al_ref[hd] = jnp.exp(s0 - a[0:1]) / zsum
        code_ref[hd] = code.astype(BF16)
        be_ref[hd] = jnp.exp(s1 - b[0:1]).astype(BF16)


def _peer_select(h2, wq_hi, wq_lo, keys, tn, split_query):
    n, d = h2.shape
    hk = jax.ShapeDtypeStruct((PEER_HEADS, PEER_NKEYS, n), F32)
    hk16 = jax.ShapeDtypeStruct((PEER_HEADS, PEER_NKEYS, n), BF16)
    blk = pl.BlockSpec((PEER_HEADS, PEER_NKEYS, tn), lambda i: (0, 0, i))
    return pl.pallas_call(
        functools.partial(_peer_sel_body, split_query=split_query),
        out_shape=(hk, hk, hk16, hk16),
        grid=(n // tn,),
        in_specs=[pl.BlockSpec((tn, d), lambda i: (i, 0)),
                  pl.BlockSpec((PEER_HEADS * PEER_QDIM, d), lambda i: (0, 0)),
                  pl.BlockSpec((PEER_HEADS * PEER_QDIM, d), lambda i: (0, 0)),
                  pl.BlockSpec((2, PEER_NKEYS, PEER_QDIM // 2), lambda i: (0, 0, 0))],
        out_specs=(blk, blk, blk, blk),
        scratch_shapes=[pltpu.VMEM((PEER_NKEYS, tn), F32), pltpu.VMEM((PEER_TOPK, tn), F32),
                        pltpu.VMEM((PEER_TOPK, tn), F32), pltpu.VMEM((PAIR_TOTAL, tn), F32),
                        pltpu.VMEM((PEER_TOPK, tn), F32)],
        compiler_params=_cparams(("arbitrary",)),
        name="peer_select",
    )(h2, wq_hi, wq_lo, keys)


PEER_TE = 1024
GELU_C0 = math.sqrt(2.0 / math.pi)
GELU_C1 = 0.044715 * math.sqrt(2.0 / math.pi)


def _gelu_tanh(x):
    hx = 0.5 * x
    return hx + hx * jnp.tanh(x * (GELU_C0 + GELU_C1 * (x * x)))


def _peer_mm_body(h_ref, u_ref, vt_ref, cnt_ref, al_ref, code_ref, be_ref, o_ref, at_s, p_s):
    e = pl.program_id(1)

    @pl.when(e == 0)
    def _():
        o_ref[...] = jnp.zeros_like(o_ref)

    at_s[...] = _dot(u_ref[...], h_ref[...], NT)
    tn = h_ref.shape[0]
    for ii in range(PEER_TE // PEER_NKEYS):
        rs = slice(ii * PEER_NKEYS, (ii + 1) * PEER_NKEYS)
        for ck in range(tn // LANE):
            cs = slice(ck * LANE, (ck + 1) * LANE)
            w = jnp.zeros((PEER_NKEYS, LANE), BF16)
            for hd in range(PEER_HEADS):
                cnt = cnt_ref[hd, ii:ii + 1, cs].astype(BF16)
                kept = jnp.where(code_ref[hd, :, cs] < cnt, be_ref[hd, :, cs], jnp.zeros((), BF16))
                w = w + al_ref[hd, ii:ii + 1, cs].astype(BF16) * kept
            p_s[rs, cs] = (w.astype(F32) * _gelu_tanh(at_s[rs, cs])).astype(BF16)
    o_ref[...] += _dot(vt_ref[...], p_s[...])


def _peer_mm(h2b, u_bf, vt_bf, cnt, al, code, be, tn):
    n, d = h2b.shape
    ne = u_bf.shape[0]
    rows_i = PEER_TE // PEER_NKEYS
    sel_i = pl.BlockSpec((PEER_HEADS, rows_i, tn), lambda i, e: (0, e, i))
    sel_j = pl.BlockSpec((PEER_HEADS, PEER_NKEYS, tn), lambda i, e: (0, 0, i))
    return pl.pallas_call(
        _peer_mm_body,
        out_shape=jax.ShapeDtypeStruct((d, n), F32),
        grid=(n // tn, ne // PEER_TE),
        in_specs=[pl.BlockSpec((tn, d), lambda i, e: (i, 0)),
                  pl.BlockSpec((PEER_TE, d), lambda i, e: (e, 0)),
                  pl.BlockSpec((d, PEER_TE), lambda i, e: (0, e)),
                  sel_i, sel_i, sel_j, sel_j],
        out_specs=pl.BlockSpec((d, tn), lambda i, e: (0, i)),
        scratch_shapes=[pltpu.VMEM((PEER_TE, tn), F32), pltpu.VMEM((PEER_TE, tn), BF16)],
        compiler_params=_cparams(("arbitrary", "arbitrary")),
        name="peer_experts",
    )(h2b, u_bf, vt_bf, cnt, al, code, be)


def _final_body(x1_ref, pt_ref, g2_ref, fw_ref, y_ref, *, seq_len):
    x2 = x1_ref[...] + _seq_rows(g2_ref, seq_len) * pt_ref[...].T
    y_ref[...] = x2 * lax.rsqrt(jnp.mean(x2 * x2, axis=-1, keepdims=True) + NORM_EPS) * fw_ref[...]


def _final(x1, peer_t, g2, fw, seq_len, tm):
    n, d = x1.shape
    g2_spec = _seq_spec(seq_len, tm, d)
    return pl.pallas_call(
        functools.partial(_final_body, seq_len=seq_len),
        out_shape=jax.ShapeDtypeStruct((n, d), F32),
        grid=(n // tm,),
        in_specs=[pl.BlockSpec((tm, d), lambda i: (i, 0)),
                  pl.BlockSpec((d, tm), lambda i: (0, i)),
                  g2_spec,
                  pl.BlockSpec((1, d), lambda i: (0, 0))],
        out_specs=pl.BlockSpec((tm, d), lambda i: (i, 0)),
        compiler_params=_cparams(("arbitrary",)),
        name="final_norm",
    )(x1, peer_t, g2, fw.reshape(1, d))


SROWS = 64
SROWS_TAIL = 16
CMP_PAGES = 16
KV_COMP = 2 * NSA_KV_HEADS
TOKEN_ROWS = KV_COMP
PAGE_VROWS = PAGE_SIZE * TOKEN_ROWS
CMP_PAIRS = CMP_BLOCK * TOKEN_ROWS // SUBLANE
BLOCK_VROWS = CMP_BLOCK * TOKEN_ROWS


def _compress_sample_body(pt_ref, *refs):
    pages = refs[:CMP_PAGES]
    pe_ref, w1_ref, b1_ref, w2_ref, o_ref, acc_s = refs[CMP_PAGES:]
    bpp = PAGE_SIZE // CMP_BLOCK
    nrow = CMP_PAGES * bpp * SUBLANE
    nh = CMP_PAGES * bpp // 2
    acc = jnp.zeros((nrow, KV_COMP * HEAD_DIM), F32)
    for q in range(CMP_PAIRS):
        parts = [r[0, n * BLOCK_VROWS + q * SUBLANE:n * BLOCK_VROWS + (q + 1) * SUBLANE, :]
                 for r in pages for n in range(bpp)]
        lhs = jnp.concatenate(parts, axis=0) + jnp.concatenate([pe_ref[q]] * (nrow // SUBLANE), axis=0)
        acc = acc + _dot(lhs.astype(BF16), w1_ref[q])
    rid = lax.broadcasted_iota(I32, (nrow, 1), 0) & (SUBLANE - 1)
    want = ((rid & (KV_COMP - 1)) >> 1) * 2 + (rid >> 2)
    picked = acc[:, 0:HEAD_DIM]
    for blk in range(1, KV_COMP):
        picked = jnp.where(want == blk, acc[:, blk * HEAD_DIM:(blk + 1) * HEAD_DIM], picked)
    acc_s[...] = picked + pltpu.roll(picked, nrow - KV_COMP, 0)
    for c in range(KV_COMP):
        s = c // NSA_KV_HEADS
        hid_e = acc_s[pl.ds(c, nh, stride=2 * SUBLANE), :]
        hid_o = acc_s[pl.ds(SUBLANE + c, nh, stride=2 * SUBLANE), :]
        hid = jax.nn.gelu(jnp.concatenate([hid_e, hid_o], axis=0) + b1_ref[s])
        out = _mm1(hid, w2_ref[s])
        o_ref[0, c, 0] = out[0:nh]
        o_ref[0, c, 1] = out[nh:2 * nh]


def _compress_sample(cache, page_table, pe, w1, b1, w2):
    nbs, n_pages = page_table.shape
    n_phys = cache.shape[0]
    cache_v = cache.reshape(n_phys, PAGE_VROWS, HEAD_DIM)
    ngrp = n_pages // CMP_PAGES
    nblk_half = n_pages * PAGE_SIZE // CMP_BLOCK // 2
    nh = CMP_PAGES * (PAGE_SIZE // CMP_BLOCK) // 2
    pe_q = jnp.transpose(pe.reshape(2, CMP_PAIRS, 2, HEAD_DIM), (1, 2, 0, 3))
    pe_q = jnp.broadcast_to(pe_q[:, :, :, None, :], (CMP_PAIRS, 2, 2, NSA_KV_HEADS, HEAD_DIM))
    pe_q = pe_q.reshape(CMP_PAIRS, SUBLANE, HEAD_DIM)
    w1_q = jnp.transpose(w1.reshape(2, CMP_PAIRS, 2, HEAD_DIM, HEAD_DIM), (1, 3, 0, 2, 4))
    w1_q = w1_q.reshape(CMP_PAIRS, HEAD_DIM, KV_COMP * HEAD_DIM).astype(BF16)
    page_spec = lambda k: pl.BlockSpec((1, PAGE_VROWS, HEAD_DIM),
                                       lambda b, g, pt, k=k: (pt[b * n_pages + g * CMP_PAGES + k], 0, 0))
    const = lambda shape: pl.BlockSpec(shape, lambda b, g, pt: (0,) * len(shape))
    grid_spec = pltpu.PrefetchScalarGridSpec(
        num_scalar_prefetch=1,
        grid=(nbs, ngrp),
        in_specs=[page_spec(k) for k in range(CMP_PAGES)]
        + [const((CMP_PAIRS, SUBLANE, HEAD_DIM)), const((CMP_PAIRS, HEAD_DIM, KV_COMP * HEAD_DIM)),
           const((2, 1, HEAD_DIM)), const((2, HEAD_DIM, HEAD_DIM))],
        out_specs=pl.BlockSpec((1, KV_COMP, 2, nh, HEAD_DIM), lambda b, g, pt: (b, 0, 0, g, 0)),
        scratch_shapes=[pltpu.VMEM((CMP_PAGES * (PAGE_SIZE // CMP_BLOCK) * SUBLANE, HEAD_DIM), F32)])
    return pl.pallas_call(
        _compress_sample_body,
        out_shape=jax.ShapeDtypeStruct((nbs, KV_COMP, 2, nblk_half, HEAD_DIM), F32),
        grid_spec=grid_spec,
        compiler_params=_cparams(("arbitrary", "arbitrary")),
        name="nsa_compress_sample",
    )(page_table.reshape(-1), *([cache_v] * CMP_PAGES), pe_q, w1_q, b1.reshape(2, 1, HEAD_DIM), w2)


def _rows_from_lanes(row, ngrp):
    ridx = lax.broadcasted_iota(I32, (SUBLANE, HEAD_DIM), 0)
    out = jnp.zeros((SUBLANE, HEAD_DIM), F32)
    for g in range(ngrp):
        out = jnp.where(ridx == g, jnp.broadcast_to(row[:, g * HEAD_DIM:(g + 1) * HEAD_DIM], (SUBLANE, HEAD_DIM)), out)
    return out


def _slope_rows(slopes_ref, h):
    ridx = lax.broadcasted_iota(I32, (SUBLANE, 1), 0)
    slope = jnp.zeros((SUBLANE, 1), F32)
    for g in range(NSA_GROUP):
        slope = jnp.where(ridx == g, slopes_ref[h * NSA_GROUP + g], slope)
    return slope


def _nsa_sample_select_body(slopes_ref, q_ref, ckv_ref, ocmp_ref, idx_ref, *, past):
    ncb = past // CMP_BLOCK // 2
    nblk = past // SEL_BLOCK
    ridx = lax.broadcasted_iota(I32, (SUBLANE, 1), 0)
    midx = lax.broadcasted_iota(I32, (1, ncb), 1)
    jb = lax.broadcasted_iota(I32, (1, nblk), 1)
    slot = lax.broadcasted_iota(I32, (1, SEL_TOPN), 1)
    for h in range(NSA_KV_HEADS):
        q8 = _rows_from_lanes(q_ref[0:1, h * NSA_GROUP * HEAD_DIM:(h + 1) * NSA_GROUP * HEAD_DIM], NSA_GROUP)
        q8 = q8 * (HEAD_DIM ** -0.5)
        slope = _slope_rows(slopes_ref, h)
        lg = []
        for par in range(2):
            s_c = _mm3(q8, ckv_ref[0, h, par], NT)
            cend = midx * (2 * CMP_BLOCK) + (CMP_BLOCK - 1) + par * CMP_BLOCK
            lg.append(s_c - slope * (past - cend).astype(F32))
        mx = jnp.maximum(jnp.max(lg[0], axis=1, keepdims=True), jnp.max(lg[1], axis=1, keepdims=True))
        e0 = jnp.exp(lg[0] - mx)
        e1 = jnp.exp(lg[1] - mx)
        den = jnp.sum(e0, axis=1, keepdims=True) + jnp.sum(e1, axis=1, keepdims=True)
        p0 = e0 / den
        p1 = e1 / den
        o_cmp = _mm1(p0, ckv_ref[0, 2 + h, 0]) + _mm1(p1, ckv_ref[0, 2 + h, 1])
        ocmp_ref[0, h * NSA_GROUP:(h + 1) * NSA_GROUP, :] = o_cmp[0:NSA_GROUP]
        imp = jnp.sum(jnp.where(ridx < NSA_GROUP, p0 + p1, 0.0), axis=0, keepdims=True)
        forced = (jb == 0) | (jb == nblk - 1)
        score = jnp.where(forced, FORCE_SCORE, imp)
        picks = jnp.full((1, SEL_TOPN), nblk, I32)
        for k in range(SEL_TOPN - 1):
            m = jnp.max(score, axis=1, keepdims=True)
            first = jnp.min(jnp.where(score == m, jb, nblk), axis=1, keepdims=True)
            score = jnp.where(jb == first, NEG_BIG, score)
            picks = jnp.where(slot == k, first, picks)
        idx_ref[0, h:h + 1, :] = picks


def _nsa_sample_select(p2s, ckv_s, slopes, nbs, past):
    nblk_half = past // CMP_BLOCK // 2
    return pl.pallas_call(
        functools.partial(_nsa_sample_select_body, past=past),
        out_shape=(jax.ShapeDtypeStruct((nbs, NSA_HEADS, HEAD_DIM), F32),
                   jax.ShapeDtypeStruct((nbs, NSA_KV_HEADS, SEL_TOPN), I32)),
        grid=(nbs,),
        in_specs=[pl.BlockSpec(memory_space=pltpu.SMEM),
                  pl.BlockSpec((SUBLANE, NSA_WIDTH), lambda b: (b * (SROWS // SUBLANE), 0)),
                  pl.BlockSpec((1, 4, 2, nblk_half, HEAD_DIM), lambda b: (b, 0, 0, 0, 0))],
        out_specs=(pl.BlockSpec((1, NSA_HEADS, HEAD_DIM), lambda b: (b, 0, 0)),
                   pl.BlockSpec((1, NSA_KV_HEADS, SEL_TOPN), lambda b: (b, 0, 0))),
        compiler_params=_cparams(("arbitrary",)),
        name="nsa_sample_select",
    )(slopes, p2s, ckv_s)


NSEL_PAST = SEL_TOPN - 1


def _nsa_sample_attend_body(idx_ref, pt_ref, slopes_ref, q_ref, gt_ref, ocmp_ref, skn_ref, svn_ref, wkn_ref, wvn_ref,
                            *refs, past, n_pages):
    blocks = refs[:NSEL_PAST]
    wc_ref, o_ref = refs[NSEL_PAST:]
    b = pl.program_id(0)
    h = pl.program_id(1)
    q8 = _rows_from_lanes(q_ref[0:1, :], NSA_GROUP) * (HEAD_DIM ** -0.5)
    q8b = q8.astype(BF16)
    slope = _slope_rows(slopes_ref, h)

    def attend_with_new(k_old, v_old, lg_bias, k_new, v_new):
        lg = _dot(q8b, k_old.astype(BF16), NT) + lg_bias
        s_new = jnp.sum(q8 * k_new, axis=1, keepdims=True)
        m = jnp.maximum(jnp.max(lg, axis=1, keepdims=True), s_new)
        p = jnp.exp(lg - m)
        pn = jnp.exp(s_new - m)
        den = jnp.sum(p, axis=1, keepdims=True) + pn
        return (_dot(p.astype(BF16), v_old.astype(BF16)) + pn * v_new) / den

    k_sel = jnp.concatenate([r[0, pl.ds(h, SEL_BLOCK, stride=TOKEN_ROWS), :] for r in blocks], axis=0)
    v_sel = jnp.concatenate([r[0, pl.ds(NSA_KV_HEADS + h, SEL_BLOCK, stride=TOKEN_ROWS), :] for r in blocks], axis=0)
    lane = lax.broadcasted_iota(I32, (1, NSEL_PAST * SEL_BLOCK), 1)
    slot = lane >> (SEL_BLOCK.bit_length() - 1)
    blk = jnp.zeros((1, NSEL_PAST * SEL_BLOCK), I32)
    for j in range(NSEL_PAST):
        blk = jnp.where(slot == j, idx_ref[(b * NSA_KV_HEADS + h) * SEL_TOPN + j], blk)
    dist = (past - (blk * SEL_BLOCK + (lane & (SEL_BLOCK - 1)))).astype(F32)
    o_sel = attend_with_new(k_sel, v_sel, -slope * dist, skn_ref[0:1, :], svn_ref[0:1, :])

    nw = wc_ref.shape[1] // TOKEN_ROWS
    r = lax.broadcasted_iota(I32, (1, nw), 1)
    delta = nw - r
    wmask = (delta < WINDOW) & (past - delta >= 0)
    k_win = wc_ref[0, pl.ds(h, nw, stride=TOKEN_ROWS), :]
    v_win = wc_ref[0, pl.ds(NSA_KV_HEADS + h, nw, stride=TOKEN_ROWS), :]
    o_win = attend_with_new(k_win, v_win, jnp.where(wmask, -slope * delta.astype(F32), NEG_INF),
                            wkn_ref[0:1, :], wvn_ref[0:1, :])

    gs = jax.nn.sigmoid(gt_ref[0:1, :])
    o_ref[...] = jnp.zeros_like(o_ref)
    for g in range(NSA_GROUP):
        def gate(c):
            a = gs[:, g * 3 + c:g * 3 + c + 1]
            bb = gs[:, NSA_GROUP * 3 + g * 3 + c:NSA_GROUP * 3 + g * 3 + c + 1]
            return jnp.where(h == 0, a, bb)

        oc = jnp.where(h == 0, ocmp_ref[0, g:g + 1, :], ocmp_ref[0, NSA_GROUP + g:NSA_GROUP + g + 1, :])
        og = (gate(0) * oc + gate(1) * o_sel[g:g + 1]) + gate(2) * o_win[g:g + 1]
        o_ref[0:1, g * HEAD_DIM:(g + 1) * HEAD_DIM] = og


def _nsa_sample_attend(p2s, ocmp, idx, page_table, cache_sel, win_state, slopes, nbs, past):
    n_pages = page_table.shape[1]
    n_phys = cache_sel.shape[0]
    nw = win_state.shape[1]
    halves = PAGE_SIZE // SEL_BLOCK
    sel_v = cache_sel.reshape(n_phys * halves, SEL_BLOCK * TOKEN_ROWS, HEAD_DIM)
    win_v = win_state.reshape(nbs, nw * TOKEN_ROWS, HEAD_DIM)
    rb = SROWS // SUBLANE
    kvs0 = COL_KVS // LANE
    kvw0 = COL_KVW // LANE
    misc = COL_MISC // LANE

    def sel_spec(j):
        def index(b, h, idx, pt):
            blk = idx[(b * NSA_KV_HEADS + h) * SEL_TOPN + j]
            page = pt[b * n_pages + blk // halves]
            return (page * halves + blk % halves, 0, 0)
        return pl.BlockSpec((1, SEL_BLOCK * TOKEN_ROWS, HEAD_DIM), index)

    new = lambda c0: pl.BlockSpec((SUBLANE, LANE), lambda b, h, idx, pt, c0=c0: (b * rb, c0 + h))
    grid_spec = pltpu.PrefetchScalarGridSpec(
        num_scalar_prefetch=2,
        grid=(nbs, NSA_KV_HEADS),
        in_specs=[pl.BlockSpec(memory_space=pltpu.SMEM),
                  pl.BlockSpec((SUBLANE, NSA_GROUP * HEAD_DIM), lambda b, h, idx, pt: (b * rb, h)),
                  pl.BlockSpec((SUBLANE, LANE), lambda b, h, idx, pt: (b * rb, misc)),
                  pl.BlockSpec((1, NSA_HEADS, HEAD_DIM), lambda b, h, idx, pt: (b, 0, 0)),
                  new(kvs0), new(kvs0 + 2), new(kvw0), new(kvw0 + 2)]
        + [sel_spec(j) for j in range(NSEL_PAST)]
        + [pl.BlockSpec((1, nw * TOKEN_ROWS, HEAD_DIM), lambda b, h, idx, pt: (b, 0, 0))],
        out_specs=pl.BlockSpec((SROWS, NSA_GROUP * HEAD_DIM), lambda b, h, idx, pt: (b, h)))
    return pl.pallas_call(
        functools.partial(_nsa_sample_attend_body, past=past, n_pages=n_pages),
        out_shape=jax.ShapeDtypeStruct((nbs * SROWS, NSA_WIDTH), F32),
        grid_spec=grid_spec,
        compiler_params=_cparams(("arbitrary", "arbitrary")),
        name="nsa_sample_attend",
    )(idx.reshape(-1), page_table.reshape(-1), slopes, p2s, p2s, ocmp, p2s, p2s, p2s, p2s,
      *([sel_v] * NSEL_PAST), win_v)


def _perm_w_in(w_in):
    d = w_in.shape[0]
    g0 = NSA_WIDTH + 3 * 2 * KV_WIDTH
    q0 = g0 + 3 * NSA_HEADS
    b0 = q0 + 3 * GDN_WIDTH + GDN_WIDTH
    pad = jnp.zeros((d, N_PROJ - (b0 + 2 * GDN_HEADS)), w_in.dtype)
    w2 = jnp.concatenate([w_in[:, :g0], w_in[:, q0:b0], w_in[:, g0:q0], w_in[:, b0:], pad], axis=1)
    return w2.astype(BF16)


def _split_bf16(w):
    hi = w.astype(BF16)
    lo = (w - hi.astype(F32)).astype(BF16)
    return hi, lo


def _mixer_tail(x, o_nsa, o_gdn, w_out_bf, g1, norm_ffn, sc2, sh2, g2, wq_hi, wq_lo, keys, u_bf, vt_bf,
                final_norm, seq_len, tm, tn_sel, tn_mm, tm_final):
    x1, h2 = _out_proj(o_nsa, o_gdn, x, w_out_bf, g1, norm_ffn, sc2, sh2, seq_len, tm)
    cnt, al, code, be = _peer_select(h2, wq_hi, wq_lo, keys, tn_sel, split_query=h2.shape[0] <= tn_sel)
    peer_t = _peer_mm(h2.astype(BF16), u_bf, vt_bf, cnt, al, code, be, tn_mm)
    return _final(x1, peer_t, g2, final_norm, seq_len, tm_final)


def kernel(x_prompt, x_sample, cache_cmp_kv, cache_sel_kv, state_win_kv, state_conv, state_gdn, page_table,
           c_prompt, c_sample, w_ada, b_ada, norm_mix, norm_ffn, w_in, cmp_pe, cmp_w1, cmp_b1, cmp_w2, conv_w,
           gdn_a_log, gdn_dt_bias, gdn_norm, w_out, peer_wq, peer_keys, peer_u, peer_v, final_norm):
    nb, seq, d = x_prompt.shape
    nbs, dec_seq, _ = x_sample.shape
    assert w_in.shape[0] == 1 and dec_seq == 1, "single layer, single decode token"
    past = page_table.shape[1] * PAGE_SIZE
    slopes = 2.0 ** (-8.0 * jnp.arange(1, NSA_HEADS + 1, dtype=F32) / NSA_HEADS)
    rows_c = 16
    c_all = jnp.concatenate([c_prompt, c_sample, jnp.zeros((rows_c - nb - nbs, d), F32)], axis=0)
    mod = _adaln(c_all, w_ada[0], b_ada[0]).reshape(rows_c, 6, d)
    pm = lambda k: mod[0:nb, k][:, None, :]
    sm = lambda k: mod[nb:nb + nbs, k][:, None, :]
    w2 = _perm_w_in(w_in[0])
    w_out_bf = w_out[0].astype(BF16)
    wq_hi, wq_lo = _split_bf16(peer_wq[0].T)
    u_bf = peer_u[0].astype(BF16)
    vt_bf = peer_v[0].T.astype(BF16)
    cw = (cmp_pe[0], cmp_w1[0], cmp_b1[0], cmp_w2[0])
    kvw = 2 * KV_WIDTH

    xp = x_prompt.reshape(nb * seq, d)
    p2 = _proj_in(xp, norm_mix[0], pm(1), pm(0), w2, seq, 1024)
    ckv = _compress_prompt(p2, *cw, nb, seq)
    o_nsa = _nsa_prompt(p2, ckv, slopes, nb, seq)
    conv0 = jnp.zeros((nb, SUBLANE, 3 * GDN_WIDTH), F32)
    s0 = jnp.zeros((nb, GDN_HEADS, HEAD_DIM, HEAD_DIM), F32)
    o_gdn, gdn_p = _gdn(p2, conv_w[0], conv0, s0, gdn_a_log[0], gdn_dt_bias[0], gdn_norm[0], nb, seq, seq)
    y_prompt = _mixer_tail(xp, o_nsa, o_gdn, w_out_bf, pm(2), norm_ffn[0], pm(4), pm(3), pm(5), wq_hi, wq_lo,
                           peer_keys[0], u_bf, vt_bf, final_norm, seq, 512, 256, 512, 512).reshape(nb, seq, d)
    p3 = p2.reshape(nb, seq, N_PROJ)
    keep = min(WINDOW, seq)
    kv5 = lambda a: a.reshape(a.shape[0], a.shape[1], 2, NSA_KV_HEADS, HEAD_DIM)
    cmp_p = kv5(p3[:, :, COL_KVC:COL_KVC + kvw])
    sel_p = kv5(p3[:, :, COL_KVS:COL_KVS + kvw])
    win_p = kv5(p3[:, seq - keep:, COL_KVW:COL_KVW + kvw])
    conv_p = p3[:, seq - (CONV_WIDTH - 1):, COL_QKV:COL_QKV + 3 * GDN_WIDTH]

    xs = jnp.pad(x_sample, ((0, 0), (0, SROWS - dec_seq), (0, 0))).reshape(nbs * SROWS, d)
    p2s = _proj_in(xs, norm_mix[0], sm(1), sm(0), w2, SROWS, nbs * SROWS)
    ckv_s = _compress_sample(cache_cmp_kv[0], page_table, *cw)
    ocmp, idx = _nsa_sample_select(p2s, ckv_s, slopes, nbs, past)
    o_nsa_s = _nsa_sample_attend(p2s, ocmp, idx, page_table, cache_sel_kv[0], state_win_kv[0], slopes, nbs, past)
    conv_buf = jnp.pad(state_conv[0], ((0, 0), (SUBLANE - (CONV_WIDTH - 1), 0), (0, 0)))
    o_gdn_s, gdn_s = _gdn(p2s, conv_w[0], conv_buf, state_gdn[0], gdn_a_log[0], gdn_dt_bias[0], gdn_norm[0],
                          nbs, SROWS, dec_seq)
    head_rows = lambda a: a.reshape(nbs, SROWS, a.shape[-1])[:, :SROWS_TAIL].reshape(nbs * SROWS_TAIL, a.shape[-1])
    nt = nbs * SROWS_TAIL
    ys = _mixer_tail(head_rows(xs), head_rows(o_nsa_s), head_rows(o_gdn_s), w_out_bf, sm(2), norm_ffn[0], sm(4),
                     sm(3), sm(5), wq_hi, wq_lo, peer_keys[0], u_bf, vt_bf, final_norm, SROWS_TAIL, nt, nt, nt, nt)
    y_sample = ys.reshape(nbs, SROWS_TAIL, d)[:, 0:dec_seq]
    p3s = p2s.reshape(nbs, SROWS, N_PROJ)[:, 0:dec_seq]
    cmp_s = kv5(p3s[:, :, COL_KVC:COL_KVC + kvw])
    sel_s = kv5(p3s[:, :, COL_KVS:COL_KVS + kvw])
    win_new = kv5(p3s[:, :, COL_KVW:COL_KVW + kvw])
    win_s = jnp.concatenate([state_win_kv[0], win_new], axis=1)[:, dec_seq:]
    conv_s = jnp.concatenate([state_conv[0], p3s[:, :, COL_QKV:COL_QKV + 3 * GDN_WIDTH]], axis=1)[:, dec_seq:]

    return (y_prompt, y_sample, cmp_p[None], cmp_s[None], sel_p[None], sel_s[None], win_p[None], win_s[None],
            conv_p[None], conv_s[None], gdn_p[None], gdn_s[None])
```

```python
import functools
import math

import jax
import jax.numpy as jnp
from jax import lax
from jax.experimental import pallas as pl
from jax.experimental.pallas import tpu as pltpu

F32 = jnp.float32
BF16 = jnp.bfloat16
I32 = jnp.int32

D_MODEL = 2048
HEAD_DIM = 128
NSA_WIDTH = 1024
GDN_WIDTH = 1024
NSA_HEADS = 8
NSA_KV_HEADS = 2
NSA_GROUP = 4
KV_WIDTH = 256
CMP_BLOCK = 32
SEL_BLOCK = 64
SEL_TOPN = 16
WINDOW = 512
GDN_HEADS = 8
CONV_WIDTH = 4
GDN_CHUNK = 64
PEER_HEADS = 8
PEER_NKEYS = 128
PEER_QDIM = 256
PEER_TOPK = 16
PAGE_SIZE = 128
NORM_EPS = 1e-6
NEG_INF = -1e30
FORCE_SCORE = 1e4
NEG_BIG = -3.0e38

LANE = 128
SUBLANE = 8
VMEM_LIMIT = 56 * 1024 * 1024

COL_Q = 0
COL_KVC = 1024
COL_KVS = 1536
COL_KVW = 2048
COL_QKV = 2560
COL_Z = 5632
COL_MISC = 6656
N_PROJ = 6912
MISC_BL = 24
MISC_AL = 32

NN = (((1,), (0,)), ((), ()))
NT = (((1,), (1,)), ((), ()))
BNN = (((2,), (1,)), ((0,), (0,)))
BNT = (((2,), (2,)), ((0,), (0,)))


def _dot(a, b, dims=NN):
    return lax.dot_general(a, b, dims, preferred_element_type=F32)


def _split(x):
    hi = x.astype(BF16)
    lo = (x - hi.astype(F32)).astype(BF16)
    return hi, lo


def _mm1(a, b, dims=NN):
    return _dot(a.astype(BF16), b.astype(BF16), dims)


def _mm3(a, b, dims=NN):
    ah, al = _split(a)
    bh, bl = _split(b)
    return _dot(ah, bh, dims) + (_dot(al, bh, dims) + _dot(ah, bl, dims))


def _cparams(sem, vmem=VMEM_LIMIT):
    return pltpu.CompilerParams(dimension_semantics=sem, vmem_limit_bytes=vmem)


def _adaln_body(c_ref, w_ref, b_ref, o_ref):
    a = jax.nn.silu(c_ref[...])
    o_ref[...] = _mm3(a, w_ref[...]) + b_ref[...]


def _adaln(c_all, w, b):
    rows, d = c_all.shape
    n = w.shape[1]
    tn = 1024
    return pl.pallas_call(
        _adaln_body,
        out_shape=jax.ShapeDtypeStruct((rows, n), F32),
        grid=(n // tn,),
        in_specs=[pl.BlockSpec((rows, d), lambda j: (0, 0)),
                  pl.BlockSpec((d, tn), lambda j: (0, j)),
                  pl.BlockSpec((1, tn), lambda j: (0, j))],
        out_specs=pl.BlockSpec((rows, tn), lambda j: (0, j)),
        compiler_params=_cparams(("arbitrary",)),
        name="adaln",
    )(c_all, w, b.reshape(1, n))


def _seq_rows(m_ref, seq_len):
    nseq = m_ref.shape[0]
    if nseq == 1:
        return m_ref[0]
    return jnp.concatenate([jnp.broadcast_to(m_ref[r], (seq_len, m_ref.shape[2])) for r in range(nseq)], axis=0)


def _seq_spec(seq_len, tm, d):
    if tm <= seq_len:
        tpb = seq_len // tm
        return pl.BlockSpec((1, 1, d), lambda i, *_: (i // tpb, 0, 0))
    return pl.BlockSpec((tm // seq_len, 1, d), lambda i, *_: (i, 0, 0))


def _proj_in_body(x_ref, nw_ref, sc_ref, sh_ref, w_ref, o_ref, h_ref, *, seq_len):
    @pl.when(pl.program_id(1) == 0)
    def _():
        x = x_ref[...]
        y = x * lax.rsqrt(jnp.mean(x * x, axis=-1, keepdims=True) + NORM_EPS)
        h = (y * nw_ref[...]) * (1.0 + _seq_rows(sc_ref, seq_len)) + _seq_rows(sh_ref, seq_len)
        h_ref[...] = h.astype(BF16)

    o_ref[...] = _dot(h_ref[...], w_ref[...])


def _proj_in(x, norm_w, sc, sh, w2, seq_len, tm):
    n, d = x.shape
    ncol = w2.shape[1]
    tn = 768
    mod = _seq_spec(seq_len, tm, d)
    return pl.pallas_call(
        functools.partial(_proj_in_body, seq_len=seq_len),
        out_shape=jax.ShapeDtypeStruct((n, ncol), F32),
        grid=(n // tm, ncol // tn),
        in_specs=[pl.BlockSpec((tm, d), lambda i, j: (i, 0)),
                  pl.BlockSpec((1, d), lambda i, j: (0, 0)),
                  mod, mod,
                  pl.BlockSpec((d, tn), lambda i, j: (0, j))],
        out_specs=pl.BlockSpec((tm, tn), lambda i, j: (i, j)),
        scratch_shapes=[pltpu.VMEM((tm, d), BF16)],
        compiler_params=_cparams(("arbitrary", "arbitrary")),
        name="proj_in",
    )(x, norm_w.reshape(1, d), sc, sh, w2)


def _compress_prompt_body(x_ref, pe_ref, w1_ref, b1_ref, w2_ref, o_ref, *, nblk):
    half = nblk // 2
    acc = jnp.zeros((nblk, HEAD_DIM), F32)
    for l in range(CMP_BLOCK):
        xe = x_ref[pl.ds(l, half, stride=2 * CMP_BLOCK), :]
        xo = x_ref[pl.ds(CMP_BLOCK + l, half, stride=2 * CMP_BLOCK), :]
        xl = jnp.concatenate([xe, xo], axis=0) + pe_ref[0, l:l + 1, :]
        acc = acc + _mm1(xl, w1_ref[0, l * HEAD_DIM:(l + 1) * HEAD_DIM, :])
    hid = jax.nn.gelu(acc + b1_ref[0])
    o_ref[0, 0] = _mm1(hid, w2_ref[0])


def _compress_prompt(p2, pe, w1, b1, w2, nbatch, seq_len):
    nblk = seq_len // CMP_BLOCK
    kv0 = COL_KVC // LANE
    return pl.pallas_call(
        functools.partial(_compress_prompt_body, nblk=nblk),
        out_shape=jax.ShapeDtypeStruct((nbatch, 4, nblk, HEAD_DIM), F32),
        grid=(nbatch, 4),
        in_specs=[pl.BlockSpec((seq_len, LANE), lambda b, sh: (b, kv0 + sh)),
                  pl.BlockSpec((1, CMP_BLOCK, HEAD_DIM), lambda b, sh: (sh // 2, 0, 0)),
                  pl.BlockSpec((1, CMP_BLOCK * HEAD_DIM, HEAD_DIM), lambda b, sh: (sh // 2, 0, 0)),
                  pl.BlockSpec((1, 1, HEAD_DIM), lambda b, sh: (sh // 2, 0, 0)),
                  pl.BlockSpec((1, HEAD_DIM, HEAD_DIM), lambda b, sh: (sh // 2, 0, 0))],
        out_specs=pl.BlockSpec((1, 1, nblk, HEAD_DIM), lambda b, sh: (b, sh, 0, 0)),
        compiler_params=_cparams(("arbitrary", "arbitrary")),
        name="nsa_compress_prompt",
    )(p2, pe, w1.reshape(2, CMP_BLOCK * HEAD_DIM, HEAD_DIM), b1.reshape(2, 1, HEAD_DIM), w2)


TQ = 128
NSA_KSPAN = 512


def _nsa_prompt_body(slopes_ref, q_ref, gt_ref, ck_ref, cv_ref, sk_ref, sv_ref, wk_ref, wv_ref,
                     o_ref, selm_ref, *, seq_len):
    h = pl.program_id(1)
    qi = pl.program_id(2)
    t0 = qi * TQ
    nsb = seq_len // SEL_BLOCK
    ncb = seq_len // CMP_BLOCK // 2
    rows = NSA_GROUP * TQ

    qb = q_ref[...] * (HEAD_DIM ** -0.5)
    q4 = jnp.concatenate([qb[:, g * HEAD_DIM:(g + 1) * HEAD_DIM] for g in range(NSA_GROUP)], axis=0)
    q4b = q4.astype(BF16)
    row = lax.broadcasted_iota(I32, (rows, 1), 0)
    tq = t0 + (row & (TQ - 1))
    grow = row >> 7
    slope = jnp.zeros((rows, 1), F32)
    for g in range(NSA_GROUP):
        slope = jnp.where(grow == g, slopes_ref[h * NSA_GROUP + g], slope)

    ck = ck_ref[0, 0]
    cv = cv_ref[0, 0]
    lane_r = lax.broadcasted_iota(I32, (1, rows), 1)
    tq_l = t0 + (lane_r & (TQ - 1))
    slope_l = jnp.zeros((1, rows), F32)
    for g in range(NSA_GROUP):
        slope_l = jnp.where((lane_r >> 7) == g, slopes_ref[h * NSA_GROUP + g], slope_l)
    midx = lax.broadcasted_iota(I32, (ncb, 1), 0)
    logits = []
    valids = []
    for par in range(2):
        s_t = _mm3(ck[par * ncb:(par + 1) * ncb], q4, NT)
        cend = midx * (2 * CMP_BLOCK) + (CMP_BLOCK - 1) + par * CMP_BLOCK
        valid = cend <= tq_l
        dist = (tq_l - cend).astype(F32)
        logits.append(jnp.where(valid, s_t - slope_l * dist, NEG_INF))
        valids.append(valid)
    mx = jnp.maximum(jnp.max(logits[0], axis=0, keepdims=True), jnp.max(logits[1], axis=0, keepdims=True))
    e0 = jnp.exp(logits[0] - mx)
    e1 = jnp.exp(logits[1] - mx)
    den = jnp.sum(e0, axis=0, keepdims=True) + jnp.sum(e1, axis=0, keepdims=True)
    p0 = jnp.where(valids[0], e0 / den, 0.0)
    p1 = jnp.where(valids[1], e1 / den, 0.0)
    ident = (lax.broadcasted_iota(I32, (TQ, TQ), 0) == lax.broadcasted_iota(I32, (TQ, TQ), 1)).astype(BF16)
    p0b = p0.astype(BF16)
    p1b = p1.astype(BF16)
    o_cmp = jnp.concatenate(
        [_dot(_dot(ident, p0b[:, g * TQ:(g + 1) * TQ], NT).astype(BF16), cv[0:ncb].astype(BF16))
         + _dot(_dot(ident, p1b[:, g * TQ:(g + 1) * TQ], NT).astype(BF16), cv[ncb:2 * ncb].astype(BF16))
         for g in range(NSA_GROUP)], axis=0)
    pb = p0 + p1
    imp = pb[:, 0:TQ]
    for g in range(1, NSA_GROUP):
        imp = imp + pb[:, g * TQ:(g + 1) * TQ]

    cur = (t0 + lax.broadcasted_iota(I32, (1, TQ), 1)) >> 6
    jb = lax.broadcasted_iota(I32, (nsb, 1), 0)
    forced = (jb == 0) | (jb == cur) | (jb == cur - 1)
    score = jnp.where(jb <= cur, jnp.where(forced, FORCE_SCORE, imp), NEG_INF)
    cnt = jnp.zeros((nsb, TQ), I32)
    for i in range(nsb):
        si = score[i:i + 1, :]
        beats = (si > score) | ((si == score) & (jb > i))
        cnt = cnt + beats.astype(I32)
    sel_t = ((cnt < min(SEL_TOPN, nsb)) & (score > 0.5 * NEG_INF)).astype(BF16)
    sel = _dot(ident, sel_t, NT).astype(BF16)
    kpos = lax.broadcasted_iota(I32, (nsb, seq_len), 1)
    kblk = lax.broadcasted_iota(I32, (nsb, seq_len), 0)
    expand = ((kpos >> 6) == kblk).astype(BF16)
    selk = _dot(sel, expand)

    kw = NSA_KSPAN
    t_row = t0 + lax.broadcasted_iota(I32, (TQ, kw), 0)
    key = lax.broadcasted_iota(I32, (TQ, kw), 1)
    for st in range(seq_len // kw):
        blk = (selk[:, st * kw:(st + 1) * kw] - 1.0) * (-NEG_INF)
        selm_ref[st] = blk + jnp.where(st * kw + key <= t_row, 0.0, NEG_INF)
    m_floor = 0.1 * NEG_INF

    def span_update(carry, k, v, bias, key0):
        m, l, acc = carry
        width = k.shape[0]
        alibi = slope * lax.broadcasted_iota(I32, (1, width), 1).astype(F32)
        x = _dot(q4b, k, NT) + alibi
        x = (x.reshape(NSA_GROUP, TQ, width) + bias[None]).reshape(rows, width)
        shift = slope * (key0 - tq).astype(F32)
        m_new = jnp.maximum(m, jnp.max(x, axis=1, keepdims=True) + shift)
        alpha = jnp.exp(m - m_new)
        p = jnp.exp(x - (m_new - shift))
        l = alpha * l + jnp.sum(p, axis=1, keepdims=True)
        acc = alpha * acc + _dot(p.astype(BF16), v)
        return m_new, l, acc

    def finish(carry):
        _, l, acc = carry
        return jnp.where(l > 0.0, acc / jnp.where(l > 0.0, l, 1.0), 0.0)

    init = (jnp.full((rows, 1), m_floor, F32), jnp.zeros((rows, 1), F32), jnp.zeros((rows, HEAD_DIM), F32))

    def sel_step(st, carry):
        start = pl.multiple_of(st * kw, kw)
        return span_update(carry, sk_ref[pl.ds(start, kw), :].astype(BF16), sv_ref[pl.ds(start, kw), :].astype(BF16),
                           selm_ref[st], start)

    o_sel = finish(lax.fori_loop(0, (t0 + TQ - 1) // kw + 1, sel_step, init))

    ww = min(WINDOW + TQ, seq_len)
    wstart = pl.multiple_of(jnp.maximum(t0 + TQ - ww, 0), TQ)
    wdelta = (t0 + lax.broadcasted_iota(I32, (TQ, ww), 0)) - (wstart + lax.broadcasted_iota(I32, (TQ, ww), 1))
    wbias = jnp.where((wdelta >= 0) & (wdelta < WINDOW), 0.0, NEG_INF)
    o_win = finish(span_update(init, wk_ref[pl.ds(wstart, ww), :].astype(BF16), wv_ref[pl.ds(wstart, ww), :].astype(BF16),
                               wbias, wstart))

    gs = jax.nn.sigmoid(gt_ref[...])
    for g in range(NSA_GROUP):
        def gate(c):
            a = gs[:, g * 3 + c:g * 3 + c + 1]
            b = gs[:, NSA_GROUP * 3 + g * 3 + c:NSA_GROUP * 3 + g * 3 + c + 1]
            return jnp.where(h == 0, a, b)

        sl = slice(g * TQ, (g + 1) * TQ)
        o_ref[:, g * HEAD_DIM:(g + 1) * HEAD_DIM] = (gate(0) * o_cmp[sl] + gate(1) * o_sel[sl]) + gate(2) * o_win[sl]


def _nsa_prompt(p2, ckv, slopes, nbatch, seq_len):
    nq = seq_len // TQ
    kvs0 = COL_KVS // LANE
    kvw0 = COL_KVW // LANE
    misc = COL_MISC // LANE
    nblk = seq_len // CMP_BLOCK
    full = lambda c0: pl.BlockSpec((seq_len, LANE), lambda b, h, q, c0=c0: (b, c0 + h))
    return pl.pallas_call(
        functools.partial(_nsa_prompt_body, seq_len=seq_len),
        out_shape=jax.ShapeDtypeStruct((nbatch * seq_len, NSA_WIDTH), F32),
        grid=(nbatch, NSA_KV_HEADS, nq),
        in_specs=[pl.BlockSpec(memory_space=pltpu.SMEM),
                  pl.BlockSpec((TQ, NSA_GROUP * HEAD_DIM), lambda b, h, q: (b * nq + q, h)),
                  pl.BlockSpec((TQ, LANE), lambda b, h, q: (b * nq + q, misc)),
                  pl.BlockSpec((1, 1, nblk, HEAD_DIM), lambda b, h, q: (b, h, 0, 0)),
                  pl.BlockSpec((1, 1, nblk, HEAD_DIM), lambda b, h, q: (b, 2 + h, 0, 0)),
                  full(kvs0), full(kvs0 + 2), full(kvw0), full(kvw0 + 2)],
        out_specs=pl.BlockSpec((TQ, NSA_GROUP * HEAD_DIM), lambda b, h, q: (b * nq + q, h)),
        scratch_shapes=[pltpu.VMEM((seq_len // NSA_KSPAN, TQ, NSA_KSPAN), F32)],
        compiler_params=_cparams(("arbitrary", "arbitrary", "arbitrary")),
        name="nsa_prompt",
    )(slopes, p2, p2, ckv, ckv, p2, p2, p2, p2)


GDN_CB = 8


def _gdn_chunk(seq_len):
    return min(GDN_CHUNK, seq_len)


def _bmm3(a, b, dims):
    ah, al = _split(a)
    bh, bl = _split(b)
    return _dot(ah, bh, dims) + (_dot(al, bh, dims) + _dot(ah, bl, dims))


GDN_HG = 2


def _gdn_body(alog_ref, dtb_ref, xq_ref, xk_ref, xv_ref, z_ref, gt_ref, cwq_ref, cwk_ref, cwv_ref,
              cbq_ref, cbk_ref, cbv_ref, s0_ref, gn_ref, o_ref, sout_ref,
              cat_s, q_s, k_s, v_s, b_s, g_s, u_s, w_s, qg_s, kdt_s, qk_s, gl_s, oc_s, *, seq_len, valid):
    hg = pl.program_id(1)
    C = _gdn_chunk(seq_len)
    nc = seq_len // C
    cb = min(GDN_CB, nc)
    row = lax.broadcasted_iota(I32, (seq_len, 1), 0)
    lane = lax.broadcasted_iota(I32, (1, LANE), 1)
    pos = row & (C - 1)
    gt = gt_ref[...]

    def conv(x_ref, cw_ref, cb_ref, ls):
        x = x_ref[:, ls]
        cat_s[0:SUBLANE, :] = cb_ref[0, :, ls]
        cat_s[SUBLANE:, :] = x
        w = cw_ref[:, ls]
        acc = cat_s[pl.ds(SUBLANE - 3, seq_len), :] * w[0:1]
        acc = acc + cat_s[pl.ds(SUBLANE - 2, seq_len), :] * w[1:2]
        acc = acc + cat_s[pl.ds(SUBLANE - 1, seq_len), :] * w[2:3]
        acc = acc + x * w[3:4]
        return jax.nn.silu(acc)

    def l2n(x):
        return x * lax.rsqrt(jnp.sum(x * x, axis=-1, keepdims=True) + NORM_EPS)

    ii = lax.broadcasted_iota(I32, (C, C), 0)
    jj = lax.broadcasted_iota(I32, (C, C), 1)
    tri = (ii >= jj)[None]
    strict = (ii > jj)[None]
    eye = (ii == jj).astype(F32)[None]
    ones_b = jnp.ones((cb, C, C), BF16)
    ident_b = jnp.broadcast_to((lax.broadcasted_iota(I32, (HEAD_DIM, HEAD_DIM), 0)
                                == lax.broadcasted_iota(I32, (HEAD_DIM, HEAD_DIM), 1)).astype(BF16)[None],
                               (cb, HEAD_DIM, HEAD_DIM))

    for hl in range(GDN_HG):
        hd = hg * GDN_HG + hl
        ls = slice(hl * HEAD_DIM, (hl + 1) * HEAD_DIM)
        q = l2n(conv(xq_ref, cwq_ref, cbq_ref, ls)) * (HEAD_DIM ** -0.5)
        k = l2n(conv(xk_ref, cwk_ref, cbk_ref, ls))
        v = conv(xv_ref, cwv_ref, cbv_ref, ls)
        bl = jnp.sum(jnp.where(lane == MISC_BL + hd, gt, 0.0), axis=1, keepdims=True)
        al = jnp.sum(jnp.where(lane == MISC_AL + hd, gt, 0.0), axis=1, keepdims=True)
        beta = jax.nn.sigmoid(bl)
        a_pos = jnp.exp(jnp.full((1, 1), alog_ref[hd], F32))
        g = -a_pos * jax.nn.softplus(al + dtb_ref[hd])
        if valid < seq_len:
            vm = row < valid
            q = jnp.where(vm, q, 0.0)
            k = jnp.where(vm, k, 0.0)
            v = jnp.where(vm, v, 0.0)
            beta = jnp.where(vm, beta, 0.0)
            g = jnp.where(vm, g, 0.0)
        gc = g
        sft = 1
        while sft < C:
            gc = gc + jnp.where(pos >= sft, pltpu.roll(gc, sft, 0), 0.0)
            sft *= 2
        q_s[...] = q
        k_s[...] = k
        v_s[...] = v
        b_s[...] = jnp.broadcast_to(beta, (seq_len, LANE))
        g_s[...] = jnp.broadcast_to(gc, (seq_len, LANE))

        def local(gi, carry, hl=hl):
            r0 = pl.multiple_of(gi * (cb * C), cb * C)
            sl = pl.ds(r0, cb * C)
            qc = q_s[sl, :].reshape(cb, C, HEAD_DIM)
            kc = k_s[sl, :].reshape(cb, C, HEAD_DIM)
            vc = v_s[sl, :].reshape(cb, C, HEAD_DIM)
            bc = b_s[sl, :].reshape(cb, C, LANE)
            gcc = g_s[sl, :].reshape(cb, C, LANE)
            gcol = gcc[:, :, 0:C]
            dg = gcol * eye
            d1 = dg.astype(BF16)
            r1 = dg - d1.astype(F32)
            d2 = r1.astype(BF16)
            d3 = (r1 - d2.astype(F32)).astype(BF16)
            grow = _dot(ones_b, d1, BNN) + (_dot(ones_b, d2, BNN) + _dot(ones_b, d3, BNN))
            diff = gcol - grow
            decay = jnp.where(tri, jnp.exp(jnp.where(tri, diff, 0.0)), 0.0)
            kb = kc * bc
            a = _dot(kb.astype(BF16), kc.astype(BF16), BNT) * jnp.where(strict, decay, 0.0)
            tinv = eye - a
            pw = a
            n = 2
            while n < C:
                pw = _bmm3(pw, pw, BNN)
                tinv = tinv + _bmm3(tinv, pw, BNN)
                n *= 2
            eg = jnp.exp(gcc)
            rhs = jnp.concatenate([vc * bc, kb * eg], axis=2)
            sol = _bmm3(tinv, rhs, BNN)
            u_s[hl, sl, :] = sol[:, :, 0:HEAD_DIM].reshape(cb * C, HEAD_DIM)
            w_s[hl, sl, :] = sol[:, :, HEAD_DIM:2 * HEAD_DIM].reshape(cb * C, HEAD_DIM)
            qk = _dot(qc.astype(BF16), kc.astype(BF16), BNT) * decay
            qk_s[hl, sl, :] = qk.reshape(cb * C, C)
            qg_s[hl, sl, :] = (qc * eg).reshape(cb * C, HEAD_DIM)
            glast = gcc[:, C - 1:C, :]
            kd = kc * jnp.exp(glast - gcc)
            kdt = _dot(ident_b, kd.astype(BF16), BNT)
            kdt_s[hl, pl.ds(pl.multiple_of(gi * (cb * HEAD_DIM), cb * HEAD_DIM), cb * HEAD_DIM), :] = (
                kdt.reshape(cb * HEAD_DIM, C))
            gl_s[hl, pl.ds(pl.multiple_of(gi * (cb * SUBLANE), cb * SUBLANE), cb * SUBLANE), :] = (
                jnp.broadcast_to(jnp.exp(glast), (cb, SUBLANE, LANE)).reshape(cb * SUBLANE, LANE))
            return carry

        lax.fori_loop(0, nc // cb, local, 0)

    def scan(c, states):
        sl = pl.ds(pl.multiple_of(c * C, C), C)
        new = []
        for hl in range(GDN_HG):
            s = states[hl]
            sb = s.astype(BF16)
            v_new = u_s[hl, sl, :] - _dot(w_s[hl, sl, :].astype(BF16), sb)
            o_c = (_dot(qg_s[hl, sl, :].astype(BF16), sb)
                   + _dot(qk_s[hl, sl, :].astype(BF16), v_new.astype(BF16)))
            oc_s[hl, sl, :] = o_c
            kdt = kdt_s[hl, pl.ds(pl.multiple_of(c * HEAD_DIM, HEAD_DIM), HEAD_DIM), :]
            gl = gl_s[hl, pl.ds(pl.multiple_of(c * SUBLANE, SUBLANE), 1), :]
            new.append(s * gl + _dot(kdt.astype(BF16), v_new.astype(BF16)))
        return tuple(new)

    s_fin = lax.fori_loop(0, nc, scan, tuple(s0_ref[0, hl] for hl in range(GDN_HG)))
    for hl in range(GDN_HG):
        ls = slice(hl * HEAD_DIM, (hl + 1) * HEAD_DIM)
        sout_ref[0, hl] = s_fin[hl]
        o = oc_s[hl]
        y = o * lax.rsqrt(jnp.mean(o * o, axis=-1, keepdims=True) + NORM_EPS) * gn_ref[...]
        o_ref[:, ls] = y * jax.nn.silu(z_ref[:, ls])


def _gdn(p2, conv_w, conv_buf8, s0, a_log, dt_bias, gn_w, nbatch, seq_len, valid):
    gw = GDN_HG * LANE
    q0 = COL_QKV // gw
    z0 = COL_Z // gw
    hgs = GDN_HEADS // GDN_HG
    misc = COL_MISC // LANE
    C = _gdn_chunk(seq_len)
    col = lambda c0: pl.BlockSpec((seq_len, gw), lambda b, h, c0=c0: (b, c0 + h))
    cw = lambda c0: pl.BlockSpec((CONV_WIDTH, gw), lambda b, h, c0=c0: (0, c0 + h))
    cbs = lambda c0: pl.BlockSpec((1, SUBLANE, gw), lambda b, h, c0=c0: (b, 0, c0 + h))
    smem = pl.BlockSpec(memory_space=pltpu.SMEM)
    nc = seq_len // C
    per_head = lambda rows, cols: pltpu.VMEM((GDN_HG, rows, cols), F32)
    return pl.pallas_call(
        functools.partial(_gdn_body, seq_len=seq_len, valid=valid),
        out_shape=(jax.ShapeDtypeStruct((nbatch * seq_len, GDN_WIDTH), F32),
                   jax.ShapeDtypeStruct((nbatch, GDN_HEADS, HEAD_DIM, HEAD_DIM), F32)),
        grid=(nbatch, hgs),
        in_specs=[smem, smem, col(q0), col(q0 + hgs), col(q0 + 2 * hgs), col(z0),
                  pl.BlockSpec((seq_len, LANE), lambda b, h: (b, misc)),
                  cw(0), cw(hgs), cw(2 * hgs), cbs(0), cbs(hgs), cbs(2 * hgs),
                  pl.BlockSpec((1, GDN_HG, HEAD_DIM, HEAD_DIM), lambda b, h: (b, h, 0, 0)),
                  pl.BlockSpec((1, HEAD_DIM), lambda b, h: (0, 0))],
        out_specs=(pl.BlockSpec((seq_len, gw), lambda b, h: (b, h)),
                   pl.BlockSpec((1, GDN_HG, HEAD_DIM, HEAD_DIM), lambda b, h: (b, h, 0, 0))),
        scratch_shapes=[pltpu.VMEM((seq_len + SUBLANE, HEAD_DIM), F32)]
        + [pltpu.VMEM((seq_len, HEAD_DIM), F32)] * 3
        + [pltpu.VMEM((seq_len, LANE), F32)] * 2
        + [per_head(seq_len, HEAD_DIM)] * 3
        + [per_head(nc * HEAD_DIM, C),
           per_head(seq_len, C),
           per_head(nc * SUBLANE, LANE),
           per_head(seq_len, HEAD_DIM)],
        compiler_params=_cparams(("arbitrary", "arbitrary")),
        name="gdn",
    )(a_log, dt_bias, p2, p2, p2, p2, p2, conv_w, conv_w, conv_w, conv_buf8, conv_buf8, conv_buf8,
      s0, gn_w.reshape(1, HEAD_DIM))


def _out_proj_body(on_ref, og_ref, x_ref, w_ref, g1_ref, nw_ref, sc_ref, sh_ref, x1_ref, h2_ref, *, seq_len):
    mix = _dot(on_ref[...].astype(BF16), w_ref[0:NSA_WIDTH, :]) + _dot(og_ref[...].astype(BF16), w_ref[NSA_WIDTH:, :])
    x1 = x_ref[...] + _seq_rows(g1_ref, seq_len) * mix
    x1_ref[...] = x1
    y = x1 * lax.rsqrt(jnp.mean(x1 * x1, axis=-1, keepdims=True) + NORM_EPS)
    h2_ref[...] = (y * nw_ref[...]) * (1.0 + _seq_rows(sc_ref, seq_len)) + _seq_rows(sh_ref, seq_len)


def _out_proj(o_nsa, o_gdn, x, w_out_bf, g1, norm_w, sc, sh, seq_len, tm):
    n, d = x.shape
    mod = _seq_spec(seq_len, tm, d)
    return pl.pallas_call(
        functools.partial(_out_proj_body, seq_len=seq_len),
        out_shape=(jax.ShapeDtypeStruct((n, d), F32), jax.ShapeDtypeStruct((n, d), F32)),
        grid=(n // tm,),
        in_specs=[pl.BlockSpec((tm, NSA_WIDTH), lambda i: (i, 0)),
                  pl.BlockSpec((tm, GDN_WIDTH), lambda i: (i, 0)),
                  pl.BlockSpec((tm, d), lambda i: (i, 0)),
                  pl.BlockSpec((d, d), lambda i: (0, 0)),
                  mod,
                  pl.BlockSpec((1, d), lambda i: (0, 0)),
                  mod, mod],
        out_specs=(pl.BlockSpec((tm, d), lambda i: (i, 0)), pl.BlockSpec((tm, d), lambda i: (i, 0))),
        compiler_params=_cparams(("arbitrary",)),
        name="out_proj",
    )(o_nsa, o_gdn, x, w_out_bf, g1, norm_w.reshape(1, d), sc, sh)


def _take_top(src_ref, dst_ref):
    s = src_ref[...]
    for k in range(PEER_TOPK):
        m = jnp.max(s, axis=0, keepdims=True)
        s = jnp.where(s == m, NEG_BIG, s)
        dst_ref[k:k + 1, :] = m


PAIR_ROWS = ((0, 16, 16), (16, 8, 8), (24, 8, 5), (32, 8, 4), (40, 8, 3), (48, 8, 2), (56, 8, 2), (64, 8, 2))
PAIR_TAIL = 72
PAIR_TOTAL = 80


def _peer_sel_body(h_ref, wqh_ref, wql_ref, keys_ref, cnt_ref, al_ref, code_ref, be_ref,
                   sc_s, ta_s, tb_s, cand_s, tc_s, *, split_query):
    hh, hl = _split(h_ref[...])
    half = PEER_QDIM // 2
    tn = h_ref.shape[0]
    r8 = lax.broadcasted_iota(I32, (SUBLANE, tn), 0)
    for hd in range(PEER_HEADS):
        wh = wqh_ref[hd * PEER_QDIM:(hd + 1) * PEER_QDIM, :]
        wl = wql_ref[hd * PEER_QDIM:(hd + 1) * PEER_QDIM, :]
        qt = _dot(wh, hh, NT)
        if split_query:
            qt = qt + (_dot(wl, hh, NT) + _dot(wh, hl, NT))
        s0 = _mm3(keys_ref[0], qt[0:half])
        s1 = _mm3(keys_ref[1], qt[half:PEER_QDIM])
        sc_s[...] = s0
        _take_top(sc_s, ta_s)
        sc_s[...] = s1
        _take_top(sc_s, tb_s)
        a = ta_s[...]
        b = tb_s[...]
        for r, (row0, nrow, nval) in enumerate(PAIR_ROWS):
            pair = a[r:r + 1] + b[0:nrow]
            cand_s[row0:row0 + nrow, :] = pair if nval == nrow else jnp.where(r8 < nval, pair, NEG_BIG)
        cand_s[PAIR_TAIL:PAIR_TOTAL, :] = a[SUBLANE:2 * SUBLANE] + b[0:1]
        _take_top(cand_s, tc_s)
        tau = tc_s[PEER_TOPK - 1:PEER_TOPK, :]
        cand = cand_s[...]
        keep = cand >= tau
        zsum = jnp.sum(jnp.where(keep, jnp.exp(cand - cand[0:1]), 0.0), axis=0, keepdims=True)
        keepf = keep.astype(F32)
        cnt = jnp.zeros((PEER_NKEYS, tn), F32)
        for r, (row0, nrow, nval) in enumerate(PAIR_ROWS):
            cnt_r = jnp.sum(keepf[row0:row0 + nrow], axis=0, keepdims=True)
            cnt = jnp.where(s0 == a[r:r + 1], cnt_r, cnt)
        for r in range(SUBLANE, 2 * SUBLANE):
            cnt = jnp.where(s0 == a[r:r + 1], keepf[PAIR_TAIL + r - SUBLANE:PAIR_TAIL + r - SUBLANE + 1], cnt)
        code = jnp.zeros((PEER_NKEYS, tn), F32)
        for r in range(PEER_TOPK):
            code = code + (b[r:r + 1] > s1).astype(F32)
        cnt_ref[hd] = cnt
        al_ref[hd] = jnp.exp(s0 - a[0:1]) / zsum
        code_ref[hd] = code.astype(BF16)
        be_ref[hd] = jnp.exp(s1 - b[0:1]).astype(BF16)


def _peer_select(h2, wq_hi, wq_lo, keys, tn, split_query):
    n, d = h2.shape
    hk = jax.ShapeDtypeStruct((PEER_HEADS, PEER_NKEYS, n), F32)
    hk16 = jax.ShapeDtypeStruct((PEER_HEADS, PEER_NKEYS, n), BF16)
    blk = pl.BlockSpec((PEER_HEADS, PEER_NKEYS, tn), lambda i: (0, 0, i))
    return pl.pallas_call(
        functools.partial(_peer_sel_body, split_query=split_query),
        out_shape=(hk, hk, hk16, hk16),
        grid=(n // tn,),
        in_specs=[pl.BlockSpec((tn, d), lambda i: (i, 0)),
                  pl.BlockSpec((PEER_HEADS * PEER_QDIM, d), lambda i: (0, 0)),
                  pl.BlockSpec((PEER_HEADS * PEER_QDIM, d), lambda i: (0, 0)),
                  pl.BlockSpec((2, PEER_NKEYS, PEER_QDIM // 2), lambda i: (0, 0, 0))],
        out_specs=(blk, blk, blk, blk),
        scratch_shapes=[pltpu.VMEM((PEER_NKEYS, tn), F32), pltpu.VMEM((PEER_TOPK, tn), F32),
                        pltpu.VMEM((PEER_TOPK, tn), F32), pltpu.VMEM((PAIR_TOTAL, tn), F32),
                        pltpu.VMEM((PEER_TOPK, tn), F32)],
        compiler_params=_cparams(("arbitrary",)),
        name="peer_select",
    )(h2, wq_hi, wq_lo, keys)


PEER_TE = 1024
GELU_C0 = math.sqrt(2.0 / math.pi)
GELU_C1 = 0.044715 * math.sqrt(2.0 / math.pi)


def _gelu_tanh(x):
    hx = 0.5 * x
    return hx + hx * jnp.tanh(x * (GELU_C0 + GELU_C1 * (x * x)))


def _peer_mm_body(h_ref, u_ref, vt_ref, cnt_ref, al_ref, code_ref, be_ref, o_ref, at_s, p_s):
    e = pl.program_id(1)

    @pl.when(e == 0)
    def _():
        o_ref[...] = jnp.zeros_like(o_ref)

    at_s[...] = _dot(u_ref[...], h_ref[...], NT)
    tn = h_ref.shape[0]
    for ii in range(PEER_TE // PEER_NKEYS):
        rs = slice(ii * PEER_NKEYS, (ii + 1) * PEER_NKEYS)
        for ck in range(tn // LANE):
            cs = slice(ck * LANE, (ck + 1) * LANE)
            w = jnp.zeros((PEER_NKEYS, LANE), BF16)
            for hd in range(PEER_HEADS):
                cnt = cnt_ref[hd, ii:ii + 1, cs].astype(BF16)
                kept = jnp.where(code_ref[hd, :, cs] < cnt, be_ref[hd, :, cs], jnp.zeros((), BF16))
                w = w + al_ref[hd, ii:ii + 1, cs].astype(BF16) * kept
            p_s[rs, cs] = w * _gelu_tanh(at_s[rs, cs].astype(BF16))
    o_ref[...] += _dot(vt_ref[...], p_s[...])


def _peer_mm(h2b, u_bf, vt_bf, cnt, al, code, be, tn):
    n, d = h2b.shape
    ne = u_bf.shape[0]
    rows_i = PEER_TE // PEER_NKEYS
    sel_i = pl.BlockSpec((PEER_HEADS, rows_i, tn), lambda i, e: (0, e, i))
    sel_j = pl.BlockSpec((PEER_HEADS, PEER_NKEYS, tn), lambda i, e: (0, 0, i))
    return pl.pallas_call(
        _peer_mm_body,
        out_shape=jax.ShapeDtypeStruct((d, n), F32),
        grid=(n // tn, ne // PEER_TE),
        in_specs=[pl.BlockSpec((tn, d), lambda i, e: (i, 0)),
                  pl.BlockSpec((PEER_TE, d), lambda i, e: (e, 0)),
                  pl.BlockSpec((d, PEER_TE), lambda i, e: (0, e)),
                  sel_i, sel_i, sel_j, sel_j],
        out_specs=pl.BlockSpec((d, tn), lambda i, e: (0, i)),
        scratch_shapes=[pltpu.VMEM((PEER_TE, tn), F32), pltpu.VMEM((PEER_TE, tn), BF16)],
        compiler_params=_cparams(("arbitrary", "arbitrary")),
        name="peer_experts",
    )(h2b, u_bf, vt_bf, cnt, al, code, be)


def _final_body(x1_ref, pt_ref, g2_ref, fw_ref, y_ref, *, seq_len):
    x2 = x1_ref[...] + _seq_rows(g2_ref, seq_len) * pt_ref[...].T
    y_ref[...] = x2 * lax.rsqrt(jnp.mean(x2 * x2, axis=-1, keepdims=True) + NORM_EPS) * fw_ref[...]


def _final(x1, peer_t, g2, fw, seq_len, tm):
    n, d = x1.shape
    g2_spec = _seq_spec(seq_len, tm, d)
    return pl.pallas_call(
        functools.partial(_final_body, seq_len=seq_len),
        out_shape=jax.ShapeDtypeStruct((n, d), F32),
        grid=(n // tm,),
        in_specs=[pl.BlockSpec((tm, d), lambda i: (i, 0)),
                  pl.BlockSpec((d, tm), lambda i: (0, i)),
                  g2_spec,
                  pl.BlockSpec((1, d), lambda i: (0, 0))],
        out_specs=pl.BlockSpec((tm, d), lambda i: (i, 0)),
        compiler_params=_cparams(("arbitrary",)),
        name="final_norm",
    )(x1, peer_t, g2, fw.reshape(1, d))


SROWS = 64
SROWS_TAIL = 16
CMP_PAGES = 16
KV_COMP = 2 * NSA_KV_HEADS
TOKEN_ROWS = KV_COMP
PAGE_VROWS = PAGE_SIZE * TOKEN_ROWS
CMP_PAIRS = CMP_BLOCK * TOKEN_ROWS // SUBLANE
BLOCK_VROWS = CMP_BLOCK * TOKEN_ROWS


def _compress_sample_body(pt_ref, *refs):
    pages = refs[:CMP_PAGES]
    pe_ref, w1_ref, b1_ref, w2_ref, o_ref, acc_s = refs[CMP_PAGES:]
    bpp = PAGE_SIZE // CMP_BLOCK
    nrow = CMP_PAGES * bpp * SUBLANE
    nh = CMP_PAGES * bpp // 2
    acc = jnp.zeros((nrow, KV_COMP * HEAD_DIM), F32)
    for q in range(CMP_PAIRS):
        parts = [r[0, n * BLOCK_VROWS + q * SUBLANE:n * BLOCK_VROWS + (q + 1) * SUBLANE, :]
                 for r in pages for n in range(bpp)]
        lhs = jnp.concatenate(parts, axis=0) + jnp.concatenate([pe_ref[q]] * (nrow // SUBLANE), axis=0)
        acc = acc + _dot(lhs.astype(BF16), w1_ref[q])
    rid = lax.broadcasted_iota(I32, (nrow, 1), 0) & (SUBLANE - 1)
    want = ((rid & (KV_COMP - 1)) >> 1) * 2 + (rid >> 2)
    picked = acc[:, 0:HEAD_DIM]
    for blk in range(1, KV_COMP):
        picked = jnp.where(want == blk, acc[:, blk * HEAD_DIM:(blk + 1) * HEAD_DIM], picked)
    acc_s[...] = picked + pltpu.roll(picked, nrow - KV_COMP, 0)
    for c in range(KV_COMP):
        s = c // NSA_KV_HEADS
        hid_e = acc_s[pl.ds(c, nh, stride=2 * SUBLANE), :]
        hid_o = acc_s[pl.ds(SUBLANE + c, nh, stride=2 * SUBLANE), :]
        hid = jax.nn.gelu(jnp.concatenate([hid_e, hid_o], axis=0) + b1_ref[s])
        out = _mm1(hid, w2_ref[s])
        o_ref[0, c, 0] = out[0:nh]
        o_ref[0, c, 1] = out[nh:2 * nh]


def _compress_sample(cache, page_table, pe, w1, b1, w2):
    nbs, n_pages = page_table.shape
    n_phys = cache.shape[0]
    cache_v = cache.reshape(n_phys, PAGE_VROWS, HEAD_DIM)
    ngrp = n_pages // CMP_PAGES
    nblk_half = n_pages * PAGE_SIZE // CMP_BLOCK // 2
    nh = CMP_PAGES * (PAGE_SIZE // CMP_BLOCK) // 2
    pe_q = jnp.transpose(pe.reshape(2, CMP_PAIRS, 2, HEAD_DIM), (1, 2, 0, 3))
    pe_q = jnp.broadcast_to(pe_q[:, :, :, None, :], (CMP_PAIRS, 2, 2, NSA_KV_HEADS, HEAD_DIM))
    pe_q = pe_q.reshape(CMP_PAIRS, SUBLANE, HEAD_DIM)
    w1_q = jnp.transpose(w1.reshape(2, CMP_PAIRS, 2, HEAD_DIM, HEAD_DIM), (1, 3, 0, 2, 4))
    w1_q = w1_q.reshape(CMP_PAIRS, HEAD_DIM, KV_COMP * HEAD_DIM).astype(BF16)
    page_spec = lambda k: pl.BlockSpec((1, PAGE_VROWS, HEAD_DIM),
                                       lambda b, g, pt, k=k: (pt[b * n_pages + g * CMP_PAGES + k], 0, 0))
    const = lambda shape: pl.BlockSpec(shape, lambda b, g, pt: (0,) * len(shape))
    grid_spec = pltpu.PrefetchScalarGridSpec(
        num_scalar_prefetch=1,
        grid=(nbs, ngrp),
        in_specs=[page_spec(k) for k in range(CMP_PAGES)]
        + [const((CMP_PAIRS, SUBLANE, HEAD_DIM)), const((CMP_PAIRS, HEAD_DIM, KV_COMP * HEAD_DIM)),
           const((2, 1, HEAD_DIM)), const((2, HEAD_DIM, HEAD_DIM))],
        out_specs=pl.BlockSpec((1, KV_COMP, 2, nh, HEAD_DIM), lambda b, g, pt: (b, 0, 0, g, 0)),
        scratch_shapes=[pltpu.VMEM((CMP_PAGES * (PAGE_SIZE // CMP_BLOCK) * SUBLANE, HEAD_DIM), F32)])
    return pl.pallas_call(
        _compress_sample_body,
        out_shape=jax.ShapeDtypeStruct((nbs, KV_COMP, 2, nblk_half, HEAD_DIM), F32),
        grid_spec=grid_spec,
        compiler_params=_cparams(("arbitrary", "arbitrary")),
        name="nsa_compress_sample",
    )(page_table.reshape(-1), *([cache_v] * CMP_PAGES), pe_q, w1_q, b1.reshape(2, 1, HEAD_DIM), w2)


def _rows_from_lanes(row, ngrp):
    ridx = lax.broadcasted_iota(I32, (SUBLANE, HEAD_DIM), 0)
    out = jnp.zeros((SUBLANE, HEAD_DIM), F32)
    for g in range(ngrp):
        out = jnp.where(ridx == g, jnp.broadcast_to(row[:, g * HEAD_DIM:(g + 1) * HEAD_DIM], (SUBLANE, HEAD_DIM)), out)
    return out


def _slope_rows(slopes_ref, h):
    ridx = lax.broadcasted_iota(I32, (SUBLANE, 1), 0)
    slope = jnp.zeros((SUBLANE, 1), F32)
    for g in range(NSA_GROUP):
        slope = jnp.where(ridx == g, slopes_ref[h * NSA_GROUP + g], slope)
    return slope


def _nsa_sample_select_body(slopes_ref, q_ref, ckv_ref, ocmp_ref, idx_ref, *, past):
    ncb = past // CMP_BLOCK // 2
    nblk = past // SEL_BLOCK
    ridx = lax.broadcasted_iota(I32, (SUBLANE, 1), 0)
    midx = lax.broadcasted_iota(I32, (1, ncb), 1)
    jb = lax.broadcasted_iota(I32, (1, nblk), 1)
    slot = lax.broadcasted_iota(I32, (1, SEL_TOPN), 1)
    for h in range(NSA_KV_HEADS):
        q8 = _rows_from_lanes(q_ref[0:1, h * NSA_GROUP * HEAD_DIM:(h + 1) * NSA_GROUP * HEAD_DIM], NSA_GROUP)
        q8 = q8 * (HEAD_DIM ** -0.5)
        slope = _slope_rows(slopes_ref, h)
        lg = []
        for par in range(2):
            s_c = _mm3(q8, ckv_ref[0, h, par], NT)
            cend = midx * (2 * CMP_BLOCK) + (CMP_BLOCK - 1) + par * CMP_BLOCK
            lg.append(s_c - slope * (past - cend).astype(F32))
        mx = jnp.maximum(jnp.max(lg[0], axis=1, keepdims=True), jnp.max(lg[1], axis=1, keepdims=True))
        e0 = jnp.exp(lg[0] - mx)
        e1 = jnp.exp(lg[1] - mx)
        den = jnp.sum(e0, axis=1, keepdims=True) + jnp.sum(e1, axis=1, keepdims=True)
        p0 = e0 / den
        p1 = e1 / den
        o_cmp = _mm1(p0, ckv_ref[0, 2 + h, 0]) + _mm1(p1, ckv_ref[0, 2 + h, 1])
        ocmp_ref[0, h * NSA_GROUP:(h + 1) * NSA_GROUP, :] = o_cmp[0:NSA_GROUP]
        imp = jnp.sum(jnp.where(ridx < NSA_GROUP, p0 + p1, 0.0), axis=0, keepdims=True)
        forced = (jb == 0) | (jb == nblk - 1)
        score = jnp.where(forced, FORCE_SCORE, imp)
        picks = jnp.full((1, SEL_TOPN), nblk, I32)
        for k in range(SEL_TOPN - 1):
            m = jnp.max(score, axis=1, keepdims=True)
            first = jnp.min(jnp.where(score == m, jb, nblk), axis=1, keepdims=True)
            score = jnp.where(jb == first, NEG_BIG, score)
            picks = jnp.where(slot == k, first, picks)
        idx_ref[0, h:h + 1, :] = picks


def _nsa_sample_select(p2s, ckv_s, slopes, nbs, past):
    nblk_half = past // CMP_BLOCK // 2
    return pl.pallas_call(
        functools.partial(_nsa_sample_select_body, past=past),
        out_shape=(jax.ShapeDtypeStruct((nbs, NSA_HEADS, HEAD_DIM), F32),
                   jax.ShapeDtypeStruct((nbs, NSA_KV_HEADS, SEL_TOPN), I32)),
        grid=(nbs,),
        in_specs=[pl.BlockSpec(memory_space=pltpu.SMEM),
                  pl.BlockSpec((SUBLANE, NSA_WIDTH), lambda b: (b * (SROWS // SUBLANE), 0)),
                  pl.BlockSpec((1, 4, 2, nblk_half, HEAD_DIM), lambda b: (b, 0, 0, 0, 0))],
        out_specs=(pl.BlockSpec((1, NSA_HEADS, HEAD_DIM), lambda b: (b, 0, 0)),
                   pl.BlockSpec((1, NSA_KV_HEADS, SEL_TOPN), lambda b: (b, 0, 0))),
        compiler_params=_cparams(("arbitrary",)),
        name="nsa_sample_select",
    )(slopes, p2s, ckv_s)


NSEL_PAST = SEL_TOPN - 1


def _nsa_sample_attend_body(idx_ref, pt_ref, slopes_ref, q_ref, gt_ref, ocmp_ref, skn_ref, svn_ref, wkn_ref, wvn_ref,
                            *refs, past, n_pages):
    blocks = refs[:NSEL_PAST]
    wc_ref, o_ref = refs[NSEL_PAST:]
    b = pl.program_id(0)
    h = pl.program_id(1)
    q8 = _rows_from_lanes(q_ref[0:1, :], NSA_GROUP) * (HEAD_DIM ** -0.5)
    q8b = q8.astype(BF16)
    slope = _slope_rows(slopes_ref, h)

    def attend_with_new(k_old, v_old, lg_bias, k_new, v_new):
        lg = _dot(q8b, k_old.astype(BF16), NT) + lg_bias
        s_new = jnp.sum(q8 * k_new, axis=1, keepdims=True)
        m = jnp.maximum(jnp.max(lg, axis=1, keepdims=True), s_new)
        p = jnp.exp(lg - m)
        pn = jnp.exp(s_new - m)
        den = jnp.sum(p, axis=1, keepdims=True) + pn
        return (_dot(p.astype(BF16), v_old.astype(BF16)) + pn * v_new) / den

    k_sel = jnp.concatenate([r[0, pl.ds(h, SEL_BLOCK, stride=TOKEN_ROWS), :] for r in blocks], axis=0)
    v_sel = jnp.concatenate([r[0, pl.ds(NSA_KV_HEADS + h, SEL_BLOCK, stride=TOKEN_ROWS), :] for r in blocks], axis=0)
    lane = lax.broadcasted_iota(I32, (1, NSEL_PAST * SEL_BLOCK), 1)
    slot = lane >> (SEL_BLOCK.bit_length() - 1)
    blk = jnp.zeros((1, NSEL_PAST * SEL_BLOCK), I32)
    for j in range(NSEL_PAST):
        blk = jnp.where(slot == j, idx_ref[(b * NSA_KV_HEADS + h) * SEL_TOPN + j], blk)
    dist = (past - (blk * SEL_BLOCK + (lane & (SEL_BLOCK - 1)))).astype(F32)
    o_sel = attend_with_new(k_sel, v_sel, -slope * dist, skn_ref[0:1, :], svn_ref[0:1, :])

    nw = wc_ref.shape[1] // TOKEN_ROWS
    r = lax.broadcasted_iota(I32, (1, nw), 1)
    delta = nw - r
    wmask = (delta < WINDOW) & (past - delta >= 0)
    k_win = wc_ref[0, pl.ds(h, nw, stride=TOKEN_ROWS), :]
    v_win = wc_ref[0, pl.ds(NSA_KV_HEADS + h, nw, stride=TOKEN_ROWS), :]
    o_win = attend_with_new(k_win, v_win, jnp.where(wmask, -slope * delta.astype(F32), NEG_INF),
                            wkn_ref[0:1, :], wvn_ref[0:1, :])

    gs = jax.nn.sigmoid(gt_ref[0:1, :])
    o_ref[...] = jnp.zeros_like(o_ref)
    for g in range(NSA_GROUP):
        def gate(c):
            a = gs[:, g * 3 + c:g * 3 + c + 1]
            bb = gs[:, NSA_GROUP * 3 + g * 3 + c:NSA_GROUP * 3 + g * 3 + c + 1]
            return jnp.where(h == 0, a, bb)

        oc = jnp.where(h == 0, ocmp_ref[0, g:g + 1, :], ocmp_ref[0, NSA_GROUP + g:NSA_GROUP + g + 1, :])
        og = (gate(0) * oc + gate(1) * o_sel[g:g + 1]) + gate(2) * o_win[g:g + 1]
        o_ref[0:1, g * HEAD_DIM:(g + 1) * HEAD_DIM] = og


def _nsa_sample_attend(p2s, ocmp, idx, page_table, cache_sel, win_state, slopes, nbs, past):
    n_pages = page_table.shape[1]
    n_phys = cache_sel.shape[0]
    nw = win_state.shape[1]
    halves = PAGE_SIZE // SEL_BLOCK
    sel_v = cache_sel.reshape(n_phys * halves, SEL_BLOCK * TOKEN_ROWS, HEAD_DIM)
    win_v = win_state.reshape(nbs, nw * TOKEN_ROWS, HEAD_DIM)
    rb = SROWS // SUBLANE
    kvs0 = COL_KVS // LANE
    kvw0 = COL_KVW // LANE
    misc = COL_MISC // LANE

    def sel_spec(j):
        def index(b, h, idx, pt):
            blk = idx[(b * NSA_KV_HEADS + h) * SEL_TOPN + j]
            page = pt[b * n_pages + blk // halves]
            return (page * halves + blk % halves, 0, 0)
        return pl.BlockSpec((1, SEL_BLOCK * TOKEN_ROWS, HEAD_DIM), index)

    new = lambda c0: pl.BlockSpec((SUBLANE, LANE), lambda b, h, idx, pt, c0=c0: (b * rb, c0 + h))
    grid_spec = pltpu.PrefetchScalarGridSpec(
        num_scalar_prefetch=2,
        grid=(nbs, NSA_KV_HEADS),
        in_specs=[pl.BlockSpec(memory_space=pltpu.SMEM),
                  pl.BlockSpec((SUBLANE, NSA_GROUP * HEAD_DIM), lambda b, h, idx, pt: (b * rb, h)),
                  pl.BlockSpec((SUBLANE, LANE), lambda b, h, idx, pt: (b * rb, misc)),
                  pl.BlockSpec((1, NSA_HEADS, HEAD_DIM), lambda b, h, idx, pt: (b, 0, 0)),
                  new(kvs0), new(kvs0 + 2), new(kvw0), new(kvw0 + 2)]
        + [sel_spec(j) for j in range(NSEL_PAST)]
        + [pl.BlockSpec((1, nw * TOKEN_ROWS, HEAD_DIM), lambda b, h, idx, pt: (b, 0, 0))],
        out_specs=pl.BlockSpec((SROWS, NSA_GROUP * HEAD_DIM), lambda b, h, idx, pt: (b, h)))
    return pl.pallas_call(
        functools.partial(_nsa_sample_attend_body, past=past, n_pages=n_pages),
        out_shape=jax.ShapeDtypeStruct((nbs * SROWS, NSA_WIDTH), F32),
        grid_spec=grid_spec,
        compiler_params=_cparams(("arbitrary", "arbitrary")),
        name="nsa_sample_attend",
    )(idx.reshape(-1), page_table.reshape(-1), slopes, p2s, p2s, ocmp, p2s, p2s, p2s, p2s,
      *([sel_v] * NSEL_PAST), win_v)


def _perm_w_in(w_in):
    d = w_in.shape[0]
    g0 = NSA_WIDTH + 3 * 2 * KV_WIDTH
    q0 = g0 + 3 * NSA_HEADS
    b0 = q0 + 3 * GDN_WIDTH + GDN_WIDTH
    pad = jnp.zeros((d, N_PROJ - (b0 + 2 * GDN_HEADS)), w_in.dtype)
    w2 = jnp.concatenate([w_in[:, :g0], w_in[:, q0:b0], w_in[:, g0:q0], w_in[:, b0:], pad], axis=1)
    return w2.astype(BF16)


def _split_bf16(w):
    hi = w.astype(BF16)
    lo = (w - hi.astype(F32)).astype(BF16)
    return hi, lo


def _mixer_tail(x, o_nsa, o_gdn, w_out_bf, g1, norm_ffn, sc2, sh2, g2, wq_hi, wq_lo, keys, u_bf, vt_bf,
                final_norm, seq_len, tm, tn_sel, tn_mm, tm_final):
    x1, h2 = _out_proj(o_nsa, o_gdn, x, w_out_bf, g1, norm_ffn, sc2, sh2, seq_len, tm)
    cnt, al, code, be = _peer_select(h2, wq_hi, wq_lo, keys, tn_sel, split_query=h2.shape[0] <= tn_sel)
    peer_t = _peer_mm(h2.astype(BF16), u_bf, vt_bf, cnt, al, code, be, tn_mm)
    return _final(x1, peer_t, g2, final_norm, seq_len, tm_final)


def kernel(x_prompt, x_sample, cache_cmp_kv, cache_sel_kv, state_win_kv, state_conv, state_gdn, page_table,
           c_prompt, c_sample, w_ada, b_ada, norm_mix, norm_ffn, w_in, cmp_pe, cmp_w1, cmp_b1, cmp_w2, conv_w,
           gdn_a_log, gdn_dt_bias, gdn_norm, w_out, peer_wq, peer_keys, peer_u, peer_v, final_norm):
    nb, seq, d = x_prompt.shape
    nbs, dec_seq, _ = x_sample.shape
    assert w_in.shape[0] == 1 and dec_seq == 1, "single layer, single decode token"
    past = page_table.shape[1] * PAGE_SIZE
    slopes = 2.0 ** (-8.0 * jnp.arange(1, NSA_HEADS + 1, dtype=F32) / NSA_HEADS)
    rows_c = 16
    c_all = jnp.concatenate([c_prompt, c_sample, jnp.zeros((rows_c - nb - nbs, d), F32)], axis=0)
    mod = _adaln(c_all, w_ada[0], b_ada[0]).reshape(rows_c, 6, d)
    pm = lambda k: mod[0:nb, k][:, None, :]
    sm = lambda k: mod[nb:nb + nbs, k][:, None, :]
    w2 = _perm_w_in(w_in[0])
    w_out_bf = w_out[0].astype(BF16)
    wq_hi, wq_lo = _split_bf16(peer_wq[0].T)
    u_bf = peer_u[0].astype(BF16)
    vt_bf = peer_v[0].T.astype(BF16)
    cw = (cmp_pe[0], cmp_w1[0], cmp_b1[0], cmp_w2[0])
    kvw = 2 * KV_WIDTH

    xp = x_prompt.reshape(nb * seq, d)
    p2 = _proj_in(xp, norm_mix[0], pm(1), pm(0), w2, seq, 1024)
    ckv = _compress_prompt(p2, *cw, nb, seq)
    o_nsa = _nsa_prompt(p2, ckv, slopes, nb, seq)
    conv0 = jnp.zeros((nb, SUBLANE, 3 * GDN_WIDTH), F32)
    s0 = jnp.zeros((nb, GDN_HEADS, HEAD_DIM, HEAD_DIM), F32)
    o_gdn, gdn_p = _gdn(p2, conv_w[0], conv0, s0, gdn_a_log[0], gdn_dt_bias[0], gdn_norm[0], nb, seq, seq)
    y_prompt = _mixer_tail(xp, o_nsa, o_gdn, w_out_bf, pm(2), norm_ffn[0], pm(4), pm(3), pm(5), wq_hi, wq_lo,
                           peer_keys[0], u_bf, vt_bf, final_norm, seq, 512, 256, 512, 512).reshape(nb, seq, d)
    p3 = p2.reshape(nb, seq, N_PROJ)
    keep = min(WINDOW, seq)
    kv5 = lambda a: a.reshape(a.shape[0], a.shape[1], 2, NSA_KV_HEADS, HEAD_DIM)
    cmp_p = kv5(p3[:, :, COL_KVC:COL_KVC + kvw])
    sel_p = kv5(p3[:, :, COL_KVS:COL_KVS + kvw])
    win_p = kv5(p3[:, seq - keep:, COL_KVW:COL_KVW + kvw])
    conv_p = p3[:, seq - (CONV_WIDTH - 1):, COL_QKV:COL_QKV + 3 * GDN_WIDTH]

    xs = jnp.pad(x_sample, ((0, 0), (0, SROWS - dec_seq), (0, 0))).reshape(nbs * SROWS, d)
    p2s = _proj_in(xs, norm_mix[0], sm(1), sm(0), w2, SROWS, nbs * SROWS)
    ckv_s = _compress_sample(cache_cmp_kv[0], page_table, *cw)
    ocmp, idx = _nsa_sample_select(p2s, ckv_s, slopes, nbs, past)
    o_nsa_s = _nsa_sample_attend(p2s, ocmp, idx, page_table, cache_sel_kv[0], state_win_kv[0], slopes, nbs, past)
    conv_buf = jnp.pad(state_conv[0], ((0, 0), (SUBLANE - (CONV_WIDTH - 1), 0), (0, 0)))
    o_gdn_s, gdn_s = _gdn(p2s, conv_w[0], conv_buf, state_gdn[0], gdn_a_log[0], gdn_dt_bias[0], gdn_norm[0],
                          nbs, SROWS, dec_seq)
    head_rows = lambda a: a.reshape(nbs, SROWS, a.shape[-1])[:, :SROWS_TAIL].reshape(nbs * SROWS_TAIL, a.shape[-1])
    nt = nbs * SROWS_TAIL
    ys = _mixer_tail(head_rows(xs), head_rows(o_nsa_s), head_rows(o_gdn_s), w_out_bf, sm(2), norm_ffn[0], sm(4),
                     sm(3), sm(5), wq_hi, wq_lo, peer_keys[0], u_bf, vt_bf, final_norm, SROWS_TAIL, nt, nt, nt, nt)
    y_sample = ys.reshape(nbs, SROWS_TAIL, d)[:, 0:dec_seq]
    p3s = p2s.reshape(nbs, SROWS, N_PROJ)[:, 0:dec_seq]
    cmp_s = kv5(p3s[:, :, COL_KVC:COL_KVC + kvw])
    sel_s = kv5(p3s[:, :, COL_KVS:COL_KVS + kvw])
    win_new = kv5(p3s[:, :, COL_KVW:COL_KVW + kvw])
    win_s = jnp.concatenate([state_win_kv[0], win_new], axis=1)[:, dec_seq:]
    conv_s = jnp.concatenate([state_conv[0], p3s[:, :, COL_QKV:COL_QKV + 3 * GDN_WIDTH]], axis=1)[:, dec_seq:]

    return (y_prompt, y_sample, cmp_p[None], cmp_s[None], sel_p[None], sel_s[None], win_p[None], win_s[None],
            conv_p[None], conv_s[None], gdn_p[None], gdn_s[None])
```

```python
import functools
import math

import jax
import jax.numpy as jnp
from jax import lax
from jax.experimental import pallas as pl
from jax.experimental.pallas import tpu as pltpu

F32 = jnp.float32
BF16 = jnp.bfloat16
I32 = jnp.int32

D_MODEL = 2048
HEAD_DIM = 128
NSA_WIDTH = 1024
GDN_WIDTH = 1024
NSA_HEADS = 8
NSA_KV_HEADS = 2
NSA_GROUP = 4
KV_WIDTH = 256
CMP_BLOCK = 32
SEL_BLOCK = 64
SEL_TOPN = 16
WINDOW = 512
GDN_HEADS = 8
CONV_WIDTH = 4
GDN_CHUNK = 64
PEER_HEADS = 8
PEER_NKEYS = 128
PEER_QDIM = 256
PEER_TOPK = 16
PAGE_SIZE = 128
NORM_EPS = 1e-6
NEG_INF = -1e30
FORCE_SCORE = 1e4
NEG_BIG = -3.0e38

LANE = 128
SUBLANE = 8
VMEM_LIMIT = 56 * 1024 * 1024

COL_Q = 0
COL_KVC = 1024
COL_KVS = 1536
COL_KVW = 2048
COL_QKV = 2560
COL_Z = 5632
COL_MISC = 6656
N_PROJ = 6912
MISC_BL = 24
MISC_AL = 32

NN = (((1,), (0,)), ((), ()))
NT = (((1,), (1,)), ((), ()))
BNN = (((2,), (1,)), ((0,), (0,)))
BNT = (((2,), (2,)), ((0,), (0,)))


def _dot(a, b, dims=NN):
    return lax.dot_general(a, b, dims, preferred_element_type=F32)


def _split(x):
    hi = x.astype(BF16)
    lo = (x - hi.astype(F32)).astype(BF16)
    return hi, lo


def _mm1(a, b, dims=NN):
    return _dot(a.astype(BF16), b.astype(BF16), dims)


def _mm3(a, b, dims=NN):
    ah, al = _split(a)
    bh, bl = _split(b)
    return _dot(ah, bh, dims) + (_dot(al, bh, dims) + _dot(ah, bl, dims))


def _cparams(sem, vmem=VMEM_LIMIT):
    return pltpu.CompilerParams(dimension_semantics=sem, vmem_limit_bytes=vmem)


def _adaln_body(c_ref, w_ref, b_ref, o_ref):
    a = jax.nn.silu(c_ref[...])
    o_ref[...] = _mm3(a, w_ref[...]) + b_ref[...]


def _adaln(c_all, w, b):
    rows, d = c_all.shape
    n = w.shape[1]
    tn = 1024
    return pl.pallas_call(
        _adaln_body,
        out_shape=jax.ShapeDtypeStruct((rows, n), F32),
        grid=(n // tn,),
        in_specs=[pl.BlockSpec((rows, d), lambda j: (0, 0)),
                  pl.BlockSpec((d, tn), lambda j: (0, j)),
                  pl.BlockSpec((1, tn), lambda j: (0, j))],
        out_specs=pl.BlockSpec((rows, tn), lambda j: (0, j)),
        compiler_params=_cparams(("arbitrary",)),
        name="adaln",
    )(c_all, w, b.reshape(1, n))


def _seq_rows(m_ref, seq_len):
    nseq = m_ref.shape[0]
    if nseq == 1:
        return m_ref[0]
    return jnp.concatenate([jnp.broadcast_to(m_ref[r], (seq_len, m_ref.shape[2])) for r in range(nseq)], axis=0)


def _seq_spec(seq_len, tm, d):
    if tm <= seq_len:
        tpb = seq_len // tm
        return pl.BlockSpec((1, 1, d), lambda i, *_: (i // tpb, 0, 0))
    return pl.BlockSpec((tm // seq_len, 1, d), lambda i, *_: (i, 0, 0))


def _proj_in_body(x_ref, nw_ref, sc_ref, sh_ref, w_ref, o_ref, h_ref, *, seq_len):
    @pl.when(pl.program_id(1) == 0)
    def _():
        x = x_ref[...]
        y = x * lax.rsqrt(jnp.mean(x * x, axis=-1, keepdims=True) + NORM_EPS)
        h = (y * nw_ref[...]) * (1.0 + _seq_rows(sc_ref, seq_len)) + _seq_rows(sh_ref, seq_len)
        h_ref[...] = h.astype(BF16)

    o_ref[...] = _dot(h_ref[...], w_ref[...])


def _proj_in(x, norm_w, sc, sh, w2, seq_len, tm):
    n, d = x.shape
    ncol = w2.shape[1]
    tn = 768
    mod = _seq_spec(seq_len, tm, d)
    return pl.pallas_call(
        functools.partial(_proj_in_body, seq_len=seq_len),
        out_shape=jax.ShapeDtypeStruct((n, ncol), F32),
        grid=(n // tm, ncol // tn),
        in_specs=[pl.BlockSpec((tm, d), lambda i, j: (i, 0)),
                  pl.BlockSpec((1, d), lambda i, j: (0, 0)),
                  mod, mod,
                  pl.BlockSpec((d, tn), lambda i, j: (0, j))],
        out_specs=pl.BlockSpec((tm, tn), lambda i, j: (i, j)),
        scratch_shapes=[pltpu.VMEM((tm, d), BF16)],
        compiler_params=_cparams(("arbitrary", "arbitrary")),
        name="proj_in",
    )(x, norm_w.reshape(1, d), sc, sh, w2)


def _compress_prompt_body(x_ref, pe_ref, w1_ref, b1_ref, w2_ref, o_ref, *, nblk):
    half = nblk // 2
    xs = []
    for l in range(CMP_BLOCK):
        xe = x_ref[pl.ds(l, half, stride=2 * CMP_BLOCK), :]
        xo = x_ref[pl.ds(CMP_BLOCK + l, half, stride=2 * CMP_BLOCK), :]
        xs.append((jnp.concatenate([xe, xo], axis=0) + pe_ref[0, l:l + 1, :]).astype(BF16))
    acc = _dot(jnp.concatenate(xs, axis=1), w1_ref[0].astype(BF16))
    hid = jax.nn.gelu(acc + b1_ref[0])
    o_ref[0, 0] = _mm1(hid, w2_ref[0])


def _compress_prompt(p2, pe, w1, b1, w2, nbatch, seq_len):
    nblk = seq_len // CMP_BLOCK
    kv0 = COL_KVC // LANE
    return pl.pallas_call(
        functools.partial(_compress_prompt_body, nblk=nblk),
        out_shape=jax.ShapeDtypeStruct((nbatch, 4, nblk, HEAD_DIM), F32),
        grid=(nbatch, 4),
        in_specs=[pl.BlockSpec((seq_len, LANE), lambda b, sh: (b, kv0 + sh)),
                  pl.BlockSpec((1, CMP_BLOCK, HEAD_DIM), lambda b, sh: (sh // 2, 0, 0)),
                  pl.BlockSpec((1, CMP_BLOCK * HEAD_DIM, HEAD_DIM), lambda b, sh: (sh // 2, 0, 0)),
                  pl.BlockSpec((1, 1, HEAD_DIM), lambda b, sh: (sh // 2, 0, 0)),
                  pl.BlockSpec((1, HEAD_DIM, HEAD_DIM), lambda b, sh: (sh // 2, 0, 0))],
        out_specs=pl.BlockSpec((1, 1, nblk, HEAD_DIM), lambda b, sh: (b, sh, 0, 0)),
        compiler_params=_cparams(("arbitrary", "arbitrary")),
        name="nsa_compress_prompt",
    )(p2, pe, w1.reshape(2, CMP_BLOCK * HEAD_DIM, HEAD_DIM), b1.reshape(2, 1, HEAD_DIM), w2)


TQ = 128
NSA_KSPAN = 512


def _nsa_prompt_body(slopes_ref, q_ref, gt_ref, ck_ref, cv_ref, sk_ref, sv_ref, wk_ref, wv_ref,
                     o_ref, selm_ref, *, seq_len):
    h = pl.program_id(1)
    qi = pl.program_id(2)
    t0 = qi * TQ
    nsb = seq_len // SEL_BLOCK
    ncb = seq_len // CMP_BLOCK // 2
    rows = NSA_GROUP * TQ

    qb = q_ref[...] * (HEAD_DIM ** -0.5)
    q4 = jnp.concatenate([qb[:, g * HEAD_DIM:(g + 1) * HEAD_DIM] for g in range(NSA_GROUP)], axis=0)
    q4b = q4.astype(BF16)
    row = lax.broadcasted_iota(I32, (rows, 1), 0)
    tq = t0 + (row & (TQ - 1))
    grow = row >> 7
    slope = jnp.zeros((rows, 1), F32)
    for g in range(NSA_GROUP):
        slope = jnp.where(grow == g, slopes_ref[h * NSA_GROUP + g], slope)

    ck = ck_ref[0, 0]
    cv = cv_ref[0, 0]
    lane_r = lax.broadcasted_iota(I32, (1, rows), 1)
    tq_l = t0 + (lane_r & (TQ - 1))
    slope_l = jnp.zeros((1, rows), F32)
    for g in range(NSA_GROUP):
        slope_l = jnp.where((lane_r >> 7) == g, slopes_ref[h * NSA_GROUP + g], slope_l)
    midx = lax.broadcasted_iota(I32, (ncb, 1), 0)
    logits = []
    valids = []
    for par in range(2):
        s_t = _mm3(ck[par * ncb:(par + 1) * ncb], q4, NT)
        cend = midx * (2 * CMP_BLOCK) + (CMP_BLOCK - 1) + par * CMP_BLOCK
        valid = cend <= tq_l
        dist = (tq_l - cend).astype(F32)
        logits.append(jnp.where(valid, s_t - slope_l * dist, NEG_INF))
        valids.append(valid)
    mx = jnp.maximum(jnp.max(logits[0], axis=0, keepdims=True), jnp.max(logits[1], axis=0, keepdims=True))
    e0 = jnp.exp(logits[0] - mx)
    e1 = jnp.exp(logits[1] - mx)
    den = jnp.sum(e0, axis=0, keepdims=True) + jnp.sum(e1, axis=0, keepdims=True)
    p0 = jnp.where(valids[0], e0 / den, 0.0)
    p1 = jnp.where(valids[1], e1 / den, 0.0)
    ident = (lax.broadcasted_iota(I32, (TQ, TQ), 0) == lax.broadcasted_iota(I32, (TQ, TQ), 1)).astype(BF16)
    p0b = p0.astype(BF16)
    p1b = p1.astype(BF16)
    o_cmp = jnp.concatenate(
        [_dot(_dot(ident, p0b[:, g * TQ:(g + 1) * TQ], NT).astype(BF16), cv[0:ncb].astype(BF16))
         + _dot(_dot(ident, p1b[:, g * TQ:(g + 1) * TQ], NT).astype(BF16), cv[ncb:2 * ncb].astype(BF16))
         for g in range(NSA_GROUP)], axis=0)
    pb = p0 + p1
    imp = pb[:, 0:TQ]
    for g in range(1, NSA_GROUP):
        imp = imp + pb[:, g * TQ:(g + 1) * TQ]

    cur = (t0 + lax.broadcasted_iota(I32, (1, TQ), 1)) >> 6
    jb = lax.broadcasted_iota(I32, (nsb, 1), 0)
    forced = (jb == 0) | (jb == cur) | (jb == cur - 1)
    score = jnp.where(jb <= cur, jnp.where(forced, FORCE_SCORE, imp), NEG_INF)
    cnt = jnp.zeros((nsb, TQ), I32)
    for i in range(nsb):
        si = score[i:i + 1, :]
        beats = (si > score) | ((si == score) & (jb > i))
        cnt = cnt + beats.astype(I32)
    sel_t = ((cnt < min(SEL_TOPN, nsb)) & (score > 0.5 * NEG_INF)).astype(BF16)
    sel = _dot(ident, sel_t, NT).astype(BF16)
    kpos = lax.broadcasted_iota(I32, (nsb, seq_len), 1)
    kblk = lax.broadcasted_iota(I32, (nsb, seq_len), 0)
    expand = ((kpos >> 6) == kblk).astype(BF16)
    selk = _dot(sel, expand)

    kw = NSA_KSPAN
    t_row = t0 + lax.broadcasted_iota(I32, (TQ, kw), 0)
    key = lax.broadcasted_iota(I32, (TQ, kw), 1)
    for st in range(seq_len // kw):
        blk = (selk[:, st * kw:(st + 1) * kw] - 1.0) * (-NEG_INF)
        selm_ref[st] = blk + jnp.where(st * kw + key <= t_row, 0.0, NEG_INF)
    m_floor = 0.1 * NEG_INF

    def span_update(carry, k, v, bias, key0):
        m, l, acc = carry
        width = k.shape[0]
        alibi = slope * lax.broadcasted_iota(I32, (1, width), 1).astype(F32)
        x = _dot(q4b, k, NT) + alibi
        x = (x.reshape(NSA_GROUP, TQ, width) + bias[None]).reshape(rows, width)
        shift = slope * (key0 - tq).astype(F32)
        m_new = jnp.maximum(m, jnp.max(x, axis=1, keepdims=True) + shift)
        alpha = jnp.exp(m - m_new)
        p = jnp.exp(x - (m_new - shift))
        l = alpha * l + jnp.sum(p, axis=1, keepdims=True)
        acc = alpha * acc + _dot(p.astype(BF16), v)
        return m_new, l, acc

    def finish(carry):
        _, l, acc = carry
        return jnp.where(l > 0.0, acc / jnp.where(l > 0.0, l, 1.0), 0.0)

    init = (jnp.full((rows, 1), m_floor, F32), jnp.zeros((rows, 1), F32), jnp.zeros((rows, HEAD_DIM), F32))

    def sel_step(st, carry):
        start = pl.multiple_of(st * kw, kw)
        return span_update(carry, sk_ref[pl.ds(start, kw), :].astype(BF16), sv_ref[pl.ds(start, kw), :].astype(BF16),
                           selm_ref[st], start)

    o_sel = finish(lax.fori_loop(0, (t0 + TQ - 1) // kw + 1, sel_step, init))

    ww = min(WINDOW + TQ, seq_len)
    wstart = pl.multiple_of(jnp.maximum(t0 + TQ - ww, 0), TQ)
    wdelta = (t0 + lax.broadcasted_iota(I32, (TQ, ww), 0)) - (wstart + lax.broadcasted_iota(I32, (TQ, ww), 1))
    wbias = jnp.where((wdelta >= 0) & (wdelta < WINDOW), 0.0, NEG_INF)
    o_win = finish(span_update(init, wk_ref[pl.ds(wstart, ww), :].astype(BF16), wv_ref[pl.ds(wstart, ww), :].astype(BF16),
                               wbias, wstart))

    gs = jax.nn.sigmoid(gt_ref[...])
    for g in range(NSA_GROUP):
        def gate(c):
            a = gs[:, g * 3 + c:g * 3 + c + 1]
            b = gs[:, NSA_GROUP * 3 + g * 3 + c:NSA_GROUP * 3 + g * 3 + c + 1]
            return jnp.where(h == 0, a, b)

        sl = slice(g * TQ, (g + 1) * TQ)
        o_ref[:, g * HEAD_DIM:(g + 1) * HEAD_DIM] = (gate(0) * o_cmp[sl] + gate(1) * o_sel[sl]) + gate(2) * o_win[sl]


def _nsa_prompt(p2, ckv, slopes, nbatch, seq_len):
    nq = seq_len // TQ
    kvs0 = COL_KVS // LANE
    kvw0 = COL_KVW // LANE
    misc = COL_MISC // LANE
    nblk = seq_len // CMP_BLOCK
    full = lambda c0: pl.BlockSpec((seq_len, LANE), lambda b, h, q, c0=c0: (b, c0 + h))
    return pl.pallas_call(
        functools.partial(_nsa_prompt_body, seq_len=seq_len),
        out_shape=jax.ShapeDtypeStruct((nbatch * seq_len, NSA_WIDTH), F32),
        grid=(nbatch, NSA_KV_HEADS, nq),
        in_specs=[pl.BlockSpec(memory_space=pltpu.SMEM),
                  pl.BlockSpec((TQ, NSA_GROUP * HEAD_DIM), lambda b, h, q: (b * nq + q, h)),
                  pl.BlockSpec((TQ, LANE), lambda b, h, q: (b * nq + q, misc)),
                  pl.BlockSpec((1, 1, nblk, HEAD_DIM), lambda b, h, q: (b, h, 0, 0)),
                  pl.BlockSpec((1, 1, nblk, HEAD_DIM), lambda b, h, q: (b, 2 + h, 0, 0)),
                  full(kvs0), full(kvs0 + 2), full(kvw0), full(kvw0 + 2)],
        out_specs=pl.BlockSpec((TQ, NSA_GROUP * HEAD_DIM), lambda b, h, q: (b * nq + q, h)),
        scratch_shapes=[pltpu.VMEM((seq_len // NSA_KSPAN, TQ, NSA_KSPAN), F32)],
        compiler_params=_cparams(("arbitrary", "arbitrary", "arbitrary")),
        name="nsa_prompt",
    )(slopes, p2, p2, ckv, ckv, p2, p2, p2, p2)


GDN_CB = 8


def _gdn_chunk(seq_len):
    return min(GDN_CHUNK, seq_len)


def _bmm3(a, b):
    ah, al = _split(a)
    bh, bl = _split(b)
    return _dot(jnp.concatenate([ah, al, ah], axis=2), jnp.concatenate([bh, bh, bl], axis=1), BNN)


GDN_HG = 2


def _gdn_body(alog_ref, dtb_ref, xq_ref, xk_ref, xv_ref, z_ref, gt_ref, cwq_ref, cwk_ref, cwv_ref,
              cbq_ref, cbk_ref, cbv_ref, s0_ref, gn_ref, o_ref, sout_ref,
              cat_s, q_s, k_s, v_s, b_s, g_s, u_s, w_s, qg_s, kdt_s, qk_s, gl_s, oc_s, *, seq_len, valid):
    hg = pl.program_id(1)
    C = _gdn_chunk(seq_len)
    nc = seq_len // C
    cb = min(GDN_CB, nc)
    row = lax.broadcasted_iota(I32, (seq_len, 1), 0)
    lane = lax.broadcasted_iota(I32, (1, LANE), 1)
    pos = row & (C - 1)
    gt = gt_ref[...]

    def conv(x_ref, cw_ref, cb_ref, ls):
        x = x_ref[:, ls]
        cat_s[0:SUBLANE, :] = cb_ref[0, :, ls]
        cat_s[SUBLANE:, :] = x
        w = cw_ref[:, ls]
        acc = cat_s[pl.ds(SUBLANE - 3, seq_len), :] * w[0:1]
        acc = acc + cat_s[pl.ds(SUBLANE - 2, seq_len), :] * w[1:2]
        acc = acc + cat_s[pl.ds(SUBLANE - 1, seq_len), :] * w[2:3]
        acc = acc + x * w[3:4]
        return jax.nn.silu(acc)

    def l2n(x):
        return x * lax.rsqrt(jnp.sum(x * x, axis=-1, keepdims=True) + NORM_EPS)

    ii = lax.broadcasted_iota(I32, (C, C), 0)
    jj = lax.broadcasted_iota(I32, (C, C), 1)
    tri = (ii >= jj)[None]
    strict = (ii > jj)[None]
    eye = (ii == jj).astype(F32)[None]
    ones_b = jnp.ones((cb, C, 3 * C), BF16)
    ident_b = jnp.broadcast_to((lax.broadcasted_iota(I32, (HEAD_DIM, HEAD_DIM), 0)
                                == lax.broadcasted_iota(I32, (HEAD_DIM, HEAD_DIM), 1)).astype(BF16)[None],
                               (cb, HEAD_DIM, HEAD_DIM))

    for hl in range(GDN_HG):
        hd = hg * GDN_HG + hl
        ls = slice(hl * HEAD_DIM, (hl + 1) * HEAD_DIM)
        q = l2n(conv(xq_ref, cwq_ref, cbq_ref, ls)) * (HEAD_DIM ** -0.5)
        k = l2n(conv(xk_ref, cwk_ref, cbk_ref, ls))
        v = conv(xv_ref, cwv_ref, cbv_ref, ls)
        bl = jnp.sum(jnp.where(lane == MISC_BL + hd, gt, 0.0), axis=1, keepdims=True)
        al = jnp.sum(jnp.where(lane == MISC_AL + hd, gt, 0.0), axis=1, keepdims=True)
        beta = jax.nn.sigmoid(bl)
        a_pos = jnp.exp(jnp.full((1, 1), alog_ref[hd], F32))
        g = -a_pos * jax.nn.softplus(al + dtb_ref[hd])
        if valid < seq_len:
            vm = row < valid
            q = jnp.where(vm, q, 0.0)
            k = jnp.where(vm, k, 0.0)
            v = jnp.where(vm, v, 0.0)
            beta = jnp.where(vm, beta, 0.0)
            g = jnp.where(vm, g, 0.0)
        gc = g
        sft = 1
        while sft < C:
            gc = gc + jnp.where(pos >= sft, pltpu.roll(gc, sft, 0), 0.0)
            sft *= 2
        q_s[...] = q
        k_s[...] = k
        v_s[...] = v
        b_s[...] = jnp.broadcast_to(beta, (seq_len, LANE))
        g_s[...] = jnp.broadcast_to(gc, (seq_len, LANE))

        def local(gi, carry, hl=hl):
            r0 = pl.multiple_of(gi * (cb * C), cb * C)
            sl = pl.ds(r0, cb * C)
            qc = q_s[sl, :].reshape(cb, C, HEAD_DIM)
            kc = k_s[sl, :].reshape(cb, C, HEAD_DIM)
            vc = v_s[sl, :].reshape(cb, C, HEAD_DIM)
            bc = b_s[sl, :].reshape(cb, C, LANE)
            gcc = g_s[sl, :].reshape(cb, C, LANE)
            gcol = gcc[:, :, 0:C]
            dg = gcol * eye
            d1 = dg.astype(BF16)
            r1 = dg - d1.astype(F32)
            d2 = r1.astype(BF16)
            d3 = (r1 - d2.astype(F32)).astype(BF16)
            grow = _dot(ones_b, jnp.concatenate([d1, d2, d3], axis=1), BNN)
            diff = gcol - grow
            decay = jnp.where(tri, jnp.exp(jnp.where(tri, diff, 0.0)), 0.0)
            kb = kc * bc
            a = _dot(kb.astype(BF16), kc.astype(BF16), BNT) * jnp.where(strict, decay, 0.0)
            tinv = eye - a
            pw = a
            n = 2
            while n < C:
                pw = _bmm3(pw, pw)
                tinv = tinv + _bmm3(tinv, pw)
                n *= 2
            eg = jnp.exp(gcc)
            rhs = jnp.concatenate([vc * bc, kb * eg], axis=2)
            sol = _bmm3(tinv, rhs)
            u_s[hl, sl, :] = sol[:, :, 0:HEAD_DIM].reshape(cb * C, HEAD_DIM)
            w_s[hl, sl, :] = sol[:, :, HEAD_DIM:2 * HEAD_DIM].reshape(cb * C, HEAD_DIM)
            qk = _dot(qc.astype(BF16), kc.astype(BF16), BNT) * decay
            qk_s[hl, sl, :] = qk.reshape(cb * C, C)
            qg_s[hl, sl, :] = (qc * eg).reshape(cb * C, HEAD_DIM)
            glast = gcc[:, C - 1:C, :]
            kd = kc * jnp.exp(glast - gcc)
            kdt = _dot(ident_b, kd.astype(BF16), BNT)
            kdt_s[hl, pl.ds(pl.multiple_of(gi * (cb * HEAD_DIM), cb * HEAD_DIM), cb * HEAD_DIM), :] = (
                kdt.reshape(cb * HEAD_DIM, C))
            gl_s[hl, pl.ds(pl.multiple_of(gi * (cb * SUBLANE), cb * SUBLANE), cb * SUBLANE), :] = (
                jnp.broadcast_to(jnp.exp(glast), (cb, SUBLANE, LANE)).reshape(cb * SUBLANE, LANE))
            return carry

        lax.fori_loop(0, nc // cb, local, 0)

    def scan(c, states):
        sl = pl.ds(pl.multiple_of(c * C, C), C)
        new = []
        for hl in range(GDN_HG):
            s = states[hl]
            sb = s.astype(BF16)
            v_new = u_s[hl, sl, :] - _dot(w_s[hl, sl, :].astype(BF16), sb)
            o_c = (_dot(qg_s[hl, sl, :].astype(BF16), sb)
                   + _dot(qk_s[hl, sl, :].astype(BF16), v_new.astype(BF16)))
            oc_s[hl, sl, :] = o_c
            kdt = kdt_s[hl, pl.ds(pl.multiple_of(c * HEAD_DIM, HEAD_DIM), HEAD_DIM), :]
            gl = gl_s[hl, pl.ds(pl.multiple_of(c * SUBLANE, SUBLANE), 1), :]
            new.append(s * gl + _dot(kdt.astype(BF16), v_new.astype(BF16)))
        return tuple(new)

    s_fin = lax.fori_loop(0, nc, scan, tuple(s0_ref[0, hl] for hl in range(GDN_HG)))
    for hl in range(GDN_HG):
        ls = slice(hl * HEAD_DIM, (hl + 1) * HEAD_DIM)
        sout_ref[0, hl] = s_fin[hl]
        o = oc_s[hl]
        y = o * lax.rsqrt(jnp.mean(o * o, axis=-1, keepdims=True) + NORM_EPS) * gn_ref[...]
        o_ref[:, ls] = y * jax.nn.silu(z_ref[:, ls])


def _gdn(p2, conv_w, conv_buf8, s0, a_log, dt_bias, gn_w, nbatch, seq_len, valid):
    gw = GDN_HG * LANE
    q0 = COL_QKV // gw
    z0 = COL_Z // gw
    hgs = GDN_HEADS // GDN_HG
    misc = COL_MISC // LANE
    C = _gdn_chunk(seq_len)
    col = lambda c0: pl.BlockSpec((seq_len, gw), lambda b, h, c0=c0: (b, c0 + h))
    cw = lambda c0: pl.BlockSpec((CONV_WIDTH, gw), lambda b, h, c0=c0: (0, c0 + h))
    cbs = lambda c0: pl.BlockSpec((1, SUBLANE, gw), lambda b, h, c0=c0: (b, 0, c0 + h))
    smem = pl.BlockSpec(memory_space=pltpu.SMEM)
    nc = seq_len // C
    per_head = lambda rows, cols: pltpu.VMEM((GDN_HG, rows, cols), F32)
    return pl.pallas_call(
        functools.partial(_gdn_body, seq_len=seq_len, valid=valid),
        out_shape=(jax.ShapeDtypeStruct((nbatch * seq_len, GDN_WIDTH), F32),
                   jax.ShapeDtypeStruct((nbatch, GDN_HEADS, HEAD_DIM, HEAD_DIM), F32)),
        grid=(nbatch, hgs),
        in_specs=[smem, smem, col(q0), col(q0 + hgs), col(q0 + 2 * hgs), col(z0),
                  pl.BlockSpec((seq_len, LANE), lambda b, h: (b, misc)),
                  cw(0), cw(hgs), cw(2 * hgs), cbs(0), cbs(hgs), cbs(2 * hgs),
                  pl.BlockSpec((1, GDN_HG, HEAD_DIM, HEAD_DIM), lambda b, h: (b, h, 0, 0)),
                  pl.BlockSpec((1, HEAD_DIM), lambda b, h: (0, 0))],
        out_specs=(pl.BlockSpec((seq_len, gw), lambda b, h: (b, h)),
                   pl.BlockSpec((1, GDN_HG, HEAD_DIM, HEAD_DIM), lambda b, h: (b, h, 0, 0))),
        scratch_shapes=[pltpu.VMEM((seq_len + SUBLANE, HEAD_DIM), F32)]
        + [pltpu.VMEM((seq_len, HEAD_DIM), F32)] * 3
        + [pltpu.VMEM((seq_len, LANE), F32)] * 2
        + [per_head(seq_len, HEAD_DIM)] * 3
        + [per_head(nc * HEAD_DIM, C),
           per_head(seq_len, C),
           per_head(nc * SUBLANE, LANE),
           per_head(seq_len, HEAD_DIM)],
        compiler_params=_cparams(("arbitrary", "arbitrary")),
        name="gdn",
    )(a_log, dt_bias, p2, p2, p2, p2, p2, conv_w, conv_w, conv_w, conv_buf8, conv_buf8, conv_buf8,
      s0, gn_w.reshape(1, HEAD_DIM))


def _out_proj_body(on_ref, og_ref, x_ref, w_ref, g1_ref, nw_ref, sc_ref, sh_ref, x1_ref, h2_ref, *, seq_len):
    mix = _dot(on_ref[...].astype(BF16), w_ref[0:NSA_WIDTH, :]) + _dot(og_ref[...].astype(BF16), w_ref[NSA_WIDTH:, :])
    x1 = x_ref[...] + _seq_rows(g1_ref, seq_len) * mix
    x1_ref[...] = x1
    y = x1 * lax.rsqrt(jnp.mean(x1 * x1, axis=-1, keepdims=True) + NORM_EPS)
    h2_ref[...] = (y * nw_ref[...]) * (1.0 + _seq_rows(sc_ref, seq_len)) + _seq_rows(sh_ref, seq_len)


def _out_proj(o_nsa, o_gdn, x, w_out_bf, g1, norm_w, sc, sh, seq_len, tm):
    n, d = x.shape
    mod = _seq_spec(seq_len, tm, d)
    return pl.pallas_call(
        functools.partial(_out_proj_body, seq_len=seq_len),
        out_shape=(jax.ShapeDtypeStruct((n, d), F32), jax.ShapeDtypeStruct((n, d), F32)),
        grid=(n // tm,),
        in_specs=[pl.BlockSpec((tm, NSA_WIDTH), lambda i: (i, 0)),
                  pl.BlockSpec((tm, GDN_WIDTH), lambda i: (i, 0)),
                  pl.BlockSpec((tm, d), lambda i: (i, 0)),
                  pl.BlockSpec((d, d), lambda i: (0, 0)),
                  mod,
                  pl.BlockSpec((1, d), lambda i: (0, 0)),
                  mod, mod],
        out_specs=(pl.BlockSpec((tm, d), lambda i: (i, 0)), pl.BlockSpec((tm, d), lambda i: (i, 0))),
        compiler_params=_cparams(("arbitrary",)),
        name="out_proj",
    )(o_nsa, o_gdn, x, w_out_bf, g1, norm_w.reshape(1, d), sc, sh)


def _take_top(src_ref, dst_ref):
    s = src_ref[...]
    for k in range(PEER_TOPK):
        m = jnp.max(s, axis=0, keepdims=True)
        s = jnp.where(s == m, NEG_BIG, s)
        dst_ref[k:k + 1, :] = m


PAIR_ROWS = ((0, 16, 16), (16, 8, 8), (24, 8, 5), (32, 8, 4), (40, 8, 3), (48, 8, 2), (56, 8, 2), (64, 8, 2))
PAIR_TAIL = 72
PAIR_TOTAL = 80


def _peer_sel_body(h_ref, wqh_ref, wql_ref, keys_ref, cnt_ref, al_ref, code_ref, be_ref,
                   sc_s, ta_s, tb_s, cand_s, tc_s, *, split_query):
    hh, hl = _split(h_ref[...])
    half = PEER_QDIM // 2
    tn = h_ref.shape[0]
    r8 = lax.broadcasted_iota(I32, (SUBLANE, tn), 0)
    for hd in range(PEER_HEADS):
        wh = wqh_ref[hd * PEER_QDIM:(hd + 1) * PEER_QDIM, :]
        wl = wql_ref[hd * PEER_QDIM:(hd + 1) * PEER_QDIM, :]
        qt = _dot(wh, hh, NT)
        if split_query:
            qt = qt + (_dot(wl, hh, NT) + _dot(wh, hl, NT))
        s0 = _mm3(keys_ref[0], qt[0:half])
        s1 = _mm3(keys_ref[1], qt[half:PEER_QDIM])
        sc_s[...] = s0
        _take_top(sc_s, ta_s)
        sc_s[...] = s1
        _take_top(sc_s, tb_s)
        a = ta_s[...]
        b = tb_s[...]
        for r, (row0, nrow, nval) in enumerate(PAIR_ROWS):
            pair = a[r:r + 1] + b[0:nrow]
            cand_s[row0:row0 + nrow, :] = pair if nval == nrow else jnp.where(r8 < nval, pair, NEG_BIG)
        cand_s[PAIR_TAIL:PAIR_TOTAL, :] = a[SUBLANE:2 * SUBLANE] + b[0:1]
        _take_top(cand_s, tc_s)
        tau = tc_s[PEER_TOPK - 1:PEER_TOPK, :]
        cand = cand_s[...]
        keep = cand >= tau
        zsum = jnp.sum(jnp.where(keep, jnp.exp(cand - cand[0:1]), 0.0), axis=0, keepdims=True)
        keepf = keep.astype(F32)
        cnt = jnp.zeros((PEER_NKEYS, tn), F32)
        for r, (row0, nrow, nval) in enumerate(PAIR_ROWS):
            cnt_r = jnp.sum(keepf[row0:row0 + nrow], axis=0, keepdims=True)
            cnt = jnp.where(s0 == a[r:r + 1], cnt_r, cnt)
        for r in range(SUBLANE, 2 * SUBLANE):
            cnt = jnp.where(s0 == a[r:r + 1], keepf[PAIR_TAIL + r - SUBLANE:PAIR_TAIL + r - SUBLANE + 1], cnt)
        code = jnp.zeros((PEER_NKEYS, tn), F32)
        for r in range(PEER_TOPK):
            code = code + (b[r:r + 1] > s1).astype(F32)
        cnt_ref[hd] = cnt
        al_ref[hd] = jnp.exp(s0 - a[0:1]) / zsum
        code_ref[hd] = code.astype(BF16)
        be_ref[hd] = jnp.exp(s1 - b[0:1]).astype(BF16)


def _peer_select(h2, wq_hi, wq_lo, keys, tn, split_query):
    n, d = h2.shape
    hk = jax.ShapeDtypeStruct((PEER_HEADS, PEER_NKEYS, n), F32)
    hk16 = jax.ShapeDtypeStruct((PEER_HEADS, PEER_NKEYS, n), BF16)
    blk = pl.BlockSpec((PEER_HEADS, PEER_NKEYS, tn), lambda i: (0, 0, i))
    return pl.pallas_call(
        functools.partial(_peer_sel_body, split_query=split_query),
        out_shape=(hk, hk, hk16, hk16),
        grid=(n // tn,),
        in_specs=[pl.BlockSpec((tn, d), lambda i: (i, 0)),
                  pl.BlockSpec((PEER_HEADS * PEER_QDIM, d), lambda i: (0, 0)),
                  pl.BlockSpec((PEER_HEADS * PEER_QDIM, d), lambda i: (0, 0)),
                  pl.BlockSpec((2, PEER_NKEYS, PEER_QDIM // 2), lambda i: (0, 0, 0))],
        out_specs=(blk, blk, blk, blk),
        scratch_shapes=[pltpu.VMEM((PEER_NKEYS, tn), F32), pltpu.VMEM((PEER_TOPK, tn), F32),
                        pltpu.VMEM((PEER_TOPK, tn), F32), pltpu.VMEM((PAIR_TOTAL, tn), F32),
                        pltpu.VMEM((PEER_TOPK, tn), F32)],
        compiler_params=_cparams(("arbitrary",)),
        name="peer_select",
    )(h2, wq_hi, wq_lo, keys)


PEER_TE = 1024
GELU_C0 = math.sqrt(2.0 / math.pi)
GELU_C1 = 0.044715 * math.sqrt(2.0 / math.pi)


def _gelu_tanh(x):
    hx = 0.5 * x
    return hx + hx * jnp.tanh(x * (GELU_C0 + GELU_C1 * (x * x)))


def _peer_mm_body(h_ref, u_ref, vt_ref, cnt_ref, al_ref, code_ref, be_ref, o_ref, at_s, p_s):
    e = pl.program_id(1)

    @pl.when(e == 0)
    def _():
        o_ref[...] = jnp.zeros_like(o_ref)

    at_s[...] = _dot(u_ref[...], h_ref[...], NT)
    tn = h_ref.shape[0]
    for ii in range(PEER_TE // PEER_NKEYS):
        rs = slice(ii * PEER_NKEYS, (ii + 1) * PEER_NKEYS)
        for ck in range(tn // LANE):
            cs = slice(ck * LANE, (ck + 1) * LANE)
            w = jnp.zeros((PEER_NKEYS, LANE), BF16)
            for hd in range(PEER_HEADS):
                cnt = cnt_ref[hd, ii:ii + 1, cs].astype(BF16)
                kept = jnp.where(code_ref[hd, :, cs] < cnt, be_ref[hd, :, cs], jnp.zeros((), BF16))
                w = w + al_ref[hd, ii:ii + 1, cs].astype(BF16) * kept
            p_s[rs, cs] = w * _gelu_tanh(at_s[rs, cs].astype(BF16))
    o_ref[...] += _dot(vt_ref[...], p_s[...])


def _peer_mm(h2b, u_bf, vt_bf, cnt, al, code, be, tn):
    n, d = h2b.shape
    ne = u_bf.shape[0]
    rows_i = PEER_TE // PEER_NKEYS
    sel_i = pl.BlockSpec((PEER_HEADS, rows_i, tn), lambda i, e: (0, e, i))
    sel_j = pl.BlockSpec((PEER_HEADS, PEER_NKEYS, tn), lambda i, e: (0, 0, i))
    return pl.pallas_call(
        _peer_mm_body,
        out_shape=jax.ShapeDtypeStruct((d, n), F32),
        grid=(n // tn, ne // PEER_TE),
        in_specs=[pl.BlockSpec((tn, d), lambda i, e: (i, 0)),
                  pl.BlockSpec((PEER_TE, d), lambda i, e: (e, 0)),
                  pl.BlockSpec((d, PEER_TE), lambda i, e: (0, e)),
                  sel_i, sel_i, sel_j, sel_j],
        out_specs=pl.BlockSpec((d, tn), lambda i, e: (0, i)),
        scratch_shapes=[pltpu.VMEM((PEER_TE, tn), F32), pltpu.VMEM((PEER_TE, tn), BF16)],
        compiler_params=_cparams(("arbitrary", "arbitrary")),
        name="peer_experts",
    )(h2b, u_bf, vt_bf, cnt, al, code, be)


def _final_body(x1_ref, pt_ref, g2_ref, fw_ref, y_ref, *, seq_len):
    x2 = x1_ref[...] + _seq_rows(g2_ref, seq_len) * pt_ref[...].T
    y_ref[...] = x2 * lax.rsqrt(jnp.mean(x2 * x2, axis=-1, keepdims=True) + NORM_EPS) * fw_ref[...]


def _final(x1, peer_t, g2, fw, seq_len, tm):
    n, d = x1.shape
    g2_spec = _seq_spec(seq_len, tm, d)
    return pl.pallas_call(
        functools.partial(_final_body, seq_len=seq_len),
        out_shape=jax.ShapeDtypeStruct((n, d), F32),
        grid=(n // tm,),
        in_specs=[pl.BlockSpec((tm, d), lambda i: (i, 0)),
                  pl.BlockSpec((d, tm), lambda i: (0, i)),
                  g2_spec,
                  pl.BlockSpec((1, d), lambda i: (0, 0))],
        out_specs=pl.BlockSpec((tm, d), lambda i: (i, 0)),
        compiler_params=_cparams(("arbitrary",)),
        name="final_norm",
    )(x1, peer_t, g2, fw.reshape(1, d))


SROWS = 64
SROWS_TAIL = 16
CMP_PAGES = 16
KV_COMP = 2 * NSA_KV_HEADS
TOKEN_ROWS = KV_COMP
PAGE_VROWS = PAGE_SIZE * TOKEN_ROWS
CMP_PAIRS = CMP_BLOCK * TOKEN_ROWS // SUBLANE
BLOCK_VROWS = CMP_BLOCK * TOKEN_ROWS


def _compress_sample_body(pt_ref, *refs):
    pages = refs[:CMP_PAGES]
    pe_ref, w1_ref, b1_ref, w2_ref, o_ref, acc_s = refs[CMP_PAGES:]
    bpp = PAGE_SIZE // CMP_BLOCK
    nrow = CMP_PAGES * bpp * SUBLANE
    nh = CMP_PAGES * bpp // 2
    lhs = []
    for q in range(CMP_PAIRS):
        parts = [r[0, n * BLOCK_VROWS + q * SUBLANE:n * BLOCK_VROWS + (q + 1) * SUBLANE, :]
                 for r in pages for n in range(bpp)]
        lhs.append((jnp.concatenate(parts, axis=0)
                    + jnp.concatenate([pe_ref[q]] * (nrow // SUBLANE), axis=0)).astype(BF16))
    acc = _dot(jnp.concatenate(lhs, axis=1), w1_ref[...])
    rid = lax.broadcasted_iota(I32, (nrow, 1), 0) & (SUBLANE - 1)
    want = ((rid & (KV_COMP - 1)) >> 1) * 2 + (rid >> 2)
    picked = acc[:, 0:HEAD_DIM]
    for blk in range(1, KV_COMP):
        picked = jnp.where(want == blk, acc[:, blk * HEAD_DIM:(blk + 1) * HEAD_DIM], picked)
    acc_s[...] = picked + pltpu.roll(picked, nrow - KV_COMP, 0)
    for c in range(KV_COMP):
        s = c // NSA_KV_HEADS
        hid_e = acc_s[pl.ds(c, nh, stride=2 * SUBLANE), :]
        hid_o = acc_s[pl.ds(SUBLANE + c, nh, stride=2 * SUBLANE), :]
        hid = jax.nn.gelu(jnp.concatenate([hid_e, hid_o], axis=0) + b1_ref[s])
        out = _mm1(hid, w2_ref[s])
        o_ref[0, c, 0] = out[0:nh]
        o_ref[0, c, 1] = out[nh:2 * nh]


def _compress_sample(cache, page_table, pe, w1, b1, w2):
    nbs, n_pages = page_table.shape
    n_phys = cache.shape[0]
    cache_v = cache.reshape(n_phys, PAGE_VROWS, HEAD_DIM)
    ngrp = n_pages // CMP_PAGES
    nblk_half = n_pages * PAGE_SIZE // CMP_BLOCK // 2
    nh = CMP_PAGES * (PAGE_SIZE // CMP_BLOCK) // 2
    pe_q = jnp.transpose(pe.reshape(2, CMP_PAIRS, 2, HEAD_DIM), (1, 2, 0, 3))
    pe_q = jnp.broadcast_to(pe_q[:, :, :, None, :], (CMP_PAIRS, 2, 2, NSA_KV_HEADS, HEAD_DIM))
    pe_q = pe_q.reshape(CMP_PAIRS, SUBLANE, HEAD_DIM)
    w1_q = jnp.transpose(w1.reshape(2, CMP_PAIRS, 2, HEAD_DIM, HEAD_DIM), (1, 3, 0, 2, 4))
    w1_q = w1_q.reshape(CMP_PAIRS * HEAD_DIM, KV_COMP * HEAD_DIM).astype(BF16)
    page_spec = lambda k: pl.BlockSpec((1, PAGE_VROWS, HEAD_DIM),
                                       lambda b, g, pt, k=k: (pt[b * n_pages + g * CMP_PAGES + k], 0, 0))
    const = lambda shape: pl.BlockSpec(shape, lambda b, g, pt: (0,) * len(shape))
    grid_spec = pltpu.PrefetchScalarGridSpec(
        num_scalar_prefetch=1,
        grid=(nbs, ngrp),
        in_specs=[page_spec(k) for k in range(CMP_PAGES)]
        + [const((CMP_PAIRS, SUBLANE, HEAD_DIM)), const((CMP_PAIRS * HEAD_DIM, KV_COMP * HEAD_DIM)),
           const((2, 1, HEAD_DIM)), const((2, HEAD_DIM, HEAD_DIM))],
        out_specs=pl.BlockSpec((1, KV_COMP, 2, nh, HEAD_DIM), lambda b, g, pt: (b, 0, 0, g, 0)),
        scratch_shapes=[pltpu.VMEM((CMP_PAGES * (PAGE_SIZE // CMP_BLOCK) * SUBLANE, HEAD_DIM), F32)])
    return pl.pallas_call(
        _compress_sample_body,
        out_shape=jax.ShapeDtypeStruct((nbs, KV_COMP, 2, nblk_half, HEAD_DIM), F32),
        grid_spec=grid_spec,
        compiler_params=_cparams(("arbitrary", "arbitrary")),
        name="nsa_compress_sample",
    )(page_table.reshape(-1), *([cache_v] * CMP_PAGES), pe_q, w1_q, b1.reshape(2, 1, HEAD_DIM), w2)


def _rows_from_lanes(row, ngrp):
    ridx = lax.broadcasted_iota(I32, (SUBLANE, HEAD_DIM), 0)
    out = jnp.zeros((SUBLANE, HEAD_DIM), F32)
    for g in range(ngrp):
        out = jnp.where(ridx == g, jnp.broadcast_to(row[:, g * HEAD_DIM:(g + 1) * HEAD_DIM], (SUBLANE, HEAD_DIM)), out)
    return out


def _slope_rows(slopes_ref, h):
    ridx = lax.broadcasted_iota(I32, (SUBLANE, 1), 0)
    slope = jnp.zeros((SUBLANE, 1), F32)
    for g in range(NSA_GROUP):
        slope = jnp.where(ridx == g, slopes_ref[h * NSA_GROUP + g], slope)
    return slope


def _nsa_sample_select_body(slopes_ref, q_ref, ckv_ref, ocmp_ref, idx_ref, *, past):
    ncb = past // CMP_BLOCK // 2
    nblk = past // SEL_BLOCK
    ridx = lax.broadcasted_iota(I32, (SUBLANE, 1), 0)
    midx = lax.broadcasted_iota(I32, (1, ncb), 1)
    jb = lax.broadcasted_iota(I32, (1, nblk), 1)
    slot = lax.broadcasted_iota(I32, (1, SEL_TOPN), 1)
    for h in range(NSA_KV_HEADS):
        q8 = _rows_from_lanes(q_ref[0:1, h * NSA_GROUP * HEAD_DIM:(h + 1) * NSA_GROUP * HEAD_DIM], NSA_GROUP)
        q8 = q8 * (HEAD_DIM ** -0.5)
        slope = _slope_rows(slopes_ref, h)
        lg = []
        for par in range(2):
            s_c = _mm3(q8, ckv_ref[0, h, par], NT)
            cend = midx * (2 * CMP_BLOCK) + (CMP_BLOCK - 1) + par * CMP_BLOCK
            lg.append(s_c - slope * (past - cend).astype(F32))
        mx = jnp.maximum(jnp.max(lg[0], axis=1, keepdims=True), jnp.max(lg[1], axis=1, keepdims=True))
        e0 = jnp.exp(lg[0] - mx)
        e1 = jnp.exp(lg[1] - mx)
        den = jnp.sum(e0, axis=1, keepdims=True) + jnp.sum(e1, axis=1, keepdims=True)
        p0 = e0 / den
        p1 = e1 / den
        o_cmp = _mm1(p0, ckv_ref[0, 2 + h, 0]) + _mm1(p1, ckv_ref[0, 2 + h, 1])
        ocmp_ref[0, h * NSA_GROUP:(h + 1) * NSA_GROUP, :] = o_cmp[0:NSA_GROUP]
        imp = jnp.sum(jnp.where(ridx < NSA_GROUP, p0 + p1, 0.0), axis=0, keepdims=True)
        forced = (jb == 0) | (jb == nblk - 1)
        score = jnp.where(forced, FORCE_SCORE, imp)
        picks = jnp.full((1, SEL_TOPN), nblk, I32)
        for k in range(SEL_TOPN - 1):
            m = jnp.max(score, axis=1, keepdims=True)
            first = jnp.min(jnp.where(score == m, jb, nblk), axis=1, keepdims=True)
            score = jnp.where(jb == first, NEG_BIG, score)
            picks = jnp.where(slot == k, first, picks)
        idx_ref[0, h:h + 1, :] = picks


def _nsa_sample_select(p2s, ckv_s, slopes, nbs, past):
    nblk_half = past // CMP_BLOCK // 2
    return pl.pallas_call(
        functools.partial(_nsa_sample_select_body, past=past),
        out_shape=(jax.ShapeDtypeStruct((nbs, NSA_HEADS, HEAD_DIM), F32),
                   jax.ShapeDtypeStruct((nbs, NSA_KV_HEADS, SEL_TOPN), I32)),
        grid=(nbs,),
        in_specs=[pl.BlockSpec(memory_space=pltpu.SMEM),
                  pl.BlockSpec((SUBLANE, NSA_WIDTH), lambda b: (b * (SROWS // SUBLANE), 0)),
                  pl.BlockSpec((1, 4, 2, nblk_half, HEAD_DIM), lambda b: (b, 0, 0, 0, 0))],
        out_specs=(pl.BlockSpec((1, NSA_HEADS, HEAD_DIM), lambda b: (b, 0, 0)),
                   pl.BlockSpec((1, NSA_KV_HEADS, SEL_TOPN), lambda b: (b, 0, 0))),
        compiler_params=_cparams(("arbitrary",)),
        name="nsa_sample_select",
    )(slopes, p2s, ckv_s)


NSEL_PAST = SEL_TOPN - 1


def _nsa_sample_attend_body(idx_ref, pt_ref, slopes_ref, q_ref, gt_ref, ocmp_ref, skn_ref, svn_ref, wkn_ref, wvn_ref,
                            *refs, past, n_pages):
    blocks = refs[:NSEL_PAST]
    wc_ref, o_ref = refs[NSEL_PAST:]
    b = pl.program_id(0)
    h = pl.program_id(1)
    q8 = _rows_from_lanes(q_ref[0:1, :], NSA_GROUP) * (HEAD_DIM ** -0.5)
    q8b = q8.astype(BF16)
    slope = _slope_rows(slopes_ref, h)

    def attend_with_new(k_old, v_old, lg_bias, k_new, v_new):
        lg = _dot(q8b, k_old.astype(BF16), NT) + lg_bias
        s_new = jnp.sum(q8 * k_new, axis=1, keepdims=True)
        m = jnp.maximum(jnp.max(lg, axis=1, keepdims=True), s_new)
        p = jnp.exp(lg - m)
        pn = jnp.exp(s_new - m)
        den = jnp.sum(p, axis=1, keepdims=True) + pn
        return (_dot(p.astype(BF16), v_old.astype(BF16)) + pn * v_new) / den

    k_sel = jnp.concatenate([r[0, pl.ds(h, SEL_BLOCK, stride=TOKEN_ROWS), :] for r in blocks], axis=0)
    v_sel = jnp.concatenate([r[0, pl.ds(NSA_KV_HEADS + h, SEL_BLOCK, stride=TOKEN_ROWS), :] for r in blocks], axis=0)
    lane = lax.broadcasted_iota(I32, (1, NSEL_PAST * SEL_BLOCK), 1)
    slot = lane >> (SEL_BLOCK.bit_length() - 1)
    blk = jnp.zeros((1, NSEL_PAST * SEL_BLOCK), I32)
    for j in range(NSEL_PAST):
        blk = jnp.where(slot == j, idx_ref[(b * NSA_KV_HEADS + h) * SEL_TOPN + j], blk)
    dist = (past - (blk * SEL_BLOCK + (lane & (SEL_BLOCK - 1)))).astype(F32)
    o_sel = attend_with_new(k_sel, v_sel, -slope * dist, skn_ref[0:1, :], svn_ref[0:1, :])

    nw = wc_ref.shape[1] // TOKEN_ROWS
    r = lax.broadcasted_iota(I32, (1, nw), 1)
    delta = nw - r
    wmask = (delta < WINDOW) & (past - delta >= 0)
    k_win = wc_ref[0, pl.ds(h, nw, stride=TOKEN_ROWS), :]
    v_win = wc_ref[0, pl.ds(NSA_KV_HEADS + h, nw, stride=TOKEN_ROWS), :]
    o_win = attend_with_new(k_win, v_win, jnp.where(wmask, -slope * delta.astype(F32), NEG_INF),
                            wkn_ref[0:1, :], wvn_ref[0:1, :])

    gs = jax.nn.sigmoid(gt_ref[0:1, :])
    o_ref[...] = jnp.zeros_like(o_ref)
    for g in range(NSA_GROUP):
        def gate(c):
            a = gs[:, g * 3 + c:g * 3 + c + 1]
            bb = gs[:, NSA_GROUP * 3 + g * 3 + c:NSA_GROUP * 3 + g * 3 + c + 1]
            return jnp.where(h == 0, a, bb)

        oc = jnp.where(h == 0, ocmp_ref[0, g:g + 1, :], ocmp_ref[0, NSA_GROUP + g:NSA_GROUP + g + 1, :])
        og = (gate(0) * oc + gate(1) * o_sel[g:g + 1]) + gate(2) * o_win[g:g + 1]
        o_ref[0:1, g * HEAD_DIM:(g + 1) * HEAD_DIM] = og


def _nsa_sample_attend(p2s, ocmp, idx, page_table, cache_sel, win_state, slopes, nbs, past):
    n_pages = page_table.shape[1]
    n_phys = cache_sel.shape[0]
    nw = win_state.shape[1]
    halves = PAGE_SIZE // SEL_BLOCK
    sel_v = cache_sel.reshape(n_phys * halves, SEL_BLOCK * TOKEN_ROWS, HEAD_DIM)
    win_v = win_state.reshape(nbs, nw * TOKEN_ROWS, HEAD_DIM)
    rb = SROWS // SUBLANE
    kvs0 = COL_KVS // LANE
    kvw0 = COL_KVW // LANE
    misc = COL_MISC // LANE

    def sel_spec(j):
        def index(b, h, idx, pt):
            blk = idx[(b * NSA_KV_HEADS + h) * SEL_TOPN + j]
            page = pt[b * n_pages + blk // halves]
            return (page * halves + blk % halves, 0, 0)
        return pl.BlockSpec((1, SEL_BLOCK * TOKEN_ROWS, HEAD_DIM), index)

    new = lambda c0: pl.BlockSpec((SUBLANE, LANE), lambda b, h, idx, pt, c0=c0: (b * rb, c0 + h))
    grid_spec = pltpu.PrefetchScalarGridSpec(
        num_scalar_prefetch=2,
        grid=(nbs, NSA_KV_HEADS),
        in_specs=[pl.BlockSpec(memory_space=pltpu.SMEM),
                  pl.BlockSpec((SUBLANE, NSA_GROUP * HEAD_DIM), lambda b, h, idx, pt: (b * rb, h)),
                  pl.BlockSpec((SUBLANE, LANE), lambda b, h, idx, pt: (b * rb, misc)),
                  pl.BlockSpec((1, NSA_HEADS, HEAD_DIM), lambda b, h, idx, pt: (b, 0, 0)),
                  new(kvs0), new(kvs0 + 2), new(kvw0), new(kvw0 + 2)]
        + [sel_spec(j) for j in range(NSEL_PAST)]
        + [pl.BlockSpec((1, nw * TOKEN_ROWS, HEAD_DIM), lambda b, h, idx, pt: (b, 0, 0))],
        out_specs=pl.BlockSpec((SROWS, NSA_GROUP * HEAD_DIM), lambda b, h, idx, pt: (b, h)))
    return pl.pallas_call(
        functools.partial(_nsa_sample_attend_body, past=past, n_pages=n_pages),
        out_shape=jax.ShapeDtypeStruct((nbs * SROWS, NSA_WIDTH), F32),
        grid_spec=grid_spec,
        compiler_params=_cparams(("arbitrary", "arbitrary")),
        name="nsa_sample_attend",
    )(idx.reshape(-1), page_table.reshape(-1), slopes, p2s, p2s, ocmp, p2s, p2s, p2s, p2s,
      *([sel_v] * NSEL_PAST), win_v)


def _perm_w_in(w_in):
    d = w_in.shape[0]
    g0 = NSA_WIDTH + 3 * 2 * KV_WIDTH
    q0 = g0 + 3 * NSA_HEADS
    b0 = q0 + 3 * GDN_WIDTH + GDN_WIDTH
    pad = jnp.zeros((d, N_PROJ - (b0 + 2 * GDN_HEADS)), w_in.dtype)
    w2 = jnp.concatenate([w_in[:, :g0], w_in[:, q0:b0], w_in[:, g0:q0], w_in[:, b0:], pad], axis=1)
    return w2.astype(BF16)


def _split_bf16(w):
    hi = w.astype(BF16)
    lo = (w - hi.astype(F32)).astype(BF16)
    return hi, lo


def _mixer_tail(x, o_nsa, o_gdn, w_out_bf, g1, norm_ffn, sc2, sh2, g2, wq_hi, wq_lo, keys, u_bf, vt_bf,
                final_norm, seq_len, tm, tn_sel, tn_mm, tm_final):
    x1, h2 = _out_proj(o_nsa, o_gdn, x, w_out_bf, g1, norm_ffn, sc2, sh2, seq_len, tm)
    cnt, al, code, be = _peer_select(h2, wq_hi, wq_lo, keys, tn_sel, split_query=h2.shape[0] <= tn_sel)
    peer_t = _peer_mm(h2.astype(BF16), u_bf, vt_bf, cnt, al, code, be, tn_mm)
    return _final(x1, peer_t, g2, final_norm, seq_len, tm_final)


def kernel(x_prompt, x_sample, cache_cmp_kv, cache_sel_kv, state_win_kv, state_conv, state_gdn, page_table,
           c_prompt, c_sample, w_ada, b_ada, norm_mix, norm_ffn, w_in, cmp_pe, cmp_w1, cmp_b1, cmp_w2, conv_w,
           gdn_a_log, gdn_dt_bias, gdn_norm, w_out, peer_wq, peer_keys, peer_u, peer_v, final_norm):
    nb, seq, d = x_prompt.shape
    nbs, dec_seq, _ = x_sample.shape
    assert w_in.shape[0] == 1 and dec_seq == 1, "single layer, single decode token"
    past = page_table.shape[1] * PAGE_SIZE
    slopes = 2.0 ** (-8.0 * jnp.arange(1, NSA_HEADS + 1, dtype=F32) / NSA_HEADS)
    rows_c = 16
    c_all = jnp.concatenate([c_prompt, c_sample, jnp.zeros((rows_c - nb - nbs, d), F32)], axis=0)
    mod = _adaln(c_all, w_ada[0], b_ada[0]).reshape(rows_c, 6, d)
    pm = lambda k: mod[0:nb, k][:, None, :]
    sm = lambda k: mod[nb:nb + nbs, k][:, None, :]
    w2 = _perm_w_in(w_in[0])
    w_out_bf = w_out[0].astype(BF16)
    wq_hi, wq_lo = _split_bf16(peer_wq[0].T)
    u_bf = peer_u[0].astype(BF16)
    vt_bf = peer_v[0].T.astype(BF16)
    cw = (cmp_pe[0], cmp_w1[0], cmp_b1[0], cmp_w2[0])
    kvw = 2 * KV_WIDTH

    xp = x_prompt.reshape(nb * seq, d)
    p2 = _proj_in(xp, norm_mix[0], pm(1), pm(0), w2, seq, 1024)
    ckv = _compress_prompt(p2, *cw, nb, seq)
    o_nsa = _nsa_prompt(p2, ckv, slopes, nb, seq)
    conv0 = jnp.zeros((nb, SUBLANE, 3 * GDN_WIDTH), F32)
    s0 = jnp.zeros((nb, GDN_HEADS, HEAD_DIM, HEAD_DIM), F32)
    o_gdn, gdn_p = _gdn(p2, conv_w[0], conv0, s0, gdn_a_log[0], gdn_dt_bias[0], gdn_norm[0], nb, seq, seq)
    y_prompt = _mixer_tail(xp, o_nsa, o_gdn, w_out_bf, pm(2), norm_ffn[0], pm(4), pm(3), pm(5), wq_hi, wq_lo,
                           peer_keys[0], u_bf, vt_bf, final_norm, seq, 512, 256, 512, 512).reshape(nb, seq, d)
    p3 = p2.reshape(nb, seq, N_PROJ)
    keep = min(WINDOW, seq)
    kv5 = lambda a: a.reshape(a.shape[0], a.shape[1], 2, NSA_KV_HEADS, HEAD_DIM)
    cmp_p = kv5(p3[:, :, COL_KVC:COL_KVC + kvw])
    sel_p = kv5(p3[:, :, COL_KVS:COL_KVS + kvw])
    win_p = kv5(p3[:, seq - keep:, COL_KVW:COL_KVW + kvw])
    conv_p = p3[:, seq - (CONV_WIDTH - 1):, COL_QKV:COL_QKV + 3 * GDN_WIDTH]

    xs = jnp.pad(x_sample, ((0, 0), (0, SROWS - dec_seq), (0, 0))).reshape(nbs * SROWS, d)
    p2s = _proj_in(xs, norm_mix[0], sm(1), sm(0), w2, SROWS, nbs * SROWS)
    ckv_s = _compress_sample(cache_cmp_kv[0], page_table, *cw)
    ocmp, idx = _nsa_sample_select(p2s, ckv_s, slopes, nbs, past)
    o_nsa_s = _nsa_sample_attend(p2s, ocmp, idx, page_table, cache_sel_kv[0], state_win_kv[0], slopes, nbs, past)
    conv_buf = jnp.pad(state_conv[0], ((0, 0), (SUBLANE - (CONV_WIDTH - 1), 0), (0, 0)))
    o_gdn_s, gdn_s = _gdn(p2s, conv_w[0], conv_buf, state_gdn[0], gdn_a_log[0], gdn_dt_bias[0], gdn_norm[0],
                          nbs, SROWS, dec_seq)
    head_rows = lambda a: a.reshape(nbs, SROWS, a.shape[-1])[:, :SROWS_TAIL].reshape(nbs * SROWS_TAIL, a.shape[-1])
    nt = nbs * SROWS_TAIL
    ys = _mixer_tail(head_rows(xs), head_rows(o_nsa_s), head_rows(o_gdn_s), w_out_bf, sm(2), norm_ffn[0], sm(4),
                     sm(3), sm(5), wq_hi, wq_lo, peer_keys[0], u_bf, vt_bf, final_norm, SROWS_TAIL, nt, nt, nt, nt)
    y_sample = ys.reshape(nbs, SROWS_TAIL, d)[:, 0:dec_seq]
    p3s = p2s.reshape(nbs, SROWS, N_PROJ)[:, 0:dec_seq]
    cmp_s = kv5(p3s[:, :, COL_KVC:COL_KVC + kvw])
    sel_s = kv5(p3s[:, :, COL_KVS:COL_KVS + kvw])
    win_new = kv5(p3s[:, :, COL_KVW:COL_KVW + kvw])
    win_s = jnp.concatenate([state_win_kv[0], win_new], axis=1)[:, dec_seq:]
    conv_s = jnp.concatenate([state_conv[0], p3s[:, :, COL_QKV:COL_QKV + 3 * GDN_WIDTH]], axis=1)[:, dec_seq:]

    return (y_prompt, y_sample, cmp_p[None], cmp_s[None], sel_p[None], sel_s[None], win_p[None], win_s[None],
            conv_p[None], conv_s[None], gdn_p[None], gdn_s[None])
```

```python
import functools
import math

import jax
import jax.numpy as jnp
from jax import lax
from jax.experimental import pallas as pl
from jax.experimental.pallas import tpu as pltpu

F32 = jnp.float32
BF16 = jnp.bfloat16
I32 = jnp.int32

D_MODEL = 2048
HEAD_DIM = 128
NSA_WIDTH = 1024
GDN_WIDTH = 1024
NSA_HEADS = 8
NSA_KV_HEADS = 2
NSA_GROUP = 4
KV_WIDTH = 256
CMP_BLOCK = 32
SEL_BLOCK = 64
SEL_TOPN = 16
WINDOW = 512
GDN_HEADS = 8
CONV_WIDTH = 4
GDN_CHUNK = 64
PEER_HEADS = 8
PEER_NKEYS = 128
PEER_QDIM = 256
PEER_TOPK = 16
PAGE_SIZE = 128
NORM_EPS = 1e-6
NEG_INF = -1e30
FORCE_SCORE = 1e4
NEG_BIG = -3.0e38

LANE = 128
SUBLANE = 8
VMEM_LIMIT = 56 * 1024 * 1024

COL_Q = 0
COL_KVC = 1024
COL_KVS = 1536
COL_KVW = 2048
COL_QKV = 2560
COL_Z = 5632
COL_MISC = 6656
N_PROJ = 6912
MISC_BL = 24
MISC_AL = 32

NN = (((1,), (0,)), ((), ()))
NT = (((1,), (1,)), ((), ()))
BNN = (((2,), (1,)), ((0,), (0,)))
BNT = (((2,), (2,)), ((0,), (0,)))


def _dot(a, b, dims=NN):
    return lax.dot_general(a, b, dims, preferred_element_type=F32)


def _split(x):
    hi = x.astype(BF16)
    lo = (x - hi.astype(F32)).astype(BF16)
    return hi, lo


def _mm1(a, b, dims=NN):
    return _dot(a.astype(BF16), b.astype(BF16), dims)


def _mm3(a, b, dims=NN):
    ah, al = _split(a)
    bh, bl = _split(b)
    return _dot(ah, bh, dims) + (_dot(al, bh, dims) + _dot(ah, bl, dims))


def _cparams(sem, vmem=VMEM_LIMIT):
    return pltpu.CompilerParams(dimension_semantics=sem, vmem_limit_bytes=vmem)


def _adaln_body(c_ref, w_ref, b_ref, o_ref):
    a = jax.nn.silu(c_ref[...])
    o_ref[...] = _mm3(a, w_ref[...]) + b_ref[...]


def _adaln(c_all, w, b):
    rows, d = c_all.shape
    n = w.shape[1]
    tn = 1024
    return pl.pallas_call(
        _adaln_body,
        out_shape=jax.ShapeDtypeStruct((rows, n), F32),
        grid=(n // tn,),
        in_specs=[pl.BlockSpec((rows, d), lambda j: (0, 0)),
                  pl.BlockSpec((d, tn), lambda j: (0, j)),
                  pl.BlockSpec((1, tn), lambda j: (0, j))],
        out_specs=pl.BlockSpec((rows, tn), lambda j: (0, j)),
        compiler_params=_cparams(("arbitrary",)),
        name="adaln",
    )(c_all, w, b.reshape(1, n))


def _seq_rows(m_ref, seq_len):
    nseq = m_ref.shape[0]
    if nseq == 1:
        return m_ref[0]
    return jnp.concatenate([jnp.broadcast_to(m_ref[r], (seq_len, m_ref.shape[2])) for r in range(nseq)], axis=0)


def _seq_spec(seq_len, tm, d):
    if tm <= seq_len:
        tpb = seq_len // tm
        return pl.BlockSpec((1, 1, d), lambda i, *_: (i // tpb, 0, 0))
    return pl.BlockSpec((tm // seq_len, 1, d), lambda i, *_: (i, 0, 0))


def _proj_in_body(x_ref, nw_ref, sc_ref, sh_ref, w_ref, o_ref, h_ref, *, seq_len):
    @pl.when(pl.program_id(1) == 0)
    def _():
        x = x_ref[...]
        y = x * lax.rsqrt(jnp.mean(x * x, axis=-1, keepdims=True) + NORM_EPS)
        h = (y * nw_ref[...]) * (1.0 + _seq_rows(sc_ref, seq_len)) + _seq_rows(sh_ref, seq_len)
        h_ref[...] = h.astype(BF16)

    o_ref[...] = _dot(h_ref[...], w_ref[...])


def _proj_in(x, norm_w, sc, sh, w2, seq_len, tm):
    n, d = x.shape
    ncol = w2.shape[1]
    tn = 1152
    mod = _seq_spec(seq_len, tm, d)
    return pl.pallas_call(
        functools.partial(_proj_in_body, seq_len=seq_len),
        out_shape=jax.ShapeDtypeStruct((n, ncol), F32),
        grid=(n // tm, ncol // tn),
        in_specs=[pl.BlockSpec((tm, d), lambda i, j: (i, 0)),
                  pl.BlockSpec((1, d), lambda i, j: (0, 0)),
                  mod, mod,
                  pl.BlockSpec((d, tn), lambda i, j: (0, j))],
        out_specs=pl.BlockSpec((tm, tn), lambda i, j: (i, j)),
        scratch_shapes=[pltpu.VMEM((tm, d), BF16)],
        compiler_params=_cparams(("arbitrary", "arbitrary")),
        name="proj_in",
    )(x, norm_w.reshape(1, d), sc, sh, w2)


def _compress_prompt_body(x_ref, pe_ref, w1_ref, b1_ref, w2_ref, o_ref, *, nblk):
    half = nblk // 2
    xs = []
    for l in range(CMP_BLOCK):
        xe = x_ref[pl.ds(l, half, stride=2 * CMP_BLOCK), :]
        xo = x_ref[pl.ds(CMP_BLOCK + l, half, stride=2 * CMP_BLOCK), :]
        xs.append((jnp.concatenate([xe, xo], axis=0) + pe_ref[0, l:l + 1, :]).astype(BF16))
    acc = _dot(jnp.concatenate(xs, axis=1), w1_ref[0].astype(BF16))
    hid = jax.nn.gelu(acc + b1_ref[0])
    o_ref[0, 0] = _mm1(hid, w2_ref[0])


def _compress_prompt(p2, pe, w1, b1, w2, nbatch, seq_len):
    nblk = seq_len // CMP_BLOCK
    kv0 = COL_KVC // LANE
    return pl.pallas_call(
        functools.partial(_compress_prompt_body, nblk=nblk),
        out_shape=jax.ShapeDtypeStruct((nbatch, 4, nblk, HEAD_DIM), F32),
        grid=(nbatch, 4),
        in_specs=[pl.BlockSpec((seq_len, LANE), lambda b, sh: (b, kv0 + sh)),
                  pl.BlockSpec((1, CMP_BLOCK, HEAD_DIM), lambda b, sh: (sh // 2, 0, 0)),
                  pl.BlockSpec((1, CMP_BLOCK * HEAD_DIM, HEAD_DIM), lambda b, sh: (sh // 2, 0, 0)),
                  pl.BlockSpec((1, 1, HEAD_DIM), lambda b, sh: (sh // 2, 0, 0)),
                  pl.BlockSpec((1, HEAD_DIM, HEAD_DIM), lambda b, sh: (sh // 2, 0, 0))],
        out_specs=pl.BlockSpec((1, 1, nblk, HEAD_DIM), lambda b, sh: (b, sh, 0, 0)),
        compiler_params=_cparams(("arbitrary", "arbitrary")),
        name="nsa_compress_prompt",
    )(p2, pe, w1.reshape(2, CMP_BLOCK * HEAD_DIM, HEAD_DIM), b1.reshape(2, 1, HEAD_DIM), w2)


TQ = 128
NSA_KSPAN = 512


def _nsa_prompt_body(slopes_ref, q_ref, gt_ref, ck_ref, cv_ref, sk_ref, sv_ref, wk_ref, wv_ref,
                     o_ref, selm_ref, *, seq_len):
    h = pl.program_id(1)
    qi = pl.program_id(2)
    t0 = qi * TQ
    nsb = seq_len // SEL_BLOCK
    ncb = seq_len // CMP_BLOCK // 2
    rows = NSA_GROUP * TQ

    qb = q_ref[...] * (HEAD_DIM ** -0.5)
    q4 = jnp.concatenate([qb[:, g * HEAD_DIM:(g + 1) * HEAD_DIM] for g in range(NSA_GROUP)], axis=0)
    q4b = q4.astype(BF16)
    row = lax.broadcasted_iota(I32, (rows, 1), 0)
    tq = t0 + (row & (TQ - 1))
    grow = row >> 7
    slope = jnp.zeros((rows, 1), F32)
    for g in range(NSA_GROUP):
        slope = jnp.where(grow == g, slopes_ref[h * NSA_GROUP + g], slope)

    ck = ck_ref[0, 0]
    cv = cv_ref[0, 0]
    lane_r = lax.broadcasted_iota(I32, (1, rows), 1)
    tq_l = t0 + (lane_r & (TQ - 1))
    slope_l = jnp.zeros((1, rows), F32)
    for g in range(NSA_GROUP):
        slope_l = jnp.where((lane_r >> 7) == g, slopes_ref[h * NSA_GROUP + g], slope_l)
    midx = lax.broadcasted_iota(I32, (ncb, 1), 0)
    logits = []
    valids = []
    for par in range(2):
        s_t = _mm3(ck[par * ncb:(par + 1) * ncb], q4, NT)
        cend = midx * (2 * CMP_BLOCK) + (CMP_BLOCK - 1) + par * CMP_BLOCK
        valid = cend <= tq_l
        dist = (tq_l - cend).astype(F32)
        logits.append(jnp.where(valid, s_t - slope_l * dist, NEG_INF))
        valids.append(valid)
    mx = jnp.maximum(jnp.max(logits[0], axis=0, keepdims=True), jnp.max(logits[1], axis=0, keepdims=True))
    e0 = jnp.exp(logits[0] - mx)
    e1 = jnp.exp(logits[1] - mx)
    den = jnp.sum(e0, axis=0, keepdims=True) + jnp.sum(e1, axis=0, keepdims=True)
    p0 = jnp.where(valids[0], e0 / den, 0.0)
    p1 = jnp.where(valids[1], e1 / den, 0.0)
    ident = (lax.broadcasted_iota(I32, (TQ, TQ), 0) == lax.broadcasted_iota(I32, (TQ, TQ), 1)).astype(BF16)
    p0b = p0.astype(BF16)
    p1b = p1.astype(BF16)
    o_cmp = jnp.concatenate(
        [_dot(_dot(ident, p0b[:, g * TQ:(g + 1) * TQ], NT).astype(BF16), cv[0:ncb].astype(BF16))
         + _dot(_dot(ident, p1b[:, g * TQ:(g + 1) * TQ], NT).astype(BF16), cv[ncb:2 * ncb].astype(BF16))
         for g in range(NSA_GROUP)], axis=0)
    pb = p0 + p1
    imp = pb[:, 0:TQ]
    for g in range(1, NSA_GROUP):
        imp = imp + pb[:, g * TQ:(g + 1) * TQ]

    cur = (t0 + lax.broadcasted_iota(I32, (1, TQ), 1)) >> 6
    jb = lax.broadcasted_iota(I32, (nsb, 1), 0)
    forced = (jb == 0) | (jb == cur) | (jb == cur - 1)
    score = jnp.where(jb <= cur, jnp.where(forced, FORCE_SCORE, imp), NEG_INF)
    cnt = jnp.zeros((nsb, TQ), I32)
    for i in range(nsb):
        si = score[i:i + 1, :]
        beats = (si > score) | ((si == score) & (jb > i))
        cnt = cnt + beats.astype(I32)
    sel_t = ((cnt < min(SEL_TOPN, nsb)) & (score > 0.5 * NEG_INF)).astype(BF16)
    sel = _dot(ident, sel_t, NT).astype(BF16)
    kpos = lax.broadcasted_iota(I32, (nsb, seq_len), 1)
    kblk = lax.broadcasted_iota(I32, (nsb, seq_len), 0)
    expand = ((kpos >> 6) == kblk).astype(BF16)
    selk = _dot(sel, expand)

    kw = NSA_KSPAN
    t_row = t0 + lax.broadcasted_iota(I32, (TQ, kw), 0)
    key = lax.broadcasted_iota(I32, (TQ, kw), 1)
    for st in range(seq_len // kw):
        blk = (selk[:, st * kw:(st + 1) * kw] - 1.0) * (-NEG_INF)
        selm_ref[st] = blk + jnp.where(st * kw + key <= t_row, 0.0, NEG_INF)
    m_floor = 0.1 * NEG_INF

    def span_update(carry, k, v, bias, key0):
        m, l, acc = carry
        width = k.shape[0]
        alibi = slope * lax.broadcasted_iota(I32, (1, width), 1).astype(F32)
        x = _dot(q4b, k, NT) + alibi
        x = (x.reshape(NSA_GROUP, TQ, width) + bias[None]).reshape(rows, width)
        shift = slope * (key0 - tq).astype(F32)
        m_new = jnp.maximum(m, jnp.max(x, axis=1, keepdims=True) + shift)
        alpha = jnp.exp(m - m_new)
        p = jnp.exp(x - (m_new - shift))
        l = alpha * l + jnp.sum(p, axis=1, keepdims=True)
        acc = alpha * acc + _dot(p.astype(BF16), v)
        return m_new, l, acc

    def finish(carry):
        _, l, acc = carry
        return jnp.where(l > 0.0, acc / jnp.where(l > 0.0, l, 1.0), 0.0)

    init = (jnp.full((rows, 1), m_floor, F32), jnp.zeros((rows, 1), F32), jnp.zeros((rows, HEAD_DIM), F32))

    def sel_step(st, carry):
        start = pl.multiple_of(st * kw, kw)
        return span_update(carry, sk_ref[pl.ds(start, kw), :].astype(BF16), sv_ref[pl.ds(start, kw), :].astype(BF16),
                           selm_ref[st], start)

    o_sel = finish(lax.fori_loop(0, (t0 + TQ - 1) // kw + 1, sel_step, init))

    ww = min(WINDOW + TQ, seq_len)
    wstart = pl.multiple_of(jnp.maximum(t0 + TQ - ww, 0), TQ)
    wdelta = (t0 + lax.broadcasted_iota(I32, (TQ, ww), 0)) - (wstart + lax.broadcasted_iota(I32, (TQ, ww), 1))
    wbias = jnp.where((wdelta >= 0) & (wdelta < WINDOW), 0.0, NEG_INF)
    o_win = finish(span_update(init, wk_ref[pl.ds(wstart, ww), :].astype(BF16), wv_ref[pl.ds(wstart, ww), :].astype(BF16),
                               wbias, wstart))

    gs = jax.nn.sigmoid(gt_ref[...])
    for g in range(NSA_GROUP):
        def gate(c):
            a = gs[:, g * 3 + c:g * 3 + c + 1]
            b = gs[:, NSA_GROUP * 3 + g * 3 + c:NSA_GROUP * 3 + g * 3 + c + 1]
            return jnp.where(h == 0, a, b)

        sl = slice(g * TQ, (g + 1) * TQ)
        o_ref[:, g * HEAD_DIM:(g + 1) * HEAD_DIM] = (gate(0) * o_cmp[sl] + gate(1) * o_sel[sl]) + gate(2) * o_win[sl]


def _nsa_prompt(p2, ckv, slopes, nbatch, seq_len):
    nq = seq_len // TQ
    kvs0 = COL_KVS // LANE
    kvw0 = COL_KVW // LANE
    misc = COL_MISC // LANE
    nblk = seq_len // CMP_BLOCK
    full = lambda c0: pl.BlockSpec((seq_len, LANE), lambda b, h, q, c0=c0: (b, c0 + h))
    return pl.pallas_call(
        functools.partial(_nsa_prompt_body, seq_len=seq_len),
        out_shape=jax.ShapeDtypeStruct((nbatch * seq_len, NSA_WIDTH), F32),
        grid=(nbatch, NSA_KV_HEADS, nq),
        in_specs=[pl.BlockSpec(memory_space=pltpu.SMEM),
                  pl.BlockSpec((TQ, NSA_GROUP * HEAD_DIM), lambda b, h, q: (b * nq + q, h)),
                  pl.BlockSpec((TQ, LANE), lambda b, h, q: (b * nq + q, misc)),
                  pl.BlockSpec((1, 1, nblk, HEAD_DIM), lambda b, h, q: (b, h, 0, 0)),
                  pl.BlockSpec((1, 1, nblk, HEAD_DIM), lambda b, h, q: (b, 2 + h, 0, 0)),
                  full(kvs0), full(kvs0 + 2), full(kvw0), full(kvw0 + 2)],
        out_specs=pl.BlockSpec((TQ, NSA_GROUP * HEAD_DIM), lambda b, h, q: (b * nq + q, h)),
        scratch_shapes=[pltpu.VMEM((seq_len // NSA_KSPAN, TQ, NSA_KSPAN), F32)],
        compiler_params=_cparams(("arbitrary", "arbitrary", "arbitrary")),
        name="nsa_prompt",
    )(slopes, p2, p2, ckv, ckv, p2, p2, p2, p2)


GDN_CB = 8


def _gdn_chunk(seq_len):
    return min(GDN_CHUNK, seq_len)


def _bmm3(a, b):
    ah, al = _split(a)
    bh, bl = _split(b)
    return _dot(jnp.concatenate([ah, al, ah], axis=2), jnp.concatenate([bh, bh, bl], axis=1), BNN)


GDN_HG = 2


def _gdn_body(alog_ref, dtb_ref, xq_ref, xk_ref, xv_ref, z_ref, gt_ref, cwq_ref, cwk_ref, cwv_ref,
              cbq_ref, cbk_ref, cbv_ref, s0_ref, gn_ref, o_ref, sout_ref,
              cat_s, q_s, k_s, v_s, b_s, g_s, u_s, w_s, qg_s, kdt_s, qk_s, gl_s, oc_s, *, seq_len, valid):
    hg = pl.program_id(1)
    C = _gdn_chunk(seq_len)
    nc = seq_len // C
    cb = min(GDN_CB, nc)
    row = lax.broadcasted_iota(I32, (seq_len, 1), 0)
    lane = lax.broadcasted_iota(I32, (1, LANE), 1)
    pos = row & (C - 1)
    gt = gt_ref[...]

    def conv(x_ref, cw_ref, cb_ref, ls):
        x = x_ref[:, ls]
        cat_s[0:SUBLANE, :] = cb_ref[0, :, ls]
        cat_s[SUBLANE:, :] = x
        w = cw_ref[:, ls]
        acc = cat_s[pl.ds(SUBLANE - 3, seq_len), :] * w[0:1]
        acc = acc + cat_s[pl.ds(SUBLANE - 2, seq_len), :] * w[1:2]
        acc = acc + cat_s[pl.ds(SUBLANE - 1, seq_len), :] * w[2:3]
        acc = acc + x * w[3:4]
        return jax.nn.silu(acc)

    def l2n(x):
        return x * lax.rsqrt(jnp.sum(x * x, axis=-1, keepdims=True) + NORM_EPS)

    ii = lax.broadcasted_iota(I32, (C, C), 0)
    jj = lax.broadcasted_iota(I32, (C, C), 1)
    tri = (ii >= jj)[None]
    strict = (ii > jj)[None]
    eye = (ii == jj).astype(F32)[None]
    ones_b = jnp.ones((cb, C, 3 * C), BF16)
    ident_b = jnp.broadcast_to((lax.broadcasted_iota(I32, (HEAD_DIM, HEAD_DIM), 0)
                                == lax.broadcasted_iota(I32, (HEAD_DIM, HEAD_DIM), 1)).astype(BF16)[None],
                               (cb, HEAD_DIM, HEAD_DIM))

    for hl in range(GDN_HG):
        hd = hg * GDN_HG + hl
        ls = slice(hl * HEAD_DIM, (hl + 1) * HEAD_DIM)
        q = l2n(conv(xq_ref, cwq_ref, cbq_ref, ls)) * (HEAD_DIM ** -0.5)
        k = l2n(conv(xk_ref, cwk_ref, cbk_ref, ls))
        v = conv(xv_ref, cwv_ref, cbv_ref, ls)
        bl = jnp.sum(jnp.where(lane == MISC_BL + hd, gt, 0.0), axis=1, keepdims=True)
        al = jnp.sum(jnp.where(lane == MISC_AL + hd, gt, 0.0), axis=1, keepdims=True)
        beta = jax.nn.sigmoid(bl)
        a_pos = jnp.exp(jnp.full((1, 1), alog_ref[hd], F32))
        g = -a_pos * jax.nn.softplus(al + dtb_ref[hd])
        if valid < seq_len:
            vm = row < valid
            q = jnp.where(vm, q, 0.0)
            k = jnp.where(vm, k, 0.0)
            v = jnp.where(vm, v, 0.0)
            beta = jnp.where(vm, beta, 0.0)
            g = jnp.where(vm, g, 0.0)
        gc = g
        sft = 1
        while sft < C:
            gc = gc + jnp.where(pos >= sft, pltpu.roll(gc, sft, 0), 0.0)
            sft *= 2
        q_s[...] = q
        k_s[...] = k
        v_s[...] = v
        b_s[...] = jnp.broadcast_to(beta, (seq_len, LANE))
        g_s[...] = jnp.broadcast_to(gc, (seq_len, LANE))

        def local(gi, carry, hl=hl):
            r0 = pl.multiple_of(gi * (cb * C), cb * C)
            sl = pl.ds(r0, cb * C)
            qc = q_s[sl, :].reshape(cb, C, HEAD_DIM)
            kc = k_s[sl, :].reshape(cb, C, HEAD_DIM)
            vc = v_s[sl, :].reshape(cb, C, HEAD_DIM)
            bc = b_s[sl, :].reshape(cb, C, LANE)
            gcc = g_s[sl, :].reshape(cb, C, LANE)
            gcol = gcc[:, :, 0:C]
            dg = gcol * eye
            d1 = dg.astype(BF16)
            r1 = dg - d1.astype(F32)
            d2 = r1.astype(BF16)
            d3 = (r1 - d2.astype(F32)).astype(BF16)
            grow = _dot(ones_b, jnp.concatenate([d1, d2, d3], axis=1), BNN)
            diff = gcol - grow
            decay = jnp.where(tri, jnp.exp(jnp.where(tri, diff, 0.0)), 0.0)
            kb = kc * bc
            a = _dot(kb.astype(BF16), kc.astype(BF16), BNT) * jnp.where(strict, decay, 0.0)
            tinv = eye - a
            pw = a
            n = 2
            while n < C:
                pw = _bmm3(pw, pw)
                tinv = tinv + _bmm3(tinv, pw)
                n *= 2
            eg = jnp.exp(gcc)
            rhs = jnp.concatenate([vc * bc, kb * eg], axis=2)
            sol = _bmm3(tinv, rhs)
            u_s[hl, sl, :] = sol[:, :, 0:HEAD_DIM].reshape(cb * C, HEAD_DIM)
            w_s[hl, sl, :] = sol[:, :, HEAD_DIM:2 * HEAD_DIM].reshape(cb * C, HEAD_DIM)
            qk = _dot(qc.astype(BF16), kc.astype(BF16), BNT) * decay
            qk_s[hl, sl, :] = qk.reshape(cb * C, C)
            qg_s[hl, sl, :] = (qc * eg).reshape(cb * C, HEAD_DIM)
            glast = gcc[:, C - 1:C, :]
            kd = kc * jnp.exp(glast - gcc)
            kdt = _dot(ident_b, kd.astype(BF16), BNT)
            kdt_s[hl, pl.ds(pl.multiple_of(gi * (cb * HEAD_DIM), cb * HEAD_DIM), cb * HEAD_DIM), :] = (
                kdt.reshape(cb * HEAD_DIM, C))
            gl_s[hl, pl.ds(pl.multiple_of(gi * (cb * SUBLANE), cb * SUBLANE), cb * SUBLANE), :] = (
                jnp.broadcast_to(jnp.exp(glast), (cb, SUBLANE, LANE)).reshape(cb * SUBLANE, LANE))
            return carry

        lax.fori_loop(0, nc // cb, local, 0)

    def scan(c, states):
        sl = pl.ds(pl.multiple_of(c * C, C), C)
        new = []
        for hl in range(GDN_HG):
            s = states[hl]
            sb = s.astype(BF16)
            v_new = u_s[hl, sl, :] - _dot(w_s[hl, sl, :].astype(BF16), sb)
            o_c = (_dot(qg_s[hl, sl, :].astype(BF16), sb)
                   + _dot(qk_s[hl, sl, :].astype(BF16), v_new.astype(BF16)))
            oc_s[hl, sl, :] = o_c
            kdt = kdt_s[hl, pl.ds(pl.multiple_of(c * HEAD_DIM, HEAD_DIM), HEAD_DIM), :]
            gl = gl_s[hl, pl.ds(pl.multiple_of(c * SUBLANE, SUBLANE), 1), :]
            new.append(s * gl + _dot(kdt.astype(BF16), v_new.astype(BF16)))
        return tuple(new)

    s_fin = lax.fori_loop(0, nc, scan, tuple(s0_ref[0, hl] for hl in range(GDN_HG)))
    for hl in range(GDN_HG):
        ls = slice(hl * HEAD_DIM, (hl + 1) * HEAD_DIM)
        sout_ref[0, hl] = s_fin[hl]
        o = oc_s[hl]
        y = o * lax.rsqrt(jnp.mean(o * o, axis=-1, keepdims=True) + NORM_EPS) * gn_ref[...]
        o_ref[:, ls] = y * jax.nn.silu(z_ref[:, ls])


def _gdn(p2, conv_w, conv_buf8, s0, a_log, dt_bias, gn_w, nbatch, seq_len, valid):
    gw = GDN_HG * LANE
    q0 = COL_QKV // gw
    z0 = COL_Z // gw
    hgs = GDN_HEADS // GDN_HG
    misc = COL_MISC // LANE
    C = _gdn_chunk(seq_len)
    col = lambda c0: pl.BlockSpec((seq_len, gw), lambda b, h, c0=c0: (b, c0 + h))
    cw = lambda c0: pl.BlockSpec((CONV_WIDTH, gw), lambda b, h, c0=c0: (0, c0 + h))
    cbs = lambda c0: pl.BlockSpec((1, SUBLANE, gw), lambda b, h, c0=c0: (b, 0, c0 + h))
    smem = pl.BlockSpec(memory_space=pltpu.SMEM)
    nc = seq_len // C
    per_head = lambda rows, cols: pltpu.VMEM((GDN_HG, rows, cols), F32)
    return pl.pallas_call(
        functools.partial(_gdn_body, seq_len=seq_len, valid=valid),
        out_shape=(jax.ShapeDtypeStruct((nbatch * seq_len, GDN_WIDTH), F32),
                   jax.ShapeDtypeStruct((nbatch, GDN_HEADS, HEAD_DIM, HEAD_DIM), F32)),
        grid=(nbatch, hgs),
        in_specs=[smem, smem, col(q0), col(q0 + hgs), col(q0 + 2 * hgs), col(z0),
                  pl.BlockSpec((seq_len, LANE), lambda b, h: (b, misc)),
                  cw(0), cw(hgs), cw(2 * hgs), cbs(0), cbs(hgs), cbs(2 * hgs),
                  pl.BlockSpec((1, GDN_HG, HEAD_DIM, HEAD_DIM), lambda b, h: (b, h, 0, 0)),
                  pl.BlockSpec((1, HEAD_DIM), lambda b, h: (0, 0))],
        out_specs=(pl.BlockSpec((seq_len, gw), lambda b, h: (b, h)),
                   pl.BlockSpec((1, GDN_HG, HEAD_DIM, HEAD_DIM), lambda b, h: (b, h, 0, 0))),
        scratch_shapes=[pltpu.VMEM((seq_len + SUBLANE, HEAD_DIM), F32)]
        + [pltpu.VMEM((seq_len, HEAD_DIM), F32)] * 3
        + [pltpu.VMEM((seq_len, LANE), F32)] * 2
        + [per_head(seq_len, HEAD_DIM)] * 3
        + [per_head(nc * HEAD_DIM, C),
           per_head(seq_len, C),
           per_head(nc * SUBLANE, LANE),
           per_head(seq_len, HEAD_DIM)],
        compiler_params=_cparams(("arbitrary", "arbitrary")),
        name="gdn",
    )(a_log, dt_bias, p2, p2, p2, p2, p2, conv_w, conv_w, conv_w, conv_buf8, conv_buf8, conv_buf8,
      s0, gn_w.reshape(1, HEAD_DIM))


def _out_proj_body(on_ref, og_ref, x_ref, w_ref, g1_ref, nw_ref, sc_ref, sh_ref, x1_ref, h2_ref, *, seq_len):
    mix = _dot(on_ref[...].astype(BF16), w_ref[0:NSA_WIDTH, :]) + _dot(og_ref[...].astype(BF16), w_ref[NSA_WIDTH:, :])
    x1 = x_ref[...] + _seq_rows(g1_ref, seq_len) * mix
    x1_ref[...] = x1
    y = x1 * lax.rsqrt(jnp.mean(x1 * x1, axis=-1, keepdims=True) + NORM_EPS)
    h2_ref[...] = (y * nw_ref[...]) * (1.0 + _seq_rows(sc_ref, seq_len)) + _seq_rows(sh_ref, seq_len)


def _out_proj(o_nsa, o_gdn, x, w_out_bf, g1, norm_w, sc, sh, seq_len, tm):
    n, d = x.shape
    mod = _seq_spec(seq_len, tm, d)
    return pl.pallas_call(
        functools.partial(_out_proj_body, seq_len=seq_len),
        out_shape=(jax.ShapeDtypeStruct((n, d), F32), jax.ShapeDtypeStruct((n, d), F32)),
        grid=(n // tm,),
        in_specs=[pl.BlockSpec((tm, NSA_WIDTH), lambda i: (i, 0)),
                  pl.BlockSpec((tm, GDN_WIDTH), lambda i: (i, 0)),
                  pl.BlockSpec((tm, d), lambda i: (i, 0)),
                  pl.BlockSpec((d, d), lambda i: (0, 0)),
                  mod,
                  pl.BlockSpec((1, d), lambda i: (0, 0)),
                  mod, mod],
        out_specs=(pl.BlockSpec((tm, d), lambda i: (i, 0)), pl.BlockSpec((tm, d), lambda i: (i, 0))),
        compiler_params=_cparams(("arbitrary",)),
        name="out_proj",
    )(o_nsa, o_gdn, x, w_out_bf, g1, norm_w.reshape(1, d), sc, sh)


def _take_top(src_ref, dst_ref):
    s = src_ref[...]
    above = jnp.full(s.shape, float(PEER_TOPK), F32)
    for k in range(PEER_TOPK):
        m = jnp.max(s, axis=0, keepdims=True)
        hit = s == m
        above = jnp.where(hit, float(k), above)
        s = jnp.where(hit, NEG_BIG, s)
        dst_ref[k:k + 1, :] = m
    return above


PAIR_ROWS = ((0, 16, 16), (16, 8, 8), (24, 8, 5), (32, 8, 4), (40, 8, 3), (48, 8, 2), (56, 8, 2), (64, 8, 2))
PAIR_TAIL = 72
PAIR_TOTAL = 80


def _peer_sel_body(h_ref, wqh_ref, wql_ref, keys_ref, cnt_ref, al_ref, code_ref, be_ref,
                   sc_s, ta_s, tb_s, cand_s, tc_s, *, split_query):
    hh, hl = _split(h_ref[...])
    half = PEER_QDIM // 2
    tn = h_ref.shape[0]
    r8 = lax.broadcasted_iota(I32, (SUBLANE, tn), 0)
    for hd in range(PEER_HEADS):
        wh = wqh_ref[hd * PEER_QDIM:(hd + 1) * PEER_QDIM, :]
        wl = wql_ref[hd * PEER_QDIM:(hd + 1) * PEER_QDIM, :]
        qt = _dot(wh, hh, NT)
        if split_query:
            qt = qt + (_dot(wl, hh, NT) + _dot(wh, hl, NT))
        s0 = _mm3(keys_ref[0], qt[0:half])
        s1 = _mm3(keys_ref[1], qt[half:PEER_QDIM])
        sc_s[...] = s0
        _take_top(sc_s, ta_s)
        sc_s[...] = s1
        code = _take_top(sc_s, tb_s)
        a = ta_s[...]
        b = tb_s[...]
        for r, (row0, nrow, nval) in enumerate(PAIR_ROWS):
            pair = a[r:r + 1] + b[0:nrow]
            cand_s[row0:row0 + nrow, :] = pair if nval == nrow else jnp.where(r8 < nval, pair, NEG_BIG)
        cand_s[PAIR_TAIL:PAIR_TOTAL, :] = a[SUBLANE:2 * SUBLANE] + b[0:1]
        _take_top(cand_s, tc_s)
        tau = tc_s[PEER_TOPK - 1:PEER_TOPK, :]
        cand = cand_s[...]
        keep = cand >= tau
        zsum = jnp.sum(jnp.where(keep, jnp.exp(cand - cand[0:1]), 0.0), axis=0, keepdims=True)
        keepf = keep.astype(F32)
        cnt = jnp.zeros((PEER_NKEYS, tn), F32)
        for r, (row0, nrow, nval) in enumerate(PAIR_ROWS):
            cnt_r = jnp.sum(keepf[row0:row0 + nrow], axis=0, keepdims=True)
            cnt = jnp.where(s0 == a[r:r + 1], cnt_r, cnt)
        for r in range(SUBLANE, 2 * SUBLANE):
            cnt = jnp.where(s0 == a[r:r + 1], keepf[PAIR_TAIL + r - SUBLANE:PAIR_TAIL + r - SUBLANE + 1], cnt)
        cnt_ref[hd] = cnt
        al_ref[hd] = jnp.exp(s0 - a[0:1]) / zsum
        code_ref[hd] = code.astype(BF16)
        be_ref[hd] = jnp.exp(s1 - b[0:1]).astype(BF16)


def _peer_select(h2, wq_hi, wq_lo, keys, tn, split_query):
    n, d = h2.shape
    hk = jax.ShapeDtypeStruct((PEER_HEADS, PEER_NKEYS, n), F32)
    hk16 = jax.ShapeDtypeStruct((PEER_HEADS, PEER_NKEYS, n), BF16)
    blk = pl.BlockSpec((PEER_HEADS, PEER_NKEYS, tn), lambda i: (0, 0, i))
    return pl.pallas_call(
        functools.partial(_peer_sel_body, split_query=split_query),
        out_shape=(hk, hk, hk16, hk16),
        grid=(n // tn,),
        in_specs=[pl.BlockSpec((tn, d), lambda i: (i, 0)),
                  pl.BlockSpec((PEER_HEADS * PEER_QDIM, d), lambda i: (0, 0)),
                  pl.BlockSpec((PEER_HEADS * PEER_QDIM, d), lambda i: (0, 0)),
                  pl.BlockSpec((2, PEER_NKEYS, PEER_QDIM // 2), lambda i: (0, 0, 0))],
        out_specs=(blk, blk, blk, blk),
        scratch_shapes=[pltpu.VMEM((PEER_NKEYS, tn), F32), pltpu.VMEM((PEER_TOPK, tn), F32),
                        pltpu.VMEM((PEER_TOPK, tn), F32), pltpu.VMEM((PAIR_TOTAL, tn), F32),
                        pltpu.VMEM((PEER_TOPK, tn), F32)],
        compiler_params=_cparams(("arbitrary",)),
        name="peer_select",
    )(h2, wq_hi, wq_lo, keys)


PEER_TE = 1024
GELU_C0 = math.sqrt(2.0 / math.pi)
GELU_C1 = 0.044715 * math.sqrt(2.0 / math.pi)


def _gelu_tanh(x):
    hx = 0.5 * x
    return hx + hx * jnp.tanh(x * (GELU_C0 + GELU_C1 * (x * x)))


def _peer_mm_body(h_ref, u_ref, vt_ref, cnt_ref, al_ref, code_ref, be_ref, o_ref, at_s, p_s):
    e = pl.program_id(1)

    @pl.when(e == 0)
    def _():
        o_ref[...] = jnp.zeros_like(o_ref)

    at_s[...] = _dot(u_ref[...], h_ref[...], NT)
    tn = h_ref.shape[0]
    for ii in range(PEER_TE // PEER_NKEYS):
        rs = slice(ii * PEER_NKEYS, (ii + 1) * PEER_NKEYS)
        for ck in range(tn // LANE):
            cs = slice(ck * LANE, (ck + 1) * LANE)
            w = jnp.zeros((PEER_NKEYS, LANE), BF16)
            for hd in range(PEER_HEADS):
                cnt = cnt_ref[hd, ii:ii + 1, cs].astype(BF16)
                kept = jnp.where(code_ref[hd, :, cs] < cnt, be_ref[hd, :, cs], jnp.zeros((), BF16))
                w = w + al_ref[hd, ii:ii + 1, cs].astype(BF16) * kept
            p_s[rs, cs] = w * _gelu_tanh(at_s[rs, cs].astype(BF16))
    o_ref[...] += _dot(vt_ref[...], p_s[...])


def _peer_mm(h2b, u_bf, vt_bf, cnt, al, code, be, tn):
    n, d = h2b.shape
    ne = u_bf.shape[0]
    rows_i = PEER_TE // PEER_NKEYS
    sel_i = pl.BlockSpec((PEER_HEADS, rows_i, tn), lambda i, e: (0, e, i))
    sel_j = pl.BlockSpec((PEER_HEADS, PEER_NKEYS, tn), lambda i, e: (0, 0, i))
    return pl.pallas_call(
        _peer_mm_body,
        out_shape=jax.ShapeDtypeStruct((d, n), F32),
        grid=(n // tn, ne // PEER_TE),
        in_specs=[pl.BlockSpec((tn, d), lambda i, e: (i, 0)),
                  pl.BlockSpec((PEER_TE, d), lambda i, e: (e, 0)),
                  pl.BlockSpec((d, PEER_TE), lambda i, e: (0, e)),
                  sel_i, sel_i, sel_j, sel_j],
        out_specs=pl.BlockSpec((d, tn), lambda i, e: (0, i)),
        scratch_shapes=[pltpu.VMEM((PEER_TE, tn), F32), pltpu.VMEM((PEER_TE, tn), BF16)],
        compiler_params=_cparams(("arbitrary", "arbitrary")),
        name="peer_experts",
    )(h2b, u_bf, vt_bf, cnt, al, code, be)


def _final_body(x1_ref, pt_ref, g2_ref, fw_ref, y_ref, *, seq_len):
    x2 = x1_ref[...] + _seq_rows(g2_ref, seq_len) * pt_ref[...].T
    y_ref[...] = x2 * lax.rsqrt(jnp.mean(x2 * x2, axis=-1, keepdims=True) + NORM_EPS) * fw_ref[...]


def _final(x1, peer_t, g2, fw, seq_len, tm):
    n, d = x1.shape
    g2_spec = _seq_spec(seq_len, tm, d)
    return pl.pallas_call(
        functools.partial(_final_body, seq_len=seq_len),
        out_shape=jax.ShapeDtypeStruct((n, d), F32),
        grid=(n // tm,),
        in_specs=[pl.BlockSpec((tm, d), lambda i: (i, 0)),
                  pl.BlockSpec((d, tm), lambda i: (0, i)),
                  g2_spec,
                  pl.BlockSpec((1, d), lambda i: (0, 0))],
        out_specs=pl.BlockSpec((tm, d), lambda i: (i, 0)),
        compiler_params=_cparams(("arbitrary",)),
        name="final_norm",
    )(x1, peer_t, g2, fw.reshape(1, d))


SROWS = 64
SROWS_TAIL = 16
CMP_PAGES = 16
KV_COMP = 2 * NSA_KV_HEADS
TOKEN_ROWS = KV_COMP
PAGE_VROWS = PAGE_SIZE * TOKEN_ROWS
CMP_PAIRS = CMP_BLOCK * TOKEN_ROWS // SUBLANE
BLOCK_VROWS = CMP_BLOCK * TOKEN_ROWS


def _compress_sample_body(pt_ref, *refs):
    pages = refs[:CMP_PAGES]
    pe_ref, w1_ref, b1_ref, w2_ref, o_ref, acc_s = refs[CMP_PAGES:]
    bpp = PAGE_SIZE // CMP_BLOCK
    nrow = CMP_PAGES * bpp * SUBLANE
    nh = CMP_PAGES * bpp // 2
    lhs = []
    for q in range(CMP_PAIRS):
        parts = [r[0, n * BLOCK_VROWS + q * SUBLANE:n * BLOCK_VROWS + (q + 1) * SUBLANE, :]
                 for r in pages for n in range(bpp)]
        lhs.append((jnp.concatenate(parts, axis=0)
                    + jnp.concatenate([pe_ref[q]] * (nrow // SUBLANE), axis=0)).astype(BF16))
    acc = _dot(jnp.concatenate(lhs, axis=1), w1_ref[...])
    rid = lax.broadcasted_iota(I32, (nrow, 1), 0) & (SUBLANE - 1)
    want = ((rid & (KV_COMP - 1)) >> 1) * 2 + (rid >> 2)
    picked = acc[:, 0:HEAD_DIM]
    for blk in range(1, KV_COMP):
        picked = jnp.where(want == blk, acc[:, blk * HEAD_DIM:(blk + 1) * HEAD_DIM], picked)
    acc_s[...] = picked + pltpu.roll(picked, nrow - KV_COMP, 0)
    for c in range(KV_COMP):
        s = c // NSA_KV_HEADS
        hid_e = acc_s[pl.ds(c, nh, stride=2 * SUBLANE), :]
        hid_o = acc_s[pl.ds(SUBLANE + c, nh, stride=2 * SUBLANE), :]
        hid = jax.nn.gelu(jnp.concatenate([hid_e, hid_o], axis=0) + b1_ref[s])
        out = _mm1(hid, w2_ref[s])
        o_ref[0, c, 0] = out[0:nh]
        o_ref[0, c, 1] = out[nh:2 * nh]


def _compress_sample(cache, page_table, pe, w1, b1, w2):
    nbs, n_pages = page_table.shape
    n_phys = cache.shape[0]
    cache_v = cache.reshape(n_phys, PAGE_VROWS, HEAD_DIM)
    ngrp = n_pages // CMP_PAGES
    nblk_half = n_pages * PAGE_SIZE // CMP_BLOCK // 2
    nh = CMP_PAGES * (PAGE_SIZE // CMP_BLOCK) // 2
    pe_q = jnp.transpose(pe.reshape(2, CMP_PAIRS, 2, HEAD_DIM), (1, 2, 0, 3))
    pe_q = jnp.broadcast_to(pe_q[:, :, :, None, :], (CMP_PAIRS, 2, 2, NSA_KV_HEADS, HEAD_DIM))
    pe_q = pe_q.reshape(CMP_PAIRS, SUBLANE, HEAD_DIM)
    w1_q = jnp.transpose(w1.reshape(2, CMP_PAIRS, 2, HEAD_DIM, HEAD_DIM), (1, 3, 0, 2, 4))
    w1_q = w1_q.reshape(CMP_PAIRS * HEAD_DIM, KV_COMP * HEAD_DIM).astype(BF16)
    page_spec = lambda k: pl.BlockSpec((1, PAGE_VROWS, HEAD_DIM),
                                       lambda b, g, pt, k=k: (pt[b * n_pages + g * CMP_PAGES + k], 0, 0))
    const = lambda shape: pl.BlockSpec(shape, lambda b, g, pt: (0,) * len(shape))
    grid_spec = pltpu.PrefetchScalarGridSpec(
        num_scalar_prefetch=1,
        grid=(nbs, ngrp),
        in_specs=[page_spec(k) for k in range(CMP_PAGES)]
        + [const((CMP_PAIRS, SUBLANE, HEAD_DIM)), const((CMP_PAIRS * HEAD_DIM, KV_COMP * HEAD_DIM)),
           const((2, 1, HEAD_DIM)), const((2, HEAD_DIM, HEAD_DIM))],
        out_specs=pl.BlockSpec((1, KV_COMP, 2, nh, HEAD_DIM), lambda b, g, pt: (b, 0, 0, g, 0)),
        scratch_shapes=[pltpu.VMEM((CMP_PAGES * (PAGE_SIZE // CMP_BLOCK) * SUBLANE, HEAD_DIM), F32)])
    return pl.pallas_call(
        _compress_sample_body,
        out_shape=jax.ShapeDtypeStruct((nbs, KV_COMP, 2, nblk_half, HEAD_DIM), F32),
        grid_spec=grid_spec,
        compiler_params=_cparams(("arbitrary", "arbitrary")),
        name="nsa_compress_sample",
    )(page_table.reshape(-1), *([cache_v] * CMP_PAGES), pe_q, w1_q, b1.reshape(2, 1, HEAD_DIM), w2)


def _rows_from_lanes(row, ngrp):
    ridx = lax.broadcasted_iota(I32, (SUBLANE, HEAD_DIM), 0)
    out = jnp.zeros((SUBLANE, HEAD_DIM), F32)
    for g in range(ngrp):
        out = jnp.where(ridx == g, jnp.broadcast_to(row[:, g * HEAD_DIM:(g + 1) * HEAD_DIM], (SUBLANE, HEAD_DIM)), out)
    return out


def _slope_rows(slopes_ref, h):
    ridx = lax.broadcasted_iota(I32, (SUBLANE, 1), 0)
    slope = jnp.zeros((SUBLANE, 1), F32)
    for g in range(NSA_GROUP):
        slope = jnp.where(ridx == g, slopes_ref[h * NSA_GROUP + g], slope)
    return slope


def _nsa_sample_select_body(slopes_ref, q_ref, ckv_ref, ocmp_ref, idx_ref, *, past):
    ncb = past // CMP_BLOCK // 2
    nblk = past // SEL_BLOCK
    ridx = lax.broadcasted_iota(I32, (SUBLANE, 1), 0)
    midx = lax.broadcasted_iota(I32, (1, ncb), 1)
    jb = lax.broadcasted_iota(I32, (1, nblk), 1)
    slot = lax.broadcasted_iota(I32, (1, SEL_TOPN), 1)
    for h in range(NSA_KV_HEADS):
        q8 = _rows_from_lanes(q_ref[0:1, h * NSA_GROUP * HEAD_DIM:(h + 1) * NSA_GROUP * HEAD_DIM], NSA_GROUP)
        q8 = q8 * (HEAD_DIM ** -0.5)
        slope = _slope_rows(slopes_ref, h)
        lg = []
        for par in range(2):
            s_c = _mm3(q8, ckv_ref[0, h, par], NT)
            cend = midx * (2 * CMP_BLOCK) + (CMP_BLOCK - 1) + par * CMP_BLOCK
            lg.append(s_c - slope * (past - cend).astype(F32))
        mx = jnp.maximum(jnp.max(lg[0], axis=1, keepdims=True), jnp.max(lg[1], axis=1, keepdims=True))
        e0 = jnp.exp(lg[0] - mx)
        e1 = jnp.exp(lg[1] - mx)
        den = jnp.sum(e0, axis=1, keepdims=True) + jnp.sum(e1, axis=1, keepdims=True)
        p0 = e0 / den
        p1 = e1 / den
        o_cmp = _mm1(p0, ckv_ref[0, 2 + h, 0]) + _mm1(p1, ckv_ref[0, 2 + h, 1])
        ocmp_ref[0, h * NSA_GROUP:(h + 1) * NSA_GROUP, :] = o_cmp[0:NSA_GROUP]
        imp = jnp.sum(jnp.where(ridx < NSA_GROUP, p0 + p1, 0.0), axis=0, keepdims=True)
        forced = (jb == 0) | (jb == nblk - 1)
        score = jnp.where(forced, FORCE_SCORE, imp)
        picks = jnp.full((1, SEL_TOPN), nblk, I32)
        for k in range(SEL_TOPN - 1):
            m = jnp.max(score, axis=1, keepdims=True)
            first = jnp.min(jnp.where(score == m, jb, nblk), axis=1, keepdims=True)
            score = jnp.where(jb == first, NEG_BIG, score)
            picks = jnp.where(slot == k, first, picks)
        idx_ref[0, h:h + 1, :] = picks


def _nsa_sample_select(p2s, ckv_s, slopes, nbs, past):
    nblk_half = past // CMP_BLOCK // 2
    return pl.pallas_call(
        functools.partial(_nsa_sample_select_body, past=past),
        out_shape=(jax.ShapeDtypeStruct((nbs, NSA_HEADS, HEAD_DIM), F32),
                   jax.ShapeDtypeStruct((nbs, NSA_KV_HEADS, SEL_TOPN), I32)),
        grid=(nbs,),
        in_specs=[pl.BlockSpec(memory_space=pltpu.SMEM),
                  pl.BlockSpec((SUBLANE, NSA_WIDTH), lambda b: (b * (SROWS // SUBLANE), 0)),
                  pl.BlockSpec((1, 4, 2, nblk_half, HEAD_DIM), lambda b: (b, 0, 0, 0, 0))],
        out_specs=(pl.BlockSpec((1, NSA_HEADS, HEAD_DIM), lambda b: (b, 0, 0)),
                   pl.BlockSpec((1, NSA_KV_HEADS, SEL_TOPN), lambda b: (b, 0, 0))),
        compiler_params=_cparams(("arbitrary",)),
        name="nsa_sample_select",
    )(slopes, p2s, ckv_s)


NSEL_PAST = SEL_TOPN - 1


def _nsa_sample_attend_body(idx_ref, pt_ref, slopes_ref, q_ref, gt_ref, ocmp_ref, skn_ref, svn_ref, wkn_ref, wvn_ref,
                            *refs, past, n_pages):
    blocks = refs[:NSEL_PAST]
    wc_ref, o_ref = refs[NSEL_PAST:]
    b = pl.program_id(0)
    h = pl.program_id(1)
    q8 = _rows_from_lanes(q_ref[0:1, :], NSA_GROUP) * (HEAD_DIM ** -0.5)
    q8b = q8.astype(BF16)
    slope = _slope_rows(slopes_ref, h)

    def attend_with_new(k_old, v_old, lg_bias, k_new, v_new):
        lg = _dot(q8b, k_old.astype(BF16), NT) + lg_bias
        s_new = jnp.sum(q8 * k_new, axis=1, keepdims=True)
        m = jnp.maximum(jnp.max(lg, axis=1, keepdims=True), s_new)
        p = jnp.exp(lg - m)
        pn = jnp.exp(s_new - m)
        den = jnp.sum(p, axis=1, keepdims=True) + pn
        return (_dot(p.astype(BF16), v_old.astype(BF16)) + pn * v_new) / den

    k_sel = jnp.concatenate([r[0, pl.ds(h, SEL_BLOCK, stride=TOKEN_ROWS), :] for r in blocks], axis=0)
    v_sel = jnp.concatenate([r[0, pl.ds(NSA_KV_HEADS + h, SEL_BLOCK, stride=TOKEN_ROWS), :] for r in blocks], axis=0)
    lane = lax.broadcasted_iota(I32, (1, NSEL_PAST * SEL_BLOCK), 1)
    slot = lane >> (SEL_BLOCK.bit_length() - 1)
    blk = jnp.zeros((1, NSEL_PAST * SEL_BLOCK), I32)
    for j in range(NSEL_PAST):
        blk = jnp.where(slot == j, idx_ref[(b * NSA_KV_HEADS + h) * SEL_TOPN + j], blk)
    dist = (past - (blk * SEL_BLOCK + (lane & (SEL_BLOCK - 1)))).astype(F32)
    o_sel = attend_with_new(k_sel, v_sel, -slope * dist, skn_ref[0:1, :], svn_ref[0:1, :])

    nw = wc_ref.shape[1] // TOKEN_ROWS
    r = lax.broadcasted_iota(I32, (1, nw), 1)
    delta = nw - r
    wmask = (delta < WINDOW) & (past - delta >= 0)
    k_win = wc_ref[0, pl.ds(h, nw, stride=TOKEN_ROWS), :]
    v_win = wc_ref[0, pl.ds(NSA_KV_HEADS + h, nw, stride=TOKEN_ROWS), :]
    o_win = attend_with_new(k_win, v_win, jnp.where(wmask, -slope * delta.astype(F32), NEG_INF),
                            wkn_ref[0:1, :], wvn_ref[0:1, :])

    gs = jax.nn.sigmoid(gt_ref[0:1, :])
    o_ref[...] = jnp.zeros_like(o_ref)
    for g in range(NSA_GROUP):
        def gate(c):
            a = gs[:, g * 3 + c:g * 3 + c + 1]
            bb = gs[:, NSA_GROUP * 3 + g * 3 + c:NSA_GROUP * 3 + g * 3 + c + 1]
            return jnp.where(h == 0, a, bb)

        oc = jnp.where(h == 0, ocmp_ref[0, g:g + 1, :], ocmp_ref[0, NSA_GROUP + g:NSA_GROUP + g + 1, :])
        og = (gate(0) * oc + gate(1) * o_sel[g:g + 1]) + gate(2) * o_win[g:g + 1]
        o_ref[0:1, g * HEAD_DIM:(g + 1) * HEAD_DIM] = og


def _nsa_sample_attend(p2s, ocmp, idx, page_table, cache_sel, win_state, slopes, nbs, past):
    n_pages = page_table.shape[1]
    n_phys = cache_sel.shape[0]
    nw = win_state.shape[1]
    halves = PAGE_SIZE // SEL_BLOCK
    sel_v = cache_sel.reshape(n_phys * halves, SEL_BLOCK * TOKEN_ROWS, HEAD_DIM)
    win_v = win_state.reshape(nbs, nw * TOKEN_ROWS, HEAD_DIM)
    rb = SROWS // SUBLANE
    kvs0 = COL_KVS // LANE
    kvw0 = COL_KVW // LANE
    misc = COL_MISC // LANE

    def sel_spec(j):
        def index(b, h, idx, pt):
            blk = idx[(b * NSA_KV_HEADS + h) * SEL_TOPN + j]
            page = pt[b * n_pages + blk // halves]
            return (page * halves + blk % halves, 0, 0)
        return pl.BlockSpec((1, SEL_BLOCK * TOKEN_ROWS, HEAD_DIM), index)

    new = lambda c0: pl.BlockSpec((SUBLANE, LANE), lambda b, h, idx, pt, c0=c0: (b * rb, c0 + h))
    grid_spec = pltpu.PrefetchScalarGridSpec(
        num_scalar_prefetch=2,
        grid=(nbs, NSA_KV_HEADS),
        in_specs=[pl.BlockSpec(memory_space=pltpu.SMEM),
                  pl.BlockSpec((SUBLANE, NSA_GROUP * HEAD_DIM), lambda b, h, idx, pt: (b * rb, h)),
                  pl.BlockSpec((SUBLANE, LANE), lambda b, h, idx, pt: (b * rb, misc)),
                  pl.BlockSpec((1, NSA_HEADS, HEAD_DIM), lambda b, h, idx, pt: (b, 0, 0)),
                  new(kvs0), new(kvs0 + 2), new(kvw0), new(kvw0 + 2)]
        + [sel_spec(j) for j in range(NSEL_PAST)]
        + [pl.BlockSpec((1, nw * TOKEN_ROWS, HEAD_DIM), lambda b, h, idx, pt: (b, 0, 0))],
        out_specs=pl.BlockSpec((SROWS, NSA_GROUP * HEAD_DIM), lambda b, h, idx, pt: (b, h)))
    return pl.pallas_call(
        functools.partial(_nsa_sample_attend_body, past=past, n_pages=n_pages),
        out_shape=jax.ShapeDtypeStruct((nbs * SROWS, NSA_WIDTH), F32),
        grid_spec=grid_spec,
        compiler_params=_cparams(("arbitrary", "arbitrary")),
        name="nsa_sample_attend",
    )(idx.reshape(-1), page_table.reshape(-1), slopes, p2s, p2s, ocmp, p2s, p2s, p2s, p2s,
      *([sel_v] * NSEL_PAST), win_v)


def _perm_w_in(w_in):
    d = w_in.shape[0]
    g0 = NSA_WIDTH + 3 * 2 * KV_WIDTH
    q0 = g0 + 3 * NSA_HEADS
    b0 = q0 + 3 * GDN_WIDTH + GDN_WIDTH
    pad = jnp.zeros((d, N_PROJ - (b0 + 2 * GDN_HEADS)), w_in.dtype)
    w2 = jnp.concatenate([w_in[:, :g0], w_in[:, q0:b0], w_in[:, g0:q0], w_in[:, b0:], pad], axis=1)
    return w2.astype(BF16)


def _split_bf16(w):
    hi = w.astype(BF16)
    lo = (w - hi.astype(F32)).astype(BF16)
    return hi, lo


def _mixer_tail(x, o_nsa, o_gdn, w_out_bf, g1, norm_ffn, sc2, sh2, g2, wq_hi, wq_lo, keys, u_bf, vt_bf,
                final_norm, seq_len, tm, tn_sel, tn_mm, tm_final):
    x1, h2 = _out_proj(o_nsa, o_gdn, x, w_out_bf, g1, norm_ffn, sc2, sh2, seq_len, tm)
    cnt, al, code, be = _peer_select(h2, wq_hi, wq_lo, keys, tn_sel, split_query=h2.shape[0] <= tn_sel)
    peer_t = _peer_mm(h2.astype(BF16), u_bf, vt_bf, cnt, al, code, be, tn_mm)
    return _final(x1, peer_t, g2, final_norm, seq_len, tm_final)


def kernel(x_prompt, x_sample, cache_cmp_kv, cache_sel_kv, state_win_kv, state_conv, state_gdn, page_table,
           c_prompt, c_sample, w_ada, b_ada, norm_mix, norm_ffn, w_in, cmp_pe, cmp_w1, cmp_b1, cmp_w2, conv_w,
           gdn_a_log, gdn_dt_bias, gdn_norm, w_out, peer_wq, peer_keys, peer_u, peer_v, final_norm):
    nb, seq, d = x_prompt.shape
    nbs, dec_seq, _ = x_sample.shape
    assert w_in.shape[0] == 1 and dec_seq == 1, "single layer, single decode token"
    past = page_table.shape[1] * PAGE_SIZE
    slopes = 2.0 ** (-8.0 * jnp.arange(1, NSA_HEADS + 1, dtype=F32) / NSA_HEADS)
    rows_c = 16
    c_all = jnp.concatenate([c_prompt, c_sample, jnp.zeros((rows_c - nb - nbs, d), F32)], axis=0)
    mod = _adaln(c_all, w_ada[0], b_ada[0]).reshape(rows_c, 6, d)
    pm = lambda k: mod[0:nb, k][:, None, :]
    sm = lambda k: mod[nb:nb + nbs, k][:, None, :]
    w2 = _perm_w_in(w_in[0])
    w_out_bf = w_out[0].astype(BF16)
    wq_hi, wq_lo = _split_bf16(peer_wq[0].T)
    u_bf = peer_u[0].astype(BF16)
    vt_bf = peer_v[0].T.astype(BF16)
    cw = (cmp_pe[0], cmp_w1[0], cmp_b1[0], cmp_w2[0])
    kvw = 2 * KV_WIDTH

    xp = x_prompt.reshape(nb * seq, d)
    p2 = _proj_in(xp, norm_mix[0], pm(1), pm(0), w2, seq, 1024)
    ckv = _compress_prompt(p2, *cw, nb, seq)
    o_nsa = _nsa_prompt(p2, ckv, slopes, nb, seq)
    conv0 = jnp.zeros((nb, SUBLANE, 3 * GDN_WIDTH), F32)
    s0 = jnp.zeros((nb, GDN_HEADS, HEAD_DIM, HEAD_DIM), F32)
    o_gdn, gdn_p = _gdn(p2, conv_w[0], conv0, s0, gdn_a_log[0], gdn_dt_bias[0], gdn_norm[0], nb, seq, seq)
    y_prompt = _mixer_tail(xp, o_nsa, o_gdn, w_out_bf, pm(2), norm_ffn[0], pm(4), pm(3), pm(5), wq_hi, wq_lo,
                           peer_keys[0], u_bf, vt_bf, final_norm, seq, 512, 256, 512, 512).reshape(nb, seq, d)
    p3 = p2.reshape(nb, seq, N_PROJ)
    keep = min(WINDOW, seq)
    kv5 = lambda a: a.reshape(a.shape[0], a.shape[1], 2, NSA_KV_HEADS, HEAD_DIM)
    cmp_p = kv5(p3[:, :, COL_KVC:COL_KVC + kvw])
    sel_p = kv5(p3[:, :, COL_KVS:COL_KVS + kvw])
    win_p = kv5(p3[:, seq - keep:, COL_KVW:COL_KVW + kvw])
    conv_p = p3[:, seq - (CONV_WIDTH - 1):, COL_QKV:COL_QKV + 3 * GDN_WIDTH]

    xs = jnp.pad(x_sample, ((0, 0), (0, SROWS - dec_seq), (0, 0))).reshape(nbs * SROWS, d)
    p2s = _proj_in(xs, norm_mix[0], sm(1), sm(0), w2, SROWS, nbs * SROWS)
    ckv_s = _compress_sample(cache_cmp_kv[0], page_table, *cw)
    ocmp, idx = _nsa_sample_select(p2s, ckv_s, slopes, nbs, past)
    o_nsa_s = _nsa_sample_attend(p2s, ocmp, idx, page_table, cache_sel_kv[0], state_win_kv[0], slopes, nbs, past)
    conv_buf = jnp.pad(state_conv[0], ((0, 0), (SUBLANE - (CONV_WIDTH - 1), 0), (0, 0)))
    o_gdn_s, gdn_s = _gdn(p2s, conv_w[0], conv_buf, state_gdn[0], gdn_a_log[0], gdn_dt_bias[0], gdn_norm[0],
                          nbs, SROWS, dec_seq)
    head_rows = lambda a: a.reshape(nbs, SROWS, a.shape[-1])[:, :SROWS_TAIL].reshape(nbs * SROWS_TAIL, a.shape[-1])
    nt = nbs * SROWS_TAIL
    ys = _mixer_tail(head_rows(xs), head_rows(o_nsa_s), head_rows(o_gdn_s), w_out_bf, sm(2), norm_ffn[0], sm(4),
                     sm(3), sm(5), wq_hi, wq_lo, peer_keys[0], u_bf, vt_bf, final_norm, SROWS_TAIL, nt, nt, nt, nt)
    y_sample = ys.reshape(nbs, SROWS_TAIL, d)[:, 0:dec_seq]
    p3s = p2s.reshape(nbs, SROWS, N_PROJ)[:, 0:dec_seq]
    cmp_s = kv5(p3s[:, :, COL_KVC:COL_KVC + kvw])
    sel_s = kv5(p3s[:, :, COL_KVS:COL_KVS + kvw])
    win_new = kv5(p3s[:, :, COL_KVW:COL_KVW + kvw])
    win_s = jnp.concatenate([state_win_kv[0], win_new], axis=1)[:, dec_seq:]
    conv_s = jnp.concatenate([state_conv[0], p3s[:, :, COL_QKV:COL_QKV + 3 * GDN_WIDTH]], axis=1)[:, dec_seq:]

    return (y_prompt, y_sample, cmp_p[None], cmp_s[None], sel_p[None], sel_s[None], win_p[None], win_s[None],
            conv_p[None], conv_s[None], gdn_p[None], gdn_s[None])
```

```python
import functools
import math

import jax
import jax.numpy as jnp
from jax import lax
from jax.experimental import pallas as pl
from jax.experimental.pallas import tpu as pltpu

F32 = jnp.float32
BF16 = jnp.bfloat16
I32 = jnp.int32

D_MODEL = 2048
HEAD_DIM = 128
NSA_WIDTH = 1024
GDN_WIDTH = 1024
NSA_HEADS = 8
NSA_KV_HEADS = 2
NSA_GROUP = 4
KV_WIDTH = 256
CMP_BLOCK = 32
SEL_BLOCK = 64
SEL_TOPN = 16
WINDOW = 512
GDN_HEADS = 8
CONV_WIDTH = 4
GDN_CHUNK = 64
PEER_HEADS = 8
PEER_NKEYS = 128
PEER_QDIM = 256
PEER_TOPK = 16
PAGE_SIZE = 128
NORM_EPS = 1e-6
NEG_INF = -1e30
FORCE_SCORE = 1e4
NEG_BIG = -3.0e38

LANE = 128
SUBLANE = 8
VMEM_LIMIT = 56 * 1024 * 1024

COL_Q = 0
COL_KVC = 1024
COL_KVS = 1536
COL_KVW = 2048
COL_QKV = 2560
COL_Z = 5632
COL_MISC = 6656
N_PROJ = 6912
MISC_BL = 24
MISC_AL = 32

NN = (((1,), (0,)), ((), ()))
NT = (((1,), (1,)), ((), ()))
BNN = (((2,), (1,)), ((0,), (0,)))
BNT = (((2,), (2,)), ((0,), (0,)))


def _dot(a, b, dims=NN):
    return lax.dot_general(a, b, dims, preferred_element_type=F32)


def _split(x):
    hi = x.astype(BF16)
    lo = (x - hi.astype(F32)).astype(BF16)
    return hi, lo


def _mm1(a, b, dims=NN):
    return _dot(a.astype(BF16), b.astype(BF16), dims)


def _mm3(a, b, dims=NN):
    ah, al = _split(a)
    bh, bl = _split(b)
    return _dot(ah, bh, dims) + (_dot(al, bh, dims) + _dot(ah, bl, dims))


def _cparams(sem, vmem=VMEM_LIMIT):
    return pltpu.CompilerParams(dimension_semantics=sem, vmem_limit_bytes=vmem)


def _adaln_body(c_ref, w_ref, b_ref, o_ref):
    a = jax.nn.silu(c_ref[...])
    o_ref[...] = _mm3(a, w_ref[...]) + b_ref[...]


def _adaln(c_all, w, b):
    rows, d = c_all.shape
    n = w.shape[1]
    tn = 1024
    return pl.pallas_call(
        _adaln_body,
        out_shape=jax.ShapeDtypeStruct((rows, n), F32),
        grid=(n // tn,),
        in_specs=[pl.BlockSpec((rows, d), lambda j: (0, 0)),
                  pl.BlockSpec((d, tn), lambda j: (0, j)),
                  pl.BlockSpec((1, tn), lambda j: (0, j))],
        out_specs=pl.BlockSpec((rows, tn), lambda j: (0, j)),
        compiler_params=_cparams(("arbitrary",)),
        name="adaln",
    )(c_all, w, b.reshape(1, n))


def _seq_rows(m_ref, seq_len):
    nseq = m_ref.shape[0]
    if nseq == 1:
        return m_ref[0]
    return jnp.concatenate([jnp.broadcast_to(m_ref[r], (seq_len, m_ref.shape[2])) for r in range(nseq)], axis=0)


def _seq_spec(seq_len, tm, d):
    if tm <= seq_len:
        tpb = seq_len // tm
        return pl.BlockSpec((1, 1, d), lambda i, *_: (i // tpb, 0, 0))
    return pl.BlockSpec((tm // seq_len, 1, d), lambda i, *_: (i, 0, 0))


PROJ_TN = 768


def _proj_in_body(x_ref, nw_ref, sc_ref, sh_ref, w_ref, o_ref, cmp_ref, sel_ref, h_ref, *, seq_len):
    j = pl.program_id(1)
    tm = x_ref.shape[0]

    @pl.when(j == 0)
    def _():
        x = x_ref[...]
        y = x * lax.rsqrt(jnp.mean(x * x, axis=-1, keepdims=True) + NORM_EPS)
        h = (y * nw_ref[...]) * (1.0 + _seq_rows(sc_ref, seq_len)) + _seq_rows(sh_ref, seq_len)
        h_ref[...] = h.astype(BF16)

    o_ref[...] = _dot(h_ref[...], w_ref[...])

    for dst, col in ((cmp_ref, COL_KVC), (sel_ref, COL_KVS)):
        @pl.when(j == col // PROJ_TN)
        def _(dst=dst, col=col):
            for c in range(TOKEN_ROWS):
                lo = col % PROJ_TN + c * HEAD_DIM
                dst[pl.ds(c, tm, stride=TOKEN_ROWS), :] = o_ref[:, lo:lo + HEAD_DIM]


def _proj_in(x, norm_w, sc, sh, w2, seq_len, tm):
    n, d = x.shape
    ncol = w2.shape[1]
    tn = PROJ_TN
    for col in (COL_KVC, COL_KVS):
        assert col % tn + 2 * KV_WIDTH <= tn, "a cached column group must not straddle column tiles"
    mod = _seq_spec(seq_len, tm, d)
    rows = jax.ShapeDtypeStruct((n * TOKEN_ROWS, HEAD_DIM), F32)
    rows_spec = pl.BlockSpec((tm * TOKEN_ROWS, HEAD_DIM), lambda i, j: (i, 0))
    return pl.pallas_call(
        functools.partial(_proj_in_body, seq_len=seq_len),
        out_shape=(jax.ShapeDtypeStruct((n, ncol), F32), rows, rows),
        grid=(n // tm, ncol // tn),
        in_specs=[pl.BlockSpec((tm, d), lambda i, j: (i, 0)),
                  pl.BlockSpec((1, d), lambda i, j: (0, 0)),
                  mod, mod,
                  pl.BlockSpec((d, tn), lambda i, j: (0, j))],
        out_specs=(pl.BlockSpec((tm, tn), lambda i, j: (i, j)), rows_spec, rows_spec),
        scratch_shapes=[pltpu.VMEM((tm, d), BF16)],
        compiler_params=_cparams(("arbitrary", "arbitrary")),
        name="proj_in",
    )(x, norm_w.reshape(1, d), sc, sh, w2)


def _compress_prompt_body(x_ref, pe_ref, w1_ref, b1_ref, w2_ref, o_ref, *, nblk):
    half = nblk // 2
    xs = []
    for l in range(CMP_BLOCK):
        xe = x_ref[pl.ds(l, half, stride=2 * CMP_BLOCK), :]
        xo = x_ref[pl.ds(CMP_BLOCK + l, half, stride=2 * CMP_BLOCK), :]
        xs.append((jnp.concatenate([xe, xo], axis=0) + pe_ref[0, l:l + 1, :]).astype(BF16))
    acc = _dot(jnp.concatenate(xs, axis=1), w1_ref[0].astype(BF16))
    hid = jax.nn.gelu(acc + b1_ref[0])
    o_ref[0, 0] = _mm1(hid, w2_ref[0])


def _compress_prompt(p2, pe, w1, b1, w2, nbatch, seq_len):
    nblk = seq_len // CMP_BLOCK
    kv0 = COL_KVC // LANE
    return pl.pallas_call(
        functools.partial(_compress_prompt_body, nblk=nblk),
        out_shape=jax.ShapeDtypeStruct((nbatch, 4, nblk, HEAD_DIM), F32),
        grid=(nbatch, 4),
        in_specs=[pl.BlockSpec((seq_len, LANE), lambda b, sh: (b, kv0 + sh)),
                  pl.BlockSpec((1, CMP_BLOCK, HEAD_DIM), lambda b, sh: (sh // 2, 0, 0)),
                  pl.BlockSpec((1, CMP_BLOCK * HEAD_DIM, HEAD_DIM), lambda b, sh: (sh // 2, 0, 0)),
                  pl.BlockSpec((1, 1, HEAD_DIM), lambda b, sh: (sh // 2, 0, 0)),
                  pl.BlockSpec((1, HEAD_DIM, HEAD_DIM), lambda b, sh: (sh // 2, 0, 0))],
        out_specs=pl.BlockSpec((1, 1, nblk, HEAD_DIM), lambda b, sh: (b, sh, 0, 0)),
        compiler_params=_cparams(("arbitrary", "arbitrary")),
        name="nsa_compress_prompt",
    )(p2, pe, w1.reshape(2, CMP_BLOCK * HEAD_DIM, HEAD_DIM), b1.reshape(2, 1, HEAD_DIM), w2)


TQ = 128
NSA_KSPAN = 512


def _nsa_prompt_body(slopes_ref, q_ref, gt_ref, ck_ref, cv_ref, sk_ref, sv_ref, wk_ref, wv_ref,
                     o_ref, selm_ref, *, seq_len):
    h = pl.program_id(1)
    qi = pl.program_id(2)
    t0 = qi * TQ
    nsb = seq_len // SEL_BLOCK
    ncb = seq_len // CMP_BLOCK // 2
    rows = NSA_GROUP * TQ

    qb = q_ref[...] * (HEAD_DIM ** -0.5)
    q4 = jnp.concatenate([qb[:, g * HEAD_DIM:(g + 1) * HEAD_DIM] for g in range(NSA_GROUP)], axis=0)
    q4b = q4.astype(BF16)
    row = lax.broadcasted_iota(I32, (rows, 1), 0)
    tq = t0 + (row & (TQ - 1))
    grow = row >> 7
    slope = jnp.zeros((rows, 1), F32)
    for g in range(NSA_GROUP):
        slope = jnp.where(grow == g, slopes_ref[h * NSA_GROUP + g], slope)

    ck = ck_ref[0, 0]
    cv = cv_ref[0, 0]
    lane_r = lax.broadcasted_iota(I32, (1, rows), 1)
    tq_l = t0 + (lane_r & (TQ - 1))
    slope_l = jnp.zeros((1, rows), F32)
    for g in range(NSA_GROUP):
        slope_l = jnp.where((lane_r >> 7) == g, slopes_ref[h * NSA_GROUP + g], slope_l)
    midx = lax.broadcasted_iota(I32, (ncb, 1), 0)
    logits = []
    valids = []
    for par in range(2):
        s_t = _mm3(ck[par * ncb:(par + 1) * ncb], q4, NT)
        cend = midx * (2 * CMP_BLOCK) + (CMP_BLOCK - 1) + par * CMP_BLOCK
        valid = cend <= tq_l
        dist = (tq_l - cend).astype(F32)
        logits.append(jnp.where(valid, s_t - slope_l * dist, NEG_INF))
        valids.append(valid)
    mx = jnp.maximum(jnp.max(logits[0], axis=0, keepdims=True), jnp.max(logits[1], axis=0, keepdims=True))
    e0 = jnp.exp(logits[0] - mx)
    e1 = jnp.exp(logits[1] - mx)
    den = jnp.sum(e0, axis=0, keepdims=True) + jnp.sum(e1, axis=0, keepdims=True)
    p0 = jnp.where(valids[0], e0 / den, 0.0)
    p1 = jnp.where(valids[1], e1 / den, 0.0)
    ident = (lax.broadcasted_iota(I32, (TQ, TQ), 0) == lax.broadcasted_iota(I32, (TQ, TQ), 1)).astype(BF16)
    p0b = p0.astype(BF16)
    p1b = p1.astype(BF16)
    o_cmp = jnp.concatenate(
        [_dot(_dot(ident, p0b[:, g * TQ:(g + 1) * TQ], NT).astype(BF16), cv[0:ncb].astype(BF16))
         + _dot(_dot(ident, p1b[:, g * TQ:(g + 1) * TQ], NT).astype(BF16), cv[ncb:2 * ncb].astype(BF16))
         for g in range(NSA_GROUP)], axis=0)
    pb = p0 + p1
    imp = pb[:, 0:TQ]
    for g in range(1, NSA_GROUP):
        imp = imp + pb[:, g * TQ:(g + 1) * TQ]

    cur = (t0 + lax.broadcasted_iota(I32, (1, TQ), 1)) >> 6
    jb = lax.broadcasted_iota(I32, (nsb, 1), 0)
    forced = (jb == 0) | (jb == cur) | (jb == cur - 1)
    score = jnp.where(jb <= cur, jnp.where(forced, FORCE_SCORE, imp), NEG_INF)
    cnt = jnp.zeros((nsb, TQ), I32)
    for i in range(nsb):
        si = score[i:i + 1, :]
        beats = (si > score) | ((si == score) & (jb > i))
        cnt = cnt + beats.astype(I32)
    sel_t = ((cnt < min(SEL_TOPN, nsb)) & (score > 0.5 * NEG_INF)).astype(BF16)
    sel = _dot(ident, sel_t, NT).astype(BF16)
    kpos = lax.broadcasted_iota(I32, (nsb, seq_len), 1)
    kblk = lax.broadcasted_iota(I32, (nsb, seq_len), 0)
    expand = ((kpos >> 6) == kblk).astype(BF16)
    selk = _dot(sel, expand)

    kw = NSA_KSPAN
    t_row = t0 + lax.broadcasted_iota(I32, (TQ, kw), 0)
    key = lax.broadcasted_iota(I32, (TQ, kw), 1)
    for st in range(seq_len // kw):
        blk = (selk[:, st * kw:(st + 1) * kw] - 1.0) * (-NEG_INF)
        selm_ref[st] = blk + jnp.where(st * kw + key <= t_row, 0.0, NEG_INF)
    m_floor = 0.1 * NEG_INF

    def span_update(carry, k, v, bias, key0):
        m, l, acc = carry
        width = k.shape[0]
        alibi = slope * lax.broadcasted_iota(I32, (1, width), 1).astype(F32)
        x = _dot(q4b, k, NT) + alibi
        x = (x.reshape(NSA_GROUP, TQ, width) + bias[None]).reshape(rows, width)
        shift = slope * (key0 - tq).astype(F32)
        m_new = jnp.maximum(m, jnp.max(x, axis=1, keepdims=True) + shift)
        alpha = jnp.exp(m - m_new)
        p = jnp.exp(x - (m_new - shift))
        l = alpha * l + jnp.sum(p, axis=1, keepdims=True)
        acc = alpha * acc + _dot(p.astype(BF16), v)
        return m_new, l, acc

    def finish(carry):
        _, l, acc = carry
        return jnp.where(l > 0.0, acc / jnp.where(l > 0.0, l, 1.0), 0.0)

    init = (jnp.full((rows, 1), m_floor, F32), jnp.zeros((rows, 1), F32), jnp.zeros((rows, HEAD_DIM), F32))

    def sel_step(st, carry):
        start = pl.multiple_of(st * kw, kw)
        return span_update(carry, sk_ref[pl.ds(start, kw), :].astype(BF16), sv_ref[pl.ds(start, kw), :].astype(BF16),
                           selm_ref[st], start)

    o_sel = finish(lax.fori_loop(0, (t0 + TQ - 1) // kw + 1, sel_step, init))

    ww = min(WINDOW + TQ, seq_len)
    wstart = pl.multiple_of(jnp.maximum(t0 + TQ - ww, 0), TQ)
    wdelta = (t0 + lax.broadcasted_iota(I32, (TQ, ww), 0)) - (wstart + lax.broadcasted_iota(I32, (TQ, ww), 1))
    wbias = jnp.where((wdelta >= 0) & (wdelta < WINDOW), 0.0, NEG_INF)
    o_win = finish(span_update(init, wk_ref[pl.ds(wstart, ww), :].astype(BF16), wv_ref[pl.ds(wstart, ww), :].astype(BF16),
                               wbias, wstart))

    gs = jax.nn.sigmoid(gt_ref[...])
    for g in range(NSA_GROUP):
        def gate(c):
            a = gs[:, g * 3 + c:g * 3 + c + 1]
            b = gs[:, NSA_GROUP * 3 + g * 3 + c:NSA_GROUP * 3 + g * 3 + c + 1]
            return jnp.where(h == 0, a, b)

        sl = slice(g * TQ, (g + 1) * TQ)
        o_ref[:, g * HEAD_DIM:(g + 1) * HEAD_DIM] = (gate(0) * o_cmp[sl] + gate(1) * o_sel[sl]) + gate(2) * o_win[sl]


def _nsa_prompt(p2, ckv, slopes, nbatch, seq_len):
    nq = seq_len // TQ
    kvs0 = COL_KVS // LANE
    kvw0 = COL_KVW // LANE
    misc = COL_MISC // LANE
    nblk = seq_len // CMP_BLOCK
    full = lambda c0: pl.BlockSpec((seq_len, LANE), lambda b, h, q, c0=c0: (b, c0 + h))
    return pl.pallas_call(
        functools.partial(_nsa_prompt_body, seq_len=seq_len),
        out_shape=jax.ShapeDtypeStruct((nbatch * seq_len, NSA_WIDTH), F32),
        grid=(nbatch, NSA_KV_HEADS, nq),
        in_specs=[pl.BlockSpec(memory_space=pltpu.SMEM),
                  pl.BlockSpec((TQ, NSA_GROUP * HEAD_DIM), lambda b, h, q: (b * nq + q, h)),
                  pl.BlockSpec((TQ, LANE), lambda b, h, q: (b * nq + q, misc)),
                  pl.BlockSpec((1, 1, nblk, HEAD_DIM), lambda b, h, q: (b, h, 0, 0)),
                  pl.BlockSpec((1, 1, nblk, HEAD_DIM), lambda b, h, q: (b, 2 + h, 0, 0)),
                  full(kvs0), full(kvs0 + 2), full(kvw0), full(kvw0 + 2)],
        out_specs=pl.BlockSpec((TQ, NSA_GROUP * HEAD_DIM), lambda b, h, q: (b * nq + q, h)),
        scratch_shapes=[pltpu.VMEM((seq_len // NSA_KSPAN, TQ, NSA_KSPAN), F32)],
        compiler_params=_cparams(("arbitrary", "arbitrary", "arbitrary")),
        name="nsa_prompt",
    )(slopes, p2, p2, ckv, ckv, p2, p2, p2, p2)


GDN_CB = 8


def _gdn_chunk(seq_len):
    return min(GDN_CHUNK, seq_len)


def _bmm3(a, b):
    ah, al = _split(a)
    bh, bl = _split(b)
    return _dot(jnp.concatenate([ah, al, ah], axis=2), jnp.concatenate([bh, bh, bl], axis=1), BNN)


GDN_HG = 2


def _gdn_body(alog_ref, dtb_ref, xq_ref, xk_ref, xv_ref, z_ref, gt_ref, cwq_ref, cwk_ref, cwv_ref,
              cbq_ref, cbk_ref, cbv_ref, s0_ref, gn_ref, o_ref, sout_ref,
              cat_s, q_s, k_s, v_s, b_s, g_s, u_s, w_s, qg_s, kdt_s, qk_s, gl_s, oc_s, *, seq_len, valid):
    hg = pl.program_id(1)
    C = _gdn_chunk(seq_len)
    nc = seq_len // C
    cb = min(GDN_CB, nc)
    row = lax.broadcasted_iota(I32, (seq_len, 1), 0)
    lane = lax.broadcasted_iota(I32, (1, LANE), 1)
    pos = row & (C - 1)
    gt = gt_ref[...]

    def conv(x_ref, cw_ref, cb_ref, ls):
        x = x_ref[:, ls]
        cat_s[0:SUBLANE, :] = cb_ref[0, :, ls]
        cat_s[SUBLANE:, :] = x
        w = cw_ref[:, ls]
        acc = cat_s[pl.ds(SUBLANE - 3, seq_len), :] * w[0:1]
        acc = acc + cat_s[pl.ds(SUBLANE - 2, seq_len), :] * w[1:2]
        acc = acc + cat_s[pl.ds(SUBLANE - 1, seq_len), :] * w[2:3]
        acc = acc + x * w[3:4]
        return jax.nn.silu(acc)

    def l2n(x):
        return x * lax.rsqrt(jnp.sum(x * x, axis=-1, keepdims=True) + NORM_EPS)

    ii = lax.broadcasted_iota(I32, (C, C), 0)
    jj = lax.broadcasted_iota(I32, (C, C), 1)
    tri = (ii >= jj)[None]
    strict = (ii > jj)[None]
    eye = (ii == jj).astype(F32)[None]
    ones_b = jnp.ones((cb, C, 3 * C), BF16)
    ident_b = jnp.broadcast_to((lax.broadcasted_iota(I32, (HEAD_DIM, HEAD_DIM), 0)
                                == lax.broadcasted_iota(I32, (HEAD_DIM, HEAD_DIM), 1)).astype(BF16)[None],
                               (cb, HEAD_DIM, HEAD_DIM))

    for hl in range(GDN_HG):
        hd = hg * GDN_HG + hl
        ls = slice(hl * HEAD_DIM, (hl + 1) * HEAD_DIM)
        q = l2n(conv(xq_ref, cwq_ref, cbq_ref, ls)) * (HEAD_DIM ** -0.5)
        k = l2n(conv(xk_ref, cwk_ref, cbk_ref, ls))
        v = conv(xv_ref, cwv_ref, cbv_ref, ls)
        bl = jnp.sum(jnp.where(lane == MISC_BL + hd, gt, 0.0), axis=1, keepdims=True)
        al = jnp.sum(jnp.where(lane == MISC_AL + hd, gt, 0.0), axis=1, keepdims=True)
        beta = jax.nn.sigmoid(bl)
        a_pos = jnp.exp(jnp.full((1, 1), alog_ref[hd], F32))
        g = -a_pos * jax.nn.softplus(al + dtb_ref[hd])
        if valid < seq_len:
            vm = row < valid
            q = jnp.where(vm, q, 0.0)
            k = jnp.where(vm, k, 0.0)
            v = jnp.where(vm, v, 0.0)
            beta = jnp.where(vm, beta, 0.0)
            g = jnp.where(vm, g, 0.0)
        gc = g
        sft = 1
        while sft < C:
            gc = gc + jnp.where(pos >= sft, pltpu.roll(gc, sft, 0), 0.0)
            sft *= 2
        q_s[...] = q
        k_s[...] = k
        v_s[...] = v
        b_s[...] = jnp.broadcast_to(beta, (seq_len, LANE))
        g_s[...] = jnp.broadcast_to(gc, (seq_len, LANE))

        def local(gi, carry, hl=hl):
            r0 = pl.multiple_of(gi * (cb * C), cb * C)
            sl = pl.ds(r0, cb * C)
            qc = q_s[sl, :].reshape(cb, C, HEAD_DIM)
            kc = k_s[sl, :].reshape(cb, C, HEAD_DIM)
            vc = v_s[sl, :].reshape(cb, C, HEAD_DIM)
            bc = b_s[sl, :].reshape(cb, C, LANE)
            gcc = g_s[sl, :].reshape(cb, C, LANE)
            gcol = gcc[:, :, 0:C]
            dg = gcol * eye
            d1 = dg.astype(BF16)
            r1 = dg - d1.astype(F32)
            d2 = r1.astype(BF16)
            d3 = (r1 - d2.astype(F32)).astype(BF16)
            grow = _dot(ones_b, jnp.concatenate([d1, d2, d3], axis=1), BNN)
            diff = gcol - grow
            decay = jnp.where(tri, jnp.exp(jnp.where(tri, diff, 0.0)), 0.0)
            kb = kc * bc
            a = _dot(kb.astype(BF16), kc.astype(BF16), BNT) * jnp.where(strict, decay, 0.0)
            tinv = eye - a
            pw = a
            n = 2
            while n < C:
                pw = _bmm3(pw, pw)
                tinv = tinv + _bmm3(tinv, pw)
                n *= 2
            eg = jnp.exp(gcc)
            rhs = jnp.concatenate([vc * bc, kb * eg], axis=2)
            sol = _bmm3(tinv, rhs)
            u_s[hl, sl, :] = sol[:, :, 0:HEAD_DIM].reshape(cb * C, HEAD_DIM)
            w_s[hl, sl, :] = sol[:, :, HEAD_DIM:2 * HEAD_DIM].reshape(cb * C, HEAD_DIM)
            qk = _dot(qc.astype(BF16), kc.astype(BF16), BNT) * decay
            qk_s[hl, sl, :] = qk.reshape(cb * C, C)
            qg_s[hl, sl, :] = (qc * eg).reshape(cb * C, HEAD_DIM)
            glast = gcc[:, C - 1:C, :]
            kd = kc * jnp.exp(glast - gcc)
            kdt = _dot(ident_b, kd.astype(BF16), BNT)
            kdt_s[hl, pl.ds(pl.multiple_of(gi * (cb * HEAD_DIM), cb * HEAD_DIM), cb * HEAD_DIM), :] = (
                kdt.reshape(cb * HEAD_DIM, C))
            gl_s[hl, pl.ds(pl.multiple_of(gi * (cb * SUBLANE), cb * SUBLANE), cb * SUBLANE), :] = (
                jnp.broadcast_to(jnp.exp(glast), (cb, SUBLANE, LANE)).reshape(cb * SUBLANE, LANE))
            return carry

        lax.fori_loop(0, nc // cb, local, 0)

    def scan(c, states):
        sl = pl.ds(pl.multiple_of(c * C, C), C)
        new = []
        for hl in range(GDN_HG):
            s = states[hl]
            sb = s.astype(BF16)
            v_new = u_s[hl, sl, :] - _dot(w_s[hl, sl, :].astype(BF16), sb)
            o_c = (_dot(qg_s[hl, sl, :].astype(BF16), sb)
                   + _dot(qk_s[hl, sl, :].astype(BF16), v_new.astype(BF16)))
            oc_s[hl, sl, :] = o_c
            kdt = kdt_s[hl, pl.ds(pl.multiple_of(c * HEAD_DIM, HEAD_DIM), HEAD_DIM), :]
            gl = gl_s[hl, pl.ds(pl.multiple_of(c * SUBLANE, SUBLANE), 1), :]
            new.append(s * gl + _dot(kdt.astype(BF16), v_new.astype(BF16)))
        return tuple(new)

    s_fin = lax.fori_loop(0, nc, scan, tuple(s0_ref[0, hl] for hl in range(GDN_HG)))
    for hl in range(GDN_HG):
        ls = slice(hl * HEAD_DIM, (hl + 1) * HEAD_DIM)
        sout_ref[0, hl] = s_fin[hl]
        o = oc_s[hl]
        y = o * lax.rsqrt(jnp.mean(o * o, axis=-1, keepdims=True) + NORM_EPS) * gn_ref[...]
        o_ref[:, ls] = y * jax.nn.silu(z_ref[:, ls])


def _gdn(p2, conv_w, conv_buf8, s0, a_log, dt_bias, gn_w, nbatch, seq_len, valid):
    gw = GDN_HG * LANE
    q0 = COL_QKV // gw
    z0 = COL_Z // gw
    hgs = GDN_HEADS // GDN_HG
    misc = COL_MISC // LANE
    C = _gdn_chunk(seq_len)
    col = lambda c0: pl.BlockSpec((seq_len, gw), lambda b, h, c0=c0: (b, c0 + h))
    cw = lambda c0: pl.BlockSpec((CONV_WIDTH, gw), lambda b, h, c0=c0: (0, c0 + h))
    cbs = lambda c0: pl.BlockSpec((1, SUBLANE, gw), lambda b, h, c0=c0: (b, 0, c0 + h))
    smem = pl.BlockSpec(memory_space=pltpu.SMEM)
    nc = seq_len // C
    per_head = lambda rows, cols: pltpu.VMEM((GDN_HG, rows, cols), F32)
    return pl.pallas_call(
        functools.partial(_gdn_body, seq_len=seq_len, valid=valid),
        out_shape=(jax.ShapeDtypeStruct((nbatch * seq_len, GDN_WIDTH), F32),
                   jax.ShapeDtypeStruct((nbatch, GDN_HEADS, HEAD_DIM, HEAD_DIM), F32)),
        grid=(nbatch, hgs),
        in_specs=[smem, smem, col(q0), col(q0 + hgs), col(q0 + 2 * hgs), col(z0),
                  pl.BlockSpec((seq_len, LANE), lambda b, h: (b, misc)),
                  cw(0), cw(hgs), cw(2 * hgs), cbs(0), cbs(hgs), cbs(2 * hgs),
                  pl.BlockSpec((1, GDN_HG, HEAD_DIM, HEAD_DIM), lambda b, h: (b, h, 0, 0)),
                  pl.BlockSpec((1, HEAD_DIM), lambda b, h: (0, 0))],
        out_specs=(pl.BlockSpec((seq_len, gw), lambda b, h: (b, h)),
                   pl.BlockSpec((1, GDN_HG, HEAD_DIM, HEAD_DIM), lambda b, h: (b, h, 0, 0))),
        scratch_shapes=[pltpu.VMEM((seq_len + SUBLANE, HEAD_DIM), F32)]
        + [pltpu.VMEM((seq_len, HEAD_DIM), F32)] * 3
        + [pltpu.VMEM((seq_len, LANE), F32)] * 2
        + [per_head(seq_len, HEAD_DIM)] * 3
        + [per_head(nc * HEAD_DIM, C),
           per_head(seq_len, C),
           per_head(nc * SUBLANE, LANE),
           per_head(seq_len, HEAD_DIM)],
        compiler_params=_cparams(("arbitrary", "arbitrary")),
        name="gdn",
    )(a_log, dt_bias, p2, p2, p2, p2, p2, conv_w, conv_w, conv_w, conv_buf8, conv_buf8, conv_buf8,
      s0, gn_w.reshape(1, HEAD_DIM))


def _out_proj_body(on_ref, og_ref, x_ref, w_ref, g1_ref, nw_ref, sc_ref, sh_ref, x1_ref, h2_ref, h2b_ref, *,
                   seq_len):
    mix = _dot(on_ref[...].astype(BF16), w_ref[0:NSA_WIDTH, :]) + _dot(og_ref[...].astype(BF16), w_ref[NSA_WIDTH:, :])
    x1 = x_ref[...] + _seq_rows(g1_ref, seq_len) * mix
    x1_ref[...] = x1
    y = x1 * lax.rsqrt(jnp.mean(x1 * x1, axis=-1, keepdims=True) + NORM_EPS)
    h2 = (y * nw_ref[...]) * (1.0 + _seq_rows(sc_ref, seq_len)) + _seq_rows(sh_ref, seq_len)
    h2_ref[...] = h2
    h2b_ref[...] = h2.astype(BF16)


def _out_proj(o_nsa, o_gdn, x, w_out_bf, g1, norm_w, sc, sh, seq_len, tm):
    n, d = x.shape
    mod = _seq_spec(seq_len, tm, d)
    return pl.pallas_call(
        functools.partial(_out_proj_body, seq_len=seq_len),
        out_shape=(jax.ShapeDtypeStruct((n, d), F32), jax.ShapeDtypeStruct((n, d), F32),
                   jax.ShapeDtypeStruct((n, d), BF16)),
        grid=(n // tm,),
        in_specs=[pl.BlockSpec((tm, NSA_WIDTH), lambda i: (i, 0)),
                  pl.BlockSpec((tm, GDN_WIDTH), lambda i: (i, 0)),
                  pl.BlockSpec((tm, d), lambda i: (i, 0)),
                  pl.BlockSpec((d, d), lambda i: (0, 0)),
                  mod,
                  pl.BlockSpec((1, d), lambda i: (0, 0)),
                  mod, mod],
        out_specs=tuple(pl.BlockSpec((tm, d), lambda i: (i, 0)) for _ in range(3)),
        compiler_params=_cparams(("arbitrary",)),
        name="out_proj",
    )(o_nsa, o_gdn, x, w_out_bf, g1, norm_w.reshape(1, d), sc, sh)


def _take_top(src_ref, dst_ref):
    s = src_ref[...]
    above = jnp.full(s.shape, float(PEER_TOPK), F32)
    for k in range(PEER_TOPK):
        m = jnp.max(s, axis=0, keepdims=True)
        hit = s == m
        above = jnp.where(hit, float(k), above)
        s = jnp.where(hit, NEG_BIG, s)
        dst_ref[k:k + 1, :] = m
    return above


PAIR_ROWS = ((0, 16, 16), (16, 8, 8), (24, 8, 5), (32, 8, 4), (40, 8, 3), (48, 8, 2), (56, 8, 2), (64, 8, 2))
PAIR_TAIL = 72
PAIR_TOTAL = 80


def _peer_sel_body(h_ref, wqh_ref, wql_ref, keys_ref, cnt_ref, al_ref, code_ref, be_ref,
                   sc_s, ta_s, tb_s, cand_s, tc_s, *, split_query):
    hh, hl = _split(h_ref[...])
    half = PEER_QDIM // 2
    tn = h_ref.shape[0]
    r8 = lax.broadcasted_iota(I32, (SUBLANE, tn), 0)
    for hd in range(PEER_HEADS):
        wh = wqh_ref[hd * PEER_QDIM:(hd + 1) * PEER_QDIM, :]
        wl = wql_ref[hd * PEER_QDIM:(hd + 1) * PEER_QDIM, :]
        qt = _dot(wh, hh, NT)
        if split_query:
            qt = qt + (_dot(wl, hh, NT) + _dot(wh, hl, NT))
        s0 = _mm3(keys_ref[0], qt[0:half])
        s1 = _mm3(keys_ref[1], qt[half:PEER_QDIM])
        sc_s[...] = s0
        _take_top(sc_s, ta_s)
        sc_s[...] = s1
        code = _take_top(sc_s, tb_s)
        a = ta_s[...]
        b = tb_s[...]
        for r, (row0, nrow, nval) in enumerate(PAIR_ROWS):
            pair = a[r:r + 1] + b[0:nrow]
            cand_s[row0:row0 + nrow, :] = pair if nval == nrow else jnp.where(r8 < nval, pair, NEG_BIG)
        cand_s[PAIR_TAIL:PAIR_TOTAL, :] = a[SUBLANE:2 * SUBLANE] + b[0:1]
        _take_top(cand_s, tc_s)
        tau = tc_s[PEER_TOPK - 1:PEER_TOPK, :]
        cand = cand_s[...]
        keep = cand >= tau
        zsum = jnp.sum(jnp.where(keep, jnp.exp(cand - cand[0:1]), 0.0), axis=0, keepdims=True)
        keepf = keep.astype(F32)
        cnt = jnp.zeros((PEER_NKEYS, tn), F32)
        for r, (row0, nrow, nval) in enumerate(PAIR_ROWS):
            cnt_r = jnp.sum(keepf[row0:row0 + nrow], axis=0, keepdims=True)
            cnt = jnp.where(s0 == a[r:r + 1], cnt_r, cnt)
        for r in range(SUBLANE, 2 * SUBLANE):
            cnt = jnp.where(s0 == a[r:r + 1], keepf[PAIR_TAIL + r - SUBLANE:PAIR_TAIL + r - SUBLANE + 1], cnt)
        cnt_ref[hd] = cnt
        al_ref[hd] = jnp.exp(s0 - a[0:1]) / zsum
        code_ref[hd] = code.astype(BF16)
        be_ref[hd] = jnp.exp(s1 - b[0:1]).astype(BF16)


def _peer_select(h2, wq_hi, wq_lo, keys, tn, split_query):
    n, d = h2.shape
    hk = jax.ShapeDtypeStruct((PEER_HEADS, PEER_NKEYS, n), F32)
    hk16 = jax.ShapeDtypeStruct((PEER_HEADS, PEER_NKEYS, n), BF16)
    blk = pl.BlockSpec((PEER_HEADS, PEER_NKEYS, tn), lambda i: (0, 0, i))
    return pl.pallas_call(
        functools.partial(_peer_sel_body, split_query=split_query),
        out_shape=(hk, hk, hk16, hk16),
        grid=(n // tn,),
        in_specs=[pl.BlockSpec((tn, d), lambda i: (i, 0)),
                  pl.BlockSpec((PEER_HEADS * PEER_QDIM, d), lambda i: (0, 0)),
                  pl.BlockSpec((PEER_HEADS * PEER_QDIM, d), lambda i: (0, 0)),
                  pl.BlockSpec((2, PEER_NKEYS, PEER_QDIM // 2), lambda i: (0, 0, 0))],
        out_specs=(blk, blk, blk, blk),
        scratch_shapes=[pltpu.VMEM((PEER_NKEYS, tn), F32), pltpu.VMEM((PEER_TOPK, tn), F32),
                        pltpu.VMEM((PEER_TOPK, tn), F32), pltpu.VMEM((PAIR_TOTAL, tn), F32),
                        pltpu.VMEM((PEER_TOPK, tn), F32)],
        compiler_params=_cparams(("arbitrary",)),
        name="peer_select",
    )(h2, wq_hi, wq_lo, keys)


PEER_TE = 1024
GELU_C0 = math.sqrt(2.0 / math.pi)
GELU_C1 = 0.044715 * math.sqrt(2.0 / math.pi)


def _gelu_tanh(x):
    hx = 0.5 * x
    return hx + hx * jnp.tanh(x * (GELU_C0 + GELU_C1 * (x * x)))


def _peer_mm_body(h_ref, u_ref, vt_ref, cnt_ref, al_ref, code_ref, be_ref, o_ref, at_s, p_s):
    e = pl.program_id(1)

    @pl.when(e == 0)
    def _():
        o_ref[...] = jnp.zeros_like(o_ref)

    at_s[...] = _dot(u_ref[...], h_ref[...], NT)
    tn = h_ref.shape[0]
    for ii in range(PEER_TE // PEER_NKEYS):
        rs = slice(ii * PEER_NKEYS, (ii + 1) * PEER_NKEYS)
        for ck in range(tn // LANE):
            cs = slice(ck * LANE, (ck + 1) * LANE)
            w = jnp.zeros((PEER_NKEYS, LANE), BF16)
            for hd in range(PEER_HEADS):
                cnt = cnt_ref[hd, ii:ii + 1, cs].astype(BF16)
                kept = jnp.where(code_ref[hd, :, cs] < cnt, be_ref[hd, :, cs], jnp.zeros((), BF16))
                w = w + al_ref[hd, ii:ii + 1, cs].astype(BF16) * kept
            p_s[rs, cs] = w * _gelu_tanh(at_s[rs, cs].astype(BF16))
    o_ref[...] += _dot(vt_ref[...], p_s[...])


def _peer_mm(h2b, u_bf, vt_bf, cnt, al, code, be, tn):
    n, d = h2b.shape
    ne = u_bf.shape[0]
    rows_i = PEER_TE // PEER_NKEYS
    sel_i = pl.BlockSpec((PEER_HEADS, rows_i, tn), lambda i, e: (0, e, i))
    sel_j = pl.BlockSpec((PEER_HEADS, PEER_NKEYS, tn), lambda i, e: (0, 0, i))
    return pl.pallas_call(
        _peer_mm_body,
        out_shape=jax.ShapeDtypeStruct((d, n), F32),
        grid=(n // tn, ne // PEER_TE),
        in_specs=[pl.BlockSpec((tn, d), lambda i, e: (i, 0)),
                  pl.BlockSpec((PEER_TE, d), lambda i, e: (e, 0)),
                  pl.BlockSpec((d, PEER_TE), lambda i, e: (0, e)),
                  sel_i, sel_i, sel_j, sel_j],
        out_specs=pl.BlockSpec((d, tn), lambda i, e: (0, i)),
        scratch_shapes=[pltpu.VMEM((PEER_TE, tn), F32), pltpu.VMEM((PEER_TE, tn), BF16)],
        compiler_params=_cparams(("arbitrary", "arbitrary")),
        name="peer_experts",
    )(h2b, u_bf, vt_bf, cnt, al, code, be)


def _final_body(x1_ref, pt_ref, g2_ref, fw_ref, y_ref, *, seq_len):
    x2 = x1_ref[...] + _seq_rows(g2_ref, seq_len) * pt_ref[...].T
    y_ref[...] = x2 * lax.rsqrt(jnp.mean(x2 * x2, axis=-1, keepdims=True) + NORM_EPS) * fw_ref[...]


def _final(x1, peer_t, g2, fw, seq_len, tm):
    n, d = x1.shape
    g2_spec = _seq_spec(seq_len, tm, d)
    return pl.pallas_call(
        functools.partial(_final_body, seq_len=seq_len),
        out_shape=jax.ShapeDtypeStruct((n, d), F32),
        grid=(n // tm,),
        in_specs=[pl.BlockSpec((tm, d), lambda i: (i, 0)),
                  pl.BlockSpec((d, tm), lambda i: (0, i)),
                  g2_spec,
                  pl.BlockSpec((1, d), lambda i: (0, 0))],
        out_specs=pl.BlockSpec((tm, d), lambda i: (i, 0)),
        compiler_params=_cparams(("arbitrary",)),
        name="final_norm",
    )(x1, peer_t, g2, fw.reshape(1, d))


SROWS = 64
SROWS_TAIL = 16
CMP_PAGES = 16
KV_COMP = 2 * NSA_KV_HEADS
TOKEN_ROWS = KV_COMP
PAGE_VROWS = PAGE_SIZE * TOKEN_ROWS
CMP_PAIRS = CMP_BLOCK * TOKEN_ROWS // SUBLANE
BLOCK_VROWS = CMP_BLOCK * TOKEN_ROWS


def _compress_sample_body(pt_ref, *refs):
    pages = refs[:CMP_PAGES]
    pe_ref, w1_ref, b1_ref, w2_ref, o_ref, acc_s = refs[CMP_PAGES:]
    bpp = PAGE_SIZE // CMP_BLOCK
    nrow = CMP_PAGES * bpp * SUBLANE
    nh = CMP_PAGES * bpp // 2
    lhs = []
    for q in range(CMP_PAIRS):
        parts = [r[0, n * BLOCK_VROWS + q * SUBLANE:n * BLOCK_VROWS + (q + 1) * SUBLANE, :]
                 for r in pages for n in range(bpp)]
        lhs.append((jnp.concatenate(parts, axis=0)
                    + jnp.concatenate([pe_ref[q]] * (nrow // SUBLANE), axis=0)).astype(BF16))
    acc = _dot(jnp.concatenate(lhs, axis=1), w1_ref[...])
    rid = lax.broadcasted_iota(I32, (nrow, 1), 0) & (SUBLANE - 1)
    want = ((rid & (KV_COMP - 1)) >> 1) * 2 + (rid >> 2)
    picked = acc[:, 0:HEAD_DIM]
    for blk in range(1, KV_COMP):
        picked = jnp.where(want == blk, acc[:, blk * HEAD_DIM:(blk + 1) * HEAD_DIM], picked)
    acc_s[...] = picked + pltpu.roll(picked, nrow - KV_COMP, 0)
    for c in range(KV_COMP):
        s = c // NSA_KV_HEADS
        hid_e = acc_s[pl.ds(c, nh, stride=2 * SUBLANE), :]
        hid_o = acc_s[pl.ds(SUBLANE + c, nh, stride=2 * SUBLANE), :]
        hid = jax.nn.gelu(jnp.concatenate([hid_e, hid_o], axis=0) + b1_ref[s])
        out = _mm1(hid, w2_ref[s])
        o_ref[0, c, 0] = out[0:nh]
        o_ref[0, c, 1] = out[nh:2 * nh]


def _compress_sample(cache, page_table, pe, w1, b1, w2):
    nbs, n_pages = page_table.shape
    n_phys = cache.shape[0]
    cache_v = cache.reshape(n_phys, PAGE_VROWS, HEAD_DIM)
    ngrp = n_pages // CMP_PAGES
    nblk_half = n_pages * PAGE_SIZE // CMP_BLOCK // 2
    nh = CMP_PAGES * (PAGE_SIZE // CMP_BLOCK) // 2
    pe_q = jnp.transpose(pe.reshape(2, CMP_PAIRS, 2, HEAD_DIM), (1, 2, 0, 3))
    pe_q = jnp.broadcast_to(pe_q[:, :, :, None, :], (CMP_PAIRS, 2, 2, NSA_KV_HEADS, HEAD_DIM))
    pe_q = pe_q.reshape(CMP_PAIRS, SUBLANE, HEAD_DIM)
    w1_q = jnp.transpose(w1.reshape(2, CMP_PAIRS, 2, HEAD_DIM, HEAD_DIM), (1, 3, 0, 2, 4))
    w1_q = w1_q.reshape(CMP_PAIRS * HEAD_DIM, KV_COMP * HEAD_DIM).astype(BF16)
    page_spec = lambda k: pl.BlockSpec((1, PAGE_VROWS, HEAD_DIM),
                                       lambda b, g, pt, k=k: (pt[b * n_pages + g * CMP_PAGES + k], 0, 0))
    const = lambda shape: pl.BlockSpec(shape, lambda b, g, pt: (0,) * len(shape))
    grid_spec = pltpu.PrefetchScalarGridSpec(
        num_scalar_prefetch=1,
        grid=(nbs, ngrp),
        in_specs=[page_spec(k) for k in range(CMP_PAGES)]
        + [const((CMP_PAIRS, SUBLANE, HEAD_DIM)), const((CMP_PAIRS * HEAD_DIM, KV_COMP * HEAD_DIM)),
           const((2, 1, HEAD_DIM)), const((2, HEAD_DIM, HEAD_DIM))],
        out_specs=pl.BlockSpec((1, KV_COMP, 2, nh, HEAD_DIM), lambda b, g, pt: (b, 0, 0, g, 0)),
        scratch_shapes=[pltpu.VMEM((CMP_PAGES * (PAGE_SIZE // CMP_BLOCK) * SUBLANE, HEAD_DIM), F32)])
    return pl.pallas_call(
        _compress_sample_body,
        out_shape=jax.ShapeDtypeStruct((nbs, KV_COMP, 2, nblk_half, HEAD_DIM), F32),
        grid_spec=grid_spec,
        compiler_params=_cparams(("arbitrary", "arbitrary")),
        name="nsa_compress_sample",
    )(page_table.reshape(-1), *([cache_v] * CMP_PAGES), pe_q, w1_q, b1.reshape(2, 1, HEAD_DIM), w2)


def _rows_from_lanes(row, ngrp):
    ridx = lax.broadcasted_iota(I32, (SUBLANE, HEAD_DIM), 0)
    out = jnp.zeros((SUBLANE, HEAD_DIM), F32)
    for g in range(ngrp):
        out = jnp.where(ridx == g, jnp.broadcast_to(row[:, g * HEAD_DIM:(g + 1) * HEAD_DIM], (SUBLANE, HEAD_DIM)), out)
    return out


def _slope_rows(slopes_ref, h):
    ridx = lax.broadcasted_iota(I32, (SUBLANE, 1), 0)
    slope = jnp.zeros((SUBLANE, 1), F32)
    for g in range(NSA_GROUP):
        slope = jnp.where(ridx == g, slopes_ref[h * NSA_GROUP + g], slope)
    return slope


def _nsa_sample_select_body(slopes_ref, q_ref, ckv_ref, ocmp_ref, idx_ref, *, past):
    ncb = past // CMP_BLOCK // 2
    nblk = past // SEL_BLOCK
    ridx = lax.broadcasted_iota(I32, (SUBLANE, 1), 0)
    midx = lax.broadcasted_iota(I32, (1, ncb), 1)
    jb = lax.broadcasted_iota(I32, (1, nblk), 1)
    slot = lax.broadcasted_iota(I32, (1, SEL_TOPN), 1)
    for h in range(NSA_KV_HEADS):
        q8 = _rows_from_lanes(q_ref[0:1, h * NSA_GROUP * HEAD_DIM:(h + 1) * NSA_GROUP * HEAD_DIM], NSA_GROUP)
        q8 = q8 * (HEAD_DIM ** -0.5)
        slope = _slope_rows(slopes_ref, h)
        lg = []
        for par in range(2):
            s_c = _mm3(q8, ckv_ref[0, h, par], NT)
            cend = midx * (2 * CMP_BLOCK) + (CMP_BLOCK - 1) + par * CMP_BLOCK
            lg.append(s_c - slope * (past - cend).astype(F32))
        mx = jnp.maximum(jnp.max(lg[0], axis=1, keepdims=True), jnp.max(lg[1], axis=1, keepdims=True))
        e0 = jnp.exp(lg[0] - mx)
        e1 = jnp.exp(lg[1] - mx)
        den = jnp.sum(e0, axis=1, keepdims=True) + jnp.sum(e1, axis=1, keepdims=True)
        p0 = e0 / den
        p1 = e1 / den
        o_cmp = _mm1(p0, ckv_ref[0, 2 + h, 0]) + _mm1(p1, ckv_ref[0, 2 + h, 1])
        ocmp_ref[0, h * NSA_GROUP:(h + 1) * NSA_GROUP, :] = o_cmp[0:NSA_GROUP]
        imp = jnp.sum(jnp.where(ridx < NSA_GROUP, p0 + p1, 0.0), axis=0, keepdims=True)
        forced = (jb == 0) | (jb == nblk - 1)
        score = jnp.where(forced, FORCE_SCORE, imp)
        picks = jnp.full((1, SEL_TOPN), nblk, I32)
        for k in range(SEL_TOPN - 1):
            m = jnp.max(score, axis=1, keepdims=True)
            first = jnp.min(jnp.where(score == m, jb, nblk), axis=1, keepdims=True)
            score = jnp.where(jb == first, NEG_BIG, score)
            picks = jnp.where(slot == k, first, picks)
        idx_ref[0, h:h + 1, :] = picks


def _nsa_sample_select(p2s, ckv_s, slopes, nbs, past):
    nblk_half = past // CMP_BLOCK // 2
    return pl.pallas_call(
        functools.partial(_nsa_sample_select_body, past=past),
        out_shape=(jax.ShapeDtypeStruct((nbs, NSA_HEADS, HEAD_DIM), F32),
                   jax.ShapeDtypeStruct((nbs, NSA_KV_HEADS, SEL_TOPN), I32)),
        grid=(nbs,),
        in_specs=[pl.BlockSpec(memory_space=pltpu.SMEM),
                  pl.BlockSpec((SUBLANE, NSA_WIDTH), lambda b: (b * (SROWS // SUBLANE), 0)),
                  pl.BlockSpec((1, 4, 2, nblk_half, HEAD_DIM), lambda b: (b, 0, 0, 0, 0))],
        out_specs=(pl.BlockSpec((1, NSA_HEADS, HEAD_DIM), lambda b: (b, 0, 0)),
                   pl.BlockSpec((1, NSA_KV_HEADS, SEL_TOPN), lambda b: (b, 0, 0))),
        compiler_params=_cparams(("arbitrary",)),
        name="nsa_sample_select",
    )(slopes, p2s, ckv_s)


NSEL_PAST = SEL_TOPN - 1


def _nsa_sample_attend_body(idx_ref, pt_ref, slopes_ref, q_ref, gt_ref, ocmp_ref, skn_ref, svn_ref, wkn_ref, wvn_ref,
                            *refs, past, n_pages):
    blocks = refs[:NSEL_PAST]
    wc_ref, o_ref = refs[NSEL_PAST:]
    b = pl.program_id(0)
    h = pl.program_id(1)
    q8 = _rows_from_lanes(q_ref[0:1, :], NSA_GROUP) * (HEAD_DIM ** -0.5)
    q8b = q8.astype(BF16)
    slope = _slope_rows(slopes_ref, h)

    def attend_with_new(k_old, v_old, lg_bias, k_new, v_new):
        lg = _dot(q8b, k_old.astype(BF16), NT) + lg_bias
        s_new = jnp.sum(q8 * k_new, axis=1, keepdims=True)
        m = jnp.maximum(jnp.max(lg, axis=1, keepdims=True), s_new)
        p = jnp.exp(lg - m)
        pn = jnp.exp(s_new - m)
        den = jnp.sum(p, axis=1, keepdims=True) + pn
        return (_dot(p.astype(BF16), v_old.astype(BF16)) + pn * v_new) / den

    k_sel = jnp.concatenate([r[0, pl.ds(h, SEL_BLOCK, stride=TOKEN_ROWS), :] for r in blocks], axis=0)
    v_sel = jnp.concatenate([r[0, pl.ds(NSA_KV_HEADS + h, SEL_BLOCK, stride=TOKEN_ROWS), :] for r in blocks], axis=0)
    lane = lax.broadcasted_iota(I32, (1, NSEL_PAST * SEL_BLOCK), 1)
    slot = lane >> (SEL_BLOCK.bit_length() - 1)
    blk = jnp.zeros((1, NSEL_PAST * SEL_BLOCK), I32)
    for j in range(NSEL_PAST):
        blk = jnp.where(slot == j, idx_ref[(b * NSA_KV_HEADS + h) * SEL_TOPN + j], blk)
    dist = (past - (blk * SEL_BLOCK + (lane & (SEL_BLOCK - 1)))).astype(F32)
    o_sel = attend_with_new(k_sel, v_sel, -slope * dist, skn_ref[0:1, :], svn_ref[0:1, :])

    nw = wc_ref.shape[1] // TOKEN_ROWS
    r = lax.broadcasted_iota(I32, (1, nw), 1)
    delta = nw - r
    wmask = (delta < WINDOW) & (past - delta >= 0)
    k_win = wc_ref[0, pl.ds(h, nw, stride=TOKEN_ROWS), :]
    v_win = wc_ref[0, pl.ds(NSA_KV_HEADS + h, nw, stride=TOKEN_ROWS), :]
    o_win = attend_with_new(k_win, v_win, jnp.where(wmask, -slope * delta.astype(F32), NEG_INF),
                            wkn_ref[0:1, :], wvn_ref[0:1, :])

    gs = jax.nn.sigmoid(gt_ref[0:1, :])
    o_ref[...] = jnp.zeros_like(o_ref)
    for g in range(NSA_GROUP):
        def gate(c):
            a = gs[:, g * 3 + c:g * 3 + c + 1]
            bb = gs[:, NSA_GROUP * 3 + g * 3 + c:NSA_GROUP * 3 + g * 3 + c + 1]
            return jnp.where(h == 0, a, bb)

        oc = jnp.where(h == 0, ocmp_ref[0, g:g + 1, :], ocmp_ref[0, NSA_GROUP + g:NSA_GROUP + g + 1, :])
        og = (gate(0) * oc + gate(1) * o_sel[g:g + 1]) + gate(2) * o_win[g:g + 1]
        o_ref[0:1, g * HEAD_DIM:(g + 1) * HEAD_DIM] = og


def _nsa_sample_attend(p2s, ocmp, idx, page_table, cache_sel, win_state, slopes, nbs, past):
    n_pages = page_table.shape[1]
    n_phys = cache_sel.shape[0]
    nw = win_state.shape[1]
    halves = PAGE_SIZE // SEL_BLOCK
    sel_v = cache_sel.reshape(n_phys * halves, SEL_BLOCK * TOKEN_ROWS, HEAD_DIM)
    win_v = win_state.reshape(nbs, nw * TOKEN_ROWS, HEAD_DIM)
    rb = SROWS // SUBLANE
    kvs0 = COL_KVS // LANE
    kvw0 = COL_KVW // LANE
    misc = COL_MISC // LANE

    def sel_spec(j):
        def index(b, h, idx, pt):
            blk = idx[(b * NSA_KV_HEADS + h) * SEL_TOPN + j]
            page = pt[b * n_pages + blk // halves]
            return (page * halves + blk % halves, 0, 0)
        return pl.BlockSpec((1, SEL_BLOCK * TOKEN_ROWS, HEAD_DIM), index)

    new = lambda c0: pl.BlockSpec((SUBLANE, LANE), lambda b, h, idx, pt, c0=c0: (b * rb, c0 + h))
    grid_spec = pltpu.PrefetchScalarGridSpec(
        num_scalar_prefetch=2,
        grid=(nbs, NSA_KV_HEADS),
        in_specs=[pl.BlockSpec(memory_space=pltpu.SMEM),
                  pl.BlockSpec((SUBLANE, NSA_GROUP * HEAD_DIM), lambda b, h, idx, pt: (b * rb, h)),
                  pl.BlockSpec((SUBLANE, LANE), lambda b, h, idx, pt: (b * rb, misc)),
                  pl.BlockSpec((1, NSA_HEADS, HEAD_DIM), lambda b, h, idx, pt: (b, 0, 0)),
                  new(kvs0), new(kvs0 + 2), new(kvw0), new(kvw0 + 2)]
        + [sel_spec(j) for j in range(NSEL_PAST)]
        + [pl.BlockSpec((1, nw * TOKEN_ROWS, HEAD_DIM), lambda b, h, idx, pt: (b, 0, 0))],
        out_specs=pl.BlockSpec((SROWS, NSA_GROUP * HEAD_DIM), lambda b, h, idx, pt: (b, h)))
    return pl.pallas_call(
        functools.partial(_nsa_sample_attend_body, past=past, n_pages=n_pages),
        out_shape=jax.ShapeDtypeStruct((nbs * SROWS, NSA_WIDTH), F32),
        grid_spec=grid_spec,
        compiler_params=_cparams(("arbitrary", "arbitrary")),
        name="nsa_sample_attend",
    )(idx.reshape(-1), page_table.reshape(-1), slopes, p2s, p2s, ocmp, p2s, p2s, p2s, p2s,
      *([sel_v] * NSEL_PAST), win_v)


def _perm_w_in(w_in):
    d = w_in.shape[0]
    g0 = NSA_WIDTH + 3 * 2 * KV_WIDTH
    q0 = g0 + 3 * NSA_HEADS
    b0 = q0 + 3 * GDN_WIDTH + GDN_WIDTH
    pad = jnp.zeros((d, N_PROJ - (b0 + 2 * GDN_HEADS)), w_in.dtype)
    w2 = jnp.concatenate([w_in[:, :g0], w_in[:, q0:b0], w_in[:, g0:q0], w_in[:, b0:], pad], axis=1)
    return w2.astype(BF16)


def _split_bf16(w):
    hi = w.astype(BF16)
    lo = (w - hi.astype(F32)).astype(BF16)
    return hi, lo


def _mixer_tail(x, o_nsa, o_gdn, w_out_bf, g1, norm_ffn, sc2, sh2, g2, wq_hi, wq_lo, keys, u_bf, vt_bf,
                final_norm, seq_len, tm, tn_sel, tn_mm, tm_final):
    x1, h2, h2b = _out_proj(o_nsa, o_gdn, x, w_out_bf, g1, norm_ffn, sc2, sh2, seq_len, tm)
    cnt, al, code, be = _peer_select(h2, wq_hi, wq_lo, keys, tn_sel, split_query=h2.shape[0] <= tn_sel)
    peer_t = _peer_mm(h2b, u_bf, vt_bf, cnt, al, code, be, tn_mm)
    return _final(x1, peer_t, g2, final_norm, seq_len, tm_final)


def kernel(x_prompt, x_sample, cache_cmp_kv, cache_sel_kv, state_win_kv, state_conv, state_gdn, page_table,
           c_prompt, c_sample, w_ada, b_ada, norm_mix, norm_ffn, w_in, cmp_pe, cmp_w1, cmp_b1, cmp_w2, conv_w,
           gdn_a_log, gdn_dt_bias, gdn_norm, w_out, peer_wq, peer_keys, peer_u, peer_v, final_norm):
    nb, seq, d = x_prompt.shape
    nbs, dec_seq, _ = x_sample.shape
    assert w_in.shape[0] == 1 and dec_seq == 1, "single layer, single decode token"
    past = page_table.shape[1] * PAGE_SIZE
    slopes = 2.0 ** (-8.0 * jnp.arange(1, NSA_HEADS + 1, dtype=F32) / NSA_HEADS)
    rows_c = 16
    c_all = jnp.concatenate([c_prompt, c_sample, jnp.zeros((rows_c - nb - nbs, d), F32)], axis=0)
    mod = _adaln(c_all, w_ada[0], b_ada[0]).reshape(rows_c, 6, d)
    pm = lambda k: mod[0:nb, k][:, None, :]
    sm = lambda k: mod[nb:nb + nbs, k][:, None, :]
    w2 = _perm_w_in(w_in[0])
    w_out_bf = w_out[0].astype(BF16)
    wq_hi, wq_lo = _split_bf16(peer_wq[0].T)
    u_bf = peer_u[0].astype(BF16)
    vt_bf = peer_v[0].T.astype(BF16)
    cw = (cmp_pe[0], cmp_w1[0], cmp_b1[0], cmp_w2[0])
    kvw = 2 * KV_WIDTH

    xp = x_prompt.reshape(nb * seq, d)
    p2, cmp_rows, sel_rows = _proj_in(xp, norm_mix[0], pm(1), pm(0), w2, seq, 1024)
    ckv = _compress_prompt(p2, *cw, nb, seq)
    o_nsa = _nsa_prompt(p2, ckv, slopes, nb, seq)
    conv0 = jnp.zeros((nb, SUBLANE, 3 * GDN_WIDTH), F32)
    s0 = jnp.zeros((nb, GDN_HEADS, HEAD_DIM, HEAD_DIM), F32)
    o_gdn, gdn_p = _gdn(p2, conv_w[0], conv0, s0, gdn_a_log[0], gdn_dt_bias[0], gdn_norm[0], nb, seq, seq)
    y_prompt = _mixer_tail(xp, o_nsa, o_gdn, w_out_bf, pm(2), norm_ffn[0], pm(4), pm(3), pm(5), wq_hi, wq_lo,
                           peer_keys[0], u_bf, vt_bf, final_norm, seq, 512, 256, 512, 512).reshape(nb, seq, d)
    p3 = p2.reshape(nb, seq, N_PROJ)
    keep = min(WINDOW, seq)
    kv5 = lambda a: a.reshape(a.shape[0], a.shape[1], 2, NSA_KV_HEADS, HEAD_DIM)
    cmp_p = cmp_rows.reshape(nb, seq, 2, NSA_KV_HEADS, HEAD_DIM)
    sel_p = sel_rows.reshape(nb, seq, 2, NSA_KV_HEADS, HEAD_DIM)
    win_p = kv5(p3[:, seq - keep:, COL_KVW:COL_KVW + kvw])
    conv_p = p3[:, seq - (CONV_WIDTH - 1):, COL_QKV:COL_QKV + 3 * GDN_WIDTH]

    xs = jnp.pad(x_sample, ((0, 0), (0, SROWS - dec_seq), (0, 0))).reshape(nbs * SROWS, d)
    p2s, cmp_rows_s, sel_rows_s = _proj_in(xs, norm_mix[0], sm(1), sm(0), w2, SROWS, nbs * SROWS)
    ckv_s = _compress_sample(cache_cmp_kv[0], page_table, *cw)
    ocmp, idx = _nsa_sample_select(p2s, ckv_s, slopes, nbs, past)
    o_nsa_s = _nsa_sample_attend(p2s, ocmp, idx, page_table, cache_sel_kv[0], state_win_kv[0], slopes, nbs, past)
    conv_buf = jnp.pad(state_conv[0], ((0, 0), (SUBLANE - (CONV_WIDTH - 1), 0), (0, 0)))
    o_gdn_s, gdn_s = _gdn(p2s, conv_w[0], conv_buf, state_gdn[0], gdn_a_log[0], gdn_dt_bias[0], gdn_norm[0],
                          nbs, SROWS, dec_seq)
    head_rows = lambda a: a.reshape(nbs, SROWS, a.shape[-1])[:, :SROWS_TAIL].reshape(nbs * SROWS_TAIL, a.shape[-1])
    nt = nbs * SROWS_TAIL
    ys = _mixer_tail(head_rows(xs), head_rows(o_nsa_s), head_rows(o_gdn_s), w_out_bf, sm(2), norm_ffn[0], sm(4),
                     sm(3), sm(5), wq_hi, wq_lo, peer_keys[0], u_bf, vt_bf, final_norm, SROWS_TAIL, nt, nt, nt, nt)
    y_sample = ys.reshape(nbs, SROWS_TAIL, d)[:, 0:dec_seq]
    p3s = p2s.reshape(nbs, SROWS, N_PROJ)[:, 0:dec_seq]
    cmp_s = cmp_rows_s.reshape(nbs, SROWS, 2, NSA_KV_HEADS, HEAD_DIM)[:, 0:dec_seq]
    sel_s = sel_rows_s.reshape(nbs, SROWS, 2, NSA_KV_HEADS, HEAD_DIM)[:, 0:dec_seq]
    win_new = kv5(p3s[:, :, COL_KVW:COL_KVW + kvw])
    win_s = jnp.concatenate([state_win_kv[0], win_new], axis=1)[:, dec_seq:]
    conv_s = jnp.concatenate([state_conv[0], p3s[:, :, COL_QKV:COL_QKV + 3 * GDN_WIDTH]], axis=1)[:, dec_seq:]

    return (y_prompt, y_sample, cmp_p[None], cmp_s[None], sel_p[None], sel_s[None], win_p[None], win_s[None],
            conv_p[None], conv_s[None], gdn_p[None], gdn_s[None])
```

```python
import functools
import math

import jax
import jax.numpy as jnp
from jax import lax
from jax.experimental import pallas as pl
from jax.experimental.pallas import tpu as pltpu

F32 = jnp.float32
BF16 = jnp.bfloat16
I32 = jnp.int32

D_MODEL = 2048
HEAD_DIM = 128
NSA_WIDTH = 1024
GDN_WIDTH = 1024
NSA_HEADS = 8
NSA_KV_HEADS = 2
NSA_GROUP = 4
KV_WIDTH = 256
CMP_BLOCK = 32
SEL_BLOCK = 64
SEL_TOPN = 16
WINDOW = 512
GDN_HEADS = 8
CONV_WIDTH = 4
GDN_CHUNK = 64
PEER_HEADS = 8
PEER_NKEYS = 128
PEER_QDIM = 256
PEER_TOPK = 16
PAGE_SIZE = 128
NORM_EPS = 1e-6
NEG_INF = -1e30
FORCE_SCORE = 1e4
NEG_BIG = -3.0e38

LANE = 128
SUBLANE = 8
VMEM_LIMIT = 56 * 1024 * 1024

COL_Q = 0
COL_KVC = 1024
COL_KVS = 1536
COL_KVW = 2048
COL_QKV = 2560
COL_Z = 5632
COL_MISC = 6656
N_PROJ = 7168
MISC_BL = 24
MISC_AL = 32

NN = (((1,), (0,)), ((), ()))
NT = (((1,), (1,)), ((), ()))
BNN = (((2,), (1,)), ((0,), (0,)))
BNT = (((2,), (2,)), ((0,), (0,)))


def _dot(a, b, dims=NN):
    return lax.dot_general(a, b, dims, preferred_element_type=F32)


def _split(x):
    hi = x.astype(BF16)
    lo = (x - hi.astype(F32)).astype(BF16)
    return hi, lo


def _mm1(a, b, dims=NN):
    return _dot(a.astype(BF16), b.astype(BF16), dims)


def _mm3(a, b, dims=NN):
    ah, al = _split(a)
    bh, bl = _split(b)
    return _dot(ah, bh, dims) + (_dot(al, bh, dims) + _dot(ah, bl, dims))


def _cparams(sem, vmem=VMEM_LIMIT):
    return pltpu.CompilerParams(dimension_semantics=sem, vmem_limit_bytes=vmem)


def _adaln_body(c_ref, w_ref, b_ref, o_ref):
    a = jax.nn.silu(c_ref[...])
    o_ref[...] = _mm3(a, w_ref[...]) + b_ref[...]


def _adaln(c_all, w, b):
    rows, d = c_all.shape
    n = w.shape[1]
    tn = 1024
    return pl.pallas_call(
        _adaln_body,
        out_shape=jax.ShapeDtypeStruct((rows, n), F32),
        grid=(n // tn,),
        in_specs=[pl.BlockSpec((rows, d), lambda j: (0, 0)),
                  pl.BlockSpec((d, tn), lambda j: (0, j)),
                  pl.BlockSpec((1, tn), lambda j: (0, j))],
        out_specs=pl.BlockSpec((rows, tn), lambda j: (0, j)),
        compiler_params=_cparams(("arbitrary",)),
        name="adaln",
    )(c_all, w, b.reshape(1, n))


def _seq_rows(m_ref, seq_len):
    nseq = m_ref.shape[0]
    if nseq == 1:
        return m_ref[0]
    return jnp.concatenate([jnp.broadcast_to(m_ref[r], (seq_len, m_ref.shape[2])) for r in range(nseq)], axis=0)


def _seq_spec(seq_len, tm, d):
    if tm <= seq_len:
        tpb = seq_len // tm
        return pl.BlockSpec((1, 1, d), lambda i, *_: (i // tpb, 0, 0))
    return pl.BlockSpec((tm // seq_len, 1, d), lambda i, *_: (i, 0, 0))


PROJ_TN = 512
PROJ_TILES_A = COL_QKV // PROJ_TN
PROJ_TILES_B = (COL_MISC - COL_QKV) // PROJ_TN


def _proj_in_body(x_ref, nw_ref, sc_ref, sh_ref, wa_ref, wb_ref, wc_ref, o_ref, cmp_ref, sel_ref, h_ref, *,
                  seq_len):
    j = pl.program_id(1)
    tm = x_ref.shape[0]

    @pl.when(j == 0)
    def _():
        x = x_ref[...]
        y = x * lax.rsqrt(jnp.mean(x * x, axis=-1, keepdims=True) + NORM_EPS)
        h = (y * nw_ref[...]) * (1.0 + _seq_rows(sc_ref, seq_len)) + _seq_rows(sh_ref, seq_len)
        h_ref[...] = h.astype(BF16)

    @pl.when(j < PROJ_TILES_A)
    def _():
        o_ref[...] = _dot(h_ref[...], wa_ref[...])

    @pl.when((j >= PROJ_TILES_A) & (j < PROJ_TILES_A + PROJ_TILES_B))
    def _():
        o_ref[...] = _dot(h_ref[...], wb_ref[...])

    @pl.when(j >= PROJ_TILES_A + PROJ_TILES_B)
    def _():
        o_ref[...] = _dot(h_ref[...], wc_ref[...])

    for dst, col in ((cmp_ref, COL_KVC), (sel_ref, COL_KVS)):
        @pl.when(j == col // PROJ_TN)
        def _(dst=dst, col=col):
            for c in range(TOKEN_ROWS):
                lo = col % PROJ_TN + c * HEAD_DIM
                dst[pl.ds(c, tm, stride=TOKEN_ROWS), :] = o_ref[:, lo:lo + HEAD_DIM]


def _proj_in(x, norm_w, sc, sh, w_all, w_gdn, w_misc, seq_len, tm):
    n, d = x.shape
    ncol = N_PROJ
    tn = PROJ_TN
    for col in (COL_KVC, COL_KVS):
        assert col % tn + 2 * KV_WIDTH <= tn, "a cached column group must not straddle column tiles"
    assert w_gdn.shape[1] == PROJ_TILES_B * tn and w_misc.shape[1] == tn
    clamp = lambda j, lo, hi: jnp.minimum(jnp.maximum(j - lo, 0), hi)
    mod = _seq_spec(seq_len, tm, d)
    rows = jax.ShapeDtypeStruct((n * TOKEN_ROWS, HEAD_DIM), F32)
    rows_spec = pl.BlockSpec((tm * TOKEN_ROWS, HEAD_DIM), lambda i, j: (i, 0))
    return pl.pallas_call(
        functools.partial(_proj_in_body, seq_len=seq_len),
        out_shape=(jax.ShapeDtypeStruct((n, ncol), F32), rows, rows),
        grid=(n // tm, ncol // tn),
        in_specs=[pl.BlockSpec((tm, d), lambda i, j: (i, 0)),
                  pl.BlockSpec((1, d), lambda i, j: (0, 0)),
                  mod, mod,
                  pl.BlockSpec((d, tn), lambda i, j: (0, clamp(j, 0, PROJ_TILES_A - 1))),
                  pl.BlockSpec((d, tn), lambda i, j: (0, clamp(j, PROJ_TILES_A, PROJ_TILES_B - 1))),
                  pl.BlockSpec((d, tn), lambda i, j: (0, 0))],
        out_specs=(pl.BlockSpec((tm, tn), lambda i, j: (i, j)), rows_spec, rows_spec),
        scratch_shapes=[pltpu.VMEM((tm, d), BF16)],
        compiler_params=_cparams(("arbitrary", "arbitrary")),
        name="proj_in",
    )(x, norm_w.reshape(1, d), sc, sh, w_all, w_gdn, w_misc)


def _compress_prompt_body(x_ref, pe_ref, w1_ref, b1_ref, w2_ref, o_ref, *, nblk):
    half = nblk // 2
    xs = []
    for l in range(CMP_BLOCK):
        xe = x_ref[pl.ds(l, half, stride=2 * CMP_BLOCK), :]
        xo = x_ref[pl.ds(CMP_BLOCK + l, half, stride=2 * CMP_BLOCK), :]
        xs.append((jnp.concatenate([xe, xo], axis=0) + pe_ref[0, l:l + 1, :]).astype(BF16))
    acc = _dot(jnp.concatenate(xs, axis=1), w1_ref[0].astype(BF16))
    hid = jax.nn.gelu(acc + b1_ref[0])
    o_ref[0, 0] = _mm1(hid, w2_ref[0])


def _compress_prompt(p2, pe, w1, b1, w2, nbatch, seq_len):
    nblk = seq_len // CMP_BLOCK
    kv0 = COL_KVC // LANE
    return pl.pallas_call(
        functools.partial(_compress_prompt_body, nblk=nblk),
        out_shape=jax.ShapeDtypeStruct((nbatch, 4, nblk, HEAD_DIM), F32),
        grid=(nbatch, 4),
        in_specs=[pl.BlockSpec((seq_len, LANE), lambda b, sh: (b, kv0 + sh)),
                  pl.BlockSpec((1, CMP_BLOCK, HEAD_DIM), lambda b, sh: (sh // 2, 0, 0)),
                  pl.BlockSpec((1, CMP_BLOCK * HEAD_DIM, HEAD_DIM), lambda b, sh: (sh // 2, 0, 0)),
                  pl.BlockSpec((1, 1, HEAD_DIM), lambda b, sh: (sh // 2, 0, 0)),
                  pl.BlockSpec((1, HEAD_DIM, HEAD_DIM), lambda b, sh: (sh // 2, 0, 0))],
        out_specs=pl.BlockSpec((1, 1, nblk, HEAD_DIM), lambda b, sh: (b, sh, 0, 0)),
        compiler_params=_cparams(("arbitrary", "arbitrary")),
        name="nsa_compress_prompt",
    )(p2, pe, w1.reshape(2, CMP_BLOCK * HEAD_DIM, HEAD_DIM), b1.reshape(2, 1, HEAD_DIM), w2)


TQ = 128
NSA_KSPAN = 512


def _nsa_prompt_body(slopes_ref, q_ref, gt_ref, ck_ref, cv_ref, sk_ref, sv_ref, wk_ref, wv_ref,
                     o_ref, selm_ref, *, seq_len):
    h = pl.program_id(1)
    qi = pl.program_id(2)
    t0 = qi * TQ
    nsb = seq_len // SEL_BLOCK
    ncb = seq_len // CMP_BLOCK // 2
    rows = NSA_GROUP * TQ

    qb = q_ref[...] * (HEAD_DIM ** -0.5)
    q4 = jnp.concatenate([qb[:, g * HEAD_DIM:(g + 1) * HEAD_DIM] for g in range(NSA_GROUP)], axis=0)
    q4b = q4.astype(BF16)
    row = lax.broadcasted_iota(I32, (rows, 1), 0)
    tq = t0 + (row & (TQ - 1))
    grow = row >> 7
    slope = jnp.zeros((rows, 1), F32)
    for g in range(NSA_GROUP):
        slope = jnp.where(grow == g, slopes_ref[h * NSA_GROUP + g], slope)

    ck = ck_ref[0, 0]
    cv = cv_ref[0, 0]
    lane_r = lax.broadcasted_iota(I32, (1, rows), 1)
    tq_l = t0 + (lane_r & (TQ - 1))
    slope_l = jnp.zeros((1, rows), F32)
    for g in range(NSA_GROUP):
        slope_l = jnp.where((lane_r >> 7) == g, slopes_ref[h * NSA_GROUP + g], slope_l)
    midx = lax.broadcasted_iota(I32, (ncb, 1), 0)
    logits = []
    valids = []
    for par in range(2):
        s_t = _mm3(ck[par * ncb:(par + 1) * ncb], q4, NT)
        cend = midx * (2 * CMP_BLOCK) + (CMP_BLOCK - 1) + par * CMP_BLOCK
        valid = cend <= tq_l
        dist = (tq_l - cend).astype(F32)
        logits.append(jnp.where(valid, s_t - slope_l * dist, NEG_INF))
        valids.append(valid)
    mx = jnp.maximum(jnp.max(logits[0], axis=0, keepdims=True), jnp.max(logits[1], axis=0, keepdims=True))
    e0 = jnp.exp(logits[0] - mx)
    e1 = jnp.exp(logits[1] - mx)
    den = jnp.sum(e0, axis=0, keepdims=True) + jnp.sum(e1, axis=0, keepdims=True)
    p0 = jnp.where(valids[0], e0 / den, 0.0)
    p1 = jnp.where(valids[1], e1 / den, 0.0)
    ident = (lax.broadcasted_iota(I32, (TQ, TQ), 0) == lax.broadcasted_iota(I32, (TQ, TQ), 1)).astype(BF16)
    p0b = p0.astype(BF16)
    p1b = p1.astype(BF16)
    o_cmp = jnp.concatenate(
        [_dot(_dot(ident, p0b[:, g * TQ:(g + 1) * TQ], NT).astype(BF16), cv[0:ncb].astype(BF16))
         + _dot(_dot(ident, p1b[:, g * TQ:(g + 1) * TQ], NT).astype(BF16), cv[ncb:2 * ncb].astype(BF16))
         for g in range(NSA_GROUP)], axis=0)
    pb = p0 + p1
    imp = pb[:, 0:TQ]
    for g in range(1, NSA_GROUP):
        imp = imp + pb[:, g * TQ:(g + 1) * TQ]

    cur = (t0 + lax.broadcasted_iota(I32, (1, TQ), 1)) >> 6
    jb = lax.broadcasted_iota(I32, (nsb, 1), 0)
    forced = (jb == 0) | (jb == cur) | (jb == cur - 1)
    score = jnp.where(jb <= cur, jnp.where(forced, FORCE_SCORE, imp), NEG_INF)
    cnt = jnp.zeros((nsb, TQ), I32)
    for i in range(nsb):
        si = score[i:i + 1, :]
        beats = (si > score) | ((si == score) & (jb > i))
        cnt = cnt + beats.astype(I32)
    sel_t = ((cnt < min(SEL_TOPN, nsb)) & (score > 0.5 * NEG_INF)).astype(BF16)
    sel = _dot(ident, sel_t, NT).astype(BF16)
    kpos = lax.broadcasted_iota(I32, (nsb, seq_len), 1)
    kblk = lax.broadcasted_iota(I32, (nsb, seq_len), 0)
    expand = ((kpos >> 6) == kblk).astype(BF16)
    selk = _dot(sel, expand)

    kw = NSA_KSPAN
    t_row = t0 + lax.broadcasted_iota(I32, (TQ, kw), 0)
    key = lax.broadcasted_iota(I32, (TQ, kw), 1)
    for st in range(seq_len // kw):
        blk = (selk[:, st * kw:(st + 1) * kw] - 1.0) * (-NEG_INF)
        selm_ref[st] = blk + jnp.where(st * kw + key <= t_row, 0.0, NEG_INF)
    m_floor = 0.1 * NEG_INF

    def span_update(carry, k, v, bias, key0):
        m, l, acc = carry
        width = k.shape[0]
        alibi = slope * lax.broadcasted_iota(I32, (1, width), 1).astype(F32)
        x = _dot(q4b, k, NT) + alibi
        x = (x.reshape(NSA_GROUP, TQ, width) + bias[None]).reshape(rows, width)
        shift = slope * (key0 - tq).astype(F32)
        m_new = jnp.maximum(m, jnp.max(x, axis=1, keepdims=True) + shift)
        alpha = jnp.exp(m - m_new)
        p = jnp.exp(x - (m_new - shift))
        l = alpha * l + jnp.sum(p, axis=1, keepdims=True)
        acc = alpha * acc + _dot(p.astype(BF16), v)
        return m_new, l, acc

    def finish(carry):
        _, l, acc = carry
        return jnp.where(l > 0.0, acc / jnp.where(l > 0.0, l, 1.0), 0.0)

    init = (jnp.full((rows, 1), m_floor, F32), jnp.zeros((rows, 1), F32), jnp.zeros((rows, HEAD_DIM), F32))

    def sel_step(st, carry):
        start = pl.multiple_of(st * kw, kw)
        return span_update(carry, sk_ref[pl.ds(start, kw), :].astype(BF16), sv_ref[pl.ds(start, kw), :].astype(BF16),
                           selm_ref[st], start)

    o_sel = finish(lax.fori_loop(0, (t0 + TQ - 1) // kw + 1, sel_step, init))

    ww = min(WINDOW + TQ, seq_len)
    wstart = pl.multiple_of(jnp.maximum(t0 + TQ - ww, 0), TQ)
    wdelta = (t0 + lax.broadcasted_iota(I32, (TQ, ww), 0)) - (wstart + lax.broadcasted_iota(I32, (TQ, ww), 1))
    wbias = jnp.where((wdelta >= 0) & (wdelta < WINDOW), 0.0, NEG_INF)
    o_win = finish(span_update(init, wk_ref[pl.ds(wstart, ww), :].astype(BF16), wv_ref[pl.ds(wstart, ww), :].astype(BF16),
                               wbias, wstart))

    gs = jax.nn.sigmoid(gt_ref[...])
    for g in range(NSA_GROUP):
        def gate(c):
            a = gs[:, g * 3 + c:g * 3 + c + 1]
            b = gs[:, NSA_GROUP * 3 + g * 3 + c:NSA_GROUP * 3 + g * 3 + c + 1]
            return jnp.where(h == 0, a, b)

        sl = slice(g * TQ, (g + 1) * TQ)
        o_ref[:, g * HEAD_DIM:(g + 1) * HEAD_DIM] = (gate(0) * o_cmp[sl] + gate(1) * o_sel[sl]) + gate(2) * o_win[sl]


def _nsa_prompt(p2, ckv, slopes, nbatch, seq_len):
    nq = seq_len // TQ
    kvs0 = COL_KVS // LANE
    kvw0 = COL_KVW // LANE
    misc = COL_MISC // LANE
    nblk = seq_len // CMP_BLOCK
    full = lambda c0: pl.BlockSpec((seq_len, LANE), lambda b, h, q, c0=c0: (b, c0 + h))
    return pl.pallas_call(
        functools.partial(_nsa_prompt_body, seq_len=seq_len),
        out_shape=jax.ShapeDtypeStruct((nbatch * seq_len, NSA_WIDTH), F32),
        grid=(nbatch, NSA_KV_HEADS, nq),
        in_specs=[pl.BlockSpec(memory_space=pltpu.SMEM),
                  pl.BlockSpec((TQ, NSA_GROUP * HEAD_DIM), lambda b, h, q: (b * nq + q, h)),
                  pl.BlockSpec((TQ, LANE), lambda b, h, q: (b * nq + q, misc)),
                  pl.BlockSpec((1, 1, nblk, HEAD_DIM), lambda b, h, q: (b, h, 0, 0)),
                  pl.BlockSpec((1, 1, nblk, HEAD_DIM), lambda b, h, q: (b, 2 + h, 0, 0)),
                  full(kvs0), full(kvs0 + 2), full(kvw0), full(kvw0 + 2)],
        out_specs=pl.BlockSpec((TQ, NSA_GROUP * HEAD_DIM), lambda b, h, q: (b * nq + q, h)),
        scratch_shapes=[pltpu.VMEM((seq_len // NSA_KSPAN, TQ, NSA_KSPAN), F32)],
        compiler_params=_cparams(("arbitrary", "arbitrary", "arbitrary")),
        name="nsa_prompt",
    )(slopes, p2, p2, ckv, ckv, p2, p2, p2, p2)


GDN_CB = 8


def _gdn_chunk(seq_len):
    return min(GDN_CHUNK, seq_len)


def _bmm3(a, b):
    ah, al = _split(a)
    bh, bl = _split(b)
    return _dot(jnp.concatenate([ah, al, ah], axis=2), jnp.concatenate([bh, bh, bl], axis=1), BNN)


GDN_HG = 2


def _gdn_body(alog_ref, dtb_ref, xq_ref, xk_ref, xv_ref, z_ref, gt_ref, cwq_ref, cwk_ref, cwv_ref,
              cbq_ref, cbk_ref, cbv_ref, s0_ref, gn_ref, o_ref, sout_ref,
              cat_s, q_s, k_s, v_s, b_s, g_s, u_s, w_s, qg_s, kdt_s, qk_s, gl_s, oc_s, *, seq_len, valid):
    hg = pl.program_id(1)
    C = _gdn_chunk(seq_len)
    nc = seq_len // C
    cb = min(GDN_CB, nc)
    row = lax.broadcasted_iota(I32, (seq_len, 1), 0)
    lane = lax.broadcasted_iota(I32, (1, LANE), 1)
    pos = row & (C - 1)
    gt = gt_ref[...]

    def conv(x_ref, cw_ref, cb_ref, ls):
        x = x_ref[:, ls]
        cat_s[0:SUBLANE, :] = cb_ref[0, :, ls]
        cat_s[SUBLANE:, :] = x
        w = cw_ref[:, ls]
        acc = cat_s[pl.ds(SUBLANE - 3, seq_len), :] * w[0:1]
        acc = acc + cat_s[pl.ds(SUBLANE - 2, seq_len), :] * w[1:2]
        acc = acc + cat_s[pl.ds(SUBLANE - 1, seq_len), :] * w[2:3]
        acc = acc + x * w[3:4]
        return jax.nn.silu(acc)

    def l2n(x):
        return x * lax.rsqrt(jnp.sum(x * x, axis=-1, keepdims=True) + NORM_EPS)

    ii = lax.broadcasted_iota(I32, (C, C), 0)
    jj = lax.broadcasted_iota(I32, (C, C), 1)
    tri = (ii >= jj)[None]
    strict = (ii > jj)[None]
    eye = (ii == jj).astype(F32)[None]
    ones_b = jnp.ones((cb, C, 3 * C), BF16)
    ident_b = jnp.broadcast_to((lax.broadcasted_iota(I32, (HEAD_DIM, HEAD_DIM), 0)
                                == lax.broadcasted_iota(I32, (HEAD_DIM, HEAD_DIM), 1)).astype(BF16)[None],
                               (cb, HEAD_DIM, HEAD_DIM))

    for hl in range(GDN_HG):
        hd = hg * GDN_HG + hl
        ls = slice(hl * HEAD_DIM, (hl + 1) * HEAD_DIM)
        q = l2n(conv(xq_ref, cwq_ref, cbq_ref, ls)) * (HEAD_DIM ** -0.5)
        k = l2n(conv(xk_ref, cwk_ref, cbk_ref, ls))
        v = conv(xv_ref, cwv_ref, cbv_ref, ls)
        bl = jnp.sum(jnp.where(lane == MISC_BL + hd, gt, 0.0), axis=1, keepdims=True)
        al = jnp.sum(jnp.where(lane == MISC_AL + hd, gt, 0.0), axis=1, keepdims=True)
        beta = jax.nn.sigmoid(bl)
        a_pos = jnp.exp(jnp.full((1, 1), alog_ref[hd], F32))
        g = -a_pos * jax.nn.softplus(al + dtb_ref[hd])
        if valid < seq_len:
            vm = row < valid
            q = jnp.where(vm, q, 0.0)
            k = jnp.where(vm, k, 0.0)
            v = jnp.where(vm, v, 0.0)
            beta = jnp.where(vm, beta, 0.0)
            g = jnp.where(vm, g, 0.0)
        gc = g
        sft = 1
        while sft < C:
            gc = gc + jnp.where(pos >= sft, pltpu.roll(gc, sft, 0), 0.0)
            sft *= 2
        q_s[...] = q
        k_s[...] = k
        v_s[...] = v
        b_s[...] = jnp.broadcast_to(beta, (seq_len, LANE))
        g_s[...] = jnp.broadcast_to(gc, (seq_len, LANE))

        def local(gi, carry, hl=hl):
            r0 = pl.multiple_of(gi * (cb * C), cb * C)
            sl = pl.ds(r0, cb * C)
            qc = q_s[sl, :].reshape(cb, C, HEAD_DIM)
            kc = k_s[sl, :].reshape(cb, C, HEAD_DIM)
            vc = v_s[sl, :].reshape(cb, C, HEAD_DIM)
            bc = b_s[sl, :].reshape(cb, C, LANE)
            gcc = g_s[sl, :].reshape(cb, C, LANE)
            gcol = gcc[:, :, 0:C]
            dg = gcol * eye
            d1 = dg.astype(BF16)
            r1 = dg - d1.astype(F32)
            d2 = r1.astype(BF16)
            d3 = (r1 - d2.astype(F32)).astype(BF16)
            grow = _dot(ones_b, jnp.concatenate([d1, d2, d3], axis=1), BNN)
            diff = gcol - grow
            decay = jnp.where(tri, jnp.exp(jnp.where(tri, diff, 0.0)), 0.0)
            kb = kc * bc
            a = _dot(kb.astype(BF16), kc.astype(BF16), BNT) * jnp.where(strict, decay, 0.0)
            tinv = eye - a
            pw = a
            n = 2
            while n < C:
                pw = _bmm3(pw, pw)
                tinv = tinv + _bmm3(tinv, pw)
                n *= 2
            eg = jnp.exp(gcc)
            rhs = jnp.concatenate([vc * bc, kb * eg], axis=2)
            sol = _bmm3(tinv, rhs)
            u_s[hl, sl, :] = sol[:, :, 0:HEAD_DIM].reshape(cb * C, HEAD_DIM)
            w_s[hl, sl, :] = sol[:, :, HEAD_DIM:2 * HEAD_DIM].reshape(cb * C, HEAD_DIM)
            qk = _dot(qc.astype(BF16), kc.astype(BF16), BNT) * decay
            qk_s[hl, sl, :] = qk.reshape(cb * C, C)
            qg_s[hl, sl, :] = (qc * eg).reshape(cb * C, HEAD_DIM)
            glast = gcc[:, C - 1:C, :]
            kd = kc * jnp.exp(glast - gcc)
            kdt = _dot(ident_b, kd.astype(BF16), BNT)
            kdt_s[hl, pl.ds(pl.multiple_of(gi * (cb * HEAD_DIM), cb * HEAD_DIM), cb * HEAD_DIM), :] = (
                kdt.reshape(cb * HEAD_DIM, C))
            gl_s[hl, pl.ds(pl.multiple_of(gi * (cb * SUBLANE), cb * SUBLANE), cb * SUBLANE), :] = (
                jnp.broadcast_to(jnp.exp(glast), (cb, SUBLANE, LANE)).reshape(cb * SUBLANE, LANE))
            return carry

        lax.fori_loop(0, nc // cb, local, 0)

    def scan(c, states):
        sl = pl.ds(pl.multiple_of(c * C, C), C)
        new = []
        for hl in range(GDN_HG):
            s = states[hl]
            sb = s.astype(BF16)
            v_new = u_s[hl, sl, :] - _dot(w_s[hl, sl, :].astype(BF16), sb)
            o_c = (_dot(qg_s[hl, sl, :].astype(BF16), sb)
                   + _dot(qk_s[hl, sl, :].astype(BF16), v_new.astype(BF16)))
            oc_s[hl, sl, :] = o_c
            kdt = kdt_s[hl, pl.ds(pl.multiple_of(c * HEAD_DIM, HEAD_DIM), HEAD_DIM), :]
            gl = gl_s[hl, pl.ds(pl.multiple_of(c * SUBLANE, SUBLANE), 1), :]
            new.append(s * gl + _dot(kdt.astype(BF16), v_new.astype(BF16)))
        return tuple(new)

    s_fin = lax.fori_loop(0, nc, scan, tuple(s0_ref[0, hl] for hl in range(GDN_HG)))
    for hl in range(GDN_HG):
        ls = slice(hl * HEAD_DIM, (hl + 1) * HEAD_DIM)
        sout_ref[0, hl] = s_fin[hl]
        o = oc_s[hl]
        y = o * lax.rsqrt(jnp.mean(o * o, axis=-1, keepdims=True) + NORM_EPS) * gn_ref[...]
        o_ref[:, ls] = y * jax.nn.silu(z_ref[:, ls])


def _gdn(p2, conv_w, conv_buf8, s0, a_log, dt_bias, gn_w, nbatch, seq_len, valid):
    gw = GDN_HG * LANE
    q0 = COL_QKV // gw
    z0 = COL_Z // gw
    hgs = GDN_HEADS // GDN_HG
    misc = COL_MISC // LANE
    C = _gdn_chunk(seq_len)
    col = lambda c0: pl.BlockSpec((seq_len, gw), lambda b, h, c0=c0: (b, c0 + h))
    cw = lambda c0: pl.BlockSpec((CONV_WIDTH, gw), lambda b, h, c0=c0: (0, c0 + h))
    cbs = lambda c0: pl.BlockSpec((1, SUBLANE, gw), lambda b, h, c0=c0: (b, 0, c0 + h))
    smem = pl.BlockSpec(memory_space=pltpu.SMEM)
    nc = seq_len // C
    per_head = lambda rows, cols: pltpu.VMEM((GDN_HG, rows, cols), F32)
    return pl.pallas_call(
        functools.partial(_gdn_body, seq_len=seq_len, valid=valid),
        out_shape=(jax.ShapeDtypeStruct((nbatch * seq_len, GDN_WIDTH), F32),
                   jax.ShapeDtypeStruct((nbatch, GDN_HEADS, HEAD_DIM, HEAD_DIM), F32)),
        grid=(nbatch, hgs),
        in_specs=[smem, smem, col(q0), col(q0 + hgs), col(q0 + 2 * hgs), col(z0),
                  pl.BlockSpec((seq_len, LANE), lambda b, h: (b, misc)),
                  cw(0), cw(hgs), cw(2 * hgs), cbs(0), cbs(hgs), cbs(2 * hgs),
                  pl.BlockSpec((1, GDN_HG, HEAD_DIM, HEAD_DIM), lambda b, h: (b, h, 0, 0)),
                  pl.BlockSpec((1, HEAD_DIM), lambda b, h: (0, 0))],
        out_specs=(pl.BlockSpec((seq_len, gw), lambda b, h: (b, h)),
                   pl.BlockSpec((1, GDN_HG, HEAD_DIM, HEAD_DIM), lambda b, h: (b, h, 0, 0))),
        scratch_shapes=[pltpu.VMEM((seq_len + SUBLANE, HEAD_DIM), F32)]
        + [pltpu.VMEM((seq_len, HEAD_DIM), F32)] * 3
        + [pltpu.VMEM((seq_len, LANE), F32)] * 2
        + [per_head(seq_len, HEAD_DIM)] * 3
        + [per_head(nc * HEAD_DIM, C),
           per_head(seq_len, C),
           per_head(nc * SUBLANE, LANE),
           per_head(seq_len, HEAD_DIM)],
        compiler_params=_cparams(("arbitrary", "arbitrary")),
        name="gdn",
    )(a_log, dt_bias, p2, p2, p2, p2, p2, conv_w, conv_w, conv_w, conv_buf8, conv_buf8, conv_buf8,
      s0, gn_w.reshape(1, HEAD_DIM))


def _out_proj_body(on_ref, og_ref, x_ref, w_ref, g1_ref, nw_ref, sc_ref, sh_ref, x1_ref, h2_ref, h2b_ref, *,
                   seq_len):
    mix = _dot(on_ref[...].astype(BF16), w_ref[0:NSA_WIDTH, :]) + _dot(og_ref[...].astype(BF16), w_ref[NSA_WIDTH:, :])
    x1 = x_ref[...] + _seq_rows(g1_ref, seq_len) * mix
    x1_ref[...] = x1
    y = x1 * lax.rsqrt(jnp.mean(x1 * x1, axis=-1, keepdims=True) + NORM_EPS)
    h2 = (y * nw_ref[...]) * (1.0 + _seq_rows(sc_ref, seq_len)) + _seq_rows(sh_ref, seq_len)
    h2_ref[...] = h2
    h2b_ref[...] = h2.astype(BF16)


def _out_proj(o_nsa, o_gdn, x, w_out_bf, g1, norm_w, sc, sh, seq_len, tm):
    n, d = x.shape
    mod = _seq_spec(seq_len, tm, d)
    return pl.pallas_call(
        functools.partial(_out_proj_body, seq_len=seq_len),
        out_shape=(jax.ShapeDtypeStruct((n, d), F32), jax.ShapeDtypeStruct((n, d), F32),
                   jax.ShapeDtypeStruct((n, d), BF16)),
        grid=(n // tm,),
        in_specs=[pl.BlockSpec((tm, NSA_WIDTH), lambda i: (i, 0)),
                  pl.BlockSpec((tm, GDN_WIDTH), lambda i: (i, 0)),
                  pl.BlockSpec((tm, d), lambda i: (i, 0)),
                  pl.BlockSpec((d, d), lambda i: (0, 0)),
                  mod,
                  pl.BlockSpec((1, d), lambda i: (0, 0)),
                  mod, mod],
        out_specs=tuple(pl.BlockSpec((tm, d), lambda i: (i, 0)) for _ in range(3)),
        compiler_params=_cparams(("arbitrary",)),
        name="out_proj",
    )(o_nsa, o_gdn, x, w_out_bf, g1, norm_w.reshape(1, d), sc, sh)


def _take_top(src_ref, dst_ref):
    s = src_ref[...]
    above = jnp.full(s.shape, float(PEER_TOPK), F32)
    for k in range(PEER_TOPK):
        m = jnp.max(s, axis=0, keepdims=True)
        hit = s == m
        above = jnp.where(hit, float(k), above)
        s = jnp.where(hit, NEG_BIG, s)
        dst_ref[k:k + 1, :] = m
    return above


PAIR_ROWS = ((0, 16, 16), (16, 8, 8), (24, 8, 5), (32, 8, 4), (40, 8, 3), (48, 8, 2), (56, 8, 2), (64, 8, 2))
PAIR_TAIL = 72
PAIR_TOTAL = 80


def _peer_sel_body(h_ref, wqh_ref, wql_ref, keys_ref, cnt_ref, al_ref, code_ref, be_ref,
                   sc_s, ta_s, tb_s, cand_s, tc_s, *, split_query):
    hh, hl = _split(h_ref[...])
    half = PEER_QDIM // 2
    tn = h_ref.shape[0]
    r8 = lax.broadcasted_iota(I32, (SUBLANE, tn), 0)
    for hd in range(PEER_HEADS):
        wh = wqh_ref[hd * PEER_QDIM:(hd + 1) * PEER_QDIM, :]
        wl = wql_ref[hd * PEER_QDIM:(hd + 1) * PEER_QDIM, :]
        qt = _dot(wh, hh, NT)
        if split_query:
            qt = qt + (_dot(wl, hh, NT) + _dot(wh, hl, NT))
        s0 = _mm3(keys_ref[0], qt[0:half])
        s1 = _mm3(keys_ref[1], qt[half:PEER_QDIM])
        sc_s[...] = s0
        _take_top(sc_s, ta_s)
        sc_s[...] = s1
        code = _take_top(sc_s, tb_s)
        a = ta_s[...]
        b = tb_s[...]
        for r, (row0, nrow, nval) in enumerate(PAIR_ROWS):
            pair = a[r:r + 1] + b[0:nrow]
            cand_s[row0:row0 + nrow, :] = pair if nval == nrow else jnp.where(r8 < nval, pair, NEG_BIG)
        cand_s[PAIR_TAIL:PAIR_TOTAL, :] = a[SUBLANE:2 * SUBLANE] + b[0:1]
        _take_top(cand_s, tc_s)
        tau = tc_s[PEER_TOPK - 1:PEER_TOPK, :]
        cand = cand_s[...]
        keep = cand >= tau
        zsum = jnp.sum(jnp.where(keep, jnp.exp(cand - cand[0:1]), 0.0), axis=0, keepdims=True)
        keepf = keep.astype(F32)
        cnt = jnp.zeros((PEER_NKEYS, tn), F32)
        for r, (row0, nrow, nval) in enumerate(PAIR_ROWS):
            cnt_r = jnp.sum(keepf[row0:row0 + nrow], axis=0, keepdims=True)
            cnt = jnp.where(s0 == a[r:r + 1], cnt_r, cnt)
        for r in range(SUBLANE, 2 * SUBLANE):
            cnt = jnp.where(s0 == a[r:r + 1], keepf[PAIR_TAIL + r - SUBLANE:PAIR_TAIL + r - SUBLANE + 1], cnt)
        cnt_ref[hd] = cnt
        al_ref[hd] = jnp.exp(s0 - a[0:1]) / zsum
        code_ref[hd] = code.astype(BF16)
        be_ref[hd] = jnp.exp(s1 - b[0:1]).astype(BF16)


def _peer_select(h2, wq_hi, wq_lo, keys, tn, split_query):
    n, d = h2.shape
    hk = jax.ShapeDtypeStruct((PEER_HEADS, PEER_NKEYS, n), F32)
    hk16 = jax.ShapeDtypeStruct((PEER_HEADS, PEER_NKEYS, n), BF16)
    blk = pl.BlockSpec((PEER_HEADS, PEER_NKEYS, tn), lambda i: (0, 0, i))
    return pl.pallas_call(
        functools.partial(_peer_sel_body, split_query=split_query),
        out_shape=(hk, hk, hk16, hk16),
        grid=(n // tn,),
        in_specs=[pl.BlockSpec((tn, d), lambda i: (i, 0)),
                  pl.BlockSpec((PEER_HEADS * PEER_QDIM, d), lambda i: (0, 0)),
                  pl.BlockSpec((PEER_HEADS * PEER_QDIM, d), lambda i: (0, 0)),
                  pl.BlockSpec((2, PEER_NKEYS, PEER_QDIM // 2), lambda i: (0, 0, 0))],
        out_specs=(blk, blk, blk, blk),
        scratch_shapes=[pltpu.VMEM((PEER_NKEYS, tn), F32), pltpu.VMEM((PEER_TOPK, tn), F32),
                        pltpu.VMEM((PEER_TOPK, tn), F32), pltpu.VMEM((PAIR_TOTAL, tn), F32),
                        pltpu.VMEM((PEER_TOPK, tn), F32)],
        compiler_params=_cparams(("arbitrary",)),
        name="peer_select",
    )(h2, wq_hi, wq_lo, keys)


PEER_TE = 1024
GELU_C0 = math.sqrt(2.0 / math.pi)
GELU_C1 = 0.044715 * math.sqrt(2.0 / math.pi)


def _gelu_tanh(x):
    hx = 0.5 * x
    return hx + hx * jnp.tanh(x * (GELU_C0 + GELU_C1 * (x * x)))


def _peer_mm_body(h_ref, u_ref, vt_ref, cnt_ref, al_ref, code_ref, be_ref, o_ref, at_s, p_s):
    e = pl.program_id(1)

    @pl.when(e == 0)
    def _():
        o_ref[...] = jnp.zeros_like(o_ref)

    at_s[...] = _dot(u_ref[...], h_ref[...], NT)
    tn = h_ref.shape[0]
    for ii in range(PEER_TE // PEER_NKEYS):
        rs = slice(ii * PEER_NKEYS, (ii + 1) * PEER_NKEYS)
        for ck in range(tn // LANE):
            cs = slice(ck * LANE, (ck + 1) * LANE)
            w = jnp.zeros((PEER_NKEYS, LANE), BF16)
            for hd in range(PEER_HEADS):
                cnt = cnt_ref[hd, ii:ii + 1, cs].astype(BF16)
                kept = jnp.where(code_ref[hd, :, cs] < cnt, be_ref[hd, :, cs], jnp.zeros((), BF16))
                w = w + al_ref[hd, ii:ii + 1, cs].astype(BF16) * kept
            p_s[rs, cs] = w * _gelu_tanh(at_s[rs, cs].astype(BF16))
    o_ref[...] += _dot(vt_ref[...], p_s[...])


def _peer_mm(h2b, u_bf, vt_bf, cnt, al, code, be, tn):
    n, d = h2b.shape
    ne = u_bf.shape[0]
    rows_i = PEER_TE // PEER_NKEYS
    sel_i = pl.BlockSpec((PEER_HEADS, rows_i, tn), lambda i, e: (0, e, i))
    sel_j = pl.BlockSpec((PEER_HEADS, PEER_NKEYS, tn), lambda i, e: (0, 0, i))
    return pl.pallas_call(
        _peer_mm_body,
        out_shape=jax.ShapeDtypeStruct((d, n), F32),
        grid=(n // tn, ne // PEER_TE),
        in_specs=[pl.BlockSpec((tn, d), lambda i, e: (i, 0)),
                  pl.BlockSpec((PEER_TE, d), lambda i, e: (e, 0)),
                  pl.BlockSpec((d, PEER_TE), lambda i, e: (0, e)),
                  sel_i, sel_i, sel_j, sel_j],
        out_specs=pl.BlockSpec((d, tn), lambda i, e: (0, i)),
        scratch_shapes=[pltpu.VMEM((PEER_TE, tn), F32), pltpu.VMEM((PEER_TE, tn), BF16)],
        compiler_params=_cparams(("arbitrary", "arbitrary")),
        name="peer_experts",
    )(h2b, u_bf, vt_bf, cnt, al, code, be)


def _final_body(x1_ref, pt_ref, g2_ref, fw_ref, y_ref, *, seq_len):
    x2 = x1_ref[...] + _seq_rows(g2_ref, seq_len) * pt_ref[...].T
    y_ref[...] = x2 * lax.rsqrt(jnp.mean(x2 * x2, axis=-1, keepdims=True) + NORM_EPS) * fw_ref[...]


def _final(x1, peer_t, g2, fw, seq_len, tm):
    n, d = x1.shape
    g2_spec = _seq_spec(seq_len, tm, d)
    return pl.pallas_call(
        functools.partial(_final_body, seq_len=seq_len),
        out_shape=jax.ShapeDtypeStruct((n, d), F32),
        grid=(n // tm,),
        in_specs=[pl.BlockSpec((tm, d), lambda i: (i, 0)),
                  pl.BlockSpec((d, tm), lambda i: (0, i)),
                  g2_spec,
                  pl.BlockSpec((1, d), lambda i: (0, 0))],
        out_specs=pl.BlockSpec((tm, d), lambda i: (i, 0)),
        compiler_params=_cparams(("arbitrary",)),
        name="final_norm",
    )(x1, peer_t, g2, fw.reshape(1, d))


SROWS = 64
SROWS_TAIL = 16
CMP_PAGES = 16
KV_COMP = 2 * NSA_KV_HEADS
TOKEN_ROWS = KV_COMP
PAGE_VROWS = PAGE_SIZE * TOKEN_ROWS
CMP_PAIRS = CMP_BLOCK * TOKEN_ROWS // SUBLANE
BLOCK_VROWS = CMP_BLOCK * TOKEN_ROWS


def _compress_sample_body(pt_ref, *refs):
    pages = refs[:CMP_PAGES]
    pe_ref, w1_ref, b1_ref, w2_ref, o_ref, acc_s = refs[CMP_PAGES:]
    bpp = PAGE_SIZE // CMP_BLOCK
    nrow = CMP_PAGES * bpp * SUBLANE
    nh = CMP_PAGES * bpp // 2
    lhs = []
    for q in range(CMP_PAIRS):
        parts = [r[0, n * BLOCK_VROWS + q * SUBLANE:n * BLOCK_VROWS + (q + 1) * SUBLANE, :]
                 for r in pages for n in range(bpp)]
        lhs.append((jnp.concatenate(parts, axis=0)
                    + jnp.concatenate([pe_ref[q]] * (nrow // SUBLANE), axis=0)).astype(BF16))
    acc = _dot(jnp.concatenate(lhs, axis=1), w1_ref[...])
    rid = lax.broadcasted_iota(I32, (nrow, 1), 0) & (SUBLANE - 1)
    want = ((rid & (KV_COMP - 1)) >> 1) * 2 + (rid >> 2)
    picked = acc[:, 0:HEAD_DIM]
    for blk in range(1, KV_COMP):
        picked = jnp.where(want == blk, acc[:, blk * HEAD_DIM:(blk + 1) * HEAD_DIM], picked)
    acc_s[...] = picked + pltpu.roll(picked, nrow - KV_COMP, 0)
    for c in range(KV_COMP):
        s = c // NSA_KV_HEADS
        hid_e = acc_s[pl.ds(c, nh, stride=2 * SUBLANE), :]
        hid_o = acc_s[pl.ds(SUBLANE + c, nh, stride=2 * SUBLANE), :]
        hid = jax.nn.gelu(jnp.concatenate([hid_e, hid_o], axis=0) + b1_ref[s])
        out = _mm1(hid, w2_ref[s])
        o_ref[0, c, 0] = out[0:nh]
        o_ref[0, c, 1] = out[nh:2 * nh]


def _compress_sample(cache, page_table, pe, w1, b1, w2):
    nbs, n_pages = page_table.shape
    n_phys = cache.shape[0]
    cache_v = cache.reshape(n_phys, PAGE_VROWS, HEAD_DIM)
    ngrp = n_pages // CMP_PAGES
    nblk_half = n_pages * PAGE_SIZE // CMP_BLOCK // 2
    nh = CMP_PAGES * (PAGE_SIZE // CMP_BLOCK) // 2
    pe_q = jnp.transpose(pe.reshape(2, CMP_PAIRS, 2, HEAD_DIM), (1, 2, 0, 3))
    pe_q = jnp.broadcast_to(pe_q[:, :, :, None, :], (CMP_PAIRS, 2, 2, NSA_KV_HEADS, HEAD_DIM))
    pe_q = pe_q.reshape(CMP_PAIRS, SUBLANE, HEAD_DIM)
    w1_q = jnp.transpose(w1.reshape(2, CMP_PAIRS, 2, HEAD_DIM, HEAD_DIM), (1, 3, 0, 2, 4))
    w1_q = w1_q.reshape(CMP_PAIRS * HEAD_DIM, KV_COMP * HEAD_DIM).astype(BF16)
    page_spec = lambda k: pl.BlockSpec((1, PAGE_VROWS, HEAD_DIM),
                                       lambda b, g, pt, k=k: (pt[b * n_pages + g * CMP_PAGES + k], 0, 0))
    const = lambda shape: pl.BlockSpec(shape, lambda b, g, pt: (0,) * len(shape))
    grid_spec = pltpu.PrefetchScalarGridSpec(
        num_scalar_prefetch=1,
        grid=(nbs, ngrp),
        in_specs=[page_spec(k) for k in range(CMP_PAGES)]
        + [const((CMP_PAIRS, SUBLANE, HEAD_DIM)), const((CMP_PAIRS * HEAD_DIM, KV_COMP * HEAD_DIM)),
           const((2, 1, HEAD_DIM)), const((2, HEAD_DIM, HEAD_DIM))],
        out_specs=pl.BlockSpec((1, KV_COMP, 2, nh, HEAD_DIM), lambda b, g, pt: (b, 0, 0, g, 0)),
        scratch_shapes=[pltpu.VMEM((CMP_PAGES * (PAGE_SIZE // CMP_BLOCK) * SUBLANE, HEAD_DIM), F32)])
    return pl.pallas_call(
        _compress_sample_body,
        out_shape=jax.ShapeDtypeStruct((nbs, KV_COMP, 2, nblk_half, HEAD_DIM), F32),
        grid_spec=grid_spec,
        compiler_params=_cparams(("arbitrary", "arbitrary")),
        name="nsa_compress_sample",
    )(page_table.reshape(-1), *([cache_v] * CMP_PAGES), pe_q, w1_q, b1.reshape(2, 1, HEAD_DIM), w2)


def _rows_from_lanes(row, ngrp):
    ridx = lax.broadcasted_iota(I32, (SUBLANE, HEAD_DIM), 0)
    out = jnp.zeros((SUBLANE, HEAD_DIM), F32)
    for g in range(ngrp):
        out = jnp.where(ridx == g, jnp.broadcast_to(row[:, g * HEAD_DIM:(g + 1) * HEAD_DIM], (SUBLANE, HEAD_DIM)), out)
    return out


def _slope_rows(slopes_ref, h):
    ridx = lax.broadcasted_iota(I32, (SUBLANE, 1), 0)
    slope = jnp.zeros((SUBLANE, 1), F32)
    for g in range(NSA_GROUP):
        slope = jnp.where(ridx == g, slopes_ref[h * NSA_GROUP + g], slope)
    return slope


def _nsa_sample_select_body(slopes_ref, q_ref, ckv_ref, ocmp_ref, idx_ref, *, past):
    ncb = past // CMP_BLOCK // 2
    nblk = past // SEL_BLOCK
    ridx = lax.broadcasted_iota(I32, (SUBLANE, 1), 0)
    midx = lax.broadcasted_iota(I32, (1, ncb), 1)
    jb = lax.broadcasted_iota(I32, (1, nblk), 1)
    slot = lax.broadcasted_iota(I32, (1, SEL_TOPN), 1)
    for h in range(NSA_KV_HEADS):
        q8 = _rows_from_lanes(q_ref[0:1, h * NSA_GROUP * HEAD_DIM:(h + 1) * NSA_GROUP * HEAD_DIM], NSA_GROUP)
        q8 = q8 * (HEAD_DIM ** -0.5)
        slope = _slope_rows(slopes_ref, h)
        lg = []
        for par in range(2):
            s_c = _mm3(q8, ckv_ref[0, h, par], NT)
            cend = midx * (2 * CMP_BLOCK) + (CMP_BLOCK - 1) + par * CMP_BLOCK
            lg.append(s_c - slope * (past - cend).astype(F32))
        mx = jnp.maximum(jnp.max(lg[0], axis=1, keepdims=True), jnp.max(lg[1], axis=1, keepdims=True))
        e0 = jnp.exp(lg[0] - mx)
        e1 = jnp.exp(lg[1] - mx)
        den = jnp.sum(e0, axis=1, keepdims=True) + jnp.sum(e1, axis=1, keepdims=True)
        p0 = e0 / den
        p1 = e1 / den
        o_cmp = _mm1(p0, ckv_ref[0, 2 + h, 0]) + _mm1(p1, ckv_ref[0, 2 + h, 1])
        ocmp_ref[0, h * NSA_GROUP:(h + 1) * NSA_GROUP, :] = o_cmp[0:NSA_GROUP]
        imp = jnp.sum(jnp.where(ridx < NSA_GROUP, p0 + p1, 0.0), axis=0, keepdims=True)
        forced = (jb == 0) | (jb == nblk - 1)
        score = jnp.where(forced, FORCE_SCORE, imp)
        picks = jnp.full((1, SEL_TOPN), nblk, I32)
        for k in range(SEL_TOPN - 1):
            m = jnp.max(score, axis=1, keepdims=True)
            first = jnp.min(jnp.where(score == m, jb, nblk), axis=1, keepdims=True)
            score = jnp.where(jb == first, NEG_BIG, score)
            picks = jnp.where(slot == k, first, picks)
        idx_ref[0, h:h + 1, :] = picks


def _nsa_sample_select(p2s, ckv_s, slopes, nbs, past):
    nblk_half = past // CMP_BLOCK // 2
    return pl.pallas_call(
        functools.partial(_nsa_sample_select_body, past=past),
        out_shape=(jax.ShapeDtypeStruct((nbs, NSA_HEADS, HEAD_DIM), F32),
                   jax.ShapeDtypeStruct((nbs, NSA_KV_HEADS, SEL_TOPN), I32)),
        grid=(nbs,),
        in_specs=[pl.BlockSpec(memory_space=pltpu.SMEM),
                  pl.BlockSpec((SUBLANE, NSA_WIDTH), lambda b: (b * (SROWS // SUBLANE), 0)),
                  pl.BlockSpec((1, 4, 2, nblk_half, HEAD_DIM), lambda b: (b, 0, 0, 0, 0))],
        out_specs=(pl.BlockSpec((1, NSA_HEADS, HEAD_DIM), lambda b: (b, 0, 0)),
                   pl.BlockSpec((1, NSA_KV_HEADS, SEL_TOPN), lambda b: (b, 0, 0))),
        compiler_params=_cparams(("arbitrary",)),
        name="nsa_sample_select",
    )(slopes, p2s, ckv_s)


NSEL_PAST = SEL_TOPN - 1


def _nsa_sample_attend_body(idx_ref, pt_ref, slopes_ref, q_ref, gt_ref, ocmp_ref, skn_ref, svn_ref, wkn_ref, wvn_ref,
                            *refs, past, n_pages):
    blocks = refs[:NSEL_PAST]
    wc_ref, o_ref = refs[NSEL_PAST:]
    b = pl.program_id(0)
    h = pl.program_id(1)
    q8 = _rows_from_lanes(q_ref[0:1, :], NSA_GROUP) * (HEAD_DIM ** -0.5)
    q8b = q8.astype(BF16)
    slope = _slope_rows(slopes_ref, h)

    def attend_with_new(k_old, v_old, lg_bias, k_new, v_new):
        lg = _dot(q8b, k_old.astype(BF16), NT) + lg_bias
        s_new = jnp.sum(q8 * k_new, axis=1, keepdims=True)
        m = jnp.maximum(jnp.max(lg, axis=1, keepdims=True), s_new)
        p = jnp.exp(lg - m)
        pn = jnp.exp(s_new - m)
        den = jnp.sum(p, axis=1, keepdims=True) + pn
        return (_dot(p.astype(BF16), v_old.astype(BF16)) + pn * v_new) / den

    k_sel = jnp.concatenate([r[0, pl.ds(h, SEL_BLOCK, stride=TOKEN_ROWS), :] for r in blocks], axis=0)
    v_sel = jnp.concatenate([r[0, pl.ds(NSA_KV_HEADS + h, SEL_BLOCK, stride=TOKEN_ROWS), :] for r in blocks], axis=0)
    lane = lax.broadcasted_iota(I32, (1, NSEL_PAST * SEL_BLOCK), 1)
    slot = lane >> (SEL_BLOCK.bit_length() - 1)
    blk = jnp.zeros((1, NSEL_PAST * SEL_BLOCK), I32)
    for j in range(NSEL_PAST):
        blk = jnp.where(slot == j, idx_ref[(b * NSA_KV_HEADS + h) * SEL_TOPN + j], blk)
    dist = (past - (blk * SEL_BLOCK + (lane & (SEL_BLOCK - 1)))).astype(F32)
    o_sel = attend_with_new(k_sel, v_sel, -slope * dist, skn_ref[0:1, :], svn_ref[0:1, :])

    nw = wc_ref.shape[1] // TOKEN_ROWS
    r = lax.broadcasted_iota(I32, (1, nw), 1)
    delta = nw - r
    wmask = (delta < WINDOW) & (past - delta >= 0)
    k_win = wc_ref[0, pl.ds(h, nw, stride=TOKEN_ROWS), :]
    v_win = wc_ref[0, pl.ds(NSA_KV_HEADS + h, nw, stride=TOKEN_ROWS), :]
    o_win = attend_with_new(k_win, v_win, jnp.where(wmask, -slope * delta.astype(F32), NEG_INF),
                            wkn_ref[0:1, :], wvn_ref[0:1, :])

    gs = jax.nn.sigmoid(gt_ref[0:1, :])
    o_ref[...] = jnp.zeros_like(o_ref)
    for g in range(NSA_GROUP):
        def gate(c):
            a = gs[:, g * 3 + c:g * 3 + c + 1]
            bb = gs[:, NSA_GROUP * 3 + g * 3 + c:NSA_GROUP * 3 + g * 3 + c + 1]
            return jnp.where(h == 0, a, bb)

        oc = jnp.where(h == 0, ocmp_ref[0, g:g + 1, :], ocmp_ref[0, NSA_GROUP + g:NSA_GROUP + g + 1, :])
        og = (gate(0) * oc + gate(1) * o_sel[g:g + 1]) + gate(2) * o_win[g:g + 1]
        o_ref[0:1, g * HEAD_DIM:(g + 1) * HEAD_DIM] = og


def _nsa_sample_attend(p2s, ocmp, idx, page_table, cache_sel, win_state, slopes, nbs, past):
    n_pages = page_table.shape[1]
    n_phys = cache_sel.shape[0]
    nw = win_state.shape[1]
    halves = PAGE_SIZE // SEL_BLOCK
    sel_v = cache_sel.reshape(n_phys * halves, SEL_BLOCK * TOKEN_ROWS, HEAD_DIM)
    win_v = win_state.reshape(nbs, nw * TOKEN_ROWS, HEAD_DIM)
    rb = SROWS // SUBLANE
    kvs0 = COL_KVS // LANE
    kvw0 = COL_KVW // LANE
    misc = COL_MISC // LANE

    def sel_spec(j):
        def index(b, h, idx, pt):
            blk = idx[(b * NSA_KV_HEADS + h) * SEL_TOPN + j]
            page = pt[b * n_pages + blk // halves]
            return (page * halves + blk % halves, 0, 0)
        return pl.BlockSpec((1, SEL_BLOCK * TOKEN_ROWS, HEAD_DIM), index)

    new = lambda c0: pl.BlockSpec((SUBLANE, LANE), lambda b, h, idx, pt, c0=c0: (b * rb, c0 + h))
    grid_spec = pltpu.PrefetchScalarGridSpec(
        num_scalar_prefetch=2,
        grid=(nbs, NSA_KV_HEADS),
        in_specs=[pl.BlockSpec(memory_space=pltpu.SMEM),
                  pl.BlockSpec((SUBLANE, NSA_GROUP * HEAD_DIM), lambda b, h, idx, pt: (b * rb, h)),
                  pl.BlockSpec((SUBLANE, LANE), lambda b, h, idx, pt: (b * rb, misc)),
                  pl.BlockSpec((1, NSA_HEADS, HEAD_DIM), lambda b, h, idx, pt: (b, 0, 0)),
                  new(kvs0), new(kvs0 + 2), new(kvw0), new(kvw0 + 2)]
        + [sel_spec(j) for j in range(NSEL_PAST)]
        + [pl.BlockSpec((1, nw * TOKEN_ROWS, HEAD_DIM), lambda b, h, idx, pt: (b, 0, 0))],
        out_specs=pl.BlockSpec((SROWS, NSA_GROUP * HEAD_DIM), lambda b, h, idx, pt: (b, h)))
    return pl.pallas_call(
        functools.partial(_nsa_sample_attend_body, past=past, n_pages=n_pages),
        out_shape=jax.ShapeDtypeStruct((nbs * SROWS, NSA_WIDTH), F32),
        grid_spec=grid_spec,
        compiler_params=_cparams(("arbitrary", "arbitrary")),
        name="nsa_sample_attend",
    )(idx.reshape(-1), page_table.reshape(-1), slopes, p2s, p2s, ocmp, p2s, p2s, p2s, p2s,
      *([sel_v] * NSEL_PAST), win_v)


def _transpose_cast_body(x_ref, o_ref):
    o_ref[...] = x_ref[...].T.astype(BF16)


def _transpose_to_bf16(x):
    rows, cols = x.shape
    tr = PEER_TE
    return pl.pallas_call(
        _transpose_cast_body,
        out_shape=jax.ShapeDtypeStruct((cols, rows), BF16),
        grid=(rows // tr,),
        in_specs=[pl.BlockSpec((tr, cols), lambda i: (i, 0))],
        out_specs=pl.BlockSpec((cols, tr), lambda i: (0, i)),
        compiler_params=_cparams(("arbitrary",)),
        name="transpose_cast",
    )(x)


def _split_w_in(w_in):
    d = w_in.shape[0]
    w_bf = w_in.astype(BF16)
    g0 = NSA_WIDTH + 3 * 2 * KV_WIDTH
    q0 = g0 + 3 * NSA_HEADS
    b0 = q0 + 3 * GDN_WIDTH + GDN_WIDTH
    n_misc = (q0 - g0) + 2 * GDN_HEADS
    w_misc = jnp.concatenate([w_bf[:, g0:q0], w_bf[:, b0:], jnp.zeros((d, N_PROJ - COL_MISC - n_misc), BF16)], axis=1)
    return w_bf, w_bf[:, q0:b0], w_misc


def _split_bf16(w):
    hi = w.astype(BF16)
    lo = (w - hi.astype(F32)).astype(BF16)
    return hi, lo


def _mixer_tail(x, o_nsa, o_gdn, w_out_bf, g1, norm_ffn, sc2, sh2, g2, wq_hi, wq_lo, keys, u_bf, vt_bf,
                final_norm, seq_len, tm, tn_sel, tn_mm, tm_final):
    x1, h2, h2b = _out_proj(o_nsa, o_gdn, x, w_out_bf, g1, norm_ffn, sc2, sh2, seq_len, tm)
    cnt, al, code, be = _peer_select(h2, wq_hi, wq_lo, keys, tn_sel, split_query=h2.shape[0] <= tn_sel)
    peer_t = _peer_mm(h2b, u_bf, vt_bf, cnt, al, code, be, tn_mm)
    return _final(x1, peer_t, g2, final_norm, seq_len, tm_final)


def kernel(x_prompt, x_sample, cache_cmp_kv, cache_sel_kv, state_win_kv, state_conv, state_gdn, page_table,
           c_prompt, c_sample, w_ada, b_ada, norm_mix, norm_ffn, w_in, cmp_pe, cmp_w1, cmp_b1, cmp_w2, conv_w,
           gdn_a_log, gdn_dt_bias, gdn_norm, w_out, peer_wq, peer_keys, peer_u, peer_v, final_norm):
    nb, seq, d = x_prompt.shape
    nbs, dec_seq, _ = x_sample.shape
    assert w_in.shape[0] == 1 and dec_seq == 1, "single layer, single decode token"
    past = page_table.shape[1] * PAGE_SIZE
    slopes = 2.0 ** (-8.0 * jnp.arange(1, NSA_HEADS + 1, dtype=F32) / NSA_HEADS)
    rows_c = 16
    c_all = jnp.concatenate([c_prompt, c_sample, jnp.zeros((rows_c - nb - nbs, d), F32)], axis=0)
    mod = _adaln(c_all, w_ada[0], b_ada[0]).reshape(rows_c, 6, d)
    pm = lambda k: mod[0:nb, k][:, None, :]
    sm = lambda k: mod[nb:nb + nbs, k][:, None, :]
    w_proj = _split_w_in(w_in[0])
    w_out_bf = w_out[0].astype(BF16)
    wq_hi, wq_lo = _split_bf16(peer_wq[0].T)
    u_bf = peer_u[0].astype(BF16)
    vt_bf = _transpose_to_bf16(peer_v[0])
    cw = (cmp_pe[0], cmp_w1[0], cmp_b1[0], cmp_w2[0])
    kvw = 2 * KV_WIDTH

    xp = x_prompt.reshape(nb * seq, d)
    p2, cmp_rows, sel_rows = _proj_in(xp, norm_mix[0], pm(1), pm(0), *w_proj, seq, 1024)
    ckv = _compress_prompt(p2, *cw, nb, seq)
    o_nsa = _nsa_prompt(p2, ckv, slopes, nb, seq)
    conv0 = jnp.zeros((nb, SUBLANE, 3 * GDN_WIDTH), F32)
    s0 = jnp.zeros((nb, GDN_HEADS, HEAD_DIM, HEAD_DIM), F32)
    o_gdn, gdn_p = _gdn(p2, conv_w[0], conv0, s0, gdn_a_log[0], gdn_dt_bias[0], gdn_norm[0], nb, seq, seq)
    y_prompt = _mixer_tail(xp, o_nsa, o_gdn, w_out_bf, pm(2), norm_ffn[0], pm(4), pm(3), pm(5), wq_hi, wq_lo,
                           peer_keys[0], u_bf, vt_bf, final_norm, seq, 512, 256, 512, 512).reshape(nb, seq, d)
    p3 = p2.reshape(nb, seq, N_PROJ)
    keep = min(WINDOW, seq)
    kv5 = lambda a: a.reshape(a.shape[0], a.shape[1], 2, NSA_KV_HEADS, HEAD_DIM)
    cmp_p = cmp_rows.reshape(nb, seq, 2, NSA_KV_HEADS, HEAD_DIM)
    sel_p = sel_rows.reshape(nb, seq, 2, NSA_KV_HEADS, HEAD_DIM)
    win_p = kv5(p3[:, seq - keep:, COL_KVW:COL_KVW + kvw])
    conv_p = p3[:, seq - (CONV_WIDTH - 1):, COL_QKV:COL_QKV + 3 * GDN_WIDTH]

    xs = jnp.pad(x_sample, ((0, 0), (0, SROWS - dec_seq), (0, 0))).reshape(nbs * SROWS, d)
    p2s, cmp_rows_s, sel_rows_s = _proj_in(xs, norm_mix[0], sm(1), sm(0), *w_proj, SROWS, nbs * SROWS)
    ckv_s = _compress_sample(cache_cmp_kv[0], page_table, *cw)
    ocmp, idx = _nsa_sample_select(p2s, ckv_s, slopes, nbs, past)
    o_nsa_s = _nsa_sample_attend(p2s, ocmp, idx, page_table, cache_sel_kv[0], state_win_kv[0], slopes, nbs, past)
    conv_buf = jnp.pad(state_conv[0], ((0, 0), (SUBLANE - (CONV_WIDTH - 1), 0), (0, 0)))
    o_gdn_s, gdn_s = _gdn(p2s, conv_w[0], conv_buf, state_gdn[0], gdn_a_log[0], gdn_dt_bias[0], gdn_norm[0],
                          nbs, SROWS, dec_seq)
    head_rows = lambda a: a.reshape(nbs, SROWS, a.shape[-1])[:, :SROWS_TAIL].reshape(nbs * SROWS_TAIL, a.shape[-1])
    nt = nbs * SROWS_TAIL
    ys = _mixer_tail(head_rows(xs), head_rows(o_nsa_s), head_rows(o_gdn_s), w_out_bf, sm(2), norm_ffn[0], sm(4),
                     sm(3), sm(5), wq_hi, wq_lo, peer_keys[0], u_bf, vt_bf, final_norm, SROWS_TAIL, nt, nt, nt, nt)
    y_sample = ys.reshape(nbs, SROWS_TAIL, d)[:, 0:dec_seq]
    p3s = p2s.reshape(nbs, SROWS, N_PROJ)[:, 0:dec_seq]
    cmp_s = cmp_rows_s.reshape(nbs, SROWS, 2, NSA_KV_HEADS, HEAD_DIM)[:, 0:dec_seq]
    sel_s = sel_rows_s.reshape(nbs, SROWS, 2, NSA_KV_HEADS, HEAD_DIM)[:, 0:dec_seq]
    win_new = kv5(p3s[:, :, COL_KVW:COL_KVW + kvw])
    win_s = jnp.concatenate([state_win_kv[0], win_new], axis=1)[:, dec_seq:]
    conv_s = jnp.concatenate([state_conv[0], p3s[:, :, COL_QKV:COL_QKV + 3 * GDN_WIDTH]], axis=1)[:, dec_seq:]

    return (y_prompt, y_sample, cmp_p[None], cmp_s[None], sel_p[None], sel_s[None], win_p[None], win_s[None],
            conv_p[None], conv_s[None], gdn_p[None], gdn_s[None])
```

```python
import functools
import math

import jax
import jax.numpy as jnp
from jax import lax
from jax.experimental import pallas as pl
from jax.experimental.pallas import tpu as pltpu

F32 = jnp.float32
BF16 = jnp.bfloat16
I32 = jnp.int32

HEAD_DIM = 128
NSA_WIDTH = 1024
GDN_WIDTH = 1024
NSA_HEADS = 8
NSA_KV_HEADS = 2
NSA_GROUP = 4
KV_WIDTH = 256
CMP_BLOCK = 32
SEL_BLOCK = 64
SEL_TOPN = 16
WINDOW = 512
GDN_HEADS = 8
CONV_WIDTH = 4
GDN_CHUNK = 64
PEER_HEADS = 8
PEER_NKEYS = 128
PEER_QDIM = 256
PEER_TOPK = 16
PAGE_SIZE = 128
NORM_EPS = 1e-6
NEG_INF = -1e30
FORCE_SCORE = 1e4
NEG_BIG = -3.0e38

LANE = 128
SUBLANE = 8
VMEM_LIMIT = 56 * 1024 * 1024

COL_KVC = 1024
COL_KVS = 1536
COL_KVW = 2048
COL_QKV = 2560
COL_Z = 5632
COL_MISC = 6656
N_PROJ = 7168
MISC_BL = 24
MISC_AL = 32

NN = (((1,), (0,)), ((), ()))
NT = (((1,), (1,)), ((), ()))
BNN = (((2,), (1,)), ((0,), (0,)))
BNT = (((2,), (2,)), ((0,), (0,)))


def _dot(a, b, dims=NN):
    return lax.dot_general(a, b, dims, preferred_element_type=F32)


def _split(x):
    hi = x.astype(BF16)
    lo = (x - hi.astype(F32)).astype(BF16)
    return hi, lo


def _mm1(a, b, dims=NN):
    return _dot(a.astype(BF16), b.astype(BF16), dims)


def _mm3(a, b, dims=NN):
    ah, al = _split(a)
    bh, bl = _split(b)
    return _dot(ah, bh, dims) + (_dot(al, bh, dims) + _dot(ah, bl, dims))


def _cparams(sem, vmem=VMEM_LIMIT):
    return pltpu.CompilerParams(dimension_semantics=sem, vmem_limit_bytes=vmem)


def _adaln_body(c_ref, w_ref, b_ref, o_ref):
    a = jax.nn.silu(c_ref[...])
    o_ref[...] = _mm3(a, w_ref[...]) + b_ref[...]


def _adaln(c_all, w, b):
    rows, d = c_all.shape
    n = w.shape[1]
    tn = 1024
    return pl.pallas_call(
        _adaln_body,
        out_shape=jax.ShapeDtypeStruct((rows, n), F32),
        grid=(n // tn,),
        in_specs=[pl.BlockSpec((rows, d), lambda j: (0, 0)),
                  pl.BlockSpec((d, tn), lambda j: (0, j)),
                  pl.BlockSpec((1, tn), lambda j: (0, j))],
        out_specs=pl.BlockSpec((rows, tn), lambda j: (0, j)),
        compiler_params=_cparams(("arbitrary",)),
        name="adaln",
    )(c_all, w, b.reshape(1, n))


def _seq_rows(m_ref, seq_len):
    nseq = m_ref.shape[0]
    if nseq == 1:
        return m_ref[0]
    return jnp.concatenate([jnp.broadcast_to(m_ref[r], (seq_len, m_ref.shape[2])) for r in range(nseq)], axis=0)


def _seq_spec(seq_len, tm, d):
    if tm <= seq_len:
        tpb = seq_len // tm
        return pl.BlockSpec((1, 1, d), lambda i, *_: (i // tpb, 0, 0))
    return pl.BlockSpec((tm // seq_len, 1, d), lambda i, *_: (i, 0, 0))


PROJ_TN = 512
PROJ_TILES_A = COL_QKV // PROJ_TN
PROJ_TILES_B = (COL_MISC - COL_QKV) // PROJ_TN


def _proj_in_body(x_ref, nw_ref, sc_ref, sh_ref, wa_ref, wb_ref, wc_ref, o_ref, cmp_ref, sel_ref, h_ref, *,
                  seq_len):
    j = pl.program_id(1)
    tm = x_ref.shape[0]

    @pl.when(j == 0)
    def _():
        x = x_ref[...]
        y = x * lax.rsqrt(jnp.mean(x * x, axis=-1, keepdims=True) + NORM_EPS)
        h = (y * nw_ref[...]) * (1.0 + _seq_rows(sc_ref, seq_len)) + _seq_rows(sh_ref, seq_len)
        h_ref[...] = h.astype(BF16)

    @pl.when(j < PROJ_TILES_A)
    def _():
        o_ref[...] = _dot(h_ref[...], wa_ref[...])

    @pl.when((j >= PROJ_TILES_A) & (j < PROJ_TILES_A + PROJ_TILES_B))
    def _():
        o_ref[...] = _dot(h_ref[...], wb_ref[...])

    @pl.when(j >= PROJ_TILES_A + PROJ_TILES_B)
    def _():
        o_ref[...] = _dot(h_ref[...], wc_ref[...])

    for dst, col in ((cmp_ref, COL_KVC), (sel_ref, COL_KVS)):
        @pl.when(j == col // PROJ_TN)
        def _(dst=dst, col=col):
            for c in range(TOKEN_ROWS):
                lo = col % PROJ_TN + c * HEAD_DIM
                dst[pl.ds(c, tm, stride=TOKEN_ROWS), :] = o_ref[:, lo:lo + HEAD_DIM]


def _proj_in(x, norm_w, sc, sh, w_all, w_gdn, w_misc, seq_len, tm):
    n, d = x.shape
    ncol = N_PROJ
    tn = PROJ_TN
    for col in (COL_KVC, COL_KVS):
        assert col % tn + 2 * KV_WIDTH <= tn, "a cached column group must not straddle column tiles"
    assert w_gdn.shape[1] == PROJ_TILES_B * tn and w_misc.shape[1] == tn
    clamp = lambda j, lo, hi: jnp.minimum(jnp.maximum(j - lo, 0), hi)
    mod = _seq_spec(seq_len, tm, d)
    rows = jax.ShapeDtypeStruct((n * TOKEN_ROWS, HEAD_DIM), F32)
    rows_spec = pl.BlockSpec((tm * TOKEN_ROWS, HEAD_DIM), lambda i, j: (i, 0))
    return pl.pallas_call(
        functools.partial(_proj_in_body, seq_len=seq_len),
        out_shape=(jax.ShapeDtypeStruct((n, ncol), F32), rows, rows),
        grid=(n // tm, ncol // tn),
        in_specs=[pl.BlockSpec((tm, d), lambda i, j: (i, 0)),
                  pl.BlockSpec((1, d), lambda i, j: (0, 0)),
                  mod, mod,
                  pl.BlockSpec((d, tn), lambda i, j: (0, clamp(j, 0, PROJ_TILES_A - 1))),
                  pl.BlockSpec((d, tn), lambda i, j: (0, clamp(j, PROJ_TILES_A, PROJ_TILES_B - 1))),
                  pl.BlockSpec((d, tn), lambda i, j: (0, 0))],
        out_specs=(pl.BlockSpec((tm, tn), lambda i, j: (i, j)), rows_spec, rows_spec),
        scratch_shapes=[pltpu.VMEM((tm, d), BF16)],
        compiler_params=_cparams(("arbitrary", "arbitrary")),
        name="proj_in",
    )(x, norm_w.reshape(1, d), sc, sh, w_all, w_gdn, w_misc)


def _compress_prompt_body(x_ref, pe_ref, w1_ref, b1_ref, w2_ref, o_ref, *, nblk):
    half = nblk // 2
    xs = []
    for l in range(CMP_BLOCK):
        xe = x_ref[pl.ds(l, half, stride=2 * CMP_BLOCK), :]
        xo = x_ref[pl.ds(CMP_BLOCK + l, half, stride=2 * CMP_BLOCK), :]
        xs.append((jnp.concatenate([xe, xo], axis=0) + pe_ref[0, l:l + 1, :]).astype(BF16))
    acc = _dot(jnp.concatenate(xs, axis=1), w1_ref[0].astype(BF16))
    hid = jax.nn.gelu(acc + b1_ref[0])
    o_ref[0, 0] = _mm1(hid, w2_ref[0])


def _compress_prompt(p2, pe, w1, b1, w2, nbatch, seq_len):
    nblk = seq_len // CMP_BLOCK
    kv0 = COL_KVC // LANE
    return pl.pallas_call(
        functools.partial(_compress_prompt_body, nblk=nblk),
        out_shape=jax.ShapeDtypeStruct((nbatch, 4, nblk, HEAD_DIM), F32),
        grid=(nbatch, 4),
        in_specs=[pl.BlockSpec((seq_len, LANE), lambda b, sh: (b, kv0 + sh)),
                  pl.BlockSpec((1, CMP_BLOCK, HEAD_DIM), lambda b, sh: (sh // 2, 0, 0)),
                  pl.BlockSpec((1, CMP_BLOCK * HEAD_DIM, HEAD_DIM), lambda b, sh: (sh // 2, 0, 0)),
                  pl.BlockSpec((1, 1, HEAD_DIM), lambda b, sh: (sh // 2, 0, 0)),
                  pl.BlockSpec((1, HEAD_DIM, HEAD_DIM), lambda b, sh: (sh // 2, 0, 0))],
        out_specs=pl.BlockSpec((1, 1, nblk, HEAD_DIM), lambda b, sh: (b, sh, 0, 0)),
        compiler_params=_cparams(("arbitrary", "arbitrary")),
        name="nsa_compress_prompt",
    )(p2, pe, w1.reshape(2, CMP_BLOCK * HEAD_DIM, HEAD_DIM), b1.reshape(2, 1, HEAD_DIM), w2)


TQ = 128
NSA_KSPAN = 512


def _nsa_prompt_body(slopes_ref, q_ref, gt_ref, ck_ref, cv_ref, sk_ref, sv_ref, wk_ref, wv_ref,
                     o_ref, selm_ref, *, seq_len):
    h = pl.program_id(1)
    qi = pl.program_id(2)
    t0 = qi * TQ
    nsb = seq_len // SEL_BLOCK
    ncb = seq_len // CMP_BLOCK // 2
    rows = NSA_GROUP * TQ

    qb = q_ref[...] * (HEAD_DIM ** -0.5)
    q4 = jnp.concatenate([qb[:, g * HEAD_DIM:(g + 1) * HEAD_DIM] for g in range(NSA_GROUP)], axis=0)
    q4b = q4.astype(BF16)
    row = lax.broadcasted_iota(I32, (rows, 1), 0)
    tq = t0 + (row & (TQ - 1))
    grow = row >> 7
    slope = jnp.zeros((rows, 1), F32)
    for g in range(NSA_GROUP):
        slope = jnp.where(grow == g, slopes_ref[h * NSA_GROUP + g], slope)

    ck = ck_ref[0, 0]
    cv = cv_ref[0, 0]
    lane_r = lax.broadcasted_iota(I32, (1, rows), 1)
    tq_l = t0 + (lane_r & (TQ - 1))
    slope_l = jnp.zeros((1, rows), F32)
    for g in range(NSA_GROUP):
        slope_l = jnp.where((lane_r >> 7) == g, slopes_ref[h * NSA_GROUP + g], slope_l)
    midx = lax.broadcasted_iota(I32, (ncb, 1), 0)
    logits = []
    valids = []
    for par in range(2):
        s_t = _mm3(ck[par * ncb:(par + 1) * ncb], q4, NT)
        cend = midx * (2 * CMP_BLOCK) + (CMP_BLOCK - 1) + par * CMP_BLOCK
        valid = cend <= tq_l
        dist = (tq_l - cend).astype(F32)
        logits.append(jnp.where(valid, s_t - slope_l * dist, NEG_INF))
        valids.append(valid)
    mx = jnp.maximum(jnp.max(logits[0], axis=0, keepdims=True), jnp.max(logits[1], axis=0, keepdims=True))
    e0 = jnp.exp(logits[0] - mx)
    e1 = jnp.exp(logits[1] - mx)
    den = jnp.sum(e0, axis=0, keepdims=True) + jnp.sum(e1, axis=0, keepdims=True)
    p0 = jnp.where(valids[0], e0 / den, 0.0)
    p1 = jnp.where(valids[1], e1 / den, 0.0)
    ident = (lax.broadcasted_iota(I32, (TQ, TQ), 0) == lax.broadcasted_iota(I32, (TQ, TQ), 1)).astype(BF16)
    p0b = p0.astype(BF16)
    p1b = p1.astype(BF16)
    o_cmp = jnp.concatenate(
        [_dot(_dot(ident, p0b[:, g * TQ:(g + 1) * TQ], NT).astype(BF16), cv[0:ncb].astype(BF16))
         + _dot(_dot(ident, p1b[:, g * TQ:(g + 1) * TQ], NT).astype(BF16), cv[ncb:2 * ncb].astype(BF16))
         for g in range(NSA_GROUP)], axis=0)
    pb = p0 + p1
    imp = pb[:, 0:TQ]
    for g in range(1, NSA_GROUP):
        imp = imp + pb[:, g * TQ:(g + 1) * TQ]

    cur = (t0 + lax.broadcasted_iota(I32, (1, TQ), 1)) >> 6
    jb = lax.broadcasted_iota(I32, (nsb, 1), 0)
    forced = (jb == 0) | (jb == cur) | (jb == cur - 1)
    score = jnp.where(jb <= cur, jnp.where(forced, FORCE_SCORE, imp), NEG_INF)
    cnt = jnp.zeros((nsb, TQ), I32)
    for i in range(nsb):
        si = score[i:i + 1, :]
        beats = (si > score) | ((si == score) & (jb > i))
        cnt = cnt + beats.astype(I32)
    sel_t = ((cnt < min(SEL_TOPN, nsb)) & (score > 0.5 * NEG_INF)).astype(BF16)
    sel = _dot(ident, sel_t, NT).astype(BF16)
    kpos = lax.broadcasted_iota(I32, (nsb, seq_len), 1)
    kblk = lax.broadcasted_iota(I32, (nsb, seq_len), 0)
    expand = ((kpos >> 6) == kblk).astype(BF16)
    selk = _dot(sel, expand)

    kw = NSA_KSPAN
    t_row = t0 + lax.broadcasted_iota(I32, (TQ, kw), 0)
    key = lax.broadcasted_iota(I32, (TQ, kw), 1)
    for st in range(seq_len // kw):
        blk = (selk[:, st * kw:(st + 1) * kw] - 1.0) * (-NEG_INF)
        selm_ref[st] = blk + jnp.where(st * kw + key <= t_row, 0.0, NEG_INF)
    m_floor = 0.1 * NEG_INF

    def span_update(carry, k, v, bias, key0):
        m, l, acc = carry
        width = k.shape[0]
        alibi = slope * lax.broadcasted_iota(I32, (1, width), 1).astype(F32)
        x = _dot(q4b, k, NT) + alibi
        x = (x.reshape(NSA_GROUP, TQ, width) + bias[None]).reshape(rows, width)
        shift = slope * (key0 - tq).astype(F32)
        m_new = jnp.maximum(m, jnp.max(x, axis=1, keepdims=True) + shift)
        alpha = jnp.exp(m - m_new)
        p = jnp.exp(x - (m_new - shift))
        l = alpha * l + jnp.sum(p, axis=1, keepdims=True)
        acc = alpha * acc + _dot(p.astype(BF16), v)
        return m_new, l, acc

    def finish(carry):
        _, l, acc = carry
        return jnp.where(l > 0.0, acc / jnp.where(l > 0.0, l, 1.0), 0.0)

    init = (jnp.full((rows, 1), m_floor, F32), jnp.zeros((rows, 1), F32), jnp.zeros((rows, HEAD_DIM), F32))

    def sel_step(st, carry):
        start = pl.multiple_of(st * kw, kw)
        return span_update(carry, sk_ref[pl.ds(start, kw), :].astype(BF16), sv_ref[pl.ds(start, kw), :].astype(BF16),
                           selm_ref[st], start)

    o_sel = finish(lax.fori_loop(0, (t0 + TQ - 1) // kw + 1, sel_step, init))

    ww = min(WINDOW + TQ, seq_len)
    wstart = pl.multiple_of(jnp.maximum(t0 + TQ - ww, 0), TQ)
    wdelta = (t0 + lax.broadcasted_iota(I32, (TQ, ww), 0)) - (wstart + lax.broadcasted_iota(I32, (TQ, ww), 1))
    wbias = jnp.where((wdelta >= 0) & (wdelta < WINDOW), 0.0, NEG_INF)
    o_win = finish(span_update(init, wk_ref[pl.ds(wstart, ww), :].astype(BF16), wv_ref[pl.ds(wstart, ww), :].astype(BF16),
                               wbias, wstart))

    gs = jax.nn.sigmoid(gt_ref[...])
    for g in range(NSA_GROUP):
        def gate(c):
            a = gs[:, g * 3 + c:g * 3 + c + 1]
            b = gs[:, NSA_GROUP * 3 + g * 3 + c:NSA_GROUP * 3 + g * 3 + c + 1]
            return jnp.where(h == 0, a, b)

        sl = slice(g * TQ, (g + 1) * TQ)
        o_ref[:, g * HEAD_DIM:(g + 1) * HEAD_DIM] = (gate(0) * o_cmp[sl] + gate(1) * o_sel[sl]) + gate(2) * o_win[sl]


def _nsa_prompt(p2, ckv, slopes, nbatch, seq_len):
    nq = seq_len // TQ
    kvs0 = COL_KVS // LANE
    kvw0 = COL_KVW // LANE
    misc = COL_MISC // LANE
    nblk = seq_len // CMP_BLOCK
    full = lambda c0: pl.BlockSpec((seq_len, LANE), lambda b, h, q, c0=c0: (b, c0 + h))
    return pl.pallas_call(
        functools.partial(_nsa_prompt_body, seq_len=seq_len),
        out_shape=jax.ShapeDtypeStruct((nbatch * seq_len, NSA_WIDTH), F32),
        grid=(nbatch, NSA_KV_HEADS, nq),
        in_specs=[pl.BlockSpec(memory_space=pltpu.SMEM),
                  pl.BlockSpec((TQ, NSA_GROUP * HEAD_DIM), lambda b, h, q: (b * nq + q, h)),
                  pl.BlockSpec((TQ, LANE), lambda b, h, q: (b * nq + q, misc)),
                  pl.BlockSpec((1, 1, nblk, HEAD_DIM), lambda b, h, q: (b, h, 0, 0)),
                  pl.BlockSpec((1, 1, nblk, HEAD_DIM), lambda b, h, q: (b, 2 + h, 0, 0)),
                  full(kvs0), full(kvs0 + 2), full(kvw0), full(kvw0 + 2)],
        out_specs=pl.BlockSpec((TQ, NSA_GROUP * HEAD_DIM), lambda b, h, q: (b * nq + q, h)),
        scratch_shapes=[pltpu.VMEM((seq_len // NSA_KSPAN, TQ, NSA_KSPAN), F32)],
        compiler_params=_cparams(("arbitrary", "arbitrary", "arbitrary")),
        name="nsa_prompt",
    )(slopes, p2, p2, ckv, ckv, p2, p2, p2, p2)


GDN_CB = 8


def _gdn_chunk(seq_len):
    return min(GDN_CHUNK, seq_len)


def _bmm3(a, b):
    ah, al = _split(a)
    bh, bl = _split(b)
    return _dot(jnp.concatenate([ah, al, ah], axis=2), jnp.concatenate([bh, bh, bl], axis=1), BNN)


GDN_HG = 2


def _gdn_body(alog_ref, dtb_ref, xq_ref, xk_ref, xv_ref, z_ref, gt_ref, cwq_ref, cwk_ref, cwv_ref,
              cbq_ref, cbk_ref, cbv_ref, s0_ref, gn_ref, o_ref, sout_ref,
              cat_s, q_s, k_s, v_s, b_s, g_s, u_s, w_s, qg_s, kdt_s, qk_s, gl_s, oc_s, *, seq_len, valid):
    hg = pl.program_id(1)
    C = _gdn_chunk(seq_len)
    nc = seq_len // C
    cb = min(GDN_CB, nc)
    row = lax.broadcasted_iota(I32, (seq_len, 1), 0)
    lane = lax.broadcasted_iota(I32, (1, LANE), 1)
    pos = row & (C - 1)
    gt = gt_ref[...]

    def conv(x_ref, cw_ref, cb_ref, ls):
        x = x_ref[:, ls]
        cat_s[0:SUBLANE, :] = cb_ref[0, :, ls]
        cat_s[SUBLANE:, :] = x
        w = cw_ref[:, ls]
        acc = cat_s[pl.ds(SUBLANE - 3, seq_len), :] * w[0:1]
        acc = acc + cat_s[pl.ds(SUBLANE - 2, seq_len), :] * w[1:2]
        acc = acc + cat_s[pl.ds(SUBLANE - 1, seq_len), :] * w[2:3]
        acc = acc + x * w[3:4]
        return jax.nn.silu(acc)

    def l2n(x):
        return x * lax.rsqrt(jnp.sum(x * x, axis=-1, keepdims=True) + NORM_EPS)

    ii = lax.broadcasted_iota(I32, (C, C), 0)
    jj = lax.broadcasted_iota(I32, (C, C), 1)
    tri = (ii >= jj)[None]
    strict = (ii > jj)[None]
    eye = (ii == jj).astype(F32)[None]
    ones_b = jnp.ones((cb, C, 3 * C), BF16)
    ident_b = jnp.broadcast_to((lax.broadcasted_iota(I32, (HEAD_DIM, HEAD_DIM), 0)
                                == lax.broadcasted_iota(I32, (HEAD_DIM, HEAD_DIM), 1)).astype(BF16)[None],
                               (cb, HEAD_DIM, HEAD_DIM))

    for hl in range(GDN_HG):
        hd = hg * GDN_HG + hl
        ls = slice(hl * HEAD_DIM, (hl + 1) * HEAD_DIM)
        q = l2n(conv(xq_ref, cwq_ref, cbq_ref, ls)) * (HEAD_DIM ** -0.5)
        k = l2n(conv(xk_ref, cwk_ref, cbk_ref, ls))
        v = conv(xv_ref, cwv_ref, cbv_ref, ls)
        bl = jnp.sum(jnp.where(lane == MISC_BL + hd, gt, 0.0), axis=1, keepdims=True)
        al = jnp.sum(jnp.where(lane == MISC_AL + hd, gt, 0.0), axis=1, keepdims=True)
        beta = jax.nn.sigmoid(bl)
        a_pos = jnp.exp(jnp.full((1, 1), alog_ref[hd], F32))
        g = -a_pos * jax.nn.softplus(al + dtb_ref[hd])
        if valid < seq_len:
            vm = row < valid
            q = jnp.where(vm, q, 0.0)
            k = jnp.where(vm, k, 0.0)
            v = jnp.where(vm, v, 0.0)
            beta = jnp.where(vm, beta, 0.0)
            g = jnp.where(vm, g, 0.0)
        gc = g
        sft = 1
        while sft < C:
            gc = gc + jnp.where(pos >= sft, pltpu.roll(gc, sft, 0), 0.0)
            sft *= 2
        q_s[...] = q
        k_s[...] = k
        v_s[...] = v
        b_s[...] = jnp.broadcast_to(beta, (seq_len, LANE))
        g_s[...] = jnp.broadcast_to(gc, (seq_len, LANE))

        def local(gi, carry, hl=hl):
            r0 = pl.multiple_of(gi * (cb * C), cb * C)
            sl = pl.ds(r0, cb * C)
            qc = q_s[sl, :].reshape(cb, C, HEAD_DIM)
            kc = k_s[sl, :].reshape(cb, C, HEAD_DIM)
            vc = v_s[sl, :].reshape(cb, C, HEAD_DIM)
            bc = b_s[sl, :].reshape(cb, C, LANE)
            gcc = g_s[sl, :].reshape(cb, C, LANE)
            gcol = gcc[:, :, 0:C]
            dg = gcol * eye
            d1 = dg.astype(BF16)
            r1 = dg - d1.astype(F32)
            d2 = r1.astype(BF16)
            d3 = (r1 - d2.astype(F32)).astype(BF16)
            grow = _dot(ones_b, jnp.concatenate([d1, d2, d3], axis=1), BNN)
            diff = gcol - grow
            decay = jnp.where(tri, jnp.exp(jnp.where(tri, diff, 0.0)), 0.0)
            kb = kc * bc
            a = _dot(kb.astype(BF16), kc.astype(BF16), BNT) * jnp.where(strict, decay, 0.0)
            tinv = eye - a
            pw = a
            n = 2
            while n < C:
                pw = _bmm3(pw, pw)
                tinv = tinv + _bmm3(tinv, pw)
                n *= 2
            eg = jnp.exp(gcc)
            rhs = jnp.concatenate([vc * bc, kb * eg], axis=2)
            sol = _bmm3(tinv, rhs)
            u_s[hl, sl, :] = sol[:, :, 0:HEAD_DIM].reshape(cb * C, HEAD_DIM)
            w_s[hl, sl, :] = sol[:, :, HEAD_DIM:2 * HEAD_DIM].reshape(cb * C, HEAD_DIM)
            qk = _dot(qc.astype(BF16), kc.astype(BF16), BNT) * decay
            qk_s[hl, sl, :] = qk.reshape(cb * C, C)
            qg_s[hl, sl, :] = (qc * eg).reshape(cb * C, HEAD_DIM)
            glast = gcc[:, C - 1:C, :]
            kd = kc * jnp.exp(glast - gcc)
            kdt = _dot(ident_b, kd.astype(BF16), BNT)
            kdt_s[hl, pl.ds(pl.multiple_of(gi * (cb * HEAD_DIM), cb * HEAD_DIM), cb * HEAD_DIM), :] = (
                kdt.reshape(cb * HEAD_DIM, C))
            gl_s[hl, pl.ds(pl.multiple_of(gi * (cb * SUBLANE), cb * SUBLANE), cb * SUBLANE), :] = (
                jnp.broadcast_to(jnp.exp(glast), (cb, SUBLANE, LANE)).reshape(cb * SUBLANE, LANE))
            return carry

        lax.fori_loop(0, nc // cb, local, 0)

    def scan(c, states):
        sl = pl.ds(pl.multiple_of(c * C, C), C)
        new = []
        for hl in range(GDN_HG):
            s = states[hl]
            sb = s.astype(BF16)
            v_new = u_s[hl, sl, :] - _dot(w_s[hl, sl, :].astype(BF16), sb)
            o_c = (_dot(qg_s[hl, sl, :].astype(BF16), sb)
                   + _dot(qk_s[hl, sl, :].astype(BF16), v_new.astype(BF16)))
            oc_s[hl, sl, :] = o_c
            kdt = kdt_s[hl, pl.ds(pl.multiple_of(c * HEAD_DIM, HEAD_DIM), HEAD_DIM), :]
            gl = gl_s[hl, pl.ds(pl.multiple_of(c * SUBLANE, SUBLANE), 1), :]
            new.append(s * gl + _dot(kdt.astype(BF16), v_new.astype(BF16)))
        return tuple(new)

    s_fin = lax.fori_loop(0, nc, scan, tuple(s0_ref[0, hl] for hl in range(GDN_HG)))
    for hl in range(GDN_HG):
        ls = slice(hl * HEAD_DIM, (hl + 1) * HEAD_DIM)
        sout_ref[0, hl] = s_fin[hl]
        o = oc_s[hl]
        y = o * lax.rsqrt(jnp.mean(o * o, axis=-1, keepdims=True) + NORM_EPS) * gn_ref[...]
        o_ref[:, ls] = y * jax.nn.silu(z_ref[:, ls])


def _gdn(p2, conv_w, conv_buf8, s0, a_log, dt_bias, gn_w, nbatch, seq_len, valid):
    gw = GDN_HG * LANE
    q0 = COL_QKV // gw
    z0 = COL_Z // gw
    hgs = GDN_HEADS // GDN_HG
    misc = COL_MISC // LANE
    C = _gdn_chunk(seq_len)
    col = lambda c0: pl.BlockSpec((seq_len, gw), lambda b, h, c0=c0: (b, c0 + h))
    cw = lambda c0: pl.BlockSpec((CONV_WIDTH, gw), lambda b, h, c0=c0: (0, c0 + h))
    cbs = lambda c0: pl.BlockSpec((1, SUBLANE, gw), lambda b, h, c0=c0: (b, 0, c0 + h))
    smem = pl.BlockSpec(memory_space=pltpu.SMEM)
    nc = seq_len // C
    per_head = lambda rows, cols: pltpu.VMEM((GDN_HG, rows, cols), F32)
    return pl.pallas_call(
        functools.partial(_gdn_body, seq_len=seq_len, valid=valid),
        out_shape=(jax.ShapeDtypeStruct((nbatch * seq_len, GDN_WIDTH), F32),
                   jax.ShapeDtypeStruct((nbatch, GDN_HEADS, HEAD_DIM, HEAD_DIM), F32)),
        grid=(nbatch, hgs),
        in_specs=[smem, smem, col(q0), col(q0 + hgs), col(q0 + 2 * hgs), col(z0),
                  pl.BlockSpec((seq_len, LANE), lambda b, h: (b, misc)),
                  cw(0), cw(hgs), cw(2 * hgs), cbs(0), cbs(hgs), cbs(2 * hgs),
                  pl.BlockSpec((1, GDN_HG, HEAD_DIM, HEAD_DIM), lambda b, h: (b, h, 0, 0)),
                  pl.BlockSpec((1, HEAD_DIM), lambda b, h: (0, 0))],
        out_specs=(pl.BlockSpec((seq_len, gw), lambda b, h: (b, h)),
                   pl.BlockSpec((1, GDN_HG, HEAD_DIM, HEAD_DIM), lambda b, h: (b, h, 0, 0))),
        scratch_shapes=[pltpu.VMEM((seq_len + SUBLANE, HEAD_DIM), F32)]
        + [pltpu.VMEM((seq_len, HEAD_DIM), F32)] * 3
        + [pltpu.VMEM((seq_len, LANE), F32)] * 2
        + [per_head(seq_len, HEAD_DIM)] * 3
        + [per_head(nc * HEAD_DIM, C),
           per_head(seq_len, C),
           per_head(nc * SUBLANE, LANE),
           per_head(seq_len, HEAD_DIM)],
        compiler_params=_cparams(("arbitrary", "arbitrary")),
        name="gdn",
    )(a_log, dt_bias, p2, p2, p2, p2, p2, conv_w, conv_w, conv_w, conv_buf8, conv_buf8, conv_buf8,
      s0, gn_w.reshape(1, HEAD_DIM))


def _out_proj_body(on_ref, og_ref, x_ref, w_ref, g1_ref, nw_ref, sc_ref, sh_ref, x1_ref, h2_ref, h2b_ref, *,
                   seq_len):
    mix = _dot(on_ref[...].astype(BF16), w_ref[0:NSA_WIDTH, :]) + _dot(og_ref[...].astype(BF16), w_ref[NSA_WIDTH:, :])
    x1 = x_ref[...] + _seq_rows(g1_ref, seq_len) * mix
    x1_ref[...] = x1
    y = x1 * lax.rsqrt(jnp.mean(x1 * x1, axis=-1, keepdims=True) + NORM_EPS)
    h2 = (y * nw_ref[...]) * (1.0 + _seq_rows(sc_ref, seq_len)) + _seq_rows(sh_ref, seq_len)
    h2_ref[...] = h2
    h2b_ref[...] = h2.astype(BF16)


def _out_proj(o_nsa, o_gdn, x, w_out_bf, g1, norm_w, sc, sh, seq_len, tm):
    n, d = x.shape
    mod = _seq_spec(seq_len, tm, d)
    return pl.pallas_call(
        functools.partial(_out_proj_body, seq_len=seq_len),
        out_shape=(jax.ShapeDtypeStruct((n, d), F32), jax.ShapeDtypeStruct((n, d), F32),
                   jax.ShapeDtypeStruct((n, d), BF16)),
        grid=(n // tm,),
        in_specs=[pl.BlockSpec((tm, NSA_WIDTH), lambda i: (i, 0)),
                  pl.BlockSpec((tm, GDN_WIDTH), lambda i: (i, 0)),
                  pl.BlockSpec((tm, d), lambda i: (i, 0)),
                  pl.BlockSpec((d, d), lambda i: (0, 0)),
                  mod,
                  pl.BlockSpec((1, d), lambda i: (0, 0)),
                  mod, mod],
        out_specs=tuple(pl.BlockSpec((tm, d), lambda i: (i, 0)) for _ in range(3)),
        compiler_params=_cparams(("arbitrary",)),
        name="out_proj",
    )(o_nsa, o_gdn, x, w_out_bf, g1, norm_w.reshape(1, d), sc, sh)


def _take_top(src_ref, dst_ref):
    s = src_ref[...]
    above = jnp.full(s.shape, float(PEER_TOPK), F32)
    for k in range(PEER_TOPK):
        m = jnp.max(s, axis=0, keepdims=True)
        hit = s == m
        above = jnp.where(hit, float(k), above)
        s = jnp.where(hit, NEG_BIG, s)
        dst_ref[k:k + 1, :] = m
    return above


PAIR_ROWS = ((0, 16, 16), (16, 8, 8), (24, 8, 5), (32, 8, 4), (40, 8, 3), (48, 8, 2), (56, 8, 2), (64, 8, 2))
PAIR_TAIL = 72
PAIR_TOTAL = 80


def _peer_sel_body(h_ref, wqh_ref, wql_ref, keys_ref, cnt_ref, al_ref, code_ref, be_ref,
                   qt_s, sc_s, ta_s, tb_s, cand_s, tc_s, *, split_query):
    hh, hl = _split(h_ref[...])
    half = PEER_QDIM // 2
    tn = h_ref.shape[0]
    r8 = lax.broadcasted_iota(I32, (SUBLANE, tn), 0)
    qt_all = _dot(wqh_ref[...], hh, NT)
    if split_query:
        qt_all = qt_all + (_dot(wql_ref[...], hh, NT) + _dot(wqh_ref[...], hl, NT))
    qt_s[...] = qt_all
    for hd in range(PEER_HEADS):
        qt = qt_s[hd * PEER_QDIM:(hd + 1) * PEER_QDIM, :]
        s0 = _mm3(keys_ref[0], qt[0:half])
        s1 = _mm3(keys_ref[1], qt[half:PEER_QDIM])
        sc_s[...] = s0
        _take_top(sc_s, ta_s)
        sc_s[...] = s1
        code = _take_top(sc_s, tb_s)
        a = ta_s[...]
        b = tb_s[...]
        for r, (row0, nrow, nval) in enumerate(PAIR_ROWS):
            pair = a[r:r + 1] + b[0:nrow]
            cand_s[row0:row0 + nrow, :] = pair if nval == nrow else jnp.where(r8 < nval, pair, NEG_BIG)
        cand_s[PAIR_TAIL:PAIR_TOTAL, :] = a[SUBLANE:2 * SUBLANE] + b[0:1]
        _take_top(cand_s, tc_s)
        tau = tc_s[PEER_TOPK - 1:PEER_TOPK, :]
        cand = cand_s[...]
        keep = cand >= tau
        zsum = jnp.sum(jnp.where(keep, jnp.exp(cand - cand[0:1]), 0.0), axis=0, keepdims=True)
        keepf = keep.astype(F32)
        cnt = jnp.zeros((PEER_NKEYS, tn), F32)
        for r, (row0, nrow, nval) in enumerate(PAIR_ROWS):
            cnt_r = jnp.sum(keepf[row0:row0 + nrow], axis=0, keepdims=True)
            cnt = jnp.where(s0 == a[r:r + 1], cnt_r, cnt)
        for r in range(SUBLANE, 2 * SUBLANE):
            cnt = jnp.where(s0 == a[r:r + 1], keepf[PAIR_TAIL + r - SUBLANE:PAIR_TAIL + r - SUBLANE + 1], cnt)
        cnt_ref[hd] = cnt
        al_ref[hd] = jnp.exp(s0 - a[0:1]) / zsum
        code_ref[hd] = code.astype(BF16)
        be_ref[hd] = jnp.exp(s1 - b[0:1]).astype(BF16)


def _peer_select(h2, wq_hi, wq_lo, keys, tn, split_query):
    n, d = h2.shape
    hk = jax.ShapeDtypeStruct((PEER_HEADS, PEER_NKEYS, n), F32)
    hk16 = jax.ShapeDtypeStruct((PEER_HEADS, PEER_NKEYS, n), BF16)
    blk = pl.BlockSpec((PEER_HEADS, PEER_NKEYS, tn), lambda i: (0, 0, i))
    return pl.pallas_call(
        functools.partial(_peer_sel_body, split_query=split_query),
        out_shape=(hk, hk, hk16, hk16),
        grid=(n // tn,),
        in_specs=[pl.BlockSpec((tn, d), lambda i: (i, 0)),
                  pl.BlockSpec((PEER_HEADS * PEER_QDIM, d), lambda i: (0, 0)),
                  pl.BlockSpec((PEER_HEADS * PEER_QDIM, d), lambda i: (0, 0)),
                  pl.BlockSpec((2, PEER_NKEYS, PEER_QDIM // 2), lambda i: (0, 0, 0))],
        out_specs=(blk, blk, blk, blk),
        scratch_shapes=[pltpu.VMEM((PEER_HEADS * PEER_QDIM, tn), F32),
                        pltpu.VMEM((PEER_NKEYS, tn), F32), pltpu.VMEM((PEER_TOPK, tn), F32),
                        pltpu.VMEM((PEER_TOPK, tn), F32), pltpu.VMEM((PAIR_TOTAL, tn), F32),
                        pltpu.VMEM((PEER_TOPK, tn), F32)],
        compiler_params=_cparams(("arbitrary",)),
        name="peer_select",
    )(h2, wq_hi, wq_lo, keys)


PEER_TE = 1024
GELU_C0 = math.sqrt(2.0 / math.pi)
GELU_C1 = 0.044715 * math.sqrt(2.0 / math.pi)


def _gelu_tanh(x):
    hx = 0.5 * x
    return hx + hx * jnp.tanh(x * (GELU_C0 + GELU_C1 * (x * x)))


def _peer_mm_body(h_ref, u_ref, vt_ref, cnt_ref, al_ref, code_ref, be_ref, o_ref, at_s, p_s):
    e = pl.program_id(1)

    @pl.when(e == 0)
    def _():
        o_ref[...] = jnp.zeros_like(o_ref)

    at_s[...] = _dot(u_ref[...], h_ref[...], NT)
    tn = h_ref.shape[0]
    for ii in range(PEER_TE // PEER_NKEYS):
        rs = slice(ii * PEER_NKEYS, (ii + 1) * PEER_NKEYS)
        for ck in range(tn // LANE):
            cs = slice(ck * LANE, (ck + 1) * LANE)
            w = jnp.zeros((PEER_NKEYS, LANE), BF16)
            for hd in range(PEER_HEADS):
                cnt = cnt_ref[hd, ii:ii + 1, cs].astype(BF16)
                kept = jnp.where(code_ref[hd, :, cs] < cnt, be_ref[hd, :, cs], jnp.zeros((), BF16))
                w = w + al_ref[hd, ii:ii + 1, cs].astype(BF16) * kept
            p_s[rs, cs] = w * _gelu_tanh(at_s[rs, cs].astype(BF16))
    o_ref[...] += _dot(vt_ref[...], p_s[...])


def _peer_mm(h2b, u_bf, vt_bf, cnt, al, code, be, tn):
    n, d = h2b.shape
    ne = u_bf.shape[0]
    rows_i = PEER_TE // PEER_NKEYS
    sel_i = pl.BlockSpec((PEER_HEADS, rows_i, tn), lambda i, e: (0, e, i))
    sel_j = pl.BlockSpec((PEER_HEADS, PEER_NKEYS, tn), lambda i, e: (0, 0, i))
    return pl.pallas_call(
        _peer_mm_body,
        out_shape=jax.ShapeDtypeStruct((d, n), F32),
        grid=(n // tn, ne // PEER_TE),
        in_specs=[pl.BlockSpec((tn, d), lambda i, e: (i, 0)),
                  pl.BlockSpec((PEER_TE, d), lambda i, e: (e, 0)),
                  pl.BlockSpec((d, PEER_TE), lambda i, e: (0, e)),
                  sel_i, sel_i, sel_j, sel_j],
        out_specs=pl.BlockSpec((d, tn), lambda i, e: (0, i)),
        scratch_shapes=[pltpu.VMEM((PEER_TE, tn), F32), pltpu.VMEM((PEER_TE, tn), BF16)],
        compiler_params=_cparams(("arbitrary", "arbitrary")),
        name="peer_experts",
    )(h2b, u_bf, vt_bf, cnt, al, code, be)


def _final_body(x1_ref, pt_ref, g2_ref, fw_ref, y_ref, *, seq_len):
    x2 = x1_ref[...] + _seq_rows(g2_ref, seq_len) * pt_ref[...].T
    y_ref[...] = x2 * lax.rsqrt(jnp.mean(x2 * x2, axis=-1, keepdims=True) + NORM_EPS) * fw_ref[...]


def _final(x1, peer_t, g2, fw, seq_len, tm):
    n, d = x1.shape
    g2_spec = _seq_spec(seq_len, tm, d)
    return pl.pallas_call(
        functools.partial(_final_body, seq_len=seq_len),
        out_shape=jax.ShapeDtypeStruct((n, d), F32),
        grid=(n // tm,),
        in_specs=[pl.BlockSpec((tm, d), lambda i: (i, 0)),
                  pl.BlockSpec((d, tm), lambda i: (0, i)),
                  g2_spec,
                  pl.BlockSpec((1, d), lambda i: (0, 0))],
        out_specs=pl.BlockSpec((tm, d), lambda i: (i, 0)),
        compiler_params=_cparams(("arbitrary",)),
        name="final_norm",
    )(x1, peer_t, g2, fw.reshape(1, d))


SROWS = 64
SROWS_TAIL = 16
CMP_PAGES = 16
KV_COMP = 2 * NSA_KV_HEADS
TOKEN_ROWS = KV_COMP
PAGE_VROWS = PAGE_SIZE * TOKEN_ROWS
CMP_PAIRS = CMP_BLOCK * TOKEN_ROWS // SUBLANE
BLOCK_VROWS = CMP_BLOCK * TOKEN_ROWS


def _compress_sample_body(pt_ref, *refs):
    pages = refs[:CMP_PAGES]
    pe_ref, w1_ref, b1_ref, w2_ref, o_ref, acc_s = refs[CMP_PAGES:]
    bpp = PAGE_SIZE // CMP_BLOCK
    nrow = CMP_PAGES * bpp * SUBLANE
    nh = CMP_PAGES * bpp // 2
    lhs = []
    for q in range(CMP_PAIRS):
        parts = [r[0, n * BLOCK_VROWS + q * SUBLANE:n * BLOCK_VROWS + (q + 1) * SUBLANE, :]
                 for r in pages for n in range(bpp)]
        lhs.append((jnp.concatenate(parts, axis=0)
                    + jnp.concatenate([pe_ref[q]] * (nrow // SUBLANE), axis=0)).astype(BF16))
    acc = _dot(jnp.concatenate(lhs, axis=1), w1_ref[...])
    rid = lax.broadcasted_iota(I32, (nrow, 1), 0) & (SUBLANE - 1)
    want = ((rid & (KV_COMP - 1)) >> 1) * 2 + (rid >> 2)
    picked = acc[:, 0:HEAD_DIM]
    for blk in range(1, KV_COMP):
        picked = jnp.where(want == blk, acc[:, blk * HEAD_DIM:(blk + 1) * HEAD_DIM], picked)
    acc_s[...] = picked + pltpu.roll(picked, nrow - KV_COMP, 0)
    for c in range(KV_COMP):
        s = c // NSA_KV_HEADS
        hid_e = acc_s[pl.ds(c, nh, stride=2 * SUBLANE), :]
        hid_o = acc_s[pl.ds(SUBLANE + c, nh, stride=2 * SUBLANE), :]
        hid = jax.nn.gelu(jnp.concatenate([hid_e, hid_o], axis=0) + b1_ref[s])
        out = _mm1(hid, w2_ref[s])
        o_ref[0, c, 0] = out[0:nh]
        o_ref[0, c, 1] = out[nh:2 * nh]


def _compress_sample(cache, page_table, pe, w1, b1, w2):
    nbs, n_pages = page_table.shape
    n_phys = cache.shape[0]
    cache_v = cache.reshape(n_phys, PAGE_VROWS, HEAD_DIM)
    ngrp = n_pages // CMP_PAGES
    nblk_half = n_pages * PAGE_SIZE // CMP_BLOCK // 2
    nh = CMP_PAGES * (PAGE_SIZE // CMP_BLOCK) // 2
    pe_q = jnp.transpose(pe.reshape(2, CMP_PAIRS, 2, HEAD_DIM), (1, 2, 0, 3))
    pe_q = jnp.broadcast_to(pe_q[:, :, :, None, :], (CMP_PAIRS, 2, 2, NSA_KV_HEADS, HEAD_DIM))
    pe_q = pe_q.reshape(CMP_PAIRS, SUBLANE, HEAD_DIM)
    w1_q = jnp.transpose(w1.reshape(2, CMP_PAIRS, 2, HEAD_DIM, HEAD_DIM), (1, 3, 0, 2, 4))
    w1_q = w1_q.reshape(CMP_PAIRS * HEAD_DIM, KV_COMP * HEAD_DIM).astype(BF16)
    page_spec = lambda k: pl.BlockSpec((1, PAGE_VROWS, HEAD_DIM),
                                       lambda b, g, pt, k=k: (pt[b * n_pages + g * CMP_PAGES + k], 0, 0))
    const = lambda shape: pl.BlockSpec(shape, lambda b, g, pt: (0,) * len(shape))
    grid_spec = pltpu.PrefetchScalarGridSpec(
        num_scalar_prefetch=1,
        grid=(nbs, ngrp),
        in_specs=[page_spec(k) for k in range(CMP_PAGES)]
        + [const((CMP_PAIRS, SUBLANE, HEAD_DIM)), const((CMP_PAIRS * HEAD_DIM, KV_COMP * HEAD_DIM)),
           const((2, 1, HEAD_DIM)), const((2, HEAD_DIM, HEAD_DIM))],
        out_specs=pl.BlockSpec((1, KV_COMP, 2, nh, HEAD_DIM), lambda b, g, pt: (b, 0, 0, g, 0)),
        scratch_shapes=[pltpu.VMEM((CMP_PAGES * (PAGE_SIZE // CMP_BLOCK) * SUBLANE, HEAD_DIM), F32)])
    return pl.pallas_call(
        _compress_sample_body,
        out_shape=jax.ShapeDtypeStruct((nbs, KV_COMP, 2, nblk_half, HEAD_DIM), F32),
        grid_spec=grid_spec,
        compiler_params=_cparams(("arbitrary", "arbitrary")),
        name="nsa_compress_sample",
    )(page_table.reshape(-1), *([cache_v] * CMP_PAGES), pe_q, w1_q, b1.reshape(2, 1, HEAD_DIM), w2)


def _rows_from_lanes(row, ngrp):
    ridx = lax.broadcasted_iota(I32, (SUBLANE, HEAD_DIM), 0)
    out = jnp.zeros((SUBLANE, HEAD_DIM), F32)
    for g in range(ngrp):
        out = jnp.where(ridx == g, jnp.broadcast_to(row[:, g * HEAD_DIM:(g + 1) * HEAD_DIM], (SUBLANE, HEAD_DIM)), out)
    return out


def _slope_rows(slopes_ref, h):
    ridx = lax.broadcasted_iota(I32, (SUBLANE, 1), 0)
    slope = jnp.zeros((SUBLANE, 1), F32)
    for g in range(NSA_GROUP):
        slope = jnp.where(ridx == g, slopes_ref[h * NSA_GROUP + g], slope)
    return slope


def _nsa_sample_select_body(slopes_ref, q_ref, ckv_ref, ocmp_ref, idx_ref, *, past):
    ncb = past // CMP_BLOCK // 2
    nblk = past // SEL_BLOCK
    ridx = lax.broadcasted_iota(I32, (SUBLANE, 1), 0)
    midx = lax.broadcasted_iota(I32, (1, ncb), 1)
    jb = lax.broadcasted_iota(I32, (1, nblk), 1)
    slot = lax.broadcasted_iota(I32, (1, SEL_TOPN), 1)
    for h in range(NSA_KV_HEADS):
        q8 = _rows_from_lanes(q_ref[0:1, h * NSA_GROUP * HEAD_DIM:(h + 1) * NSA_GROUP * HEAD_DIM], NSA_GROUP)
        q8 = q8 * (HEAD_DIM ** -0.5)
        slope = _slope_rows(slopes_ref, h)
        lg = []
        for par in range(2):
            s_c = _mm3(q8, ckv_ref[0, h, par], NT)
            cend = midx * (2 * CMP_BLOCK) + (CMP_BLOCK - 1) + par * CMP_BLOCK
            lg.append(s_c - slope * (past - cend).astype(F32))
        mx = jnp.maximum(jnp.max(lg[0], axis=1, keepdims=True), jnp.max(lg[1], axis=1, keepdims=True))
        e0 = jnp.exp(lg[0] - mx)
        e1 = jnp.exp(lg[1] - mx)
        den = jnp.sum(e0, axis=1, keepdims=True) + jnp.sum(e1, axis=1, keepdims=True)
        p0 = e0 / den
        p1 = e1 / den
        o_cmp = _mm1(p0, ckv_ref[0, 2 + h, 0]) + _mm1(p1, ckv_ref[0, 2 + h, 1])
        ocmp_ref[0, h * NSA_GROUP:(h + 1) * NSA_GROUP, :] = o_cmp[0:NSA_GROUP]
        imp = jnp.sum(jnp.where(ridx < NSA_GROUP, p0 + p1, 0.0), axis=0, keepdims=True)
        forced = (jb == 0) | (jb == nblk - 1)
        score = jnp.where(forced, FORCE_SCORE, imp)
        picks = jnp.full((1, SEL_TOPN), nblk, I32)
        for k in range(SEL_TOPN - 1):
            m = jnp.max(score, axis=1, keepdims=True)
            first = jnp.min(jnp.where(score == m, jb, nblk), axis=1, keepdims=True)
            score = jnp.where(jb == first, NEG_BIG, score)
            picks = jnp.where(slot == k, first, picks)
        idx_ref[0, h:h + 1, :] = picks


def _nsa_sample_select(p2s, ckv_s, slopes, nbs, past):
    nblk_half = past // CMP_BLOCK // 2
    return pl.pallas_call(
        functools.partial(_nsa_sample_select_body, past=past),
        out_shape=(jax.ShapeDtypeStruct((nbs, NSA_HEADS, HEAD_DIM), F32),
                   jax.ShapeDtypeStruct((nbs, NSA_KV_HEADS, SEL_TOPN), I32)),
        grid=(nbs,),
        in_specs=[pl.BlockSpec(memory_space=pltpu.SMEM),
                  pl.BlockSpec((SUBLANE, NSA_WIDTH), lambda b: (b * (SROWS // SUBLANE), 0)),
                  pl.BlockSpec((1, 4, 2, nblk_half, HEAD_DIM), lambda b: (b, 0, 0, 0, 0))],
        out_specs=(pl.BlockSpec((1, NSA_HEADS, HEAD_DIM), lambda b: (b, 0, 0)),
                   pl.BlockSpec((1, NSA_KV_HEADS, SEL_TOPN), lambda b: (b, 0, 0))),
        compiler_params=_cparams(("arbitrary",)),
        name="nsa_sample_select",
    )(slopes, p2s, ckv_s)


NSEL_PAST = SEL_TOPN - 1


def _nsa_sample_attend_body(idx_ref, pt_ref, slopes_ref, q_ref, gt_ref, ocmp_ref, skn_ref, svn_ref, wkn_ref, wvn_ref,
                            *refs, past, n_pages):
    blocks = refs[:NSEL_PAST]
    wc_ref, o_ref = refs[NSEL_PAST:]
    b = pl.program_id(0)
    h = pl.program_id(1)
    q8 = _rows_from_lanes(q_ref[0:1, :], NSA_GROUP) * (HEAD_DIM ** -0.5)
    q8b = q8.astype(BF16)
    slope = _slope_rows(slopes_ref, h)

    def attend_with_new(k_old, v_old, lg_bias, k_new, v_new):
        lg = _dot(q8b, k_old.astype(BF16), NT) + lg_bias
        s_new = jnp.sum(q8 * k_new, axis=1, keepdims=True)
        m = jnp.maximum(jnp.max(lg, axis=1, keepdims=True), s_new)
        p = jnp.exp(lg - m)
        pn = jnp.exp(s_new - m)
        den = jnp.sum(p, axis=1, keepdims=True) + pn
        return (_dot(p.astype(BF16), v_old.astype(BF16)) + pn * v_new) / den

    k_sel = jnp.concatenate([r[0, pl.ds(h, SEL_BLOCK, stride=TOKEN_ROWS), :] for r in blocks], axis=0)
    v_sel = jnp.concatenate([r[0, pl.ds(NSA_KV_HEADS + h, SEL_BLOCK, stride=TOKEN_ROWS), :] for r in blocks], axis=0)
    lane = lax.broadcasted_iota(I32, (1, NSEL_PAST * SEL_BLOCK), 1)
    slot = lane >> (SEL_BLOCK.bit_length() - 1)
    blk = jnp.zeros((1, NSEL_PAST * SEL_BLOCK), I32)
    for j in range(NSEL_PAST):
        blk = jnp.where(slot == j, idx_ref[(b * NSA_KV_HEADS + h) * SEL_TOPN + j], blk)
    dist = (past - (blk * SEL_BLOCK + (lane & (SEL_BLOCK - 1)))).astype(F32)
    o_sel = attend_with_new(k_sel, v_sel, -slope * dist, skn_ref[0:1, :], svn_ref[0:1, :])

    nw = wc_ref.shape[1] // TOKEN_ROWS
    r = lax.broadcasted_iota(I32, (1, nw), 1)
    delta = nw - r
    wmask = (delta < WINDOW) & (past - delta >= 0)
    k_win = wc_ref[0, pl.ds(h, nw, stride=TOKEN_ROWS), :]
    v_win = wc_ref[0, pl.ds(NSA_KV_HEADS + h, nw, stride=TOKEN_ROWS), :]
    o_win = attend_with_new(k_win, v_win, jnp.where(wmask, -slope * delta.astype(F32), NEG_INF),
                            wkn_ref[0:1, :], wvn_ref[0:1, :])

    gs = jax.nn.sigmoid(gt_ref[0:1, :])
    o_ref[...] = jnp.zeros_like(o_ref)
    for g in range(NSA_GROUP):
        def gate(c):
            a = gs[:, g * 3 + c:g * 3 + c + 1]
            bb = gs[:, NSA_GROUP * 3 + g * 3 + c:NSA_GROUP * 3 + g * 3 + c + 1]
            return jnp.where(h == 0, a, bb)

        oc = jnp.where(h == 0, ocmp_ref[0, g:g + 1, :], ocmp_ref[0, NSA_GROUP + g:NSA_GROUP + g + 1, :])
        og = (gate(0) * oc + gate(1) * o_sel[g:g + 1]) + gate(2) * o_win[g:g + 1]
        o_ref[0:1, g * HEAD_DIM:(g + 1) * HEAD_DIM] = og


def _nsa_sample_attend(p2s, ocmp, idx, page_table, cache_sel, win_state, slopes, nbs, past):
    n_pages = page_table.shape[1]
    n_phys = cache_sel.shape[0]
    nw = win_state.shape[1]
    halves = PAGE_SIZE // SEL_BLOCK
    sel_v = cache_sel.reshape(n_phys * halves, SEL_BLOCK * TOKEN_ROWS, HEAD_DIM)
    win_v = win_state.reshape(nbs, nw * TOKEN_ROWS, HEAD_DIM)
    rb = SROWS // SUBLANE
    kvs0 = COL_KVS // LANE
    kvw0 = COL_KVW // LANE
    misc = COL_MISC // LANE

    def sel_spec(j):
        def index(b, h, idx, pt):
            blk = idx[(b * NSA_KV_HEADS + h) * SEL_TOPN + j]
            page = pt[b * n_pages + blk // halves]
            return (page * halves + blk % halves, 0, 0)
        return pl.BlockSpec((1, SEL_BLOCK * TOKEN_ROWS, HEAD_DIM), index)

    new = lambda c0: pl.BlockSpec((SUBLANE, LANE), lambda b, h, idx, pt, c0=c0: (b * rb, c0 + h))
    grid_spec = pltpu.PrefetchScalarGridSpec(
        num_scalar_prefetch=2,
        grid=(nbs, NSA_KV_HEADS),
        in_specs=[pl.BlockSpec(memory_space=pltpu.SMEM),
                  pl.BlockSpec((SUBLANE, NSA_GROUP * HEAD_DIM), lambda b, h, idx, pt: (b * rb, h)),
                  pl.BlockSpec((SUBLANE, LANE), lambda b, h, idx, pt: (b * rb, misc)),
                  pl.BlockSpec((1, NSA_HEADS, HEAD_DIM), lambda b, h, idx, pt: (b, 0, 0)),
                  new(kvs0), new(kvs0 + 2), new(kvw0), new(kvw0 + 2)]
        + [sel_spec(j) for j in range(NSEL_PAST)]
        + [pl.BlockSpec((1, nw * TOKEN_ROWS, HEAD_DIM), lambda b, h, idx, pt: (b, 0, 0))],
        out_specs=pl.BlockSpec((SROWS, NSA_GROUP * HEAD_DIM), lambda b, h, idx, pt: (b, h)))
    return pl.pallas_call(
        functools.partial(_nsa_sample_attend_body, past=past, n_pages=n_pages),
        out_shape=jax.ShapeDtypeStruct((nbs * SROWS, NSA_WIDTH), F32),
        grid_spec=grid_spec,
        compiler_params=_cparams(("arbitrary", "arbitrary")),
        name="nsa_sample_attend",
    )(idx.reshape(-1), page_table.reshape(-1), slopes, p2s, p2s, ocmp, p2s, p2s, p2s, p2s,
      *([sel_v] * NSEL_PAST), win_v)


def _transpose_cast_body(x_ref, o_ref):
    o_ref[...] = x_ref[...].T.astype(BF16)


def _transpose_to_bf16(x):
    rows, cols = x.shape
    tr = PEER_TE
    return pl.pallas_call(
        _transpose_cast_body,
        out_shape=jax.ShapeDtypeStruct((cols, rows), BF16),
        grid=(rows // tr,),
        in_specs=[pl.BlockSpec((tr, cols), lambda i: (i, 0))],
        out_specs=pl.BlockSpec((cols, tr), lambda i: (0, i)),
        compiler_params=_cparams(("arbitrary",)),
        name="transpose_cast",
    )(x)


def _split_w_in(w_in):
    d = w_in.shape[0]
    w_bf = w_in.astype(BF16)
    g0 = NSA_WIDTH + 3 * 2 * KV_WIDTH
    q0 = g0 + 3 * NSA_HEADS
    b0 = q0 + 3 * GDN_WIDTH + GDN_WIDTH
    n_misc = (q0 - g0) + 2 * GDN_HEADS
    w_misc = jnp.concatenate([w_bf[:, g0:q0], w_bf[:, b0:], jnp.zeros((d, N_PROJ - COL_MISC - n_misc), BF16)], axis=1)
    return w_bf, w_bf[:, q0:b0], w_misc


def _split_bf16(w):
    hi = w.astype(BF16)
    lo = (w - hi.astype(F32)).astype(BF16)
    return hi, lo


def _mixer_tail(x, o_nsa, o_gdn, w_out_bf, g1, norm_ffn, sc2, sh2, g2, wq_hi, wq_lo, keys, u_bf, vt_bf,
                final_norm, seq_len, tm, tn_sel, tn_mm, tm_final):
    x1, h2, h2b = _out_proj(o_nsa, o_gdn, x, w_out_bf, g1, norm_ffn, sc2, sh2, seq_len, tm)
    cnt, al, code, be = _peer_select(h2, wq_hi, wq_lo, keys, tn_sel, split_query=h2.shape[0] <= tn_sel)
    peer_t = _peer_mm(h2b, u_bf, vt_bf, cnt, al, code, be, tn_mm)
    return _final(x1, peer_t, g2, final_norm, seq_len, tm_final)


def kernel(x_prompt, x_sample, cache_cmp_kv, cache_sel_kv, state_win_kv, state_conv, state_gdn, page_table,
           c_prompt, c_sample, w_ada, b_ada, norm_mix, norm_ffn, w_in, cmp_pe, cmp_w1, cmp_b1, cmp_w2, conv_w,
           gdn_a_log, gdn_dt_bias, gdn_norm, w_out, peer_wq, peer_keys, peer_u, peer_v, final_norm):
    nb, seq, d = x_prompt.shape
    nbs, dec_seq, _ = x_sample.shape
    assert w_in.shape[0] == 1 and dec_seq == 1, "single layer, single decode token"
    past = page_table.shape[1] * PAGE_SIZE
    slopes = 2.0 ** (-8.0 * jnp.arange(1, NSA_HEADS + 1, dtype=F32) / NSA_HEADS)
    rows_c = 16
    c_all = jnp.concatenate([c_prompt, c_sample, jnp.zeros((rows_c - nb - nbs, d), F32)], axis=0)
    mod = _adaln(c_all, w_ada[0], b_ada[0]).reshape(rows_c, 6, d)
    pm = lambda k: mod[0:nb, k][:, None, :]
    sm = lambda k: mod[nb:nb + nbs, k][:, None, :]
    w_proj = _split_w_in(w_in[0])
    w_out_bf = w_out[0].astype(BF16)
    wq_hi, wq_lo = _split_bf16(peer_wq[0].T)
    u_bf = peer_u[0].astype(BF16)
    vt_bf = _transpose_to_bf16(peer_v[0])
    cw = (cmp_pe[0], cmp_w1[0], cmp_b1[0], cmp_w2[0])
    kvw = 2 * KV_WIDTH

    xp = x_prompt.reshape(nb * seq, d)
    p2, cmp_rows, sel_rows = _proj_in(xp, norm_mix[0], pm(1), pm(0), *w_proj, seq, 1024)
    ckv = _compress_prompt(p2, *cw, nb, seq)
    o_nsa = _nsa_prompt(p2, ckv, slopes, nb, seq)
    conv0 = jnp.zeros((nb, SUBLANE, 3 * GDN_WIDTH), F32)
    s0 = jnp.zeros((nb, GDN_HEADS, HEAD_DIM, HEAD_DIM), F32)
    o_gdn, gdn_p = _gdn(p2, conv_w[0], conv0, s0, gdn_a_log[0], gdn_dt_bias[0], gdn_norm[0], nb, seq, seq)
    y_prompt = _mixer_tail(xp, o_nsa, o_gdn, w_out_bf, pm(2), norm_ffn[0], pm(4), pm(3), pm(5), wq_hi, wq_lo,
                           peer_keys[0], u_bf, vt_bf, final_norm, seq, 512, 256, 512, 512).reshape(nb, seq, d)
    p3 = p2.reshape(nb, seq, N_PROJ)
    keep = min(WINDOW, seq)
    kv5 = lambda a: a.reshape(a.shape[0], a.shape[1], 2, NSA_KV_HEADS, HEAD_DIM)
    cmp_p = cmp_rows.reshape(nb, seq, 2, NSA_KV_HEADS, HEAD_DIM)
    sel_p = sel_rows.reshape(nb, seq, 2, NSA_KV_HEADS, HEAD_DIM)
    win_p = kv5(p3[:, seq - keep:, COL_KVW:COL_KVW + kvw])
    conv_p = p3[:, seq - (CONV_WIDTH - 1):, COL_QKV:COL_QKV + 3 * GDN_WIDTH]

    xs = jnp.pad(x_sample, ((0, 0), (0, SROWS - dec_seq), (0, 0))).reshape(nbs * SROWS, d)
    p2s, cmp_rows_s, sel_rows_s = _proj_in(xs, norm_mix[0], sm(1), sm(0), *w_proj, SROWS, nbs * SROWS)
    ckv_s = _compress_sample(cache_cmp_kv[0], page_table, *cw)
    ocmp, idx = _nsa_sample_select(p2s, ckv_s, slopes, nbs, past)
    o_nsa_s = _nsa_sample_attend(p2s, ocmp, idx, page_table, cache_sel_kv[0], state_win_kv[0], slopes, nbs, past)
    conv_buf = jnp.pad(state_conv[0], ((0, 0), (SUBLANE - (CONV_WIDTH - 1), 0), (0, 0)))
    o_gdn_s, gdn_s = _gdn(p2s, conv_w[0], conv_buf, state_gdn[0], gdn_a_log[0], gdn_dt_bias[0], gdn_norm[0],
                          nbs, SROWS, dec_seq)
    head_rows = lambda a: a.reshape(nbs, SROWS, a.shape[-1])[:, :SROWS_TAIL].reshape(nbs * SROWS_TAIL, a.shape[-1])
    nt = nbs * SROWS_TAIL
    ys = _mixer_tail(head_rows(xs), head_rows(o_nsa_s), head_rows(o_gdn_s), w_out_bf, sm(2), norm_ffn[0], sm(4),
                     sm(3), sm(5), wq_hi, wq_lo, peer_keys[0], u_bf, vt_bf, final_norm, SROWS_TAIL, nt, nt, nt, nt)
    y_sample = ys.reshape(nbs, SROWS_TAIL, d)[:, 0:dec_seq]
    p3s = p2s.reshape(nbs, SROWS, N_PROJ)[:, 0:dec_seq]
    cmp_s = cmp_rows_s.reshape(nbs, SROWS, 2, NSA_KV_HEADS, HEAD_DIM)[:, 0:dec_seq]
    sel_s = sel_rows_s.reshape(nbs, SROWS, 2, NSA_KV_HEADS, HEAD_DIM)[:, 0:dec_seq]
    win_new = kv5(p3s[:, :, COL_KVW:COL_KVW + kvw])
    win_s = jnp.concatenate([state_win_kv[0], win_new], axis=1)[:, dec_seq:]
    conv_s = jnp.concatenate([state_conv[0], p3s[:, :, COL_QKV:COL_QKV + 3 * GDN_WIDTH]], axis=1)[:, dec_seq:]

    return (y_prompt, y_sample, cmp_p[None], cmp_s[None], sel_p[None], sel_s[None], win_p[None], win_s[None],
            conv_p[None], conv_s[None], gdn_p[None], gdn_s[None])
```

```python
import functools
import math

import jax
import jax.numpy as jnp
from jax import lax
from jax.experimental import pallas as pl
from jax.experimental.pallas import tpu as pltpu

F32 = jnp.float32
BF16 = jnp.bfloat16
I32 = jnp.int32

HEAD_DIM = 128
NSA_WIDTH = 1024
GDN_WIDTH = 1024
NSA_HEADS = 8
NSA_KV_HEADS = 2
NSA_GROUP = 4
KV_WIDTH = 256
CMP_BLOCK = 32
SEL_BLOCK = 64
SEL_TOPN = 16
WINDOW = 512
GDN_HEADS = 8
CONV_WIDTH = 4
GDN_CHUNK = 64
PEER_HEADS = 8
PEER_NKEYS = 128
PEER_QDIM = 256
PEER_TOPK = 16
PAGE_SIZE = 128
NORM_EPS = 1e-6
NEG_INF = -1e30
FORCE_SCORE = 1e4
NEG_BIG = -3.0e38

LANE = 128
SUBLANE = 8
VMEM_LIMIT = 56 * 1024 * 1024

COL_KVC = 1024
COL_KVS = 1536
COL_KVW = 2048
COL_QKV = 2560
COL_Z = 5632
COL_MISC = 6656
N_PROJ = 7168
MISC_BL = 24
MISC_AL = 32

NN = (((1,), (0,)), ((), ()))
NT = (((1,), (1,)), ((), ()))
BNN = (((2,), (1,)), ((0,), (0,)))
BNT = (((2,), (2,)), ((0,), (0,)))


def _dot(a, b, dims=NN):
    return lax.dot_general(a, b, dims, preferred_element_type=F32)


def _split(x):
    hi = x.astype(BF16)
    lo = (x - hi.astype(F32)).astype(BF16)
    return hi, lo


def _mm1(a, b, dims=NN):
    return _dot(a.astype(BF16), b.astype(BF16), dims)


def _mm3(a, b, dims=NN):
    ah, al = _split(a)
    bh, bl = _split(b)
    return _dot(ah, bh, dims) + (_dot(al, bh, dims) + _dot(ah, bl, dims))


def _cparams(sem, vmem=VMEM_LIMIT):
    return pltpu.CompilerParams(dimension_semantics=sem, vmem_limit_bytes=vmem)


def _adaln_body(c_ref, w_ref, b_ref, o_ref):
    a = jax.nn.silu(c_ref[...])
    o_ref[...] = _mm3(a, w_ref[...]) + b_ref[...]


def _adaln(c_all, w, b):
    rows, d = c_all.shape
    n = w.shape[1]
    tn = 1024
    return pl.pallas_call(
        _adaln_body,
        out_shape=jax.ShapeDtypeStruct((rows, n), F32),
        grid=(n // tn,),
        in_specs=[pl.BlockSpec((rows, d), lambda j: (0, 0)),
                  pl.BlockSpec((d, tn), lambda j: (0, j)),
                  pl.BlockSpec((1, tn), lambda j: (0, j))],
        out_specs=pl.BlockSpec((rows, tn), lambda j: (0, j)),
        compiler_params=_cparams(("arbitrary",)),
        name="adaln",
    )(c_all, w, b.reshape(1, n))


def _seq_rows(m_ref, seq_len):
    nseq = m_ref.shape[0]
    if nseq == 1:
        return m_ref[0]
    return jnp.concatenate([jnp.broadcast_to(m_ref[r], (seq_len, m_ref.shape[2])) for r in range(nseq)], axis=0)


def _seq_spec(seq_len, tm, d):
    if tm <= seq_len:
        tpb = seq_len // tm
        return pl.BlockSpec((1, 1, d), lambda i, *_: (i // tpb, 0, 0))
    return pl.BlockSpec((tm // seq_len, 1, d), lambda i, *_: (i, 0, 0))


PROJ_TN = 512
PROJ_TILES_A = COL_QKV // PROJ_TN
PROJ_TILES_B = (COL_MISC - COL_QKV) // PROJ_TN


def _proj_in_body(x_ref, nw_ref, sc_ref, sh_ref, wa_ref, wb_ref, wc_ref, o_ref, cmp_ref, sel_ref, h_ref, *,
                  seq_len):
    j = pl.program_id(1)
    tm = x_ref.shape[0]

    @pl.when(j == 0)
    def _():
        x = x_ref[...]
        y = x * lax.rsqrt(jnp.mean(x * x, axis=-1, keepdims=True) + NORM_EPS)
        h = (y * nw_ref[...]) * (1.0 + _seq_rows(sc_ref, seq_len)) + _seq_rows(sh_ref, seq_len)
        h_ref[...] = h.astype(BF16)

    @pl.when(j < PROJ_TILES_A)
    def _():
        o_ref[...] = _dot(h_ref[...], wa_ref[...])

    @pl.when((j >= PROJ_TILES_A) & (j < PROJ_TILES_A + PROJ_TILES_B))
    def _():
        o_ref[...] = _dot(h_ref[...], wb_ref[...])

    @pl.when(j >= PROJ_TILES_A + PROJ_TILES_B)
    def _():
        o_ref[...] = _dot(h_ref[...], wc_ref[...])

    for dst, col in ((cmp_ref, COL_KVC), (sel_ref, COL_KVS)):
        @pl.when(j == col // PROJ_TN)
        def _(dst=dst, col=col):
            for c in range(TOKEN_ROWS):
                lo = col % PROJ_TN + c * HEAD_DIM
                dst[pl.ds(c, tm, stride=TOKEN_ROWS), :] = o_ref[:, lo:lo + HEAD_DIM]


def _proj_in(x, norm_w, sc, sh, w_all, w_gdn, w_misc, seq_len, tm):
    n, d = x.shape
    ncol = N_PROJ
    tn = PROJ_TN
    for col in (COL_KVC, COL_KVS):
        assert col % tn + 2 * KV_WIDTH <= tn, "a cached column group must not straddle column tiles"
    assert w_gdn.shape[1] == PROJ_TILES_B * tn and w_misc.shape[1] == tn
    clamp = lambda j, lo, hi: jnp.minimum(jnp.maximum(j - lo, 0), hi)
    mod = _seq_spec(seq_len, tm, d)
    rows = jax.ShapeDtypeStruct((n * TOKEN_ROWS, HEAD_DIM), F32)
    rows_spec = pl.BlockSpec((tm * TOKEN_ROWS, HEAD_DIM), lambda i, j: (i, 0))
    return pl.pallas_call(
        functools.partial(_proj_in_body, seq_len=seq_len),
        out_shape=(jax.ShapeDtypeStruct((n, ncol), F32), rows, rows),
        grid=(n // tm, ncol // tn),
        in_specs=[pl.BlockSpec((tm, d), lambda i, j: (i, 0)),
                  pl.BlockSpec((1, d), lambda i, j: (0, 0)),
                  mod, mod,
                  pl.BlockSpec((d, tn), lambda i, j: (0, clamp(j, 0, PROJ_TILES_A - 1))),
                  pl.BlockSpec((d, tn), lambda i, j: (0, clamp(j, PROJ_TILES_A, PROJ_TILES_B - 1))),
                  pl.BlockSpec((d, tn), lambda i, j: (0, 0))],
        out_specs=(pl.BlockSpec((tm, tn), lambda i, j: (i, j)), rows_spec, rows_spec),
        scratch_shapes=[pltpu.VMEM((tm, d), BF16)],
        compiler_params=_cparams(("arbitrary", "arbitrary")),
        name="proj_in",
    )(x, norm_w.reshape(1, d), sc, sh, w_all, w_gdn, w_misc)


def _compress_prompt_body(x_ref, pe_ref, w1_ref, b1_ref, w2_ref, o_ref, *, nblk):
    half = nblk // 2
    xs = []
    for l in range(CMP_BLOCK):
        xe = x_ref[pl.ds(l, half, stride=2 * CMP_BLOCK), :]
        xo = x_ref[pl.ds(CMP_BLOCK + l, half, stride=2 * CMP_BLOCK), :]
        xs.append((jnp.concatenate([xe, xo], axis=0) + pe_ref[0, l:l + 1, :]).astype(BF16))
    acc = _dot(jnp.concatenate(xs, axis=1), w1_ref[0].astype(BF16))
    hid = jax.nn.gelu(acc + b1_ref[0])
    o_ref[0, 0] = _mm1(hid, w2_ref[0])


def _compress_prompt(p2, pe, w1, b1, w2, nbatch, seq_len):
    nblk = seq_len // CMP_BLOCK
    kv0 = COL_KVC // LANE
    return pl.pallas_call(
        functools.partial(_compress_prompt_body, nblk=nblk),
        out_shape=jax.ShapeDtypeStruct((nbatch, 4, nblk, HEAD_DIM), F32),
        grid=(nbatch, 4),
        in_specs=[pl.BlockSpec((seq_len, LANE), lambda b, sh: (b, kv0 + sh)),
                  pl.BlockSpec((1, CMP_BLOCK, HEAD_DIM), lambda b, sh: (sh // 2, 0, 0)),
                  pl.BlockSpec((1, CMP_BLOCK * HEAD_DIM, HEAD_DIM), lambda b, sh: (sh // 2, 0, 0)),
                  pl.BlockSpec((1, 1, HEAD_DIM), lambda b, sh: (sh // 2, 0, 0)),
                  pl.BlockSpec((1, HEAD_DIM, HEAD_DIM), lambda b, sh: (sh // 2, 0, 0))],
        out_specs=pl.BlockSpec((1, 1, nblk, HEAD_DIM), lambda b, sh: (b, sh, 0, 0)),
        compiler_params=_cparams(("arbitrary", "arbitrary")),
        name="nsa_compress_prompt",
    )(p2, pe, w1.reshape(2, CMP_BLOCK * HEAD_DIM, HEAD_DIM), b1.reshape(2, 1, HEAD_DIM), w2)


TQ = 128
NSA_KSPAN = 512


def _nsa_prompt_body(slopes_ref, q_ref, gt_ref, ck_ref, cv_ref, sk_ref, sv_ref, wk_ref, wv_ref,
                     o_ref, selm_ref, *, seq_len):
    h = pl.program_id(1)
    qi = pl.program_id(2)
    t0 = qi * TQ
    nsb = seq_len // SEL_BLOCK
    ncb = seq_len // CMP_BLOCK // 2
    rows = NSA_GROUP * TQ

    qb = q_ref[...] * (HEAD_DIM ** -0.5)
    q4 = jnp.concatenate([qb[:, g * HEAD_DIM:(g + 1) * HEAD_DIM] for g in range(NSA_GROUP)], axis=0)
    q4b = q4.astype(BF16)
    row = lax.broadcasted_iota(I32, (rows, 1), 0)
    tq = t0 + (row & (TQ - 1))
    grow = row >> 7
    slope = jnp.zeros((rows, 1), F32)
    for g in range(NSA_GROUP):
        slope = jnp.where(grow == g, slopes_ref[h * NSA_GROUP + g], slope)

    ck = ck_ref[0, 0]
    cv = cv_ref[0, 0]
    lane_r = lax.broadcasted_iota(I32, (1, rows), 1)
    tq_l = t0 + (lane_r & (TQ - 1))
    slope_l = jnp.zeros((1, rows), F32)
    for g in range(NSA_GROUP):
        slope_l = jnp.where((lane_r >> 7) == g, slopes_ref[h * NSA_GROUP + g], slope_l)
    midx = lax.broadcasted_iota(I32, (ncb, 1), 0)
    logits = []
    valids = []
    for par in range(2):
        s_t = _mm3(ck[par * ncb:(par + 1) * ncb], q4, NT)
        cend = midx * (2 * CMP_BLOCK) + (CMP_BLOCK - 1) + par * CMP_BLOCK
        valid = cend <= tq_l
        dist = (tq_l - cend).astype(F32)
        logits.append(jnp.where(valid, s_t - slope_l * dist, NEG_INF))
        valids.append(valid)
    mx = jnp.maximum(jnp.max(logits[0], axis=0, keepdims=True), jnp.max(logits[1], axis=0, keepdims=True))
    e0 = jnp.exp(logits[0] - mx)
    e1 = jnp.exp(logits[1] - mx)
    den = jnp.sum(e0, axis=0, keepdims=True) + jnp.sum(e1, axis=0, keepdims=True)
    p0 = jnp.where(valids[0], e0 / den, 0.0)
    p1 = jnp.where(valids[1], e1 / den, 0.0)
    ident = (lax.broadcasted_iota(I32, (TQ, TQ), 0) == lax.broadcasted_iota(I32, (TQ, TQ), 1)).astype(BF16)
    p0b = p0.astype(BF16)
    p1b = p1.astype(BF16)
    o_cmp = jnp.concatenate(
        [_dot(_dot(ident, p0b[:, g * TQ:(g + 1) * TQ], NT).astype(BF16), cv[0:ncb].astype(BF16))
         + _dot(_dot(ident, p1b[:, g * TQ:(g + 1) * TQ], NT).astype(BF16), cv[ncb:2 * ncb].astype(BF16))
         for g in range(NSA_GROUP)], axis=0)
    pb = p0 + p1
    imp = pb[:, 0:TQ]
    for g in range(1, NSA_GROUP):
        imp = imp + pb[:, g * TQ:(g + 1) * TQ]

    cur = (t0 + lax.broadcasted_iota(I32, (1, TQ), 1)) >> 6
    jb = lax.broadcasted_iota(I32, (nsb, 1), 0)
    forced = (jb == 0) | (jb == cur) | (jb == cur - 1)
    score = jnp.where(jb <= cur, jnp.where(forced, FORCE_SCORE, imp), NEG_INF)
    cnt = jnp.zeros((nsb, TQ), I32)
    for i in range(nsb):
        si = score[i:i + 1, :]
        beats = (si > score) | ((si == score) & (jb > i))
        cnt = cnt + beats.astype(I32)
    sel_t = ((cnt < min(SEL_TOPN, nsb)) & (score > 0.5 * NEG_INF)).astype(BF16)
    sel = _dot(ident, sel_t, NT).astype(BF16)
    kpos = lax.broadcasted_iota(I32, (nsb, seq_len), 1)
    kblk = lax.broadcasted_iota(I32, (nsb, seq_len), 0)
    expand = ((kpos >> 6) == kblk).astype(BF16)
    selk = _dot(sel, expand)

    kw = NSA_KSPAN
    t_row = t0 + lax.broadcasted_iota(I32, (TQ, kw), 0)
    key = lax.broadcasted_iota(I32, (TQ, kw), 1)
    for st in range(seq_len // kw):
        blk = (selk[:, st * kw:(st + 1) * kw] - 1.0) * (-NEG_INF)
        selm_ref[st] = blk + jnp.where(st * kw + key <= t_row, 0.0, NEG_INF)
    m_floor = 0.1 * NEG_INF

    def span_update(carry, k, v, bias, key0):
        m, l, acc = carry
        width = k.shape[0]
        alibi = slope * lax.broadcasted_iota(I32, (1, width), 1).astype(F32)
        x = _dot(q4b, k, NT) + alibi
        x = (x.reshape(NSA_GROUP, TQ, width) + bias[None]).reshape(rows, width)
        shift = slope * (key0 - tq).astype(F32)
        m_new = jnp.maximum(m, jnp.max(x, axis=1, keepdims=True) + shift)
        alpha = jnp.exp(m - m_new)
        p = jnp.exp(x - (m_new - shift))
        l = alpha * l + jnp.sum(p, axis=1, keepdims=True)
        acc = alpha * acc + _dot(p.astype(BF16), v)
        return m_new, l, acc

    def finish(carry):
        _, l, acc = carry
        return jnp.where(l > 0.0, acc / jnp.where(l > 0.0, l, 1.0), 0.0)

    init = (jnp.full((rows, 1), m_floor, F32), jnp.zeros((rows, 1), F32), jnp.zeros((rows, HEAD_DIM), F32))

    def sel_step(st, carry):
        start = pl.multiple_of(st * kw, kw)
        return span_update(carry, sk_ref[pl.ds(start, kw), :].astype(BF16), sv_ref[pl.ds(start, kw), :].astype(BF16),
                           selm_ref[st], start)

    o_sel = finish(lax.fori_loop(0, (t0 + TQ - 1) // kw + 1, sel_step, init))

    ww = min(WINDOW + TQ, seq_len)
    wstart = pl.multiple_of(jnp.maximum(t0 + TQ - ww, 0), TQ)
    wdelta = (t0 + lax.broadcasted_iota(I32, (TQ, ww), 0)) - (wstart + lax.broadcasted_iota(I32, (TQ, ww), 1))
    wbias = jnp.where((wdelta >= 0) & (wdelta < WINDOW), 0.0, NEG_INF)
    o_win = finish(span_update(init, wk_ref[pl.ds(wstart, ww), :].astype(BF16), wv_ref[pl.ds(wstart, ww), :].astype(BF16),
                               wbias, wstart))

    gs = jax.nn.sigmoid(gt_ref[...])
    for g in range(NSA_GROUP):
        def gate(c):
            a = gs[:, g * 3 + c:g * 3 + c + 1]
            b = gs[:, NSA_GROUP * 3 + g * 3 + c:NSA_GROUP * 3 + g * 3 + c + 1]
            return jnp.where(h == 0, a, b)

        sl = slice(g * TQ, (g + 1) * TQ)
        o_ref[:, g * HEAD_DIM:(g + 1) * HEAD_DIM] = (gate(0) * o_cmp[sl] + gate(1) * o_sel[sl]) + gate(2) * o_win[sl]


def _nsa_prompt(p2, ckv, slopes, nbatch, seq_len):
    nq = seq_len // TQ
    kvs0 = COL_KVS // LANE
    kvw0 = COL_KVW // LANE
    misc = COL_MISC // LANE
    nblk = seq_len // CMP_BLOCK
    full = lambda c0: pl.BlockSpec((seq_len, LANE), lambda b, h, q, c0=c0: (b, c0 + h))
    return pl.pallas_call(
        functools.partial(_nsa_prompt_body, seq_len=seq_len),
        out_shape=jax.ShapeDtypeStruct((nbatch * seq_len, NSA_WIDTH), F32),
        grid=(nbatch, NSA_KV_HEADS, nq),
        in_specs=[pl.BlockSpec(memory_space=pltpu.SMEM),
                  pl.BlockSpec((TQ, NSA_GROUP * HEAD_DIM), lambda b, h, q: (b * nq + q, h)),
                  pl.BlockSpec((TQ, LANE), lambda b, h, q: (b * nq + q, misc)),
                  pl.BlockSpec((1, 1, nblk, HEAD_DIM), lambda b, h, q: (b, h, 0, 0)),
                  pl.BlockSpec((1, 1, nblk, HEAD_DIM), lambda b, h, q: (b, 2 + h, 0, 0)),
                  full(kvs0), full(kvs0 + 2), full(kvw0), full(kvw0 + 2)],
        out_specs=pl.BlockSpec((TQ, NSA_GROUP * HEAD_DIM), lambda b, h, q: (b * nq + q, h)),
        scratch_shapes=[pltpu.VMEM((seq_len // NSA_KSPAN, TQ, NSA_KSPAN), F32)],
        compiler_params=_cparams(("arbitrary", "arbitrary", "arbitrary")),
        name="nsa_prompt",
    )(slopes, p2, p2, ckv, ckv, p2, p2, p2, p2)


GDN_CB = 8


def _gdn_chunk(seq_len):
    return min(GDN_CHUNK, seq_len)


def _bmm3(a, b):
    ah, al = _split(a)
    bh, bl = _split(b)
    return _dot(jnp.concatenate([ah, al, ah], axis=2), jnp.concatenate([bh, bh, bl], axis=1), BNN)


GDN_HG = 2


def _gdn_body(alog_ref, dtb_ref, xq_ref, xk_ref, xv_ref, z_ref, gt_ref, cwq_ref, cwk_ref, cwv_ref,
              cbq_ref, cbk_ref, cbv_ref, s0_ref, gn_ref, o_ref, sout_ref,
              cat_s, q_s, k_s, v_s, b_s, g_s, u_s, w_s, qg_s, kdt_s, qk_s, gl_s, oc_s, *, seq_len, valid):
    hg = pl.program_id(1)
    C = _gdn_chunk(seq_len)
    nc = seq_len // C
    cb = min(GDN_CB, nc)
    row = lax.broadcasted_iota(I32, (seq_len, 1), 0)
    lane = lax.broadcasted_iota(I32, (1, LANE), 1)
    pos = row & (C - 1)
    gt = gt_ref[...]

    def conv(x_ref, cw_ref, cb_ref, ls):
        x = x_ref[:, ls]
        cat_s[0:SUBLANE, :] = cb_ref[0, :, ls]
        cat_s[SUBLANE:, :] = x
        w = cw_ref[:, ls]
        acc = cat_s[pl.ds(SUBLANE - 3, seq_len), :] * w[0:1]
        acc = acc + cat_s[pl.ds(SUBLANE - 2, seq_len), :] * w[1:2]
        acc = acc + cat_s[pl.ds(SUBLANE - 1, seq_len), :] * w[2:3]
        acc = acc + x * w[3:4]
        return jax.nn.silu(acc)

    def l2n(x):
        return x * lax.rsqrt(jnp.sum(x * x, axis=-1, keepdims=True) + NORM_EPS)

    ii = lax.broadcasted_iota(I32, (C, C), 0)
    jj = lax.broadcasted_iota(I32, (C, C), 1)
    tri = (ii >= jj)[None]
    strict = (ii > jj)[None]
    eye = (ii == jj).astype(F32)[None]
    ones_b = jnp.ones((cb, C, 3 * C), BF16)
    ident_b = jnp.broadcast_to((lax.broadcasted_iota(I32, (HEAD_DIM, HEAD_DIM), 0)
                                == lax.broadcasted_iota(I32, (HEAD_DIM, HEAD_DIM), 1)).astype(BF16)[None],
                               (cb, HEAD_DIM, HEAD_DIM))

    for hl in range(GDN_HG):
        hd = hg * GDN_HG + hl
        ls = slice(hl * HEAD_DIM, (hl + 1) * HEAD_DIM)
        q = l2n(conv(xq_ref, cwq_ref, cbq_ref, ls)) * (HEAD_DIM ** -0.5)
        k = l2n(conv(xk_ref, cwk_ref, cbk_ref, ls))
        v = conv(xv_ref, cwv_ref, cbv_ref, ls)
        bl = jnp.sum(jnp.where(lane == MISC_BL + hd, gt, 0.0), axis=1, keepdims=True)
        al = jnp.sum(jnp.where(lane == MISC_AL + hd, gt, 0.0), axis=1, keepdims=True)
        beta = jax.nn.sigmoid(bl)
        a_pos = jnp.exp(jnp.full((1, 1), alog_ref[hd], F32))
        g = -a_pos * jax.nn.softplus(al + dtb_ref[hd])
        if valid < seq_len:
            vm = row < valid
            q = jnp.where(vm, q, 0.0)
            k = jnp.where(vm, k, 0.0)
            v = jnp.where(vm, v, 0.0)
            beta = jnp.where(vm, beta, 0.0)
            g = jnp.where(vm, g, 0.0)
        gc = g
        sft = 1
        while sft < C:
            gc = gc + jnp.where(pos >= sft, pltpu.roll(gc, sft, 0), 0.0)
            sft *= 2
        q_s[...] = q
        k_s[...] = k
        v_s[...] = v
        b_s[...] = jnp.broadcast_to(beta, (seq_len, LANE))
        g_s[...] = jnp.broadcast_to(gc, (seq_len, LANE))

        def local(gi, carry, hl=hl):
            r0 = pl.multiple_of(gi * (cb * C), cb * C)
            sl = pl.ds(r0, cb * C)
            qc = q_s[sl, :].reshape(cb, C, HEAD_DIM)
            kc = k_s[sl, :].reshape(cb, C, HEAD_DIM)
            vc = v_s[sl, :].reshape(cb, C, HEAD_DIM)
            bc = b_s[sl, :].reshape(cb, C, LANE)
            gcc = g_s[sl, :].reshape(cb, C, LANE)
            gcol = gcc[:, :, 0:C]
            dg = gcol * eye
            d1 = dg.astype(BF16)
            r1 = dg - d1.astype(F32)
            d2 = r1.astype(BF16)
            d3 = (r1 - d2.astype(F32)).astype(BF16)
            grow = _dot(ones_b, jnp.concatenate([d1, d2, d3], axis=1), BNN)
            diff = gcol - grow
            decay = jnp.where(tri, jnp.exp(jnp.where(tri, diff, 0.0)), 0.0)
            kb = kc * bc
            a = _dot(kb.astype(BF16), kc.astype(BF16), BNT) * jnp.where(strict, decay, 0.0)
            tinv = eye - a
            pw = a
            n = 2
            while n < C:
                pw = _bmm3(pw, pw)
                tinv = tinv + _bmm3(tinv, pw)
                n *= 2
            eg = jnp.exp(gcc)
            rhs = jnp.concatenate([vc * bc, kb * eg], axis=2)
            sol = _bmm3(tinv, rhs)
            u_s[hl, sl, :] = sol[:, :, 0:HEAD_DIM].reshape(cb * C, HEAD_DIM)
            w_s[hl, sl, :] = sol[:, :, HEAD_DIM:2 * HEAD_DIM].reshape(cb * C, HEAD_DIM)
            qk = _dot(qc.astype(BF16), kc.astype(BF16), BNT) * decay
            qk_s[hl, sl, :] = qk.reshape(cb * C, C)
            qg_s[hl, sl, :] = (qc * eg).reshape(cb * C, HEAD_DIM)
            glast = gcc[:, C - 1:C, :]
            kd = kc * jnp.exp(glast - gcc)
            kdt = _dot(ident_b, kd.astype(BF16), BNT)
            kdt_s[hl, pl.ds(pl.multiple_of(gi * (cb * HEAD_DIM), cb * HEAD_DIM), cb * HEAD_DIM), :] = (
                kdt.reshape(cb * HEAD_DIM, C))
            gl_s[hl, pl.ds(pl.multiple_of(gi * (cb * SUBLANE), cb * SUBLANE), cb * SUBLANE), :] = (
                jnp.broadcast_to(jnp.exp(glast), (cb, SUBLANE, LANE)).reshape(cb * SUBLANE, LANE))
            return carry

        lax.fori_loop(0, nc // cb, local, 0)

    def scan(c, states):
        sl = pl.ds(pl.multiple_of(c * C, C), C)
        new = []
        for hl in range(GDN_HG):
            s = states[hl]
            sb = s.astype(BF16)
            v_new = u_s[hl, sl, :] - _dot(w_s[hl, sl, :].astype(BF16), sb)
            o_c = (_dot(qg_s[hl, sl, :].astype(BF16), sb)
                   + _dot(qk_s[hl, sl, :].astype(BF16), v_new.astype(BF16)))
            oc_s[hl, sl, :] = o_c
            kdt = kdt_s[hl, pl.ds(pl.multiple_of(c * HEAD_DIM, HEAD_DIM), HEAD_DIM), :]
            gl = gl_s[hl, pl.ds(pl.multiple_of(c * SUBLANE, SUBLANE), 1), :]
            new.append(s * gl + _dot(kdt.astype(BF16), v_new.astype(BF16)))
        return tuple(new)

    s_fin = lax.fori_loop(0, nc, scan, tuple(s0_ref[0, hl] for hl in range(GDN_HG)))
    for hl in range(GDN_HG):
        ls = slice(hl * HEAD_DIM, (hl + 1) * HEAD_DIM)
        sout_ref[0, hl] = s_fin[hl]
        o = oc_s[hl]
        y = o * lax.rsqrt(jnp.mean(o * o, axis=-1, keepdims=True) + NORM_EPS) * gn_ref[...]
        o_ref[:, ls] = y * jax.nn.silu(z_ref[:, ls])


def _gdn(p2, conv_w, conv_buf8, s0, a_log, dt_bias, gn_w, nbatch, seq_len, valid):
    gw = GDN_HG * LANE
    q0 = COL_QKV // gw
    z0 = COL_Z // gw
    hgs = GDN_HEADS // GDN_HG
    misc = COL_MISC // LANE
    C = _gdn_chunk(seq_len)
    col = lambda c0: pl.BlockSpec((seq_len, gw), lambda b, h, c0=c0: (b, c0 + h))
    cw = lambda c0: pl.BlockSpec((CONV_WIDTH, gw), lambda b, h, c0=c0: (0, c0 + h))
    cbs = lambda c0: pl.BlockSpec((1, SUBLANE, gw), lambda b, h, c0=c0: (b, 0, c0 + h))
    smem = pl.BlockSpec(memory_space=pltpu.SMEM)
    nc = seq_len // C
    per_head = lambda rows, cols: pltpu.VMEM((GDN_HG, rows, cols), F32)
    return pl.pallas_call(
        functools.partial(_gdn_body, seq_len=seq_len, valid=valid),
        out_shape=(jax.ShapeDtypeStruct((nbatch * seq_len, GDN_WIDTH), F32),
                   jax.ShapeDtypeStruct((nbatch, GDN_HEADS, HEAD_DIM, HEAD_DIM), F32)),
        grid=(nbatch, hgs),
        in_specs=[smem, smem, col(q0), col(q0 + hgs), col(q0 + 2 * hgs), col(z0),
                  pl.BlockSpec((seq_len, LANE), lambda b, h: (b, misc)),
                  cw(0), cw(hgs), cw(2 * hgs), cbs(0), cbs(hgs), cbs(2 * hgs),
                  pl.BlockSpec((1, GDN_HG, HEAD_DIM, HEAD_DIM), lambda b, h: (b, h, 0, 0)),
                  pl.BlockSpec((1, HEAD_DIM), lambda b, h: (0, 0))],
        out_specs=(pl.BlockSpec((seq_len, gw), lambda b, h: (b, h)),
                   pl.BlockSpec((1, GDN_HG, HEAD_DIM, HEAD_DIM), lambda b, h: (b, h, 0, 0))),
        scratch_shapes=[pltpu.VMEM((seq_len + SUBLANE, HEAD_DIM), F32)]
        + [pltpu.VMEM((seq_len, HEAD_DIM), F32)] * 3
        + [pltpu.VMEM((seq_len, LANE), F32)] * 2
        + [per_head(seq_len, HEAD_DIM)] * 3
        + [per_head(nc * HEAD_DIM, C),
           per_head(seq_len, C),
           per_head(nc * SUBLANE, LANE),
           per_head(seq_len, HEAD_DIM)],
        compiler_params=_cparams(("arbitrary", "arbitrary")),
        name="gdn",
    )(a_log, dt_bias, p2, p2, p2, p2, p2, conv_w, conv_w, conv_w, conv_buf8, conv_buf8, conv_buf8,
      s0, gn_w.reshape(1, HEAD_DIM))


def _out_proj_body(on_ref, og_ref, x_ref, w_ref, g1_ref, nw_ref, sc_ref, sh_ref, x1_ref, h2_ref, h2b_ref, *,
                   seq_len):
    mix = _dot(on_ref[...].astype(BF16), w_ref[0:NSA_WIDTH, :]) + _dot(og_ref[...].astype(BF16), w_ref[NSA_WIDTH:, :])
    x1 = x_ref[...] + _seq_rows(g1_ref, seq_len) * mix
    x1_ref[...] = x1
    y = x1 * lax.rsqrt(jnp.mean(x1 * x1, axis=-1, keepdims=True) + NORM_EPS)
    h2 = (y * nw_ref[...]) * (1.0 + _seq_rows(sc_ref, seq_len)) + _seq_rows(sh_ref, seq_len)
    h2_ref[...] = h2
    h2b_ref[...] = h2.astype(BF16)


def _out_proj(o_nsa, o_gdn, x, w_out_bf, g1, norm_w, sc, sh, seq_len, tm):
    n, d = x.shape
    mod = _seq_spec(seq_len, tm, d)
    return pl.pallas_call(
        functools.partial(_out_proj_body, seq_len=seq_len),
        out_shape=(jax.ShapeDtypeStruct((n, d), F32), jax.ShapeDtypeStruct((n, d), F32),
                   jax.ShapeDtypeStruct((n, d), BF16)),
        grid=(n // tm,),
        in_specs=[pl.BlockSpec((tm, NSA_WIDTH), lambda i: (i, 0)),
                  pl.BlockSpec((tm, GDN_WIDTH), lambda i: (i, 0)),
                  pl.BlockSpec((tm, d), lambda i: (i, 0)),
                  pl.BlockSpec((d, d), lambda i: (0, 0)),
                  mod,
                  pl.BlockSpec((1, d), lambda i: (0, 0)),
                  mod, mod],
        out_specs=tuple(pl.BlockSpec((tm, d), lambda i: (i, 0)) for _ in range(3)),
        compiler_params=_cparams(("arbitrary",)),
        name="out_proj",
    )(o_nsa, o_gdn, x, w_out_bf, g1, norm_w.reshape(1, d), sc, sh)


def _take_top(src_ref, dst_ref):
    s = src_ref[...]
    above = jnp.full(s.shape, float(PEER_TOPK), F32)
    for k in range(PEER_TOPK):
        m = jnp.max(s, axis=0, keepdims=True)
        hit = s == m
        above = jnp.where(hit, float(k), above)
        s = jnp.where(hit, NEG_BIG, s)
        dst_ref[k:k + 1, :] = m
    return above


PAIR_ROWS = ((0, 16, 16), (16, 8, 8), (24, 8, 5), (32, 8, 4), (40, 8, 3), (48, 8, 2), (56, 8, 2), (64, 8, 2))
PAIR_TAIL = 72
PAIR_TOTAL = 80


def _peer_sel_body(h_ref, wqh_ref, wql_ref, keys_ref, cnt_ref, al_ref, code_ref, be_ref,
                   qt_s, sc_s, ta_s, tb_s, cand_s, tc_s, *, split_query):
    hh, hl = _split(h_ref[...])
    half = PEER_QDIM // 2
    tn = h_ref.shape[0]
    r8 = lax.broadcasted_iota(I32, (SUBLANE, tn), 0)
    qt_all = _dot(wqh_ref[...], hh, NT)
    if split_query:
        qt_all = qt_all + (_dot(wql_ref[...], hh, NT) + _dot(wqh_ref[...], hl, NT))
    qt_s[...] = qt_all
    for hd in range(PEER_HEADS):
        qt = qt_s[hd * PEER_QDIM:(hd + 1) * PEER_QDIM, :]
        s0 = _mm3(keys_ref[0], qt[0:half])
        s1 = _mm3(keys_ref[1], qt[half:PEER_QDIM])
        sc_s[...] = s0
        _take_top(sc_s, ta_s)
        sc_s[...] = s1
        code = _take_top(sc_s, tb_s)
        a = ta_s[...]
        b = tb_s[...]
        for r, (row0, nrow, nval) in enumerate(PAIR_ROWS):
            pair = a[r:r + 1] + b[0:nrow]
            cand_s[row0:row0 + nrow, :] = pair if nval == nrow else jnp.where(r8 < nval, pair, NEG_BIG)
        cand_s[PAIR_TAIL:PAIR_TOTAL, :] = a[SUBLANE:2 * SUBLANE] + b[0:1]
        _take_top(cand_s, tc_s)
        tau = tc_s[PEER_TOPK - 1:PEER_TOPK, :]
        cand = cand_s[...]
        keep = cand >= tau
        zsum = jnp.sum(jnp.where(keep, jnp.exp(cand - cand[0:1]), 0.0), axis=0, keepdims=True)
        keepf = keep.astype(F32)
        cnt = jnp.zeros((PEER_NKEYS, tn), F32)
        for r, (row0, nrow, nval) in enumerate(PAIR_ROWS):
            cnt_r = jnp.sum(keepf[row0:row0 + nrow], axis=0, keepdims=True)
            cnt = jnp.where(s0 == a[r:r + 1], cnt_r, cnt)
        for r in range(SUBLANE, 2 * SUBLANE):
            cnt = jnp.where(s0 == a[r:r + 1], keepf[PAIR_TAIL + r - SUBLANE:PAIR_TAIL + r - SUBLANE + 1], cnt)
        cnt_ref[hd] = cnt
        al_ref[hd] = jnp.exp(s0 - a[0:1]) / zsum
        code_ref[hd] = code.astype(BF16)
        be_ref[hd] = jnp.exp(s1 - b[0:1]).astype(BF16)


def _peer_select(h2, wq_hi, wq_lo, keys, tn, split_query):
    n, d = h2.shape
    hk = jax.ShapeDtypeStruct((PEER_HEADS, PEER_NKEYS, n), F32)
    hk16 = jax.ShapeDtypeStruct((PEER_HEADS, PEER_NKEYS, n), BF16)
    blk = pl.BlockSpec((PEER_HEADS, PEER_NKEYS, tn), lambda i: (0, 0, i))
    return pl.pallas_call(
        functools.partial(_peer_sel_body, split_query=split_query),
        out_shape=(hk, hk, hk16, hk16),
        grid=(n // tn,),
        in_specs=[pl.BlockSpec((tn, d), lambda i: (i, 0)),
                  pl.BlockSpec((PEER_HEADS * PEER_QDIM, d), lambda i: (0, 0)),
                  pl.BlockSpec((PEER_HEADS * PEER_QDIM, d), lambda i: (0, 0)),
                  pl.BlockSpec((2, PEER_NKEYS, PEER_QDIM // 2), lambda i: (0, 0, 0))],
        out_specs=(blk, blk, blk, blk),
        scratch_shapes=[pltpu.VMEM((PEER_HEADS * PEER_QDIM, tn), F32),
                        pltpu.VMEM((PEER_NKEYS, tn), F32), pltpu.VMEM((PEER_TOPK, tn), F32),
                        pltpu.VMEM((PEER_TOPK, tn), F32), pltpu.VMEM((PAIR_TOTAL, tn), F32),
                        pltpu.VMEM((PEER_TOPK, tn), F32)],
        compiler_params=_cparams(("arbitrary",)),
        name="peer_select",
    )(h2, wq_hi, wq_lo, keys)


PEER_TE = 1024
GELU_C0 = math.sqrt(2.0 / math.pi)
GELU_C1 = 0.044715 * math.sqrt(2.0 / math.pi)


def _gelu_tanh(x):
    hx = 0.5 * x
    return hx + hx * jnp.tanh(x * (GELU_C0 + GELU_C1 * (x * x)))


def _peer_mm_body(h_ref, u_ref, vt_ref, cnt_ref, al_ref, code_ref, be_ref, o_ref, at_s, p_s):
    e = pl.program_id(1)

    @pl.when(e == 0)
    def _():
        o_ref[...] = jnp.zeros_like(o_ref)

    at_s[...] = _dot(u_ref[...], h_ref[...], NT)
    tn = h_ref.shape[0]
    for ii in range(PEER_TE // PEER_NKEYS):
        rs = slice(ii * PEER_NKEYS, (ii + 1) * PEER_NKEYS)
        for ck in range(tn // LANE):
            cs = slice(ck * LANE, (ck + 1) * LANE)
            w = jnp.zeros((PEER_NKEYS, LANE), BF16)
            for hd in range(PEER_HEADS):
                cnt = cnt_ref[hd, ii:ii + 1, cs].astype(BF16)
                kept = jnp.where(code_ref[hd, :, cs] < cnt, be_ref[hd, :, cs], jnp.zeros((), BF16))
                w = w + al_ref[hd, ii:ii + 1, cs].astype(BF16) * kept
            p_s[rs, cs] = w * _gelu_tanh(at_s[rs, cs].astype(BF16))
    o_ref[...] += _dot(vt_ref[...], p_s[...])


def _peer_mm(h2b, u_bf, vt_bf, cnt, al, code, be, tn):
    n, d = h2b.shape
    ne = u_bf.shape[0]
    rows_i = PEER_TE // PEER_NKEYS
    sel_i = pl.BlockSpec((PEER_HEADS, rows_i, tn), lambda i, e: (0, e, i))
    once = pl.Buffered(1)
    sel_j = pl.BlockSpec((PEER_HEADS, PEER_NKEYS, tn), lambda i, e: (0, 0, i), pipeline_mode=once)
    return pl.pallas_call(
        _peer_mm_body,
        out_shape=jax.ShapeDtypeStruct((d, n), F32),
        grid=(n // tn, ne // PEER_TE),
        in_specs=[pl.BlockSpec((tn, d), lambda i, e: (i, 0), pipeline_mode=once),
                  pl.BlockSpec((PEER_TE, d), lambda i, e: (e, 0)),
                  pl.BlockSpec((d, PEER_TE), lambda i, e: (0, e)),
                  sel_i, sel_i, sel_j, sel_j],
        out_specs=pl.BlockSpec((d, tn), lambda i, e: (0, i)),
        scratch_shapes=[pltpu.VMEM((PEER_TE, tn), F32), pltpu.VMEM((PEER_TE, tn), BF16)],
        compiler_params=_cparams(("arbitrary", "arbitrary")),
        name="peer_experts",
    )(h2b, u_bf, vt_bf, cnt, al, code, be)


def _final_body(x1_ref, pt_ref, g2_ref, fw_ref, y_ref, *, seq_len):
    x2 = x1_ref[...] + _seq_rows(g2_ref, seq_len) * pt_ref[...].T
    y_ref[...] = x2 * lax.rsqrt(jnp.mean(x2 * x2, axis=-1, keepdims=True) + NORM_EPS) * fw_ref[...]


def _final(x1, peer_t, g2, fw, seq_len, tm):
    n, d = x1.shape
    g2_spec = _seq_spec(seq_len, tm, d)
    return pl.pallas_call(
        functools.partial(_final_body, seq_len=seq_len),
        out_shape=jax.ShapeDtypeStruct((n, d), F32),
        grid=(n // tm,),
        in_specs=[pl.BlockSpec((tm, d), lambda i: (i, 0)),
                  pl.BlockSpec((d, tm), lambda i: (0, i)),
                  g2_spec,
                  pl.BlockSpec((1, d), lambda i: (0, 0))],
        out_specs=pl.BlockSpec((tm, d), lambda i: (i, 0)),
        compiler_params=_cparams(("arbitrary",)),
        name="final_norm",
    )(x1, peer_t, g2, fw.reshape(1, d))


SROWS = 64
SROWS_TAIL = 16
CMP_PAGES = 16
KV_COMP = 2 * NSA_KV_HEADS
TOKEN_ROWS = KV_COMP
PAGE_VROWS = PAGE_SIZE * TOKEN_ROWS
CMP_PAIRS = CMP_BLOCK * TOKEN_ROWS // SUBLANE
BLOCK_VROWS = CMP_BLOCK * TOKEN_ROWS


def _compress_sample_body(pt_ref, *refs):
    pages = refs[:CMP_PAGES]
    pe_ref, w1_ref, b1_ref, w2_ref, o_ref, acc_s = refs[CMP_PAGES:]
    bpp = PAGE_SIZE // CMP_BLOCK
    nrow = CMP_PAGES * bpp * SUBLANE
    nh = CMP_PAGES * bpp // 2
    lhs = []
    for q in range(CMP_PAIRS):
        parts = [r[0, n * BLOCK_VROWS + q * SUBLANE:n * BLOCK_VROWS + (q + 1) * SUBLANE, :]
                 for r in pages for n in range(bpp)]
        lhs.append((jnp.concatenate(parts, axis=0)
                    + jnp.concatenate([pe_ref[q]] * (nrow // SUBLANE), axis=0)).astype(BF16))
    acc = _dot(jnp.concatenate(lhs, axis=1), w1_ref[...])
    rid = lax.broadcasted_iota(I32, (nrow, 1), 0) & (SUBLANE - 1)
    want = ((rid & (KV_COMP - 1)) >> 1) * 2 + (rid >> 2)
    picked = acc[:, 0:HEAD_DIM]
    for blk in range(1, KV_COMP):
        picked = jnp.where(want == blk, acc[:, blk * HEAD_DIM:(blk + 1) * HEAD_DIM], picked)
    acc_s[...] = picked + pltpu.roll(picked, nrow - KV_COMP, 0)
    for c in range(KV_COMP):
        s = c // NSA_KV_HEADS
        hid_e = acc_s[pl.ds(c, nh, stride=2 * SUBLANE), :]
        hid_o = acc_s[pl.ds(SUBLANE + c, nh, stride=2 * SUBLANE), :]
        hid = jax.nn.gelu(jnp.concatenate([hid_e, hid_o], axis=0) + b1_ref[s])
        out = _mm1(hid, w2_ref[s])
        o_ref[0, c, 0] = out[0:nh]
        o_ref[0, c, 1] = out[nh:2 * nh]


def _compress_sample(cache, page_table, pe, w1, b1, w2):
    nbs, n_pages = page_table.shape
    n_phys = cache.shape[0]
    cache_v = cache.reshape(n_phys, PAGE_VROWS, HEAD_DIM)
    ngrp = n_pages // CMP_PAGES
    nblk_half = n_pages * PAGE_SIZE // CMP_BLOCK // 2
    nh = CMP_PAGES * (PAGE_SIZE // CMP_BLOCK) // 2
    pe_q = jnp.transpose(pe.reshape(2, CMP_PAIRS, 2, HEAD_DIM), (1, 2, 0, 3))
    pe_q = jnp.broadcast_to(pe_q[:, :, :, None, :], (CMP_PAIRS, 2, 2, NSA_KV_HEADS, HEAD_DIM))
    pe_q = pe_q.reshape(CMP_PAIRS, SUBLANE, HEAD_DIM)
    w1_q = jnp.transpose(w1.reshape(2, CMP_PAIRS, 2, HEAD_DIM, HEAD_DIM), (1, 3, 0, 2, 4))
    w1_q = w1_q.reshape(CMP_PAIRS * HEAD_DIM, KV_COMP * HEAD_DIM).astype(BF16)
    page_spec = lambda k: pl.BlockSpec((1, PAGE_VROWS, HEAD_DIM),
                                       lambda b, g, pt, k=k: (pt[b * n_pages + g * CMP_PAGES + k], 0, 0))
    const = lambda shape: pl.BlockSpec(shape, lambda b, g, pt: (0,) * len(shape))
    grid_spec = pltpu.PrefetchScalarGridSpec(
        num_scalar_prefetch=1,
        grid=(nbs, ngrp),
        in_specs=[page_spec(k) for k in range(CMP_PAGES)]
        + [const((CMP_PAIRS, SUBLANE, HEAD_DIM)), const((CMP_PAIRS * HEAD_DIM, KV_COMP * HEAD_DIM)),
           const((2, 1, HEAD_DIM)), const((2, HEAD_DIM, HEAD_DIM))],
        out_specs=pl.BlockSpec((1, KV_COMP, 2, nh, HEAD_DIM), lambda b, g, pt: (b, 0, 0, g, 0)),
        scratch_shapes=[pltpu.VMEM((CMP_PAGES * (PAGE_SIZE // CMP_BLOCK) * SUBLANE, HEAD_DIM), F32)])
    return pl.pallas_call(
        _compress_sample_body,
        out_shape=jax.ShapeDtypeStruct((nbs, KV_COMP, 2, nblk_half, HEAD_DIM), F32),
        grid_spec=grid_spec,
        compiler_params=_cparams(("arbitrary", "arbitrary")),
        name="nsa_compress_sample",
    )(page_table.reshape(-1), *([cache_v] * CMP_PAGES), pe_q, w1_q, b1.reshape(2, 1, HEAD_DIM), w2)


def _rows_from_lanes(row, ngrp):
    ridx = lax.broadcasted_iota(I32, (SUBLANE, HEAD_DIM), 0)
    out = jnp.zeros((SUBLANE, HEAD_DIM), F32)
    for g in range(ngrp):
        out = jnp.where(ridx == g, jnp.broadcast_to(row[:, g * HEAD_DIM:(g + 1) * HEAD_DIM], (SUBLANE, HEAD_DIM)), out)
    return out


def _slope_rows(slopes_ref, h):
    ridx = lax.broadcasted_iota(I32, (SUBLANE, 1), 0)
    slope = jnp.zeros((SUBLANE, 1), F32)
    for g in range(NSA_GROUP):
        slope = jnp.where(ridx == g, slopes_ref[h * NSA_GROUP + g], slope)
    return slope


def _nsa_sample_select_body(slopes_ref, q_ref, ckv_ref, ocmp_ref, idx_ref, *, past):
    ncb = past // CMP_BLOCK // 2
    nblk = past // SEL_BLOCK
    ridx = lax.broadcasted_iota(I32, (SUBLANE, 1), 0)
    midx = lax.broadcasted_iota(I32, (1, ncb), 1)
    jb = lax.broadcasted_iota(I32, (1, nblk), 1)
    slot = lax.broadcasted_iota(I32, (1, SEL_TOPN), 1)
    for h in range(NSA_KV_HEADS):
        q8 = _rows_from_lanes(q_ref[0:1, h * NSA_GROUP * HEAD_DIM:(h + 1) * NSA_GROUP * HEAD_DIM], NSA_GROUP)
        q8 = q8 * (HEAD_DIM ** -0.5)
        slope = _slope_rows(slopes_ref, h)
        lg = []
        for par in range(2):
            s_c = _mm3(q8, ckv_ref[0, h, par], NT)
            cend = midx * (2 * CMP_BLOCK) + (CMP_BLOCK - 1) + par * CMP_BLOCK
            lg.append(s_c - slope * (past - cend).astype(F32))
        mx = jnp.maximum(jnp.max(lg[0], axis=1, keepdims=True), jnp.max(lg[1], axis=1, keepdims=True))
        e0 = jnp.exp(lg[0] - mx)
        e1 = jnp.exp(lg[1] - mx)
        den = jnp.sum(e0, axis=1, keepdims=True) + jnp.sum(e1, axis=1, keepdims=True)
        p0 = e0 / den
        p1 = e1 / den
        o_cmp = _mm1(p0, ckv_ref[0, 2 + h, 0]) + _mm1(p1, ckv_ref[0, 2 + h, 1])
        ocmp_ref[0, h * NSA_GROUP:(h + 1) * NSA_GROUP, :] = o_cmp[0:NSA_GROUP]
        imp = jnp.sum(jnp.where(ridx < NSA_GROUP, p0 + p1, 0.0), axis=0, keepdims=True)
        forced = (jb == 0) | (jb == nblk - 1)
        score = jnp.where(forced, FORCE_SCORE, imp)
        picks = jnp.full((1, SEL_TOPN), nblk, I32)
        for k in range(SEL_TOPN - 1):
            m = jnp.max(score, axis=1, keepdims=True)
            first = jnp.min(jnp.where(score == m, jb, nblk), axis=1, keepdims=True)
            score = jnp.where(jb == first, NEG_BIG, score)
            picks = jnp.where(slot == k, first, picks)
        idx_ref[0, h:h + 1, :] = picks


def _nsa_sample_select(p2s, ckv_s, slopes, nbs, past):
    nblk_half = past // CMP_BLOCK // 2
    return pl.pallas_call(
        functools.partial(_nsa_sample_select_body, past=past),
        out_shape=(jax.ShapeDtypeStruct((nbs, NSA_HEADS, HEAD_DIM), F32),
                   jax.ShapeDtypeStruct((nbs, NSA_KV_HEADS, SEL_TOPN), I32)),
        grid=(nbs,),
        in_specs=[pl.BlockSpec(memory_space=pltpu.SMEM),
                  pl.BlockSpec((SUBLANE, NSA_WIDTH), lambda b: (b * (SROWS // SUBLANE), 0)),
                  pl.BlockSpec((1, 4, 2, nblk_half, HEAD_DIM), lambda b: (b, 0, 0, 0, 0))],
        out_specs=(pl.BlockSpec((1, NSA_HEADS, HEAD_DIM), lambda b: (b, 0, 0)),
                   pl.BlockSpec((1, NSA_KV_HEADS, SEL_TOPN), lambda b: (b, 0, 0))),
        compiler_params=_cparams(("arbitrary",)),
        name="nsa_sample_select",
    )(slopes, p2s, ckv_s)


NSEL_PAST = SEL_TOPN - 1


def _nsa_sample_attend_body(idx_ref, pt_ref, slopes_ref, q_ref, gt_ref, ocmp_ref, skn_ref, svn_ref, wkn_ref, wvn_ref,
                            *refs, past, n_pages):
    blocks = refs[:NSEL_PAST]
    wc_ref, o_ref = refs[NSEL_PAST:]
    b = pl.program_id(0)
    h = pl.program_id(1)
    q8 = _rows_from_lanes(q_ref[0:1, :], NSA_GROUP) * (HEAD_DIM ** -0.5)
    q8b = q8.astype(BF16)
    slope = _slope_rows(slopes_ref, h)

    def attend_with_new(k_old, v_old, lg_bias, k_new, v_new):
        lg = _dot(q8b, k_old.astype(BF16), NT) + lg_bias
        s_new = jnp.sum(q8 * k_new, axis=1, keepdims=True)
        m = jnp.maximum(jnp.max(lg, axis=1, keepdims=True), s_new)
        p = jnp.exp(lg - m)
        pn = jnp.exp(s_new - m)
        den = jnp.sum(p, axis=1, keepdims=True) + pn
        return (_dot(p.astype(BF16), v_old.astype(BF16)) + pn * v_new) / den

    k_sel = jnp.concatenate([r[0, pl.ds(h, SEL_BLOCK, stride=TOKEN_ROWS), :] for r in blocks], axis=0)
    v_sel = jnp.concatenate([r[0, pl.ds(NSA_KV_HEADS + h, SEL_BLOCK, stride=TOKEN_ROWS), :] for r in blocks], axis=0)
    lane = lax.broadcasted_iota(I32, (1, NSEL_PAST * SEL_BLOCK), 1)
    slot = lane >> (SEL_BLOCK.bit_length() - 1)
    blk = jnp.zeros((1, NSEL_PAST * SEL_BLOCK), I32)
    for j in range(NSEL_PAST):
        blk = jnp.where(slot == j, idx_ref[(b * NSA_KV_HEADS + h) * SEL_TOPN + j], blk)
    dist = (past - (blk * SEL_BLOCK + (lane & (SEL_BLOCK - 1)))).astype(F32)
    o_sel = attend_with_new(k_sel, v_sel, -slope * dist, skn_ref[0:1, :], svn_ref[0:1, :])

    nw = wc_ref.shape[1] // TOKEN_ROWS
    r = lax.broadcasted_iota(I32, (1, nw), 1)
    delta = nw - r
    wmask = (delta < WINDOW) & (past - delta >= 0)
    k_win = wc_ref[0, pl.ds(h, nw, stride=TOKEN_ROWS), :]
    v_win = wc_ref[0, pl.ds(NSA_KV_HEADS + h, nw, stride=TOKEN_ROWS), :]
    o_win = attend_with_new(k_win, v_win, jnp.where(wmask, -slope * delta.astype(F32), NEG_INF),
                            wkn_ref[0:1, :], wvn_ref[0:1, :])

    gs = jax.nn.sigmoid(gt_ref[0:1, :])
    o_ref[...] = jnp.zeros_like(o_ref)
    for g in range(NSA_GROUP):
        def gate(c):
            a = gs[:, g * 3 + c:g * 3 + c + 1]
            bb = gs[:, NSA_GROUP * 3 + g * 3 + c:NSA_GROUP * 3 + g * 3 + c + 1]
            return jnp.where(h == 0, a, bb)

        oc = jnp.where(h == 0, ocmp_ref[0, g:g + 1, :], ocmp_ref[0, NSA_GROUP + g:NSA_GROUP + g + 1, :])
        og = (gate(0) * oc + gate(1) * o_sel[g:g + 1]) + gate(2) * o_win[g:g + 1]
        o_ref[0:1, g * HEAD_DIM:(g + 1) * HEAD_DIM] = og


def _nsa_sample_attend(p2s, ocmp, idx, page_table, cache_sel, win_state, slopes, nbs, past):
    n_pages = page_table.shape[1]
    n_phys = cache_sel.shape[0]
    nw = win_state.shape[1]
    halves = PAGE_SIZE // SEL_BLOCK
    sel_v = cache_sel.reshape(n_phys * halves, SEL_BLOCK * TOKEN_ROWS, HEAD_DIM)
    win_v = win_state.reshape(nbs, nw * TOKEN_ROWS, HEAD_DIM)
    rb = SROWS // SUBLANE
    kvs0 = COL_KVS // LANE
    kvw0 = COL_KVW // LANE
    misc = COL_MISC // LANE

    def sel_spec(j):
        def index(b, h, idx, pt):
            blk = idx[(b * NSA_KV_HEADS + h) * SEL_TOPN + j]
            page = pt[b * n_pages + blk // halves]
            return (page * halves + blk % halves, 0, 0)
        return pl.BlockSpec((1, SEL_BLOCK * TOKEN_ROWS, HEAD_DIM), index)

    new = lambda c0: pl.BlockSpec((SUBLANE, LANE), lambda b, h, idx, pt, c0=c0: (b * rb, c0 + h))
    grid_spec = pltpu.PrefetchScalarGridSpec(
        num_scalar_prefetch=2,
        grid=(nbs, NSA_KV_HEADS),
        in_specs=[pl.BlockSpec(memory_space=pltpu.SMEM),
                  pl.BlockSpec((SUBLANE, NSA_GROUP * HEAD_DIM), lambda b, h, idx, pt: (b * rb, h)),
                  pl.BlockSpec((SUBLANE, LANE), lambda b, h, idx, pt: (b * rb, misc)),
                  pl.BlockSpec((1, NSA_HEADS, HEAD_DIM), lambda b, h, idx, pt: (b, 0, 0)),
                  new(kvs0), new(kvs0 + 2), new(kvw0), new(kvw0 + 2)]
        + [sel_spec(j) for j in range(NSEL_PAST)]
        + [pl.BlockSpec((1, nw * TOKEN_ROWS, HEAD_DIM), lambda b, h, idx, pt: (b, 0, 0))],
        out_specs=pl.BlockSpec((SROWS, NSA_GROUP * HEAD_DIM), lambda b, h, idx, pt: (b, h)))
    return pl.pallas_call(
        functools.partial(_nsa_sample_attend_body, past=past, n_pages=n_pages),
        out_shape=jax.ShapeDtypeStruct((nbs * SROWS, NSA_WIDTH), F32),
        grid_spec=grid_spec,
        compiler_params=_cparams(("arbitrary", "arbitrary")),
        name="nsa_sample_attend",
    )(idx.reshape(-1), page_table.reshape(-1), slopes, p2s, p2s, ocmp, p2s, p2s, p2s, p2s,
      *([sel_v] * NSEL_PAST), win_v)


def _transpose_cast_body(x_ref, o_ref):
    o_ref[...] = x_ref[...].T.astype(BF16)


def _transpose_to_bf16(x):
    rows, cols = x.shape
    tr = PEER_TE
    return pl.pallas_call(
        _transpose_cast_body,
        out_shape=jax.ShapeDtypeStruct((cols, rows), BF16),
        grid=(rows // tr,),
        in_specs=[pl.BlockSpec((tr, cols), lambda i: (i, 0))],
        out_specs=pl.BlockSpec((cols, tr), lambda i: (0, i)),
        compiler_params=_cparams(("arbitrary",)),
        name="transpose_cast",
    )(x)


def _split_w_in(w_in):
    d = w_in.shape[0]
    w_bf = w_in.astype(BF16)
    g0 = NSA_WIDTH + 3 * 2 * KV_WIDTH
    q0 = g0 + 3 * NSA_HEADS
    b0 = q0 + 3 * GDN_WIDTH + GDN_WIDTH
    n_misc = (q0 - g0) + 2 * GDN_HEADS
    w_misc = jnp.concatenate([w_bf[:, g0:q0], w_bf[:, b0:], jnp.zeros((d, N_PROJ - COL_MISC - n_misc), BF16)], axis=1)
    return w_bf, w_bf[:, q0:b0], w_misc


def _split_bf16(w):
    hi = w.astype(BF16)
    lo = (w - hi.astype(F32)).astype(BF16)
    return hi, lo


def _mixer_tail(x, o_nsa, o_gdn, w_out_bf, g1, norm_ffn, sc2, sh2, g2, wq_hi, wq_lo, keys, u_bf, vt_bf,
                final_norm, seq_len, tm, tn_sel, tn_mm, tm_final):
    x1, h2, h2b = _out_proj(o_nsa, o_gdn, x, w_out_bf, g1, norm_ffn, sc2, sh2, seq_len, tm)
    cnt, al, code, be = _peer_select(h2, wq_hi, wq_lo, keys, tn_sel, split_query=h2.shape[0] <= tn_sel)
    peer_t = _peer_mm(h2b, u_bf, vt_bf, cnt, al, code, be, tn_mm)
    return _final(x1, peer_t, g2, final_norm, seq_len, tm_final)


def kernel(x_prompt, x_sample, cache_cmp_kv, cache_sel_kv, state_win_kv, state_conv, state_gdn, page_table,
           c_prompt, c_sample, w_ada, b_ada, norm_mix, norm_ffn, w_in, cmp_pe, cmp_w1, cmp_b1, cmp_w2, conv_w,
           gdn_a_log, gdn_dt_bias, gdn_norm, w_out, peer_wq, peer_keys, peer_u, peer_v, final_norm):
    nb, seq, d = x_prompt.shape
    nbs, dec_seq, _ = x_sample.shape
    assert w_in.shape[0] == 1 and dec_seq == 1, "single layer, single decode token"
    past = page_table.shape[1] * PAGE_SIZE
    slopes = 2.0 ** (-8.0 * jnp.arange(1, NSA_HEADS + 1, dtype=F32) / NSA_HEADS)
    rows_c = 16
    c_all = jnp.concatenate([c_prompt, c_sample, jnp.zeros((rows_c - nb - nbs, d), F32)], axis=0)
    mod = _adaln(c_all, w_ada[0], b_ada[0]).reshape(rows_c, 6, d)
    pm = lambda k: mod[0:nb, k][:, None, :]
    sm = lambda k: mod[nb:nb + nbs, k][:, None, :]
    w_proj = _split_w_in(w_in[0])
    w_out_bf = w_out[0].astype(BF16)
    wq_hi, wq_lo = _split_bf16(peer_wq[0].T)
    u_bf = peer_u[0].astype(BF16)
    vt_bf = _transpose_to_bf16(peer_v[0])
    cw = (cmp_pe[0], cmp_w1[0], cmp_b1[0], cmp_w2[0])
    kvw = 2 * KV_WIDTH

    xp = x_prompt.reshape(nb * seq, d)
    p2, cmp_rows, sel_rows = _proj_in(xp, norm_mix[0], pm(1), pm(0), *w_proj, seq, 1024)
    ckv = _compress_prompt(p2, *cw, nb, seq)
    o_nsa = _nsa_prompt(p2, ckv, slopes, nb, seq)
    conv0 = jnp.zeros((nb, SUBLANE, 3 * GDN_WIDTH), F32)
    s0 = jnp.zeros((nb, GDN_HEADS, HEAD_DIM, HEAD_DIM), F32)
    o_gdn, gdn_p = _gdn(p2, conv_w[0], conv0, s0, gdn_a_log[0], gdn_dt_bias[0], gdn_norm[0], nb, seq, seq)
    y_prompt = _mixer_tail(xp, o_nsa, o_gdn, w_out_bf, pm(2), norm_ffn[0], pm(4), pm(3), pm(5), wq_hi, wq_lo,
                           peer_keys[0], u_bf, vt_bf, final_norm, seq, 512, 256, 1024, 512).reshape(nb, seq, d)
    p3 = p2.reshape(nb, seq, N_PROJ)
    keep = min(WINDOW, seq)
    kv5 = lambda a: a.reshape(a.shape[0], a.shape[1], 2, NSA_KV_HEADS, HEAD_DIM)
    cmp_p = cmp_rows.reshape(nb, seq, 2, NSA_KV_HEADS, HEAD_DIM)
    sel_p = sel_rows.reshape(nb, seq, 2, NSA_KV_HEADS, HEAD_DIM)
    win_p = kv5(p3[:, seq - keep:, COL_KVW:COL_KVW + kvw])
    conv_p = p3[:, seq - (CONV_WIDTH - 1):, COL_QKV:COL_QKV + 3 * GDN_WIDTH]

    xs = jnp.pad(x_sample, ((0, 0), (0, SROWS - dec_seq), (0, 0))).reshape(nbs * SROWS, d)
    p2s, cmp_rows_s, sel_rows_s = _proj_in(xs, norm_mix[0], sm(1), sm(0), *w_proj, SROWS, nbs * SROWS)
    ckv_s = _compress_sample(cache_cmp_kv[0], page_table, *cw)
    ocmp, idx = _nsa_sample_select(p2s, ckv_s, slopes, nbs, past)
    o_nsa_s = _nsa_sample_attend(p2s, ocmp, idx, page_table, cache_sel_kv[0], state_win_kv[0], slopes, nbs, past)
    conv_buf = jnp.pad(state_conv[0], ((0, 0), (SUBLANE - (CONV_WIDTH - 1), 0), (0, 0)))
    o_gdn_s, gdn_s = _gdn(p2s, conv_w[0], conv_buf, state_gdn[0], gdn_a_log[0], gdn_dt_bias[0], gdn_norm[0],
                          nbs, SROWS, dec_seq)
    head_rows = lambda a: a.reshape(nbs, SROWS, a.shape[-1])[:, :SROWS_TAIL].reshape(nbs * SROWS_TAIL, a.shape[-1])
    nt = nbs * SROWS_TAIL
    ys = _mixer_tail(head_rows(xs), head_rows(o_nsa_s), head_rows(o_gdn_s), w_out_bf, sm(2), norm_ffn[0], sm(4),
                     sm(3), sm(5), wq_hi, wq_lo, peer_keys[0], u_bf, vt_bf, final_norm, SROWS_TAIL, nt, nt, nt, nt)
    y_sample = ys.reshape(nbs, SROWS_TAIL, d)[:, 0:dec_seq]
    p3s = p2s.reshape(nbs, SROWS, N_PROJ)[:, 0:dec_seq]
    cmp_s = cmp_rows_s.reshape(nbs, SROWS, 2, NSA_KV_HEADS, HEAD_DIM)[:, 0:dec_seq]
    sel_s = sel_rows_s.reshape(nbs, SROWS, 2, NSA_KV_HEADS, HEAD_DIM)[:, 0:dec_seq]
    win_new = kv5(p3s[:, :, COL_KVW:COL_KVW + kvw])
    win_s = jnp.concatenate([state_win_kv[0], win_new], axis=1)[:, dec_seq:]
    conv_s = jnp.concatenate([state_conv[0], p3s[:, :, COL_QKV:COL_QKV + 3 * GDN_WIDTH]], axis=1)[:, dec_seq:]

    return (y_prompt, y_sample, cmp_p[None], cmp_s[None], sel_p[None], sel_s[None], win_p[None], win_s[None],
            conv_p[None], conv_s[None], gdn_p[None], gdn_s[None])
```
